```python
import jax, jax.numpy as jnp
from jax import lax
import numpy as np

D_MODEL = 1024
BATCH = 32
SEQ = 2048
DEPTH = 2

N_BRANCH = 4
BRANCH_W = D_MODEL // 2
N_GROUPS = 4
GROUP_W = BRANCH_W // N_GROUPS
POOL_WINDOWS = (2, 4, 8, 16)
CONV_K = 31
SHORT_K = 3
CHUNK = 128
N_PIECES = 12
N_BRANCH_COLS = N_PIECES * BRANCH_W
IN_COLS = N_BRANCH_COLS + N_BRANCH * D_MODEL
RMS_EPS = 1e-6
LN_EPS = 1e-5

kernel_name = "hybrid_gated_parallel_mixers"


def rms_norm(x, g):
    xf = x.astype(jnp.float32)
    y = xf * lax.rsqrt(jnp.mean(xf * xf, axis=-1, keepdims=True) + RMS_EPS)
    return (y * g.astype(jnp.float32)).astype(x.dtype)


def layer_norm(x, g, b):
    xf = x.astype(jnp.float32)
    mu = jnp.mean(xf, axis=-1, keepdims=True)
    var = jnp.mean(jnp.square(xf - mu), axis=-1, keepdims=True)
    y = (xf - mu) * lax.rsqrt(var + LN_EPS)
    return (y * g.astype(jnp.float32) + b.astype(jnp.float32)).astype(x.dtype)


def causal_dwconv(x, w):
    k, c = w.shape
    return lax.conv_general_dilated(
        x, w[:, None, :].astype(x.dtype), window_strides=(1,), padding=[(k - 1, 0)],
        dimension_numbers=("NWC", "WIO", "NWC"), feature_group_count=c)


def pool_mixer(xa, pool_w, pool_scale):
    b, s, _ = xa.shape
    xf = xa.astype(jnp.float32)
    csum = jnp.cumsum(xf, axis=1)
    t = jnp.arange(1, s + 1, dtype=jnp.float32)
    groups = []
    for j, win in enumerate(POOL_WINDOWS):
        cj = csum[..., j * GROUP_W:(j + 1) * GROUP_W]
        prev = jnp.pad(cj, ((0, 0), (win, 0), (0, 0)))[:, :s]
        mean = (cj - prev) / jnp.minimum(t, float(win))[None, :, None]
        groups.append(mean - xf[..., j * GROUP_W:(j + 1) * GROUP_W])
    pooled = jnp.stack(groups, axis=2)
    mixed = jnp.einsum("bsgc,gcd->bsgd", pooled, pool_w.astype(jnp.float32))
    return (mixed.reshape(b, s, BRANCH_W) * pool_scale.astype(jnp.float32)).astype(xa.dtype)


def conformer_conv(a, gb, conv_w, conv_b, ln_g, ln_b):
    y = a * jax.nn.sigmoid(gb)
    y = causal_dwconv(y, conv_w) + conv_b.astype(y.dtype)
    return jax.nn.silu(layer_norm(y, ln_g, ln_b))


def spatial_gating(u, v, ln_g, ln_b, sgu_w, sgu_b):
    b, s, _ = u.shape
    v = layer_norm(v, ln_g, ln_b).reshape(b, s // CHUNK, CHUNK, N_GROUPS, GROUP_W)
    mask = jnp.tril(jnp.ones((CHUNK, CHUNK), dtype=v.dtype))
    ws = sgu_w.astype(v.dtype) * mask[None]
    sp = jnp.einsum("gts,bnsgc->bntgc", ws, v) + sgu_b.T.astype(v.dtype)[None, None, :, :, None]
    return u * sp.reshape(b, s, BRANCH_W)


def short_gated_conv(bg, cg, xs, sc_w):
    return bg * causal_dwconv(cg * xs, sc_w)


def _fwd_setup_inputs(seed: int = 0) -> dict:
    key = jax.random.key(seed)
    ks = jax.random.split(key, 20)
    f32 = jnp.float32
    nrm = lambda k, shape, scale: jax.random.normal(k, shape, f32) * scale
    return {
        "x": jax.random.normal(ks[0], (BATCH, SEQ, D_MODEL), f32),
        "norm_g": 1.0 + nrm(ks[1], (DEPTH, D_MODEL), 0.02),
        "w_in": nrm(ks[2], (DEPTH, D_MODEL, IN_COLS), D_MODEL ** -0.5),
        "pool_w": nrm(ks[3], (DEPTH, N_GROUPS, GROUP_W, GROUP_W), GROUP_W ** -0.5),
        "pool_scale": 1.0 + nrm(ks[4], (DEPTH, BRANCH_W), 0.02),
        "conv_w": nrm(ks[5], (DEPTH, CONV_K, BRANCH_W), CONV_K ** -0.5),
        "conv_b": nrm(ks[6], (DEPTH, BRANCH_W), 0.01),
        "conv_ln_g": 1.0 + nrm(ks[7], (DEPTH, BRANCH_W), 0.02),
        "conv_ln_b": nrm(ks[8], (DEPTH, BRANCH_W), 0.01),
        "sgu_ln_g": 1.0 + nrm(ks[9], (DEPTH, BRANCH_W), 0.02),
        "sgu_ln_b": nrm(ks[10], (DEPTH, BRANCH_W), 0.01),
        "sgu_w": nrm(ks[11], (DEPTH, N_GROUPS, CHUNK, CHUNK), CHUNK ** -0.5),
        "sgu_b": 1.0 + nrm(ks[12], (DEPTH, N_GROUPS, CHUNK), 0.01),
        "sc_w": nrm(ks[13], (DEPTH, SHORT_K, BRANCH_W), SHORT_K ** -0.5),
        "w_branch": nrm(ks[14], (DEPTH, N_BRANCH, BRANCH_W, D_MODEL), BRANCH_W ** -0.5),
        "w_o": nrm(ks[15], (DEPTH, D_MODEL, D_MODEL), D_MODEL ** -0.5),
        "final_g": 1.0 + nrm(ks[16], (D_MODEL,), 0.02),
    }


def _fwd_reference(x, norm_g, w_in, pool_w, pool_scale, conv_w, conv_b, conv_ln_g, conv_ln_b,
              sgu_ln_g, sgu_ln_b, sgu_w, sgu_b, sc_w, w_branch, w_o, final_g):
    b, s, d = x.shape
    for l in range(DEPTH):
        h = rms_norm(x, norm_g[l])
        proj = jnp.einsum("bsd,dk->bsk", h, w_in[l].astype(h.dtype))
        (p_x, p_gate, c_a, c_b, c_gate, g_u, g_v, g_gate,
         s_b, s_c, s_x, s_gate) = jnp.split(proj[..., :N_BRANCH_COLS], N_PIECES, axis=-1)
        merge_gates = jax.nn.sigmoid(proj[..., N_BRANCH_COLS:].reshape(b, s, N_BRANCH, d))

        z_pool = pool_mixer(p_x, pool_w[l], pool_scale[l]) * jax.nn.silu(p_gate)
        z_conv = conformer_conv(c_a, c_b, conv_w[l], conv_b[l], conv_ln_g[l], conv_ln_b[l]) * jax.nn.silu(c_gate)
        z_sgu = spatial_gating(g_u, g_v, sgu_ln_g[l], sgu_ln_b[l], sgu_w[l], sgu_b[l]) * jax.nn.silu(g_gate)
        z_sc = short_gated_conv(s_b, s_c, s_x, sc_w[l]) * jax.nn.silu(s_gate)

        z = jnp.stack([z_pool, z_conv, z_sgu, z_sc], axis=2)
        branch_out = jnp.einsum("bsnc,ncd->bsnd", z, w_branch[l].astype(z.dtype))
        merged = jnp.sum(merge_gates * branch_out, axis=2)
        x = x + jnp.einsum("bsd,de->bse", merged, w_o[l].astype(merged.dtype))
    return rms_norm(x, final_g)


import jax as _jax
import jax.numpy as _jnp

TWIN_FORMAT = 'train_step'
FWD_PARAMS = ['x', 'norm_g', 'w_in', 'pool_w', 'pool_scale', 'conv_w', 'conv_b', 'conv_ln_g', 'conv_ln_b', 'sgu_ln_g', 'sgu_ln_b', 'sgu_w', 'sgu_b', 'sc_w', 'w_branch', 'w_o', 'final_g']
TWIN_WEIGHTS = ['norm_g', 'w_in', 'pool_w', 'pool_scale', 'conv_w', 'conv_b', 'conv_ln_g', 'conv_ln_b', 'sgu_ln_g', 'sgu_ln_b', 'sgu_w', 'sgu_b', 'sc_w', 'w_branch', 'w_o', 'final_g']
TWIN_DIFF_INPUT = 'x'
TWIN_INPUTS = ['x', 'norm_g', 'w_in', 'pool_w', 'pool_scale', 'conv_w', 'conv_b', 'conv_ln_g', 'conv_ln_b', 'sgu_ln_g', 'sgu_ln_b', 'sgu_w', 'sgu_b', 'sc_w', 'w_branch', 'w_o', 'final_g', 'loss_target', 'm_norm_g', 'm_w_in', 'm_pool_w', 'm_pool_scale', 'm_conv_w', 'm_conv_b', 'm_conv_ln_g', 'm_conv_ln_b', 'm_sgu_ln_g', 'm_sgu_ln_b', 'm_sgu_w', 'm_sgu_b', 'm_sc_w', 'm_w_branch', 'm_w_o', 'm_final_g', 'v_norm_g', 'v_w_in', 'v_pool_w', 'v_pool_scale', 'v_conv_w', 'v_conv_b', 'v_conv_ln_g', 'v_conv_ln_b', 'v_sgu_ln_g', 'v_sgu_ln_b', 'v_sgu_w', 'v_sgu_b', 'v_sc_w', 'v_w_branch', 'v_w_o', 'v_final_g']
TWIN_OUTPUTS = ['loss', 'grad_x', 'grad_norm_g', 'grad_w_in', 'grad_pool_w', 'grad_pool_scale', 'grad_conv_w', 'grad_conv_b', 'grad_conv_ln_g', 'grad_conv_ln_b', 'grad_sgu_ln_g', 'grad_sgu_ln_b', 'grad_sgu_w', 'grad_sgu_b', 'grad_sc_w', 'grad_w_branch', 'grad_w_o', 'grad_final_g', 'delta_norm_g', 'delta_w_in', 'delta_pool_w', 'delta_pool_scale', 'delta_conv_w', 'delta_conv_b', 'delta_conv_ln_g', 'delta_conv_ln_b', 'delta_sgu_ln_g', 'delta_sgu_ln_b', 'delta_sgu_w', 'delta_sgu_b', 'delta_sc_w', 'delta_w_branch', 'delta_w_o', 'delta_final_g', 'new_m_norm_g', 'new_m_w_in', 'new_m_pool_w', 'new_m_pool_scale', 'new_m_conv_w', 'new_m_conv_b', 'new_m_conv_ln_g', 'new_m_conv_ln_b', 'new_m_sgu_ln_g', 'new_m_sgu_ln_b', 'new_m_sgu_w', 'new_m_sgu_b', 'new_m_sc_w', 'new_m_w_branch', 'new_m_w_o', 'new_m_final_g', 'new_v_norm_g', 'new_v_w_in', 'new_v_pool_w', 'new_v_pool_scale', 'new_v_conv_w', 'new_v_conv_b', 'new_v_conv_ln_g', 'new_v_conv_ln_b', 'new_v_sgu_ln_g', 'new_v_sgu_ln_b', 'new_v_sgu_w', 'new_v_sgu_b', 'new_v_sc_w', 'new_v_w_branch', 'new_v_w_o', 'new_v_final_g']
TWIN_LEAF_KINDS = {'loss': 'loss', 'grad_x': 'grad_x', 'grad_norm_g': 'grad_w', 'grad_w_in': 'grad_w', 'grad_pool_w': 'grad_w', 'grad_pool_scale': 'grad_w', 'grad_conv_w': 'grad_w', 'grad_conv_b': 'grad_w', 'grad_conv_ln_g': 'grad_w', 'grad_conv_ln_b': 'grad_w', 'grad_sgu_ln_g': 'grad_w', 'grad_sgu_ln_b': 'grad_w', 'grad_sgu_w': 'grad_w', 'grad_sgu_b': 'grad_w', 'grad_sc_w': 'grad_w', 'grad_w_branch': 'grad_w', 'grad_w_o': 'grad_w', 'grad_final_g': 'grad_w', 'delta_norm_g': 'delta_w', 'delta_w_in': 'delta_w', 'delta_pool_w': 'delta_w', 'delta_pool_scale': 'delta_w', 'delta_conv_w': 'delta_w', 'delta_conv_b': 'delta_w', 'delta_conv_ln_g': 'delta_w', 'delta_conv_ln_b': 'delta_w', 'delta_sgu_ln_g': 'delta_w', 'delta_sgu_ln_b': 'delta_w', 'delta_sgu_w': 'delta_w', 'delta_sgu_b': 'delta_w', 'delta_sc_w': 'delta_w', 'delta_w_branch': 'delta_w', 'delta_w_o': 'delta_w', 'delta_final_g': 'delta_w', 'new_m_norm_g': 'new_m', 'new_m_w_in': 'new_m', 'new_m_pool_w': 'new_m', 'new_m_pool_scale': 'new_m', 'new_m_conv_w': 'new_m', 'new_m_conv_b': 'new_m', 'new_m_conv_ln_g': 'new_m', 'new_m_conv_ln_b': 'new_m', 'new_m_sgu_ln_g': 'new_m', 'new_m_sgu_ln_b': 'new_m', 'new_m_sgu_w': 'new_m', 'new_m_sgu_b': 'new_m', 'new_m_sc_w': 'new_m', 'new_m_w_branch': 'new_m', 'new_m_w_o': 'new_m', 'new_m_final_g': 'new_m', 'new_v_norm_g': 'new_v', 'new_v_w_in': 'new_v', 'new_v_pool_w': 'new_v', 'new_v_pool_scale': 'new_v', 'new_v_conv_w': 'new_v', 'new_v_conv_b': 'new_v', 'new_v_conv_ln_g': 'new_v', 'new_v_conv_ln_b': 'new_v', 'new_v_sgu_ln_g': 'new_v', 'new_v_sgu_ln_b': 'new_v', 'new_v_sgu_w': 'new_v', 'new_v_sgu_b': 'new_v', 'new_v_sc_w': 'new_v', 'new_v_w_branch': 'new_v', 'new_v_w_o': 'new_v', 'new_v_final_g': 'new_v'}


def _forward(args):
    return _fwd_reference(*[args[k] for k in FWD_PARAMS])


def _output_shape():
    out = _jax.eval_shape(lambda: _forward(_fwd_setup_inputs(0)))
    return out.shape, out.dtype

N_MICROBATCH = 1
ADAM_LR = 0.001
ADAM_B1 = 0.9
ADAM_B2 = 0.999
ADAM_EPS = 1e-08
ADAM_WD = 0.01
ADAM_STEP = 10
PER_EXAMPLE_BATCH_AXIS = {'x': 0, 'loss_target': 0}
SHARED_INPUTS = []
_WEIGHT_DTYPES = {'norm_g': _jnp.float32, 'w_in': _jnp.float32, 'pool_w': _jnp.float32, 'pool_scale': _jnp.float32, 'conv_w': _jnp.float32, 'conv_b': _jnp.float32, 'conv_ln_g': _jnp.float32, 'conv_ln_b': _jnp.float32, 'sgu_ln_g': _jnp.float32, 'sgu_ln_b': _jnp.float32, 'sgu_w': _jnp.float32, 'sgu_b': _jnp.float32, 'sc_w': _jnp.float32, 'w_branch': _jnp.float32, 'w_o': _jnp.float32, 'final_g': _jnp.float32}
MOMENT_SCALE = {'norm_g': 2.509062e-01, 'w_in': 7.517900e-02, 'pool_w': 9.167991e-02, 'pool_scale': 9.051708e-02, 'conv_w': 6.368627e-02, 'conv_b': 1.362562e-01, 'conv_ln_g': 7.485702e-02, 'conv_ln_b': 6.499837e-02, 'sgu_ln_g': 7.025697e-02, 'sgu_ln_b': 6.802701e-02, 'sgu_w': 7.244047e-02, 'sgu_b': 1.017053e-01, 'sc_w': 1.038438e-01, 'w_branch': 6.930734e-02, 'w_o': 1.386425e-01, 'final_g': 6.407271e+01}


def _to_microbatches(a, axis):
    t = _jnp.moveaxis(a, axis, 0)
    t = t.reshape((N_MICROBATCH, t.shape[0] // N_MICROBATCH) + t.shape[1:])
    return _jnp.moveaxis(t, 1, axis + 1)


def setup_inputs(seed: int = 0) -> dict:
    inp = _fwd_setup_inputs(seed)
    key = _jax.random.fold_in(_jax.random.key(seed), 7919)
    shape, _ = _output_shape()
    out = dict(inp)
    out["loss_target"] = _jax.random.normal(_jax.random.fold_in(key, 0), shape, _jnp.float32)
    for i, name in enumerate(TWIN_WEIGHTS):
        w = inp[name].astype(_jnp.float32)
        if MOMENT_SCALE is None:
            s = _jnp.sqrt(_jnp.mean(_jnp.square(w)) + 1e-30)
        else:
            s = MOMENT_SCALE[name]
        km, kv = _jax.random.split(_jax.random.fold_in(key, i + 1))
        out[name] = w
        out["m_" + name] = s * _jax.random.normal(km, w.shape, _jnp.float32)
        out["v_" + name] = (s * s) * _jax.random.uniform(kv, w.shape, _jnp.float32, 0.5, 1.5)
    if N_MICROBATCH > 1:
        for name, axis in PER_EXAMPLE_BATCH_AXIS.items():
            out[name] = _to_microbatches(out[name], axis)
    return {'x': out['x'], 'norm_g': out['norm_g'], 'w_in': out['w_in'], 'pool_w': out['pool_w'], 'pool_scale': out['pool_scale'], 'conv_w': out['conv_w'], 'conv_b': out['conv_b'], 'conv_ln_g': out['conv_ln_g'], 'conv_ln_b': out['conv_ln_b'], 'sgu_ln_g': out['sgu_ln_g'], 'sgu_ln_b': out['sgu_ln_b'], 'sgu_w': out['sgu_w'], 'sgu_b': out['sgu_b'], 'sc_w': out['sc_w'], 'w_branch': out['w_branch'], 'w_o': out['w_o'], 'final_g': out['final_g'], 'loss_target': out['loss_target'], 'm_norm_g': out['m_norm_g'], 'm_w_in': out['m_w_in'], 'm_pool_w': out['m_pool_w'], 'm_pool_scale': out['m_pool_scale'], 'm_conv_w': out['m_conv_w'], 'm_conv_b': out['m_conv_b'], 'm_conv_ln_g': out['m_conv_ln_g'], 'm_conv_ln_b': out['m_conv_ln_b'], 'm_sgu_ln_g': out['m_sgu_ln_g'], 'm_sgu_ln_b': out['m_sgu_ln_b'], 'm_sgu_w': out['m_sgu_w'], 'm_sgu_b': out['m_sgu_b'], 'm_sc_w': out['m_sc_w'], 'm_w_branch': out['m_w_branch'], 'm_w_o': out['m_w_o'], 'm_final_g': out['m_final_g'], 'v_norm_g': out['v_norm_g'], 'v_w_in': out['v_w_in'], 'v_pool_w': out['v_pool_w'], 'v_pool_scale': out['v_pool_scale'], 'v_conv_w': out['v_conv_w'], 'v_conv_b': out['v_conv_b'], 'v_conv_ln_g': out['v_conv_ln_g'], 'v_conv_ln_b': out['v_conv_ln_b'], 'v_sgu_ln_g': out['v_sgu_ln_g'], 'v_sgu_ln_b': out['v_sgu_ln_b'], 'v_sgu_w': out['v_sgu_w'], 'v_sgu_b': out['v_sgu_b'], 'v_sc_w': out['v_sc_w'], 'v_w_branch': out['v_w_branch'], 'v_w_o': out['v_w_o'], 'v_final_g': out['v_final_g']}


def _loss(weights, diff, rest, loss_target):
    with _jax.named_scope("forward"):
        args = {**rest, TWIN_DIFF_INPUT: diff, **{k: w.astype(_WEIGHT_DTYPES[k]) for k, w in weights.items()}}
        y = _forward(args)
    with _jax.named_scope("loss_head"):
        err = _jnp.square(y.astype(_jnp.float32) - loss_target)
        return 0.5 * _jnp.sum(_jnp.mean(err, axis=-1)) if err.ndim else 0.5 * err


def _adamw(w, g, m, v):
    m = ADAM_B1 * m + (1.0 - ADAM_B1) * g
    v = ADAM_B2 * v + (1.0 - ADAM_B2) * _jnp.square(g)
    m_hat = m / (1.0 - ADAM_B1 ** ADAM_STEP)
    v_hat = v / (1.0 - ADAM_B2 ** ADAM_STEP)
    delta = -ADAM_LR * (m_hat / (_jnp.sqrt(v_hat) + ADAM_EPS) + ADAM_WD * w)
    return delta, m, v


def reference(x, norm_g, w_in, pool_w, pool_scale, conv_w, conv_b, conv_ln_g, conv_ln_b, sgu_ln_g, sgu_ln_b, sgu_w, sgu_b, sc_w, w_branch, w_o, final_g, loss_target, m_norm_g, m_w_in, m_pool_w, m_pool_scale, m_conv_w, m_conv_b, m_conv_ln_g, m_conv_ln_b, m_sgu_ln_g, m_sgu_ln_b, m_sgu_w, m_sgu_b, m_sc_w, m_w_branch, m_w_o, m_final_g, v_norm_g, v_w_in, v_pool_w, v_pool_scale, v_conv_w, v_conv_b, v_conv_ln_g, v_conv_ln_b, v_sgu_ln_g, v_sgu_ln_b, v_sgu_w, v_sgu_b, v_sc_w, v_w_branch, v_w_o, v_final_g):
    given = dict(x=x, norm_g=norm_g, w_in=w_in, pool_w=pool_w, pool_scale=pool_scale, conv_w=conv_w, conv_b=conv_b, conv_ln_g=conv_ln_g, conv_ln_b=conv_ln_b, sgu_ln_g=sgu_ln_g, sgu_ln_b=sgu_ln_b, sgu_w=sgu_w, sgu_b=sgu_b, sc_w=sc_w, w_branch=w_branch, w_o=w_o, final_g=final_g, loss_target=loss_target, m_norm_g=m_norm_g, m_w_in=m_w_in, m_pool_w=m_pool_w, m_pool_scale=m_pool_scale, m_conv_w=m_conv_w, m_conv_b=m_conv_b, m_conv_ln_g=m_conv_ln_g, m_conv_ln_b=m_conv_ln_b, m_sgu_ln_g=m_sgu_ln_g, m_sgu_ln_b=m_sgu_ln_b, m_sgu_w=m_sgu_w, m_sgu_b=m_sgu_b, m_sc_w=m_sc_w, m_w_branch=m_w_branch, m_w_o=m_w_o, m_final_g=m_final_g, v_norm_g=v_norm_g, v_w_in=v_w_in, v_pool_w=v_pool_w, v_pool_scale=v_pool_scale, v_conv_w=v_conv_w, v_conv_b=v_conv_b, v_conv_ln_g=v_conv_ln_g, v_conv_ln_b=v_conv_ln_b, v_sgu_ln_g=v_sgu_ln_g, v_sgu_ln_b=v_sgu_ln_b, v_sgu_w=v_sgu_w, v_sgu_b=v_sgu_b, v_sc_w=v_sc_w, v_w_branch=v_w_branch, v_w_o=v_w_o, v_final_g=v_final_g)
    weights = {n: given[n] for n in TWIN_WEIGHTS}
    shared = {n: given[n] for n in SHARED_INPUTS}
    per_example = {n: given[n] for n in ['x']}
    grad_fn = _jax.value_and_grad(_loss, argnums=(0, 1))

    def one_microbatch(ex, loss_target):
        ex = dict(ex)
        diff = ex.pop(TWIN_DIFF_INPUT)
        return grad_fn(weights, diff, {**shared, **ex}, loss_target)

    if N_MICROBATCH == 1:
        loss, (grad_w, grad_x) = one_microbatch(per_example, given["loss_target"])
    else:
        def body(carry, xs):
            loss_sum, grad_sum = carry
            l_k, (gw_k, gx_k) = one_microbatch(xs[0], xs[1])
            with _jax.named_scope("update"):
                return (loss_sum + l_k, _jax.tree.map(_jnp.add, grad_sum, gw_k)), gx_k

        init = (_jnp.zeros((), _jnp.float32), _jax.tree.map(_jnp.zeros_like, weights))
        (loss, grad_w), grad_x = _jax.lax.scan(body, init, (per_example, given["loss_target"]))
    with _jax.named_scope("update"):
        delta_w, new_m, new_v = {}, {}, {}
        for n in TWIN_WEIGHTS:
            delta_w[n], new_m[n], new_v[n] = _adamw(weights[n], grad_w[n], given["m_" + n], given["v_" + n])
    return (loss, grad_x, *[grad_w[n] for n in TWIN_WEIGHTS], *[delta_w[n] for n in TWIN_WEIGHTS],
            *[new_m[n] for n in TWIN_WEIGHTS], *[new_v[n] for n in TWIN_WEIGHTS])
```

```python
import functools

import jax
import jax.numpy as jnp
from jax import lax
from jax.experimental import pallas as pl
from jax.experimental.pallas import tpu as pltpu

F32 = jnp.float32
BF16 = jnp.bfloat16
SDS = jax.ShapeDtypeStruct
MESH = pl.DeviceIdType.MESH

D = 1024
BW = 512
NG = 4
GW = 128
CONV_K = 31
SHORT_K = 3
POOL_WINDOWS = (2, 4, 8, 16)
N_PIECE_COLS = 12 * BW
IN_COLS = N_PIECE_COLS + 4 * D
NDEV = 8
SHARD = IN_COLS // NDEV
DEPTH = 2
RMS_EPS = 1e-6
LN_EPS = 1e-5
HALO_POOL, HALO_CONV, HALO_SC = 16, 32, 8

ADAM_LR = 0.001
ADAM_B1 = 0.9
ADAM_B2 = 0.999
ADAM_EPS = 1e-08
ADAM_WD = 0.01
ADAM_STEP = 10

VMEM_LIMIT = 56 * 1024 * 1024


def _params(*sem):
    return pltpu.CompilerParams(dimension_semantics=sem, vmem_limit_bytes=VMEM_LIMIT)


def _sigmoid(x):
    return jax.nn.sigmoid(x)


def _silu(x):
    return x * _sigmoid(x)


def _dsilu(x):
    s = _sigmoid(x)
    return s * (1.0 + x * (1.0 - s))


def _colsum(x):
    return jnp.sum(x, axis=0, keepdims=True)


def _dot(a, b):
    return jnp.dot(a, b, preferred_element_type=F32)


def _dot_nt(a, b):
    return lax.dot_general(a, b, (((1,), (1,)), ((), ())), preferred_element_type=F32)


def _dot_tn(a, b):
    return lax.dot_general(a, b, (((0,), (0,)), ((), ())), preferred_element_type=F32)


def _layer_norm_stats(y):
    mu = jnp.mean(y, axis=-1, keepdims=True)
    yc = y - mu
    var = jnp.mean(yc * yc, axis=-1, keepdims=True)
    rstd = lax.rsqrt(var + LN_EPS)
    return yc * rstd, rstd


def _layer_norm_bwd(d_hat, hat, rstd):
    return rstd * (d_hat - jnp.mean(d_hat, axis=-1, keepdims=True)
                   - hat * jnp.mean(d_hat * hat, axis=-1, keepdims=True))


def _in_proj(x2d, g_row, w_g, tm):
    n = x2d.shape[0]

    def body(x_ref, g_ref, w_ref, proj_ref, h_ref):
        @pl.when(pl.program_id(1) == 0)
        def _():
            x = x_ref[...]
            r = lax.rsqrt(jnp.mean(x * x, axis=-1, keepdims=True) + RMS_EPS)
            h_ref[...] = (x * r * g_ref[...]).astype(BF16)

        proj_ref[...] = _dot(h_ref[...], w_ref[...])

    return pl.pallas_call(
        body, name="in_proj", grid=(n // tm, NDEV),
        in_specs=[pl.BlockSpec((tm, D), lambda i, j: (i, 0)),
                  pl.BlockSpec((1, D), lambda i, j: (0, 0)),
                  pl.BlockSpec((None, D, SHARD), lambda i, j: (j, 0, 0))],
        out_specs=[pl.BlockSpec((tm, SHARD), lambda i, j: (i, j)),
                   pl.BlockSpec((tm, D), lambda i, j: (i, 0))],
        out_shape=[SDS((n, IN_COLS), F32), SDS((n, D), BF16)],
        compiler_params=_params("parallel", "arbitrary"),
    )(x2d, g_row, w_g)


def _tril_bf16(w):
    r = lax.broadcasted_iota(jnp.int32, (GW, GW), 0)
    c = lax.broadcasted_iota(jnp.int32, (GW, GW), 1)
    return jnp.where(r >= c, w, 0.0).astype(BF16)


def _mixers_fwd(proj, pool_w, vecs, conv_w, sgu_w, sgu_bc, sc_w, nb, seq, t):
    n = nb * seq
    nt = seq // t

    def body(p_ref, poolw_ref, vec_ref, convw_ref, sguw_ref, sgub_ref, scw_ref,
             z_ref, y_ref, q_ref, pooled_ref, pext, uext, vext):
        i = pl.program_id(1)

        @pl.when(i == 0)
        def _():
            pext[0:HALO_POOL, :] = jnp.zeros((HALO_POOL, BW), F32)
            uext[0:HALO_CONV, :] = jnp.zeros((HALO_CONV, BW), F32)
            vext[0:HALO_SC, :] = jnp.zeros((HALO_SC, BW), F32)

        @pl.when(i > 0)
        def _():
            pext[0:HALO_POOL, :] = pext[t:t + HALO_POOL, :]
            uext[0:HALO_CONV, :] = uext[t:t + HALO_CONV, :]
            vext[0:HALO_SC, :] = vext[t:t + HALO_SC, :]

        def piece(k):
            return p_ref[:, k * BW:(k + 1) * BW]

        pos = lax.broadcasted_iota(jnp.int32, (t, 1), 0) + i * t

        px = piece(0)
        pext[HALO_POOL:, :] = px
        for j, win in enumerate(POOL_WINDOWS):
            cols = slice(j * GW, (j + 1) * GW)
            s = pext[:, cols]
            step = 1
            while step < win:
                s = s + pltpu.roll(s, step, 0)
                step *= 2
            cnt = jnp.minimum(pos + 1, win).astype(F32)
            pooled = (s[HALO_POOL:, :] / cnt - px[:, cols]).astype(BF16)
            pooled_ref[:, cols] = pooled
            mixed = _dot(pooled, poolw_ref[j].astype(BF16))
            z_ref[:, cols] = (mixed * vec_ref[0:1, cols] * _silu(p_ref[:, BW + j * GW:BW + (j + 1) * GW])).astype(BF16)

        uext[HALO_CONV:, :] = piece(2) * _sigmoid(piece(3))
        ue = uext[...]
        acc = jnp.zeros((t, BW), F32)
        for j in range(CONV_K):
            sh = ue if j == 0 else pltpu.roll(ue, j, 0)
            acc = acc + convw_ref[CONV_K - 1 - j:CONV_K - j, :] * sh[HALO_CONV:, :]
        y = acc + vec_ref[1:2, :]
        y_ref[...] = y
        yhat, _ = _layer_norm_stats(y)
        act = _silu(yhat * vec_ref[2:3, :] + vec_ref[3:4, :])
        z_ref[:, BW:2 * BW] = (act * _silu(piece(4))).astype(BF16)

        vhat, _ = _layer_norm_stats(piece(6))
        vn = (vhat * vec_ref[4:5, :] + vec_ref[5:6, :]).astype(BF16)
        for g in range(NG):
            ws = _tril_bf16(sguw_ref[g])
            cols = slice(g * GW, (g + 1) * GW)
            for c in range(t // GW):
                rows = slice(c * GW, (c + 1) * GW)
                sp = _dot(ws, vn[rows, cols]) + sgub_ref[g]
                gate = _silu(p_ref[rows, 7 * BW + g * GW:7 * BW + (g + 1) * GW])
                z_ref[rows, 2 * BW + g * GW:2 * BW + (g + 1) * GW] = (
                    p_ref[rows, 5 * BW + g * GW:5 * BW + (g + 1) * GW] * sp * gate).astype(BF16)

        vext[HALO_SC:, :] = piece(9) * piece(10)
        ve = vext[...]
        q = jnp.zeros((t, BW), F32)
        for j in range(SHORT_K):
            sh = ve if j == 0 else pltpu.roll(ve, j, 0)
            q = q + scw_ref[SHORT_K - 1 - j:SHORT_K - j, :] * sh[HALO_SC:, :]
        q_ref[...] = q
        z_ref[:, 3 * BW:] = (piece(8) * q * _silu(piece(11))).astype(BF16)

    row = lambda b, i: (b * nt + i, 0)
    full2 = lambda b, i: (0, 0)
    full3 = lambda b, i: (0, 0, 0)
    return pl.pallas_call(
        body, name="mixers_fwd", grid=(nb, nt),
        in_specs=[pl.BlockSpec((t, N_PIECE_COLS), row),
                  pl.BlockSpec((NG, GW, GW), full3),
                  pl.BlockSpec((8, BW), full2),
                  pl.BlockSpec((32, BW), full2),
                  pl.BlockSpec((NG, GW, GW), full3),
                  pl.BlockSpec((NG, GW, 1), full3),
                  pl.BlockSpec((8, BW), full2)],
        out_specs=[pl.BlockSpec((t, 4 * BW), row), pl.BlockSpec((t, BW), row),
                   pl.BlockSpec((t, BW), row), pl.BlockSpec((t, BW), row)],
        out_shape=[SDS((n, 4 * BW), BF16), SDS((n, BW), F32), SDS((n, BW), F32), SDS((n, BW), BF16)],
        scratch_shapes=[pltpu.VMEM((t + HALO_POOL, BW), F32), pltpu.VMEM((t + HALO_CONV, BW), F32),
                        pltpu.VMEM((t + HALO_SC, BW), F32)],
        compiler_params=_params("arbitrary", "arbitrary"),
    )(proj, pool_w, vecs, conv_w, sgu_w, sgu_bc, sc_w)


def _merge_out(z, proj, x2d, wb, wo, tm):
    n = x2d.shape[0]

    def body(z_ref, mga_ref, mgb_ref, x_ref, wb_ref, wo_ref, xo_ref, m_ref):
        merged = jnp.zeros((tm, D), F32)
        for k in range(4):
            bo = _dot(z_ref[:, k * BW:(k + 1) * BW], wb_ref[k])
            mg_ref = mga_ref if k < 2 else mgb_ref
            merged = merged + _sigmoid(mg_ref[:, (k % 2) * D:(k % 2 + 1) * D]) * bo
        mb = merged.astype(BF16)
        m_ref[...] = mb
        xo_ref[...] = x_ref[...] + _dot(mb, wo_ref[...])

    return pl.pallas_call(
        body, name="merge_out", grid=(n // tm,),
        in_specs=[pl.BlockSpec((tm, 4 * BW), lambda i: (i, 0)),
                  pl.BlockSpec((tm, 2 * D), lambda i: (i, 3)),
                  pl.BlockSpec((tm, 2 * D), lambda i: (i, 4)),
                  pl.BlockSpec((tm, D), lambda i: (i, 0)),
                  pl.BlockSpec((4, BW, D), lambda i: (0, 0, 0)),
                  pl.BlockSpec((D, D), lambda i: (0, 0))],
        out_specs=[pl.BlockSpec((tm, D), lambda i: (i, 0)), pl.BlockSpec((tm, D), lambda i: (i, 0))],
        out_shape=[SDS((n, D), F32), SDS((n, D), BF16)],
        compiler_params=_params("parallel"),
    )(z, proj, proj, x2d, wb, wo)


def _loss_head(x2d, tgt2d, g_row, tm):
    n = x2d.shape[0]

    def body(x_ref, t_ref, g_ref, dx_ref, loss_ref, dg_ref):
        @pl.when(pl.program_id(0) == 0)
        def _():
            loss_ref[...] = jnp.zeros_like(loss_ref)
            dg_ref[...] = jnp.zeros_like(dg_ref)

        x = x_ref[...]
        g = g_ref[...]
        r = lax.rsqrt(jnp.mean(x * x, axis=-1, keepdims=True) + RMS_EPS)
        xn = x * r
        e = xn * g - t_ref[...]
        loss_ref[...] += 0.5 * jnp.sum(jnp.mean(e * e, axis=-1, keepdims=True))
        dy = e * (1.0 / D)
        dg_ref[...] += _colsum(dy * xn)
        gy = dy * g
        dx_ref[...] = r * gy - xn * (r * jnp.mean(gy * xn, axis=-1, keepdims=True))

    return pl.pallas_call(
        body, name="loss_head", grid=(n // tm,),
        in_specs=[pl.BlockSpec((tm, D), lambda i: (i, 0)), pl.BlockSpec((tm, D), lambda i: (i, 0)),
                  pl.BlockSpec((1, D), lambda i: (0, 0))],
        out_specs=[pl.BlockSpec((tm, D), lambda i: (i, 0)), pl.BlockSpec((8, 128), lambda i: (0, 0)),
                   pl.BlockSpec((1, D), lambda i: (0, 0))],
        out_shape=[SDS((n, D), F32), SDS((8, 128), F32), SDS((1, D), F32)],
        compiler_params=_params("arbitrary"),
    )(x2d, tgt2d, g_row)


def _merge_out_bwd(dxo, z, proj, wb, wo, tm):
    n = dxo.shape[0]

    def body(dx_ref, z_ref, mg_ref, wb_ref, wo_ref, dz_ref, dbo_ref, dp_ref, dm_ref):
        @pl.when(pl.program_id(1) == 0)
        def _():
            dm_ref[...] = _dot_nt(dx_ref[...].astype(BF16), wo_ref[...])

        dm = dm_ref[...]
        for k in range(2):
            bo = _dot(z_ref[:, k * BW:(k + 1) * BW], wb_ref[k])
            sig = _sigmoid(mg_ref[:, k * D:(k + 1) * D])
            dbo = (dm * sig).astype(BF16)
            dbo_ref[:, k * D:(k + 1) * D] = dbo
            dp_ref[:, k * D:(k + 1) * D] = (dm * bo * sig * (1.0 - sig)).astype(BF16)
            dz_ref[:, k * BW:(k + 1) * BW] = _dot_nt(dbo, wb_ref[k])

    return pl.pallas_call(
        body, name="merge_out_bwd", grid=(n // tm, 2),
        in_specs=[pl.BlockSpec((tm, D), lambda i, h: (i, 0)),
                  pl.BlockSpec((tm, 2 * BW), lambda i, h: (i, h)),
                  pl.BlockSpec((tm, 2 * D), lambda i, h: (i, 3 + h)),
                  pl.BlockSpec((2, BW, D), lambda i, h: (h, 0, 0)),
                  pl.BlockSpec((D, D), lambda i, h: (0, 0))],
        out_specs=[pl.BlockSpec((tm, 2 * BW), lambda i, h: (i, h)),
                   pl.BlockSpec((tm, 2 * D), lambda i, h: (i, h)),
                   pl.BlockSpec((tm, 2 * D), lambda i, h: (i, 3 + h))],
        out_shape=[SDS((n, 4 * BW), F32), SDS((n, 4 * D), BF16), SDS((n, IN_COLS), BF16)],
        scratch_shapes=[pltpu.VMEM((tm, D), F32)],
        compiler_params=_params("parallel", "arbitrary"),
    )(dxo, z, proj, wb, wo)


def _matmul_tn(a, b, nblk, a_cols, b_cols, a_follows, tk, name):
    kdim = a.shape[0]

    def body(a_ref, b_ref, o_ref):
        @pl.when(pl.program_id(1) == 0)
        def _():
            o_ref[...] = jnp.zeros_like(o_ref)

        o_ref[...] += _dot_tn(a_ref[...].astype(BF16), b_ref[...].astype(BF16))

    a_map = (lambda j, k: (k, j)) if a_follows else (lambda j, k: (k, 0))
    return pl.pallas_call(
        body, name=name, grid=(nblk, kdim // tk),
        in_specs=[pl.BlockSpec((tk, a_cols), a_map), pl.BlockSpec((tk, b_cols), lambda j, k: (k, j))],
        out_specs=pl.BlockSpec((None, a_cols, b_cols), lambda j, k: (j, 0, 0)),
        out_shape=SDS((nblk, a_cols, b_cols), F32),
        compiler_params=_params("parallel", "arbitrary"),
    )(a, b)


def _mixers_bwd(proj, dz, y_conv, q_sc, pooled, d_proj, pool_w, vecs, conv_w, sgu_w, sgu_bc, sc_w, nb, seq, t):
    n = nb * seq
    nt = seq // t

    def body(p_ref, dz_ref, y_ref, q_ref, pooled_ref, dpin_ref,
             poolw_ref, vec_ref, convw_ref, sguw_ref, sgub_ref, scw_ref,
             dp_ref, dvec_ref, dconvw_ref, dscw_ref, dpoolw_ref, dsguw_ref, dsgub_ref,
             rext, dyext, dqext, bacc):
        b = pl.program_id(0)
        i = pl.program_id(1)
        first = jnp.logical_and(b == 0, i == 0)
        last = jnp.logical_and(b == nb - 1, i == nt - 1)

        @pl.when(first)
        def _():
            dvec_ref[...] = jnp.zeros_like(dvec_ref)
            dconvw_ref[...] = jnp.zeros_like(dconvw_ref)
            dscw_ref[...] = jnp.zeros_like(dscw_ref)
            dpoolw_ref[...] = jnp.zeros_like(dpoolw_ref)
            dsguw_ref[...] = jnp.zeros_like(dsguw_ref)
            dsgub_ref[...] = jnp.zeros_like(dsgub_ref)
            bacc[...] = jnp.zeros_like(bacc)

        @pl.when(i == 0)
        def _():
            rext[t:, :] = jnp.zeros((HALO_POOL, BW), F32)
            dyext[t:, :] = jnp.zeros((HALO_CONV, BW), F32)
            dqext[t:, :] = jnp.zeros((HALO_SC, BW), F32)

        @pl.when(i > 0)
        def _():
            rext[t:, :] = rext[0:HALO_POOL, :]
            dyext[t:, :] = dyext[0:HALO_CONV, :]
            dqext[t:, :] = dqext[0:HALO_SC, :]

        def piece(k):
            return p_ref[:, k * BW:(k + 1) * BW]

        def put(k, v):
            dp_ref[:, k * BW:(k + 1) * BW] = v.astype(BF16)

        pos = lax.broadcasted_iota(jnp.int32, (t, 1), 0) + (nt - 1 - i) * t

        dzp = dz_ref[:, 0:BW]
        pg = piece(1)
        d_pm = dzp * _silu(pg)
        for j, win in enumerate(POOL_WINDOWS):
            cols = slice(j * GW, (j + 1) * GW)
            pw = poolw_ref[j].astype(BF16)
            pooled = pooled_ref[:, cols]
            mixed = _dot(pooled, pw)
            scale = vec_ref[0:1, cols]
            dvec_ref[0:1, cols] += _colsum(d_pm[:, cols] * mixed)
            dp_ref[:, BW + j * GW:BW + (j + 1) * GW] = (dzp[:, cols] * (mixed * scale) * _dsilu(pg[:, cols])).astype(BF16)
            d_mixed = (d_pm[:, cols] * scale).astype(BF16)
            dpoolw_ref[j] += _dot_tn(pooled, d_mixed)
            d_pooled = _dot_nt(d_mixed, pw)
            cnt = jnp.minimum(pos + 1, win).astype(F32)
            rext[0:t, cols] = d_pooled / cnt
            s = rext[:, cols]
            step = 1
            while step < win:
                s = s + pltpu.roll(s, t + HALO_POOL - step, 0)
                step *= 2
            dp_ref[:, cols] = (s[0:t, :] - d_pooled).astype(BF16)

        dzc = dz_ref[:, BW:2 * BW]
        cgate = piece(4)
        yhat, rstd = _layer_norm_stats(y_ref[...])
        ln = yhat * vec_ref[2:3, :] + vec_ref[3:4, :]
        put(4, dzc * _silu(ln) * _dsilu(cgate))
        d_ln = dzc * _silu(cgate) * _dsilu(ln)
        dvec_ref[2:3, :] += _colsum(d_ln * yhat)
        dvec_ref[3:4, :] += _colsum(d_ln)
        dy = _layer_norm_bwd(d_ln * vec_ref[2:3, :], yhat, rstd)
        dvec_ref[1:2, :] += _colsum(dy)
        dyext[0:t, :] = dy
        dye = dyext[...]
        ca = piece(2)
        sig_cb = _sigmoid(piece(3))
        u = ca * sig_cb
        du = jnp.zeros((t, BW), F32)
        for j in range(CONV_K):
            sh = (dye if j == 0 else pltpu.roll(dye, t + HALO_CONV - j, 0))[0:t, :]
            du = du + convw_ref[CONV_K - 1 - j:CONV_K - j, :] * sh
            dconvw_ref[CONV_K - 1 - j:CONV_K - j, :] += _colsum(u * sh)
        put(2, du * sig_cb)
        put(3, du * u * (1.0 - sig_cb))

        dzg = dz_ref[:, 2 * BW:3 * BW]
        gu = piece(5)
        ggate = piece(7)
        vhat, vrstd = _layer_norm_stats(piece(6))
        vn = (vhat * vec_ref[4:5, :] + vec_ref[5:6, :]).astype(BF16)
        t1 = dzg * _silu(ggate)
        d_sp_all = t1 * gu
        d_vn_cols = []
        sp_cols = []
        for g in range(NG):
            ws = _tril_bf16(sguw_ref[g])
            cols = slice(g * GW, (g + 1) * GW)
            d_vn_rows = []
            sp_rows = []
            for c in range(t // GW):
                rows = slice(c * GW, (c + 1) * GW)
                sp_rows.append(_dot(ws, vn[rows, cols]) + sgub_ref[g])
                d_sp = d_sp_all[rows, cols]
                bacc[g] += d_sp
                d_spb = d_sp.astype(BF16)
                d_vn_rows.append(_dot_tn(ws, d_spb))
                dsguw_ref[g] += _dot_nt(d_spb, vn[rows, cols])
            d_vn_cols.append(jnp.concatenate(d_vn_rows, axis=0))
            sp_cols.append(jnp.concatenate(sp_rows, axis=0))
        d_vn = jnp.concatenate(d_vn_cols, axis=1)
        sp = jnp.concatenate(sp_cols, axis=1)
        put(5, t1 * sp)
        put(7, dzg * gu * sp * _dsilu(ggate))
        dvec_ref[4:5, :] += _colsum(d_vn * vhat)
        dvec_ref[5:6, :] += _colsum(d_vn)
        put(6, _layer_norm_bwd(d_vn * vec_ref[4:5, :], vhat, vrstd))

        dzs = dz_ref[:, 3 * BW:]
        sb = piece(8)
        scp = piece(9)
        sx = piece(10)
        sgate = piece(11)
        q = q_ref[...]
        t2 = dzs * _silu(sgate)
        put(8, t2 * q)
        put(11, dzs * sb * q * _dsilu(sgate))
        dqext[0:t, :] = t2 * sb
        dqe = dqext[...]
        vv = scp * sx
        dv = jnp.zeros((t, BW), F32)
        for j in range(SHORT_K):
            sh = (dqe if j == 0 else pltpu.roll(dqe, t + HALO_SC - j, 0))[0:t, :]
            dv = dv + scw_ref[SHORT_K - 1 - j:SHORT_K - j, :] * sh
            dscw_ref[SHORT_K - 1 - j:SHORT_K - j, :] += _colsum(vv * sh)
        put(9, dv * sx)
        put(10, dv * scp)

        @pl.when(last)
        def _():
            r = lax.broadcasted_iota(jnp.int32, (GW, GW), 0)
            c = lax.broadcasted_iota(jnp.int32, (GW, GW), 1)
            for g in range(NG):
                dsguw_ref[g] = jnp.where(r >= c, dsguw_ref[g], 0.0)
                dsgub_ref[g:g + 1, :] = _colsum(bacc[g].T)

    row = lambda b, i: (b * nt + (nt - 1 - i), 0)
    full2 = lambda b, i: (0, 0)
    full3 = lambda b, i: (0, 0, 0)
    return pl.pallas_call(
        body, name="mixers_bwd", grid=(nb, nt),
        in_specs=[pl.BlockSpec((t, N_PIECE_COLS), row),
                  pl.BlockSpec((t, 4 * BW), row),
                  pl.BlockSpec((t, BW), row), pl.BlockSpec((t, BW), row), pl.BlockSpec((t, BW), row),
                  pl.BlockSpec(memory_space=pl.ANY),
                  pl.BlockSpec((NG, GW, GW), full3),
                  pl.BlockSpec((8, BW), full2),
                  pl.BlockSpec((32, BW), full2),
                  pl.BlockSpec((NG, GW, GW), full3),
                  pl.BlockSpec((NG, GW, 1), full3),
                  pl.BlockSpec((8, BW), full2)],
        out_specs=[pl.BlockSpec((t, N_PIECE_COLS), row),
                   pl.BlockSpec((8, BW), full2), pl.BlockSpec((32, BW), full2), pl.BlockSpec((8, BW), full2),
                   pl.BlockSpec((NG, GW, GW), full3), pl.BlockSpec((NG, GW, GW), full3),
                   pl.BlockSpec((8, GW), full2)],
        out_shape=[SDS((n, IN_COLS), BF16), SDS((8, BW), F32), SDS((32, BW), F32), SDS((8, BW), F32),
                   SDS((NG, GW, GW), F32), SDS((NG, GW, GW), F32), SDS((8, GW), F32)],
        scratch_shapes=[pltpu.VMEM((t + HALO_POOL, BW), F32), pltpu.VMEM((t + HALO_CONV, BW), F32),
                        pltpu.VMEM((t + HALO_SC, BW), F32), pltpu.VMEM((NG, GW, GW), F32)],
        input_output_aliases={5: 0},
        compiler_params=_params("arbitrary", "arbitrary"),
    )(proj, dz, y_conv, q_sc, pooled, d_proj, pool_w, vecs, conv_w, sgu_w, sgu_bc, sc_w)


def _in_proj_bwd(d_proj, w_g, x2d, g_row, dxo, tm):
    n = x2d.shape[0]

    def body(dp_ref, w_ref, x_ref, g_ref, dxo_ref, dx_ref, dg_ref, acc):
        k = pl.program_id(1)

        @pl.when(jnp.logical_and(pl.program_id(0) == 0, k == 0))
        def _():
            dg_ref[...] = jnp.zeros_like(dg_ref)

        @pl.when(k == 0)
        def _():
            acc[...] = jnp.zeros_like(acc)

        acc[...] += _dot_nt(dp_ref[...], w_ref[...])

        @pl.when(k == NDEV - 1)
        def _():
            dh = acc[...]
            x = x_ref[...]
            r = lax.rsqrt(jnp.mean(x * x, axis=-1, keepdims=True) + RMS_EPS)
            xn = x * r
            dg_ref[...] += _colsum(dh * xn)
            gy = dh * g_ref[...]
            dx_ref[...] = dxo_ref[...] + r * gy - xn * (r * jnp.mean(gy * xn, axis=-1, keepdims=True))

    return pl.pallas_call(
        body, name="in_proj_bwd", grid=(n // tm, NDEV),
        in_specs=[pl.BlockSpec((tm, SHARD), lambda i, k: (i, k)),
                  pl.BlockSpec((None, D, SHARD), lambda i, k: (k, 0, 0)),
                  pl.BlockSpec((tm, D), lambda i, k: (i, 0)),
                  pl.BlockSpec((1, D), lambda i, k: (0, 0)),
                  pl.BlockSpec((tm, D), lambda i, k: (i, 0))],
        out_specs=[pl.BlockSpec((tm, D), lambda i, k: (i, 0)), pl.BlockSpec((1, D), lambda i, k: (0, 0))],
        out_shape=[SDS((n, D), F32), SDS((1, D), F32)],
        scratch_shapes=[pltpu.VMEM((tm, D), F32)],
        compiler_params=_params("arbitrary", "arbitrary"),
    )(d_proj, w_g, x2d, g_row, dxo)


def _position():
    return lax.axis_index("x"), lax.axis_index("y"), lax.axis_index("c")


def _all_gather(arrs, name):
    na = len(arrs)

    def body(*refs):
        ins, outs = refs[:na], refs[na:2 * na]
        send_sems, recv_sems, local_sems = refs[2 * na:]
        x, y, c = _position()
        me, sibling = (x, y, c), (x, y, 1 - c)
        chips = [(1 - x, y), (x, 1 - y), (1 - x, 1 - y)]

        def copy(a, k, block, to, from_input=False):
            px, py, pc = block
            dst = outs[a].at[4 * px + 2 * py + pc]
            return pltpu.make_async_remote_copy(
                src_ref=ins[a] if from_input else dst, dst_ref=dst,
                send_sem=send_sems.at[7 * a + k], recv_sem=recv_sems.at[7 * a + k],
                device_id=to, device_id_type=MESH)

        mine = [pltpu.make_async_copy(ins[a], outs[a].at[4 * x + 2 * y + c], local_sems.at[a]) for a in range(na)]
        for cp in mine:
            cp.start()
        first = []
        for a in range(na):
            first.append(copy(a, 0, me, sibling, from_input=True))
            first += [copy(a, 1 + j, me, (*chip, c), from_input=True) for j, chip in enumerate(chips)]
        for cp in first:
            cp.start()
        passed = []
        for j, chip in enumerate(chips):
            for a in range(na):
                copy(a, 1 + j, (*chip, c), me).wait_recv()
                fwd = copy(a, 4 + j, (*chip, c), sibling)
                fwd.start()
                passed.append(fwd)
        for a in range(na):
            copy(a, 0, sibling, me).wait_recv()
            for j, chip in enumerate(chips):
                copy(a, 4 + j, (*chip, 1 - c), me).wait_recv()
        for cp in first + passed:
            cp.wait_send()
        for cp in mine:
            cp.wait()

    any_spec = pl.BlockSpec(memory_space=pl.ANY)
    return pl.pallas_call(
        body, name=name,
        in_specs=[any_spec] * na, out_specs=[any_spec] * na,
        out_shape=[SDS((NDEV,) + a.shape, a.dtype) for a in arrs],
        scratch_shapes=[pltpu.SemaphoreType.DMA((7 * na,)), pltpu.SemaphoreType.DMA((7 * na,)),
                        pltpu.SemaphoreType.DMA((na,))],
    )(*arrs)


def _exchange_sibling(arrs):
    na = len(arrs)

    def body(*refs):
        ins, outs = refs[:na], refs[na:2 * na]
        send_sems, recv_sems = refs[2 * na:]
        x, y, c = _position()
        chip = 2 * x + y
        copies = []
        for a in range(na):
            for r in range(4):
                src = ins[a].at[2 * jnp.bitwise_xor(chip, r) + (1 - c)]
                copies.append(pltpu.make_async_remote_copy(
                    src_ref=src, dst_ref=outs[a].at[r],
                    send_sem=send_sems.at[4 * a + r], recv_sem=recv_sems.at[4 * a + r],
                    device_id=(x, y, 1 - c), device_id_type=MESH))
        for cp in copies:
            cp.start()
        for cp in copies:
            cp.wait()

    any_spec = pl.BlockSpec(memory_space=pl.ANY)
    return pl.pallas_call(
        body, name="grad_exchange_sibling",
        in_specs=[any_spec] * na, out_specs=[any_spec] * na,
        out_shape=[SDS((4,) + a.shape[1:], a.dtype) for a in arrs],
        scratch_shapes=[pltpu.SemaphoreType.DMA((4 * na,)), pltpu.SemaphoreType.DMA((4 * na,))],
    )(*arrs)


def _add_sibling(table, part, landed, tr):
    _, rows, cols = part.shape

    def body(tbl_ref, p_ref, l_ref, o_ref):
        o_ref[...] = p_ref[...] + l_ref[...]

    return pl.pallas_call(
        body, name="grad_add_sibling",
        grid_spec=pltpu.PrefetchScalarGridSpec(
            num_scalar_prefetch=1, grid=(4, rows // tr),
            in_specs=[pl.BlockSpec((None, tr, cols), lambda r, i, tbl: (tbl[r], i, 0)),
                      pl.BlockSpec((None, tr, cols), lambda r, i, tbl: (r, i, 0))],
            out_specs=pl.BlockSpec((None, tr, cols), lambda r, i, tbl: (r, i, 0))),
        out_shape=SDS((4, rows, cols), part.dtype),
        compiler_params=_params("arbitrary", "arbitrary"),
    )(table, part, landed)


def _exchange_chips(arrs):
    na = len(arrs)

    def body(*refs):
        ins, outs = refs[:na], refs[na:2 * na]
        send_sems, recv_sems = refs[2 * na:]
        x, y, c = _position()
        peers = [(x, 1 - y, c), (1 - x, y, c), (1 - x, 1 - y, c)]
        copies = []
        for a in range(na):
            for r, peer in enumerate(peers):
                copies.append(pltpu.make_async_remote_copy(
                    src_ref=ins[a].at[r + 1], dst_ref=outs[a].at[r],
                    send_sem=send_sems.at[3 * a + r], recv_sem=recv_sems.at[3 * a + r],
                    device_id=peer, device_id_type=MESH))
        for cp in copies:
            cp.start()
        for cp in copies:
            cp.wait()

    any_spec = pl.BlockSpec(memory_space=pl.ANY)
    return pl.pallas_call(
        body, name="grad_exchange_chips",
        in_specs=[any_spec] * na, out_specs=[any_spec] * na,
        out_shape=[SDS((3,) + a.shape[1:], a.dtype) for a in arrs],
        scratch_shapes=[pltpu.SemaphoreType.DMA((3 * na,)), pltpu.SemaphoreType.DMA((3 * na,))],
    )(*arrs)


def _adamw_math(g, w, m, v):
    m = ADAM_B1 * m + (1.0 - ADAM_B1) * g
    v = ADAM_B2 * v + (1.0 - ADAM_B2) * (g * g)
    m_hat = m / (1.0 - ADAM_B1 ** ADAM_STEP)
    v_hat = v / (1.0 - ADAM_B2 ** ADAM_STEP)
    delta = -ADAM_LR * (m_hat / (jnp.sqrt(v_hat) + ADAM_EPS) + ADAM_WD * w)
    return delta, m, v


def _adamw_sharded(q, landed, w, m, v, tr, name):
    rows, cols = w.shape

    def body(q_ref, l_ref, w_ref, m_ref, v_ref, g_out, d_out, m_out, v_out):
        g = q_ref[...] + l_ref[0] + l_ref[1] + l_ref[2]
        g_out[...] = g
        d_out[...], m_out[...], v_out[...] = _adamw_math(g, w_ref[...], m_ref[...], v_ref[...])

    blk = pl.BlockSpec((tr, cols), lambda i: (i, 0))
    return pl.pallas_call(
        body, name=name, grid=(rows // tr,),
        in_specs=[pl.BlockSpec((None, tr, cols), lambda i: (0, i, 0)),
                  pl.BlockSpec((3, tr, cols), lambda i: (0, i, 0)), blk, blk, blk],
        out_specs=[blk] * 4, out_shape=[SDS((rows, cols), F32)] * 4,
        compiler_params=_params("parallel"),
    )(q, landed, w, m, v)


def _adamw_gathered(parts, w, m, v, tr):
    rows, cols = w.shape

    def body(p_ref, w_ref, m_ref, v_ref, g_out, d_out, m_out, v_out):
        g = p_ref[0]
        for k in range(1, NDEV):
            g = g + p_ref[k]
        g_out[...] = g
        d_out[...], m_out[...], v_out[...] = _adamw_math(g, w_ref[...], m_ref[...], v_ref[...])

    blk = pl.BlockSpec((tr, cols), lambda i: (i, 0))
    return pl.pallas_call(
        body, name="adamw_small", grid=(rows // tr,),
        in_specs=[pl.BlockSpec((NDEV, tr, cols), lambda i: (0, i, 0)), blk, blk, blk],
        out_specs=[blk] * 4, out_shape=[SDS((rows, cols), F32)] * 4,
        compiler_params=_params("parallel"),
    )(parts, w, m, v)


def _adamw_plain(g, w, m, v):
    def body(g_ref, w_ref, m_ref, v_ref, d_out, m_out, v_out):
        d_out[...], m_out[...], v_out[...] = _adamw_math(g_ref[...], w_ref[...], m_ref[...], v_ref[...])

    return pl.pallas_call(body, name="adamw_conv_shards", out_shape=[SDS(w.shape, F32)] * 3)(g, w, m, v)


SMALL = ("norm_g", "pool_w", "pool_scale", "conv_w", "conv_b", "conv_ln_g", "conv_ln_b",
         "sgu_ln_g", "sgu_ln_b", "sgu_w", "sgu_b", "sc_w")
SMALL_SHAPES = {"norm_g": (D,), "pool_w": (NG, GW, GW), "pool_scale": (BW,), "conv_w": (CONV_K, BW), "conv_b": (BW,),
                "conv_ln_g": (BW,), "conv_ln_b": (BW,), "sgu_ln_g": (BW,), "sgu_ln_b": (BW,),
                "sgu_w": (NG, GW, GW), "sgu_b": (NG, GW), "sc_w": (SHORT_K, BW)}
REPLICATED = tuple(k for k in SMALL if k not in ("conv_w", "sc_w"))
PACK_UNIT = 8 * 128


def _size(shape):
    out = 1
    for s in shape:
        out *= s
    return out


def _padded(a):
    flat = a.reshape(-1)
    pad = -flat.shape[0] % PACK_UNIT
    return jnp.pad(flat, (0, pad)) if pad else flat


def _pack(arrays):
    return jnp.concatenate([_padded(a) for a in arrays]).reshape(-1, 128)


def _unpack(pack, shapes):
    flat = pack.reshape(-1)
    out, off = [], 0
    for shape in shapes:
        size = _size(shape)
        out.append(flat[off:off + size].reshape(shape))
        off += size + (-size % PACK_UNIT)
    return out


def _layer_small_inputs(l, full):
    vecs = jnp.stack([full["pool_scale"][l], full["conv_b"][l], full["conv_ln_g"][l], full["conv_ln_b"][l],
                      full["sgu_ln_g"][l], full["sgu_ln_b"][l], jnp.zeros((BW,), F32), jnp.zeros((BW,), F32)])
    conv_w = jnp.pad(full["conv_w"][l], ((0, 32 - CONV_K), (0, 0)))
    sc_w = jnp.pad(full["sc_w"][l], ((0, 8 - SHORT_K), (0, 0)))
    return (full["pool_w"][l], vecs, conv_w, full["sgu_w"][l], full["sgu_b"][l].reshape(NG, GW, 1), sc_w)


def _local_step(x, target, norm_g, final_g, small_full, w_in_g, wb_g, wo_g, t_fwd, t_bwd):
    nb, seq, _ = x.shape
    n = nb * seq
    x2d = x.reshape(n, D)
    tm = min(512, n)
    tm2 = min(256, n)
    saved = []
    for l in range(DEPTH):
        small = _layer_small_inputs(l, small_full)
        proj, h = _in_proj(x2d, norm_g[l:l + 1], w_in_g[l], tm)
        z, y_conv, q_sc, pooled = _mixers_fwd(proj, *small, nb, seq, t_fwd)
        x_new, merged = _merge_out(z, proj, x2d, wb_g[l], wo_g[l], tm2)
        saved.append((x2d, proj, h, z, y_conv, q_sc, pooled, merged, small))
        x2d = x_new

    dx, loss_acc, d_final_g = _loss_head(x2d, target.reshape(n, D), final_g.reshape(1, D), tm)

    grads = {k: [None] * DEPTH for k in SMALL}
    d_w_in, d_wb, d_wo = [None] * DEPTH, [None] * DEPTH, [None] * DEPTH
    for l in reversed(range(DEPTH)):
        x_in, proj, h, z, y_conv, q_sc, pooled, merged, small = saved[l]
        dz, d_bo, d_proj = _merge_out_bwd(dx, z, proj, wb_g[l], wo_g[l], tm2)
        d_wo[l] = _matmul_tn(merged, dx, 1, D, D, False, tm, "grad_w_o")[0]
        d_wb[l] = _matmul_tn(z, d_bo, 4, BW, D, True, tm, "grad_w_branch")
        d_proj, dvec, dconvw, dscw, dpoolw, dsguw, dsgub = _mixers_bwd(
            proj, dz, y_conv, q_sc, pooled, d_proj, *small, nb, seq, t_bwd)
        d_w_in[l] = _matmul_tn(h, d_proj, NDEV, D, SHARD, False, tm, "grad_w_in")
        dx, d_norm_g = _in_proj_bwd(d_proj, w_in_g[l], x_in, norm_g[l:l + 1], dx, tm)
        grads["norm_g"][l] = d_norm_g[0]
        grads["pool_w"][l] = dpoolw
        grads["pool_scale"][l] = dvec[0]
        grads["conv_w"][l] = dconvw[:CONV_K]
        grads["conv_b"][l] = dvec[1]
        grads["conv_ln_g"][l] = dvec[2]
        grads["conv_ln_b"][l] = dvec[3]
        grads["sgu_ln_g"][l] = dvec[4]
        grads["sgu_ln_b"][l] = dvec[5]
        grads["sgu_w"][l] = dsguw
        grads["sgu_b"][l] = dsgub[:NG]
        grads["sc_w"][l] = dscw[:SHORT_K]
    small_grads = {k: jnp.stack(v) for k, v in grads.items()}
    return loss_acc[0, 0], dx.reshape(nb, seq, D), small_grads, d_final_g[0], d_w_in, d_wb, d_wo


def kernel(x, norm_g, w_in, pool_w, pool_scale, conv_w, conv_b, conv_ln_g, conv_ln_b, sgu_ln_g, sgu_ln_b, sgu_w, sgu_b, sc_w, w_branch, w_o, final_g, loss_target, m_norm_g, m_w_in, m_pool_w, m_pool_scale, m_conv_w, m_conv_b, m_conv_ln_g, m_conv_ln_b, m_sgu_ln_g, m_sgu_ln_b, m_sgu_w, m_sgu_b, m_sc_w, m_w_branch, m_w_o, m_final_g, v_norm_g, v_w_in, v_pool_w, v_pool_scale, v_conv_w, v_conv_b, v_conv_ln_g, v_conv_ln_b, v_sgu_ln_g, v_sgu_ln_b, v_sgu_w, v_sgu_b, v_sc_w, v_w_branch, v_w_o, v_final_g):
    args = dict(locals())
    w = {k: args[k] for k in SMALL + ("w_in", "w_branch", "w_o", "final_g")}
    mom = {k: args["m_" + k] for k in w}
    var = {k: args["v_" + k] for k in w}
    xi, yi, ci = _position()
    me = 4 * xi + 2 * yi + ci
    cw = CONV_K * (BW // NDEV)
    sw = SHORT_K * (BW // NDEV)

    conv_pack = [_pack([conv_w[l], sc_w[l]]) for l in range(DEPTH)]
    shards = []
    for l in range(DEPTH):
        shards += [w_in[l].astype(BF16), w_branch[l].astype(BF16).reshape(4 * BW, D // NDEV),
                   w_o[l].astype(BF16), conv_pack[l]]
    gathered = _all_gather(shards, "weights_all_gather")
    w_in_g, wb_g, wo_g = [], [], []
    conv_full, sc_full = [], []
    for l in range(DEPTH):
        a, b, c, d = gathered[4 * l:4 * l + 4]
        w_in_g.append(a)
        wb_g.append(b.reshape(NDEV, 4, BW, D // NDEV).transpose(1, 2, 0, 3).reshape(4, BW, D))
        wo_g.append(c.reshape(D, D))
        flat = d.reshape(NDEV, -1)
        sc_at = cw + (-cw % PACK_UNIT)
        conv_full.append(flat[:, :cw].reshape(NDEV, CONV_K, BW // NDEV).transpose(1, 0, 2).reshape(CONV_K, BW))
        sc_full.append(flat[:, sc_at:sc_at + sw].reshape(NDEV, SHORT_K, BW // NDEV).transpose(1, 0, 2).reshape(SHORT_K, BW))
    small_full = {k: w[k] for k in REPLICATED}
    small_full["conv_w"] = jnp.stack(conv_full)
    small_full["sc_w"] = jnp.stack(sc_full)

    seq = x.shape[1]
    loss_part, grad_x, small_grads, d_final_g, d_w_in, d_wb, d_wo = _local_step(
        x, loss_target, norm_g, final_g, small_full, w_in_g, wb_g, wo_g, min(256, seq), min(128, seq))
    loss = lax.psum(loss_part, ("x", "y", "c"))

    names = list(SMALL) + ["final_g"]
    shapes = [(DEPTH,) + SMALL_SHAPES[k] for k in SMALL] + [(D,)]
    part_pack = _pack([small_grads[k] for k in SMALL] + [d_final_g])
    w_pack = _pack([w[k] if k in REPLICATED + ("final_g",) else jnp.zeros(s, F32) for k, s in zip(names, shapes)])
    m_pack = _pack([mom[k] if k in REPLICATED + ("final_g",) else jnp.zeros(s, F32) for k, s in zip(names, shapes)])
    v_pack = _pack([var[k] if k in REPLICATED + ("final_g",) else jnp.ones(s, F32) for k, s in zip(names, shapes)])
    (parts,) = _all_gather([part_pack], "small_grads_all_gather")
    g_pack, d_pack, nm_pack, nv_pack = _adamw_gathered(parts, w_pack, m_pack, v_pack, part_pack.shape[0] // 4)
    out_g = dict(zip(names, _unpack(g_pack, shapes)))
    out_d = dict(zip(names, _unpack(d_pack, shapes)))
    out_m = dict(zip(names, _unpack(nm_pack, shapes)))
    out_v = dict(zip(names, _unpack(nv_pack, shapes)))

    col = me * (BW // NDEV)
    for k in ("conv_w", "sc_w"):
        out_g[k] = lax.dynamic_slice_in_dim(out_g[k], col, BW // NDEV, axis=2)
    sharded_shapes = [w["conv_w"].shape, w["sc_w"].shape]
    d2, m2, v2 = _adamw_plain(_pack([out_g["conv_w"], out_g["sc_w"]]), _pack([w["conv_w"], w["sc_w"]]),
                              _pack([mom["conv_w"], mom["sc_w"]]), _pack([var["conv_w"], var["sc_w"]]))
    for dst, pk in ((out_d, d2), (out_m, m2), (out_v, v2)):
        dst["conv_w"], dst["sc_w"] = _unpack(pk, sharded_shapes)

    g_in = jnp.stack(d_w_in, axis=1).reshape(NDEV, DEPTH * D, SHARD)
    g_wb = jnp.stack(d_wb).reshape(DEPTH, 4, BW, NDEV, D // NDEV).transpose(3, 0, 1, 2, 4).reshape(NDEV, DEPTH * 4 * BW, D // NDEV)
    g_wo = jnp.stack(d_wo).reshape(DEPTH, NDEV, D // NDEV, D).transpose(1, 0, 2, 3).reshape(NDEV, DEPTH * D // NDEV, D)
    parts3 = [g_in, g_wb, g_wo]
    landed_sib = _exchange_sibling(parts3)
    chip = 2 * xi + yi
    table = (2 * jnp.bitwise_xor(chip, jnp.arange(4, dtype=jnp.int32)) + ci).astype(jnp.int32)
    q = [_add_sibling(table, p, ls, tr) for p, ls, tr in zip(parts3, landed_sib, (256, 512, 64))]
    landed = _exchange_chips(q)
    big = {}
    for k, qq, ll, tr in zip(("w_in", "w_branch", "w_o"), q, landed, (256, 512, 64)):
        rows, cols = qq.shape[1:]
        outs = _adamw_sharded(qq, ll, w[k].reshape(rows, cols), mom[k].reshape(rows, cols), var[k].reshape(rows, cols),
                              tr, "adamw_" + k)
        big[k] = [o.reshape(w[k].shape) for o in outs]
    for k in big:
        out_g[k], out_d[k], out_m[k], out_v[k] = big[k]

    order = ("norm_g", "w_in", "pool_w", "pool_scale", "conv_w", "conv_b", "conv_ln_g", "conv_ln_b", "sgu_ln_g",
             "sgu_ln_b", "sgu_w", "sgu_b", "sc_w", "w_branch", "w_o", "final_g")
    return (loss, grad_x, *[out_g[k] for k in order], *[out_d[k] for k in order],
            *[out_m[k] for k in order], *[out_v[k] for k in order])
```

```python
import functools

import jax
import jax.numpy as jnp
from jax import lax
from jax.experimental import pallas as pl
from jax.experimental.pallas import tpu as pltpu

F32 = jnp.float32
BF16 = jnp.bfloat16
SDS = jax.ShapeDtypeStruct
MESH = pl.DeviceIdType.MESH

D = 1024
BW = 512
NG = 4
GW = 128
CONV_K = 31
SHORT_K = 3
POOL_WINDOWS = (2, 4, 8, 16)
N_PIECE_COLS = 12 * BW
IN_COLS = N_PIECE_COLS + 4 * D
NDEV = 8
SHARD = IN_COLS // NDEV
DEPTH = 2
RMS_EPS = 1e-6
LN_EPS = 1e-5
HALO_POOL, HALO_CONV, HALO_SC = 16, 32, 8

ADAM_LR = 0.001
ADAM_B1 = 0.9
ADAM_B2 = 0.999
ADAM_EPS = 1e-08
ADAM_WD = 0.01
ADAM_STEP = 10

VMEM_LIMIT = 56 * 1024 * 1024


def _params(*sem):
    return pltpu.CompilerParams(dimension_semantics=sem, vmem_limit_bytes=VMEM_LIMIT)


def _sigmoid(x):
    return jax.nn.sigmoid(x)


def _silu(x):
    return x * _sigmoid(x)


def _dsilu(x):
    s = _sigmoid(x)
    return s * (1.0 + x * (1.0 - s))


def _colsum(x):
    return jnp.sum(x, axis=0, keepdims=True)


def _dot(a, b):
    return jnp.dot(a, b, preferred_element_type=F32)


def _dot_nt(a, b):
    return lax.dot_general(a, b, (((1,), (1,)), ((), ())), preferred_element_type=F32)


def _dot_tn(a, b):
    return lax.dot_general(a, b, (((0,), (0,)), ((), ())), preferred_element_type=F32)


def _layer_norm_stats(y):
    mu = jnp.mean(y, axis=-1, keepdims=True)
    yc = y - mu
    var = jnp.mean(yc * yc, axis=-1, keepdims=True)
    rstd = lax.rsqrt(var + LN_EPS)
    return yc * rstd, rstd


def _layer_norm_bwd(d_hat, hat, rstd):
    return rstd * (d_hat - jnp.mean(d_hat, axis=-1, keepdims=True)
                   - hat * jnp.mean(d_hat * hat, axis=-1, keepdims=True))


def _position():
    return lax.axis_index("x"), lax.axis_index("y"), lax.axis_index("c")


def _flip(v, bit):
    return 1 - v if bit else v


class _Exchange:
    def __init__(self, gathers=(), scatters=()):
        self.gathers = list(gathers)
        self.scatters = list(scatters)
        self.inputs = self.gathers + [a for pair in self.scatters for a in pair]
        self.out_shapes = [SDS((NDEV,) + a.shape, a.dtype) for a in self.gathers]
        for send, keep in self.scatters:
            self.out_shapes += [SDS((NDEV - 1,) + send.shape[1:], send.dtype), SDS(keep.shape[1:], keep.dtype)]
        n = len(self.gathers) + len(self.scatters)
        self.scratch = [pltpu.SemaphoreType.DMA((7 * n,)), pltpu.SemaphoreType.DMA((7 * n,)),
                        pltpu.SemaphoreType.DMA((n,))]

    def _copies(self, ins, outs, sems):
        send_sems, recv_sems, local_sems = sems
        x, y, c = _position()
        me = 4 * x + 2 * y + c
        ng = len(self.gathers)
        local, remote = [], []
        for a in range(ng + len(self.scatters)):
            if a < ng:
                local.append(pltpu.make_async_copy(ins[a], outs[a].at[me], local_sems.at[a]))
            else:
                s = a - ng
                send, keep = ins[ng + 2 * s], ins[ng + 2 * s + 1]
                landed, own = outs[ng + 2 * s], outs[ng + 2 * s + 1]
                local.append(pltpu.make_async_copy(keep.at[me], own, local_sems.at[a]))
            for r in range(1, NDEV):
                peer = (_flip(x, r & 4), _flip(y, r & 2), _flip(c, r & 1))
                if a < ng:
                    src, dst = ins[a], outs[a].at[me]
                else:
                    src, dst = send.at[jnp.bitwise_xor(me, r)], landed.at[r - 1]
                remote.append(pltpu.make_async_remote_copy(
                    src_ref=src, dst_ref=dst, send_sem=send_sems.at[7 * a + r - 1],
                    recv_sem=recv_sems.at[7 * a + r - 1], device_id=peer, device_id_type=MESH))
        return local, remote

    def start(self, ins, outs, sems):
        local, remote = self._copies(ins, outs, sems)
        for cp in remote + local:
            cp.start()

    def finish(self, ins, outs, sems):
        local, remote = self._copies(ins, outs, sems)
        for cp in remote + local:
            cp.wait()


def _carried_call(body, comm, args, *, name, grid, in_specs, out_specs, out_shape, scratch_shapes=(),
                  semantics, input_output_aliases=None):
    aliases = input_output_aliases or {}
    if comm is None:
        out = pl.pallas_call(body, name=name, grid=grid, in_specs=in_specs, out_specs=out_specs, out_shape=out_shape,
                             scratch_shapes=list(scratch_shapes), input_output_aliases=aliases,
                             compiler_params=_params(*semantics))(*args)
        return list(out), []
    n_in, n_out, n_scr = len(in_specs), len(out_specs), len(scratch_shapes)
    c_in, c_out = len(comm.inputs), len(comm.out_shapes)

    def full_body(*refs):
        refs = list(refs)
        ins, refs = refs[:n_in], refs[n_in:]
        cins, refs = refs[:c_in], refs[c_in:]
        outs, refs = refs[:n_out], refs[n_out:]
        couts, refs = refs[:c_out], refs[c_out:]
        scr, sems = refs[:n_scr], refs[n_scr:]
        ids = [pl.program_id(d) for d in range(len(grid))]
        first = functools.reduce(jnp.logical_and, [i == 0 for i in ids])
        last = functools.reduce(jnp.logical_and, [i == g - 1 for i, g in zip(ids, grid)])

        @pl.when(first)
        def _():
            comm.start(cins, couts, sems)

        body(*ins, *outs, *scr)

        @pl.when(last)
        def _():
            comm.finish(cins, couts, sems)

    any_spec = pl.BlockSpec(memory_space=pl.ANY)
    out = pl.pallas_call(
        full_body, name=name, grid=grid,
        in_specs=list(in_specs) + [any_spec] * c_in, out_specs=list(out_specs) + [any_spec] * c_out,
        out_shape=list(out_shape) + comm.out_shapes, scratch_shapes=list(scratch_shapes) + comm.scratch,
        input_output_aliases=aliases, compiler_params=_params(*["arbitrary"] * len(grid)),
    )(*args, *comm.inputs)
    return list(out[:n_out]), list(out[n_out:])


def _in_proj(x2d, g_row, w_g, tm, comm=None):
    n = x2d.shape[0]

    def body(x_ref, g_ref, w_ref, proj_ref, h_ref):
        @pl.when(pl.program_id(1) == 0)
        def _():
            x = x_ref[...]
            r = lax.rsqrt(jnp.mean(x * x, axis=-1, keepdims=True) + RMS_EPS)
            h_ref[...] = (x * r * g_ref[...]).astype(BF16)

        proj_ref[...] = _dot(h_ref[...], w_ref[...])

    return _carried_call(
        body, comm, (x2d, g_row, w_g), name="in_proj", grid=(n // tm, NDEV),
        in_specs=[pl.BlockSpec((tm, D), lambda i, j: (i, 0)),
                  pl.BlockSpec((1, D), lambda i, j: (0, 0)),
                  pl.BlockSpec((None, D, SHARD), lambda i, j: (j, 0, 0))],
        out_specs=[pl.BlockSpec((tm, SHARD), lambda i, j: (i, j)),
                   pl.BlockSpec((tm, D), lambda i, j: (i, 0))],
        out_shape=[SDS((n, IN_COLS), F32), SDS((n, D), BF16)],
        semantics=("parallel", "arbitrary"))


def _tril_bf16(w):
    r = lax.broadcasted_iota(jnp.int32, (GW, GW), 0)
    c = lax.broadcasted_iota(jnp.int32, (GW, GW), 1)
    return jnp.where(r >= c, w, 0.0).astype(BF16)


def _mixers_fwd(proj, pool_w, vecs, conv_w, sgu_w, sgu_bc, sc_w, nb, seq, t, comm=None):
    n = nb * seq
    nt = seq // t

    def body(p_ref, poolw_ref, vec_ref, convw_ref, sguw_ref, sgub_ref, scw_ref,
             z_ref, y_ref, q_ref, pooled_ref, pext, uext, vext):
        i = pl.program_id(1)

        @pl.when(i == 0)
        def _():
            pext[0:HALO_POOL, :] = jnp.zeros((HALO_POOL, BW), F32)
            uext[0:HALO_CONV, :] = jnp.zeros((HALO_CONV, BW), F32)
            vext[0:HALO_SC, :] = jnp.zeros((HALO_SC, BW), F32)

        @pl.when(i > 0)
        def _():
            pext[0:HALO_POOL, :] = pext[t:t + HALO_POOL, :]
            uext[0:HALO_CONV, :] = uext[t:t + HALO_CONV, :]
            vext[0:HALO_SC, :] = vext[t:t + HALO_SC, :]

        def piece(k):
            return p_ref[:, k * BW:(k + 1) * BW]

        pos = lax.broadcasted_iota(jnp.int32, (t, 1), 0) + i * t

        px = piece(0)
        pext[HALO_POOL:, :] = px
        for j, win in enumerate(POOL_WINDOWS):
            cols = slice(j * GW, (j + 1) * GW)
            s = pext[:, cols]
            step = 1
            while step < win:
                s = s + pltpu.roll(s, step, 0)
                step *= 2
            cnt = jnp.minimum(pos + 1, win).astype(F32)
            pooled = (s[HALO_POOL:, :] / cnt - px[:, cols]).astype(BF16)
            pooled_ref[:, cols] = pooled
            mixed = _dot(pooled, poolw_ref[j].astype(BF16))
            z_ref[:, cols] = (mixed * vec_ref[0:1, cols] * _silu(p_ref[:, BW + j * GW:BW + (j + 1) * GW])).astype(BF16)

        uext[HALO_CONV:, :] = piece(2) * _sigmoid(piece(3))
        ue = uext[...]
        acc = jnp.zeros((t, BW), F32)
        for j in range(CONV_K):
            sh = ue if j == 0 else pltpu.roll(ue, j, 0)
            acc = acc + convw_ref[CONV_K - 1 - j:CONV_K - j, :] * sh[HALO_CONV:, :]
        y = acc + vec_ref[1:2, :]
        y_ref[...] = y
        yhat, _ = _layer_norm_stats(y)
        act = _silu(yhat * vec_ref[2:3, :] + vec_ref[3:4, :])
        z_ref[:, BW:2 * BW] = (act * _silu(piece(4))).astype(BF16)

        vhat, _ = _layer_norm_stats(piece(6))
        vn = (vhat * vec_ref[4:5, :] + vec_ref[5:6, :]).astype(BF16)
        for g in range(NG):
            ws = _tril_bf16(sguw_ref[g])
            cols = slice(g * GW, (g + 1) * GW)
            for c in range(t // GW):
                rows = slice(c * GW, (c + 1) * GW)
                sp = _dot(ws, vn[rows, cols]) + sgub_ref[g]
                gate = _silu(p_ref[rows, 7 * BW + g * GW:7 * BW + (g + 1) * GW])
                z_ref[rows, 2 * BW + g * GW:2 * BW + (g + 1) * GW] = (
                    p_ref[rows, 5 * BW + g * GW:5 * BW + (g + 1) * GW] * sp * gate).astype(BF16)

        vext[HALO_SC:, :] = piece(9) * piece(10)
        ve = vext[...]
        q = jnp.zeros((t, BW), F32)
        for j in range(SHORT_K):
            sh = ve if j == 0 else pltpu.roll(ve, j, 0)
            q = q + scw_ref[SHORT_K - 1 - j:SHORT_K - j, :] * sh[HALO_SC:, :]
        q_ref[...] = q
        z_ref[:, 3 * BW:] = (piece(8) * q * _silu(piece(11))).astype(BF16)

    row = lambda b, i: (b * nt + i, 0)
    full2 = lambda b, i: (0, 0)
    full3 = lambda b, i: (0, 0, 0)
    return _carried_call(
        body, comm, (proj, pool_w, vecs, conv_w, sgu_w, sgu_bc, sc_w), name="mixers_fwd", grid=(nb, nt),
        in_specs=[pl.BlockSpec((t, N_PIECE_COLS), row),
                  pl.BlockSpec((NG, GW, GW), full3),
                  pl.BlockSpec((8, BW), full2),
                  pl.BlockSpec((32, BW), full2),
                  pl.BlockSpec((NG, GW, GW), full3),
                  pl.BlockSpec((NG, GW, 1), full3),
                  pl.BlockSpec((8, BW), full2)],
        out_specs=[pl.BlockSpec((t, 4 * BW), row), pl.BlockSpec((t, BW), row),
                   pl.BlockSpec((t, BW), row), pl.BlockSpec((t, BW), row)],
        out_shape=[SDS((n, 4 * BW), BF16), SDS((n, BW), F32), SDS((n, BW), F32), SDS((n, BW), BF16)],
        scratch_shapes=[pltpu.VMEM((t + HALO_POOL, BW), F32), pltpu.VMEM((t + HALO_CONV, BW), F32),
                        pltpu.VMEM((t + HALO_SC, BW), F32)],
        semantics=("arbitrary", "arbitrary"))


def _merge_out(z, proj, x2d, wb, wo, tm, comm=None):
    n = x2d.shape[0]

    def body(z_ref, mga_ref, mgb_ref, x_ref, wb_ref, wo_ref, xo_ref, m_ref):
        merged = jnp.zeros((tm, D), F32)
        for k in range(4):
            bo = _dot(z_ref[:, k * BW:(k + 1) * BW], wb_ref[k])
            mg_ref = mga_ref if k < 2 else mgb_ref
            merged = merged + _sigmoid(mg_ref[:, (k % 2) * D:(k % 2 + 1) * D]) * bo
        mb = merged.astype(BF16)
        m_ref[...] = mb
        xo_ref[...] = x_ref[...] + _dot(mb, wo_ref[...])

    return _carried_call(
        body, comm, (z, proj, proj, x2d, wb, wo), name="merge_out", grid=(n // tm,),
        in_specs=[pl.BlockSpec((tm, 4 * BW), lambda i: (i, 0)),
                  pl.BlockSpec((tm, 2 * D), lambda i: (i, 3)),
                  pl.BlockSpec((tm, 2 * D), lambda i: (i, 4)),
                  pl.BlockSpec((tm, D), lambda i: (i, 0)),
                  pl.BlockSpec((4, BW, D), lambda i: (0, 0, 0)),
                  pl.BlockSpec((D, D), lambda i: (0, 0))],
        out_specs=[pl.BlockSpec((tm, D), lambda i: (i, 0)), pl.BlockSpec((tm, D), lambda i: (i, 0))],
        out_shape=[SDS((n, D), F32), SDS((n, D), BF16)], semantics=("parallel",))


def _loss_head(x2d, tgt2d, g_row, tm):
    n = x2d.shape[0]

    def body(x_ref, t_ref, g_ref, dx_ref, loss_ref, dg_ref):
        @pl.when(pl.program_id(0) == 0)
        def _():
            loss_ref[...] = jnp.zeros_like(loss_ref)
            dg_ref[...] = jnp.zeros_like(dg_ref)

        x = x_ref[...]
        g = g_ref[...]
        r = lax.rsqrt(jnp.mean(x * x, axis=-1, keepdims=True) + RMS_EPS)
        xn = x * r
        e = xn * g - t_ref[...]
        loss_ref[...] += 0.5 * jnp.sum(jnp.mean(e * e, axis=-1, keepdims=True))
        dy = e * (1.0 / D)
        dg_ref[...] += _colsum(dy * xn)
        gy = dy * g
        dx_ref[...] = r * gy - xn * (r * jnp.mean(gy * xn, axis=-1, keepdims=True))

    return pl.pallas_call(
        body, name="loss_head", grid=(n // tm,),
        in_specs=[pl.BlockSpec((tm, D), lambda i: (i, 0)), pl.BlockSpec((tm, D), lambda i: (i, 0)),
                  pl.BlockSpec((1, D), lambda i: (0, 0))],
        out_specs=[pl.BlockSpec((tm, D), lambda i: (i, 0)), pl.BlockSpec((8, 128), lambda i: (0, 0)),
                   pl.BlockSpec((1, D), lambda i: (0, 0))],
        out_shape=[SDS((n, D), F32), SDS((8, 128), F32), SDS((1, D), F32)],
        compiler_params=_params("arbitrary"),
    )(x2d, tgt2d, g_row)


def _merge_out_bwd(dxo, z, proj, wb, wo, tm):
    n = dxo.shape[0]

    def body(dx_ref, z_ref, mg_ref, wb_ref, wo_ref, dz_ref, dbo_ref, dp_ref, dm_ref):
        @pl.when(pl.program_id(1) == 0)
        def _():
            dm_ref[...] = _dot_nt(dx_ref[...].astype(BF16), wo_ref[...])

        dm = dm_ref[...]
        for k in range(2):
            bo = _dot(z_ref[:, k * BW:(k + 1) * BW], wb_ref[k])
            sig = _sigmoid(mg_ref[:, k * D:(k + 1) * D])
            dbo = (dm * sig).astype(BF16)
            dbo_ref[:, k * D:(k + 1) * D] = dbo
            dp_ref[:, k * D:(k + 1) * D] = (dm * bo * sig * (1.0 - sig)).astype(BF16)
            dz_ref[:, k * BW:(k + 1) * BW] = _dot_nt(dbo, wb_ref[k])

    return pl.pallas_call(
        body, name="merge_out_bwd", grid=(n // tm, 2),
        in_specs=[pl.BlockSpec((tm, D), lambda i, h: (i, 0)),
                  pl.BlockSpec((tm, 2 * BW), lambda i, h: (i, h)),
                  pl.BlockSpec((tm, 2 * D), lambda i, h: (i, 3 + h)),
                  pl.BlockSpec((2, BW, D), lambda i, h: (h, 0, 0)),
                  pl.BlockSpec((D, D), lambda i, h: (0, 0))],
        out_specs=[pl.BlockSpec((tm, 2 * BW), lambda i, h: (i, h)),
                   pl.BlockSpec((tm, 2 * D), lambda i, h: (i, h)),
                   pl.BlockSpec((tm, 2 * D), lambda i, h: (i, 3 + h))],
        out_shape=[SDS((n, 4 * BW), F32), SDS((n, 4 * D), BF16), SDS((n, IN_COLS), BF16)],
        scratch_shapes=[pltpu.VMEM((tm, D), F32)],
        compiler_params=_params("parallel", "arbitrary"),
    )(dxo, z, proj, wb, wo)


def _matmul_tn(a, b, nblk, a_cols, b_cols, a_follows, tk, name, by_device_cols=False):
    kdim = a.shape[0]
    nk = kdim // tk
    sub = b_cols // NDEV

    def body(a_ref, b_ref, o_ref, ob_ref, *acc):
        k = pl.program_id(1)
        acc_ref = acc[0] if by_device_cols else o_ref

        @pl.when(k == 0)
        def _():
            acc_ref[...] = jnp.zeros_like(acc_ref)

        acc_ref[...] += _dot_tn(a_ref[...].astype(BF16), b_ref[...].astype(BF16))

        @pl.when(k == nk - 1)
        def _():
            if by_device_cols:
                for d in range(NDEV):
                    part = acc_ref[:, d * sub:(d + 1) * sub]
                    o_ref[d] = part
                    ob_ref[d] = part.astype(BF16)
            else:
                ob_ref[...] = o_ref[...].astype(BF16)

    a_map = (lambda j, k: (k, j)) if a_follows else (lambda j, k: (k, 0))
    if by_device_cols:
        shape = (NDEV, nblk, a_cols, sub)
        out_spec = pl.BlockSpec((NDEV, None, a_cols, sub), lambda j, k: (0, j, 0, 0))
        scratch = [pltpu.VMEM((a_cols, b_cols), F32)]
    else:
        shape = (nblk, a_cols, b_cols)
        out_spec = pl.BlockSpec((None, a_cols, b_cols), lambda j, k: (j, 0, 0))
        scratch = []
    return pl.pallas_call(
        body, name=name, grid=(nblk, nk),
        in_specs=[pl.BlockSpec((tk, a_cols), a_map), pl.BlockSpec((tk, b_cols), lambda j, k: (k, j))],
        out_specs=[out_spec, out_spec], out_shape=[SDS(shape, F32), SDS(shape, BF16)],
        scratch_shapes=scratch, compiler_params=_params("parallel", "arbitrary"),
    )(a, b)


def _mixers_bwd(proj, dz, y_conv, q_sc, pooled, d_proj, pool_w, vecs, conv_w, sgu_w, sgu_bc, sc_w, nb, seq, t, comm=None):
    n = nb * seq
    nt = seq // t

    def body(p_ref, dz_ref, y_ref, q_ref, pooled_ref, dpin_ref,
             poolw_ref, vec_ref, convw_ref, sguw_ref, sgub_ref, scw_ref,
             dp_ref, dvec_ref, dconvw_ref, dscw_ref, dpoolw_ref, dsguw_ref, dsgub_ref,
             rext, dyext, dqext, bacc):
        b = pl.program_id(0)
        i = pl.program_id(1)
        first = jnp.logical_and(b == 0, i == 0)
        last = jnp.logical_and(b == nb - 1, i == nt - 1)

        @pl.when(first)
        def _():
            dvec_ref[...] = jnp.zeros_like(dvec_ref)
            dconvw_ref[...] = jnp.zeros_like(dconvw_ref)
            dscw_ref[...] = jnp.zeros_like(dscw_ref)
            dpoolw_ref[...] = jnp.zeros_like(dpoolw_ref)
            dsguw_ref[...] = jnp.zeros_like(dsguw_ref)
            dsgub_ref[...] = jnp.zeros_like(dsgub_ref)
            bacc[...] = jnp.zeros_like(bacc)

        @pl.when(i == 0)
        def _():
            rext[t:, :] = jnp.zeros((HALO_POOL, BW), F32)
            dyext[t:, :] = jnp.zeros((HALO_CONV, BW), F32)
            dqext[t:, :] = jnp.zeros((HALO_SC, BW), F32)

        @pl.when(i > 0)
        def _():
            rext[t:, :] = rext[0:HALO_POOL, :]
            dyext[t:, :] = dyext[0:HALO_CONV, :]
            dqext[t:, :] = dqext[0:HALO_SC, :]

        def piece(k):
            return p_ref[:, k * BW:(k + 1) * BW]

        def put(k, v):
            dp_ref[:, k * BW:(k + 1) * BW] = v.astype(BF16)

        pos = lax.broadcasted_iota(jnp.int32, (t, 1), 0) + (nt - 1 - i) * t

        dzp = dz_ref[:, 0:BW]
        pg = piece(1)
        d_pm = dzp * _silu(pg)
        for j, win in enumerate(POOL_WINDOWS):
            cols = slice(j * GW, (j + 1) * GW)
            pw = poolw_ref[j].astype(BF16)
            pooled = pooled_ref[:, cols]
            mixed = _dot(pooled, pw)
            scale = vec_ref[0:1, cols]
            dvec_ref[0:1, cols] += _colsum(d_pm[:, cols] * mixed)
            dp_ref[:, BW + j * GW:BW + (j + 1) * GW] = (dzp[:, cols] * (mixed * scale) * _dsilu(pg[:, cols])).astype(BF16)
            d_mixed = (d_pm[:, cols] * scale).astype(BF16)
            dpoolw_ref[j] += _dot_tn(pooled, d_mixed)
            d_pooled = _dot_nt(d_mixed, pw)
            cnt = jnp.minimum(pos + 1, win).astype(F32)
            rext[0:t, cols] = d_pooled / cnt
            s = rext[:, cols]
            step = 1
            while step < win:
                s = s + pltpu.roll(s, t + HALO_POOL - step, 0)
                step *= 2
            dp_ref[:, cols] = (s[0:t, :] - d_pooled).astype(BF16)

        dzc = dz_ref[:, BW:2 * BW]
        cgate = piece(4)
        yhat, rstd = _layer_norm_stats(y_ref[...])
        ln = yhat * vec_ref[2:3, :] + vec_ref[3:4, :]
        put(4, dzc * _silu(ln) * _dsilu(cgate))
        d_ln = dzc * _silu(cgate) * _dsilu(ln)
        dvec_ref[2:3, :] += _colsum(d_ln * yhat)
        dvec_ref[3:4, :] += _colsum(d_ln)
        dy = _layer_norm_bwd(d_ln * vec_ref[2:3, :], yhat, rstd)
        dvec_ref[1:2, :] += _colsum(dy)
        dyext[0:t, :] = dy
        dye = dyext[...]
        ca = piece(2)
        sig_cb = _sigmoid(piece(3))
        u = ca * sig_cb
        du = jnp.zeros((t, BW), F32)
        for j in range(CONV_K):
            sh = (dye if j == 0 else pltpu.roll(dye, t + HALO_CONV - j, 0))[0:t, :]
            du = du + convw_ref[CONV_K - 1 - j:CONV_K - j, :] * sh
            dconvw_ref[CONV_K - 1 - j:CONV_K - j, :] += _colsum(u * sh)
        put(2, du * sig_cb)
        put(3, du * u * (1.0 - sig_cb))

        dzg = dz_ref[:, 2 * BW:3 * BW]
        gu = piece(5)
        ggate = piece(7)
        vhat, vrstd = _layer_norm_stats(piece(6))
        vn = (vhat * vec_ref[4:5, :] + vec_ref[5:6, :]).astype(BF16)
        t1 = dzg * _silu(ggate)
        d_sp_all = t1 * gu
        d_vn_cols = []
        sp_cols = []
        for g in range(NG):
            ws = _tril_bf16(sguw_ref[g])
            cols = slice(g * GW, (g + 1) * GW)
            d_vn_rows = []
            sp_rows = []
            for c in range(t // GW):
                rows = slice(c * GW, (c + 1) * GW)
                sp_rows.append(_dot(ws, vn[rows, cols]) + sgub_ref[g])
                d_sp = d_sp_all[rows, cols]
                bacc[g] += d_sp
                d_spb = d_sp.astype(BF16)
                d_vn_rows.append(_dot_tn(ws, d_spb))
                dsguw_ref[g] += _dot_nt(d_spb, vn[rows, cols])
            d_vn_cols.append(jnp.concatenate(d_vn_rows, axis=0))
            sp_cols.append(jnp.concatenate(sp_rows, axis=0))
        d_vn = jnp.concatenate(d_vn_cols, axis=1)
        sp = jnp.concatenate(sp_cols, axis=1)
        put(5, t1 * sp)
        put(7, dzg * gu * sp * _dsilu(ggate))
        dvec_ref[4:5, :] += _colsum(d_vn * vhat)
        dvec_ref[5:6, :] += _colsum(d_vn)
        put(6, _layer_norm_bwd(d_vn * vec_ref[4:5, :], vhat, vrstd))

        dzs = dz_ref[:, 3 * BW:]
        sb = piece(8)
        scp = piece(9)
        sx = piece(10)
        sgate = piece(11)
        q = q_ref[...]
        t2 = dzs * _silu(sgate)
        put(8, t2 * q)
        put(11, dzs * sb * q * _dsilu(sgate))
        dqext[0:t, :] = t2 * sb
        dqe = dqext[...]
        vv = scp * sx
        dv = jnp.zeros((t, BW), F32)
        for j in range(SHORT_K):
            sh = (dqe if j == 0 else pltpu.roll(dqe, t + HALO_SC - j, 0))[0:t, :]
            dv = dv + scw_ref[SHORT_K - 1 - j:SHORT_K - j, :] * sh
            dscw_ref[SHORT_K - 1 - j:SHORT_K - j, :] += _colsum(vv * sh)
        put(9, dv * sx)
        put(10, dv * scp)

        @pl.when(last)
        def _():
            r = lax.broadcasted_iota(jnp.int32, (GW, GW), 0)
            c = lax.broadcasted_iota(jnp.int32, (GW, GW), 1)
            for g in range(NG):
                dsguw_ref[g] = jnp.where(r >= c, dsguw_ref[g], 0.0)
                dsgub_ref[g:g + 1, :] = _colsum(bacc[g].T)

    row = lambda b, i: (b * nt + (nt - 1 - i), 0)
    full2 = lambda b, i: (0, 0)
    full3 = lambda b, i: (0, 0, 0)
    return _carried_call(
        body, comm, (proj, dz, y_conv, q_sc, pooled, d_proj, pool_w, vecs, conv_w, sgu_w, sgu_bc, sc_w),
        name="mixers_bwd", grid=(nb, nt),
        in_specs=[pl.BlockSpec((t, N_PIECE_COLS), row),
                  pl.BlockSpec((t, 4 * BW), row),
                  pl.BlockSpec((t, BW), row), pl.BlockSpec((t, BW), row), pl.BlockSpec((t, BW), row),
                  pl.BlockSpec(memory_space=pl.ANY),
                  pl.BlockSpec((NG, GW, GW), full3),
                  pl.BlockSpec((8, BW), full2),
                  pl.BlockSpec((32, BW), full2),
                  pl.BlockSpec((NG, GW, GW), full3),
                  pl.BlockSpec((NG, GW, 1), full3),
                  pl.BlockSpec((8, BW), full2)],
        out_specs=[pl.BlockSpec((t, N_PIECE_COLS), row),
                   pl.BlockSpec((8, BW), full2), pl.BlockSpec((32, BW), full2), pl.BlockSpec((8, BW), full2),
                   pl.BlockSpec((NG, GW, GW), full3), pl.BlockSpec((NG, GW, GW), full3),
                   pl.BlockSpec((8, GW), full2)],
        out_shape=[SDS((n, IN_COLS), BF16), SDS((8, BW), F32), SDS((32, BW), F32), SDS((8, BW), F32),
                   SDS((NG, GW, GW), F32), SDS((NG, GW, GW), F32), SDS((8, GW), F32)],
        scratch_shapes=[pltpu.VMEM((t + HALO_POOL, BW), F32), pltpu.VMEM((t + HALO_CONV, BW), F32),
                        pltpu.VMEM((t + HALO_SC, BW), F32), pltpu.VMEM((NG, GW, GW), F32)],
        input_output_aliases={5: 0}, semantics=("arbitrary", "arbitrary"))


def _in_proj_bwd(d_proj, w_g, x2d, g_row, dxo, tm, comm=None):
    n = x2d.shape[0]

    def body(dp_ref, w_ref, x_ref, g_ref, dxo_ref, dx_ref, dg_ref, acc):
        k = pl.program_id(1)

        @pl.when(jnp.logical_and(pl.program_id(0) == 0, k == 0))
        def _():
            dg_ref[...] = jnp.zeros_like(dg_ref)

        @pl.when(k == 0)
        def _():
            acc[...] = jnp.zeros_like(acc)

        acc[...] += _dot_nt(dp_ref[...], w_ref[...])

        @pl.when(k == NDEV - 1)
        def _():
            dh = acc[...]
            x = x_ref[...]
            r = lax.rsqrt(jnp.mean(x * x, axis=-1, keepdims=True) + RMS_EPS)
            xn = x * r
            dg_ref[...] += _colsum(dh * xn)
            gy = dh * g_ref[...]
            dx_ref[...] = dxo_ref[...] + r * gy - xn * (r * jnp.mean(gy * xn, axis=-1, keepdims=True))

    return _carried_call(
        body, comm, (d_proj, w_g, x2d, g_row, dxo), name="in_proj_bwd", grid=(n // tm, NDEV),
        in_specs=[pl.BlockSpec((tm, SHARD), lambda i, k: (i, k)),
                  pl.BlockSpec((None, D, SHARD), lambda i, k: (k, 0, 0)),
                  pl.BlockSpec((tm, D), lambda i, k: (i, 0)),
                  pl.BlockSpec((1, D), lambda i, k: (0, 0)),
                  pl.BlockSpec((tm, D), lambda i, k: (i, 0))],
        out_specs=[pl.BlockSpec((tm, D), lambda i, k: (i, 0)), pl.BlockSpec((1, D), lambda i, k: (0, 0))],
        out_shape=[SDS((n, D), F32), SDS((1, D), F32)],
        scratch_shapes=[pltpu.VMEM((tm, D), F32)], semantics=("arbitrary", "arbitrary"))


def _all_gather(arrs, name):
    na = len(arrs)

    def body(*refs):
        ins, outs = refs[:na], refs[na:2 * na]
        send_sems, recv_sems, local_sems = refs[2 * na:]
        x, y, c = _position()
        me, sibling = (x, y, c), (x, y, 1 - c)
        chips = [(1 - x, y), (x, 1 - y), (1 - x, 1 - y)]

        def copy(a, k, block, to, from_input=False):
            px, py, pc = block
            dst = outs[a].at[4 * px + 2 * py + pc]
            return pltpu.make_async_remote_copy(
                src_ref=ins[a] if from_input else dst, dst_ref=dst,
                send_sem=send_sems.at[7 * a + k], recv_sem=recv_sems.at[7 * a + k],
                device_id=to, device_id_type=MESH)

        mine = [pltpu.make_async_copy(ins[a], outs[a].at[4 * x + 2 * y + c], local_sems.at[a]) for a in range(na)]
        for cp in mine:
            cp.start()
        first = []
        for a in range(na):
            first.append(copy(a, 0, me, sibling, from_input=True))
            first += [copy(a, 1 + j, me, (*chip, c), from_input=True) for j, chip in enumerate(chips)]
        for cp in first:
            cp.start()
        passed = []
        for j, chip in enumerate(chips):
            for a in range(na):
                copy(a, 1 + j, (*chip, c), me).wait_recv()
                fwd = copy(a, 4 + j, (*chip, c), sibling)
                fwd.start()
                passed.append(fwd)
        for a in range(na):
            copy(a, 0, sibling, me).wait_recv()
            for j, chip in enumerate(chips):
                copy(a, 4 + j, (*chip, 1 - c), me).wait_recv()
        for cp in first + passed:
            cp.wait_send()
        for cp in mine:
            cp.wait()

    any_spec = pl.BlockSpec(memory_space=pl.ANY)
    return pl.pallas_call(
        body, name=name,
        in_specs=[any_spec] * na, out_specs=[any_spec] * na,
        out_shape=[SDS((NDEV,) + a.shape, a.dtype) for a in arrs],
        scratch_shapes=[pltpu.SemaphoreType.DMA((7 * na,)), pltpu.SemaphoreType.DMA((7 * na,)),
                        pltpu.SemaphoreType.DMA((na,))],
    )(*arrs)


def _adamw_math(g, w, m, v):
    m = ADAM_B1 * m + (1.0 - ADAM_B1) * g
    v = ADAM_B2 * v + (1.0 - ADAM_B2) * (g * g)
    m_hat = m / (1.0 - ADAM_B1 ** ADAM_STEP)
    v_hat = v / (1.0 - ADAM_B2 ** ADAM_STEP)
    delta = -ADAM_LR * (m_hat / (jnp.sqrt(v_hat) + ADAM_EPS) + ADAM_WD * w)
    return delta, m, v


def _adamw_layer(own, landed, w, m, v, layer, earlier, tr, name):
    _, rows, cols = w.shape

    def body(o_ref, l_ref, w_ref, m_ref, v_ref, *rest):
        g_out, d_out, m_out, v_out = rest[-4:]
        g = o_ref[...]
        for r in range(NDEV - 1):
            g = g + l_ref[r].astype(F32)
        g_out[...] = g
        d_out[...], m_out[...], v_out[...] = _adamw_math(g, w_ref[...], m_ref[...], v_ref[...])

    lay = pl.BlockSpec((None, tr, cols), lambda i: (layer, i, 0))
    any_spec = pl.BlockSpec(memory_space=pl.ANY)
    prior = list(earlier) if earlier is not None else []
    return pl.pallas_call(
        body, name=name, grid=(rows // tr,),
        in_specs=[pl.BlockSpec((tr, cols), lambda i: (i, 0)),
                  pl.BlockSpec((NDEV - 1, tr, cols), lambda i: (0, i, 0)), lay, lay, lay] + [any_spec] * len(prior),
        out_specs=[lay] * 4, out_shape=[SDS(w.shape, F32)] * 4,
        input_output_aliases={5 + k: k for k in range(len(prior))},
        compiler_params=_params("parallel"),
    )(own, landed, w, m, v, *prior)


def _adamw_gathered(parts, w, m, v, tr):
    rows, cols = w.shape

    def body(p_ref, w_ref, m_ref, v_ref, g_out, d_out, m_out, v_out):
        g = p_ref[0]
        for k in range(1, NDEV):
            g = g + p_ref[k]
        g_out[...] = g
        d_out[...], m_out[...], v_out[...] = _adamw_math(g, w_ref[...], m_ref[...], v_ref[...])

    blk = pl.BlockSpec((tr, cols), lambda i: (i, 0))
    return pl.pallas_call(
        body, name="adamw_small", grid=(rows // tr,),
        in_specs=[pl.BlockSpec((NDEV, tr, cols), lambda i: (0, i, 0)), blk, blk, blk],
        out_specs=[blk] * 4, out_shape=[SDS((rows, cols), F32)] * 4,
        compiler_params=_params("parallel"),
    )(parts, w, m, v)


def _adamw_plain(g, w, m, v):
    def body(g_ref, w_ref, m_ref, v_ref, d_out, m_out, v_out):
        d_out[...], m_out[...], v_out[...] = _adamw_math(g_ref[...], w_ref[...], m_ref[...], v_ref[...])

    return pl.pallas_call(body, name="adamw_conv_shards", out_shape=[SDS(w.shape, F32)] * 3)(g, w, m, v)


SMALL = ("norm_g", "pool_w", "pool_scale", "conv_w", "conv_b", "conv_ln_g", "conv_ln_b",
         "sgu_ln_g", "sgu_ln_b", "sgu_w", "sgu_b", "sc_w")
SMALL_SHAPES = {"norm_g": (D,), "pool_w": (NG, GW, GW), "pool_scale": (BW,), "conv_w": (CONV_K, BW), "conv_b": (BW,),
                "conv_ln_g": (BW,), "conv_ln_b": (BW,), "sgu_ln_g": (BW,), "sgu_ln_b": (BW,),
                "sgu_w": (NG, GW, GW), "sgu_b": (NG, GW), "sc_w": (SHORT_K, BW)}
REPLICATED = tuple(k for k in SMALL if k not in ("conv_w", "sc_w"))
PACK_UNIT = 8 * 128


def _size(shape):
    out = 1
    for s in shape:
        out *= s
    return out


def _padded(a):
    flat = a.reshape(-1)
    pad = -flat.shape[0] % PACK_UNIT
    return jnp.pad(flat, (0, pad)) if pad else flat


def _pack(arrays):
    return jnp.concatenate([_padded(a) for a in arrays]).reshape(-1, 128)


def _unpack(pack, shapes):
    flat = pack.reshape(-1)
    out, off = [], 0
    for shape in shapes:
        size = _size(shape)
        out.append(flat[off:off + size].reshape(shape))
        off += size + (-size % PACK_UNIT)
    return out


def _gathered_weights(wb, wo, conv):
    wb = wb.reshape(NDEV, 4, BW, D // NDEV).transpose(1, 2, 0, 3).reshape(4, BW, D)
    wo = wo.reshape(D, D)
    cw = CONV_K * (BW // NDEV)
    sw = SHORT_K * (BW // NDEV)
    sc_at = cw + (-cw % PACK_UNIT)
    flat = conv.reshape(NDEV, -1)
    conv_w = flat[:, :cw].reshape(NDEV, CONV_K, BW // NDEV).transpose(1, 0, 2).reshape(CONV_K, BW)
    sc_w = flat[:, sc_at:sc_at + sw].reshape(NDEV, SHORT_K, BW // NDEV).transpose(1, 0, 2).reshape(SHORT_K, BW)
    return wb, wo, conv_w, sc_w


def _layer_small_inputs(l, rep, conv_w, sc_w):
    vecs = jnp.stack([rep["pool_scale"][l], rep["conv_b"][l], rep["conv_ln_g"][l], rep["conv_ln_b"][l],
                      rep["sgu_ln_g"][l], rep["sgu_ln_b"][l], jnp.zeros((BW,), F32), jnp.zeros((BW,), F32)])
    conv_w = jnp.pad(conv_w, ((0, 32 - CONV_K), (0, 0)))
    sc_w = jnp.pad(sc_w, ((0, 8 - SHORT_K), (0, 0)))
    return (rep["pool_w"][l], vecs, conv_w, rep["sgu_w"][l], rep["sgu_b"][l].reshape(NG, GW, 1), sc_w)


def kernel(x, norm_g, w_in, pool_w, pool_scale, conv_w, conv_b, conv_ln_g, conv_ln_b, sgu_ln_g, sgu_ln_b, sgu_w, sgu_b, sc_w, w_branch, w_o, final_g, loss_target, m_norm_g, m_w_in, m_pool_w, m_pool_scale, m_conv_w, m_conv_b, m_conv_ln_g, m_conv_ln_b, m_sgu_ln_g, m_sgu_ln_b, m_sgu_w, m_sgu_b, m_sc_w, m_w_branch, m_w_o, m_final_g, v_norm_g, v_w_in, v_pool_w, v_pool_scale, v_conv_w, v_conv_b, v_conv_ln_g, v_conv_ln_b, v_sgu_ln_g, v_sgu_ln_b, v_sgu_w, v_sgu_b, v_sc_w, v_w_branch, v_w_o, v_final_g):
    args = dict(locals())
    w = {k: args[k] for k in SMALL + ("w_in", "w_branch", "w_o", "final_g")}
    mom = {k: args["m_" + k] for k in w}
    var = {k: args["v_" + k] for k in w}
    rep = {k: w[k] for k in REPLICATED}
    xi, yi, ci = _position()
    me = 4 * xi + 2 * yi + ci

    nb, seq, _ = x.shape
    n = nb * seq
    tm = min(512, n)
    tm2 = min(256, n)
    t_fwd = min(256, seq)
    t_bwd = min(128, seq)

    w_in_b = [w_in[l].astype(BF16) for l in range(DEPTH)]
    wb_b = [w_branch[l].astype(BF16).reshape(4 * BW, D // NDEV) for l in range(DEPTH)]
    wo_b = [w_o[l].astype(BF16) for l in range(DEPTH)]
    conv_b_ = [_pack([conv_w[l], sc_w[l]]) for l in range(DEPTH)]

    (w_in_g0,) = _all_gather([w_in_b[0]], "w_in_all_gather")
    x2d = x.reshape(n, D)
    saved = []
    ride = _Exchange(gathers=[wb_b[0], wo_b[0], conv_b_[0], w_in_b[1]])
    (proj, h), (wb_g, wo_g, conv_g, w_in_g1) = _in_proj(x2d, norm_g[0:1], w_in_g0, tm, ride)
    w_in_g = [w_in_g0, w_in_g1]
    for l in range(DEPTH):
        wb_l, wo_l, conv_l, sc_l = _gathered_weights(wb_g, wo_g, conv_g)
        small = _layer_small_inputs(l, rep, conv_l, sc_l)
        if l == 0:
            ride = _Exchange(gathers=[wb_b[1], wo_b[1], conv_b_[1]])
            (z, y_conv, q_sc, pooled), (wb_g, wo_g, conv_g) = _mixers_fwd(proj, *small, nb, seq, t_fwd, ride)
        else:
            (proj, h), _ = _in_proj(x2d, norm_g[l:l + 1], w_in_g[l], tm)
            (z, y_conv, q_sc, pooled), _ = _mixers_fwd(proj, *small, nb, seq, t_fwd)
        (x_new, merged), _ = _merge_out(z, proj, x2d, wb_l, wo_l, tm2)
        saved.append((x2d, proj, h, z, y_conv, q_sc, pooled, merged, small, wb_l, wo_l))
        x2d = x_new

    dx, loss_acc, d_final_g = _loss_head(x2d, loss_target.reshape(n, D), final_g.reshape(1, D), tm)
    loss = lax.psum(loss_acc[0, 0], ("x", "y", "c"))

    grads = {k: [None] * DEPTH for k in SMALL}
    big = {"w_in": None, "w_branch": None, "w_o": None}
    big_w = {"w_in": (w_in, m_w_in, v_w_in, (DEPTH, D, SHARD), 256),
             "w_branch": (w_branch, m_w_branch, v_w_branch, (DEPTH, 4 * BW, D // NDEV), 512),
             "w_o": (w_o, m_w_o, v_w_o, (DEPTH, D // NDEV, D), 64)}

    def adamw(k, own, landed, l):
        wk, mk, vk, shape, tr = big_w[k]
        big[k] = _adamw_layer(own, landed, wk.reshape(shape), mk.reshape(shape), vk.reshape(shape), l, big[k], tr,
                              "adamw_" + k)

    for l in reversed(range(DEPTH)):
        x_in, proj, h, z, y_conv, q_sc, pooled, merged, small, wb_l, wo_l = saved[l]
        dz, d_bo, d_proj = _merge_out_bwd(dx, z, proj, wb_l, wo_l, tm2)
        d_wo, d_wo_b = _matmul_tn(merged, dx, 1, D, D, False, tm, "grad_w_o")
        d_wb, d_wb_b = _matmul_tn(z, d_bo, 4, BW, D, True, tm, "grad_w_branch", by_device_cols=True)
        ride = _Exchange(scatters=[(d_wo_b.reshape(NDEV, D // NDEV, D), d_wo.reshape(NDEV, D // NDEV, D)),
                                   (d_wb_b.reshape(NDEV, 4 * BW, D // NDEV), d_wb.reshape(NDEV, 4 * BW, D // NDEV))])
        (d_proj, dvec, dconvw, dscw, dpoolw, dsguw, dsgub), (wo_landed, wo_own, wb_landed, wb_own) = _mixers_bwd(
            proj, dz, y_conv, q_sc, pooled, d_proj, *small, nb, seq, t_bwd, ride)
        d_w_in, d_w_in_b = _matmul_tn(h, d_proj, NDEV, D, SHARD, False, tm, "grad_w_in")
        ride = _Exchange(scatters=[(d_w_in_b, d_w_in)])
        (dx, d_norm_g), (in_landed, in_own) = _in_proj_bwd(d_proj, w_in_g[l], x_in, norm_g[l:l + 1], dx, tm, ride)
        adamw("w_o", wo_own, wo_landed, l)
        adamw("w_branch", wb_own, wb_landed, l)
        adamw("w_in", in_own, in_landed, l)
        grads["norm_g"][l] = d_norm_g[0]
        grads["pool_w"][l] = dpoolw
        grads["pool_scale"][l] = dvec[0]
        grads["conv_w"][l] = dconvw[:CONV_K]
        grads["conv_b"][l] = dvec[1]
        grads["conv_ln_g"][l] = dvec[2]
        grads["conv_ln_b"][l] = dvec[3]
        grads["sgu_ln_g"][l] = dvec[4]
        grads["sgu_ln_b"][l] = dvec[5]
        grads["sgu_w"][l] = dsguw
        grads["sgu_b"][l] = dsgub[:NG]
        grads["sc_w"][l] = dscw[:SHORT_K]
    grad_x = dx.reshape(nb, seq, D)
    small_grads = {k: jnp.stack(v) for k, v in grads.items()}

    names = list(SMALL) + ["final_g"]
    shapes = [(DEPTH,) + SMALL_SHAPES[k] for k in SMALL] + [(D,)]
    whole = REPLICATED + ("final_g",)
    part_pack = _pack([small_grads[k] for k in SMALL] + [d_final_g[0]])
    w_pack = _pack([w[k] if k in whole else jnp.zeros(s, F32) for k, s in zip(names, shapes)])
    m_pack = _pack([mom[k] if k in whole else jnp.zeros(s, F32) for k, s in zip(names, shapes)])
    v_pack = _pack([var[k] if k in whole else jnp.ones(s, F32) for k, s in zip(names, shapes)])
    (parts,) = _all_gather([part_pack], "small_grads_all_gather")
    g_pack, d_pack, nm_pack, nv_pack = _adamw_gathered(parts, w_pack, m_pack, v_pack, part_pack.shape[0] // 4)
    out_g = dict(zip(names, _unpack(g_pack, shapes)))
    out_d = dict(zip(names, _unpack(d_pack, shapes)))
    out_m = dict(zip(names, _unpack(nm_pack, shapes)))
    out_v = dict(zip(names, _unpack(nv_pack, shapes)))

    col = me * (BW // NDEV)
    for k in ("conv_w", "sc_w"):
        out_g[k] = lax.dynamic_slice_in_dim(out_g[k], col, BW // NDEV, axis=2)
    sharded_shapes = [w["conv_w"].shape, w["sc_w"].shape]
    d2, m2, v2 = _adamw_plain(_pack([out_g["conv_w"], out_g["sc_w"]]), _pack([w["conv_w"], w["sc_w"]]),
                              _pack([mom["conv_w"], mom["sc_w"]]), _pack([var["conv_w"], var["sc_w"]]))
    for dst, pk in ((out_d, d2), (out_m, m2), (out_v, v2)):
        dst["conv_w"], dst["sc_w"] = _unpack(pk, sharded_shapes)

    for k in big:
        out_g[k], out_d[k], out_m[k], out_v[k] = [o.reshape(w[k].shape) for o in big[k]]

    order = ("norm_g", "w_in", "pool_w", "pool_scale", "conv_w", "conv_b", "conv_ln_g", "conv_ln_b", "sgu_ln_g",
             "sgu_ln_b", "sgu_w", "sgu_b", "sc_w", "w_branch", "w_o", "final_g")
    return (loss, grad_x, *[out_g[k] for k in order], *[out_d[k] for k in order],
            *[out_m[k] for k in order], *[out_v[k] for k in order])
```

```python
import functools

import jax
import jax.numpy as jnp
from jax import lax
from jax.experimental import pallas as pl
from jax.experimental.pallas import tpu as pltpu

F32 = jnp.float32
BF16 = jnp.bfloat16
SDS = jax.ShapeDtypeStruct
MESH = pl.DeviceIdType.MESH

D = 1024
BW = 512
NG = 4
GW = 128
CONV_K = 31
SHORT_K = 3
POOL_WINDOWS = (2, 4, 8, 16)
N_PIECE_COLS = 12 * BW
IN_COLS = N_PIECE_COLS + 4 * D
NDEV = 8
SHARD = IN_COLS // NDEV
DEPTH = 2
RMS_EPS = 1e-6
LN_EPS = 1e-5
HALO_POOL, HALO_CONV, HALO_SC = 16, 32, 8

ADAM_LR = 0.001
ADAM_B1 = 0.9
ADAM_B2 = 0.999
ADAM_EPS = 1e-08
ADAM_WD = 0.01
ADAM_STEP = 10

VMEM_LIMIT = 56 * 1024 * 1024


def _params(*sem):
    return pltpu.CompilerParams(dimension_semantics=sem, vmem_limit_bytes=VMEM_LIMIT)


def _sigmoid(x):
    return jax.nn.sigmoid(x)


def _silu(x):
    return x * _sigmoid(x)


def _dsilu(x):
    s = _sigmoid(x)
    return s * (1.0 + x * (1.0 - s))


def _colsum(x):
    return jnp.sum(x, axis=0, keepdims=True)


def _dot(a, b):
    return jnp.dot(a, b, preferred_element_type=F32)


def _dot_nt(a, b):
    return lax.dot_general(a, b, (((1,), (1,)), ((), ())), preferred_element_type=F32)


def _dot_tn(a, b):
    return lax.dot_general(a, b, (((0,), (0,)), ((), ())), preferred_element_type=F32)


def _layer_norm_stats(y):
    mu = jnp.mean(y, axis=-1, keepdims=True)
    yc = y - mu
    var = jnp.mean(yc * yc, axis=-1, keepdims=True)
    rstd = lax.rsqrt(var + LN_EPS)
    return yc * rstd, rstd


def _layer_norm_bwd(d_hat, hat, rstd):
    return rstd * (d_hat - jnp.mean(d_hat, axis=-1, keepdims=True)
                   - hat * jnp.mean(d_hat * hat, axis=-1, keepdims=True))


def _position():
    return lax.axis_index("x"), lax.axis_index("y"), lax.axis_index("c")


def _flip(v, bit):
    return 1 - v if bit else v


class _Exchange:
    def __init__(self, gathers=(), scatters=()):
        self.gathers = list(gathers)
        self.scatters = list(scatters)
        self.inputs = self.gathers + [a for pair in self.scatters for a in pair]
        self.out_shapes = [SDS((NDEV,) + a.shape, a.dtype) for a in self.gathers]
        for send, keep in self.scatters:
            self.out_shapes += [SDS((NDEV - 1,) + send.shape[1:], send.dtype), SDS(keep.shape[1:], keep.dtype)]
        n = len(self.gathers) + len(self.scatters)
        self.scratch = [pltpu.SemaphoreType.DMA((7 * n,)), pltpu.SemaphoreType.DMA((7 * n,)),
                        pltpu.SemaphoreType.DMA((n,))]

    def _copies(self, ins, outs, sems):
        send_sems, recv_sems, local_sems = sems
        x, y, c = _position()
        me = 4 * x + 2 * y + c
        ng = len(self.gathers)
        local, remote = [], []
        for a in range(ng + len(self.scatters)):
            if a < ng:
                local.append(pltpu.make_async_copy(ins[a], outs[a].at[me], local_sems.at[a]))
            else:
                s = a - ng
                send, keep = ins[ng + 2 * s], ins[ng + 2 * s + 1]
                landed, own = outs[ng + 2 * s], outs[ng + 2 * s + 1]
                local.append(pltpu.make_async_copy(keep.at[me], own, local_sems.at[a]))
            for r in range(1, NDEV):
                peer = (_flip(x, r & 4), _flip(y, r & 2), _flip(c, r & 1))
                if a < ng:
                    src, dst = ins[a], outs[a].at[me]
                else:
                    src, dst = send.at[jnp.bitwise_xor(me, r)], landed.at[r - 1]
                remote.append(pltpu.make_async_remote_copy(
                    src_ref=src, dst_ref=dst, send_sem=send_sems.at[7 * a + r - 1],
                    recv_sem=recv_sems.at[7 * a + r - 1], device_id=peer, device_id_type=MESH))
        return local, remote

    def start(self, ins, outs, sems):
        local, remote = self._copies(ins, outs, sems)
        for cp in remote + local:
            cp.start()

    def finish(self, ins, outs, sems):
        local, remote = self._copies(ins, outs, sems)
        for cp in remote + local:
            cp.wait()


def _carried_call(body, comm, args, *, name, grid, in_specs, out_specs, out_shape, scratch_shapes=(),
                  semantics, input_output_aliases=None):
    aliases = input_output_aliases or {}
    if comm is None:
        out = pl.pallas_call(body, name=name, grid=grid, in_specs=in_specs, out_specs=out_specs, out_shape=out_shape,
                             scratch_shapes=list(scratch_shapes), input_output_aliases=aliases,
                             compiler_params=_params(*semantics))(*args)
        return list(out), []
    n_in, n_out, n_scr = len(in_specs), len(out_specs), len(scratch_shapes)
    c_in, c_out = len(comm.inputs), len(comm.out_shapes)

    def full_body(*refs):
        refs = list(refs)
        ins, refs = refs[:n_in], refs[n_in:]
        cins, refs = refs[:c_in], refs[c_in:]
        outs, refs = refs[:n_out], refs[n_out:]
        couts, refs = refs[:c_out], refs[c_out:]
        scr, sems = refs[:n_scr], refs[n_scr:]
        ids = [pl.program_id(d) for d in range(len(grid))]
        first = functools.reduce(jnp.logical_and, [i == 0 for i in ids])
        last = functools.reduce(jnp.logical_and, [i == g - 1 for i, g in zip(ids, grid)])

        @pl.when(first)
        def _():
            comm.start(cins, couts, sems)

        body(*ins, *outs, *scr)

        @pl.when(last)
        def _():
            comm.finish(cins, couts, sems)

    any_spec = pl.BlockSpec(memory_space=pl.ANY)
    out = pl.pallas_call(
        full_body, name=name, grid=grid,
        in_specs=list(in_specs) + [any_spec] * c_in, out_specs=list(out_specs) + [any_spec] * c_out,
        out_shape=list(out_shape) + comm.out_shapes, scratch_shapes=list(scratch_shapes) + comm.scratch,
        input_output_aliases=aliases, compiler_params=_params(*["arbitrary"] * len(grid)),
    )(*args, *comm.inputs)
    return list(out[:n_out]), list(out[n_out:])


def _in_proj(x2d, g_row, w_g, tm, comm=None):
    n = x2d.shape[0]

    def body(x_ref, g_ref, w_ref, proj_ref, h_ref):
        @pl.when(pl.program_id(1) == 0)
        def _():
            x = x_ref[...]
            r = lax.rsqrt(jnp.mean(x * x, axis=-1, keepdims=True) + RMS_EPS)
            h_ref[...] = (x * r * g_ref[...]).astype(BF16)

        proj_ref[...] = _dot(h_ref[...], w_ref[...])

    return _carried_call(
        body, comm, (x2d, g_row, w_g), name="in_proj", grid=(n // tm, NDEV),
        in_specs=[pl.BlockSpec((tm, D), lambda i, j: (i, 0)),
                  pl.BlockSpec((1, D), lambda i, j: (0, 0)),
                  pl.BlockSpec((None, D, SHARD), lambda i, j: (j, 0, 0))],
        out_specs=[pl.BlockSpec((tm, SHARD), lambda i, j: (i, j)),
                   pl.BlockSpec((tm, D), lambda i, j: (i, 0))],
        out_shape=[SDS((n, IN_COLS), F32), SDS((n, D), BF16)],
        semantics=("parallel", "arbitrary"))


def _tril_bf16(w):
    r = lax.broadcasted_iota(jnp.int32, (GW, GW), 0)
    c = lax.broadcasted_iota(jnp.int32, (GW, GW), 1)
    return jnp.where(r >= c, w, 0.0).astype(BF16)


def _mixers_fwd(proj, pool_w, vecs, conv_w, sgu_w, sgu_bc, sc_w, nb, seq, t, comm=None):
    n = nb * seq
    nt = seq // t

    def body(p_ref, poolw_ref, vec_ref, convw_ref, sguw_ref, sgub_ref, scw_ref,
             z_ref, y_ref, q_ref, pooled_ref, pext, uext, vext):
        i = pl.program_id(1)

        @pl.when(i == 0)
        def _():
            pext[0:HALO_POOL, :] = jnp.zeros((HALO_POOL, BW), F32)
            uext[0:HALO_CONV, :] = jnp.zeros((HALO_CONV, BW), F32)
            vext[0:HALO_SC, :] = jnp.zeros((HALO_SC, BW), F32)

        @pl.when(i > 0)
        def _():
            pext[0:HALO_POOL, :] = pext[t:t + HALO_POOL, :]
            uext[0:HALO_CONV, :] = uext[t:t + HALO_CONV, :]
            vext[0:HALO_SC, :] = vext[t:t + HALO_SC, :]

        def piece(k):
            return p_ref[:, k * BW:(k + 1) * BW]

        pos = lax.broadcasted_iota(jnp.int32, (t, 1), 0) + i * t

        px = piece(0)
        pext[HALO_POOL:, :] = px
        for j, win in enumerate(POOL_WINDOWS):
            cols = slice(j * GW, (j + 1) * GW)
            s = pext[:, cols]
            step = 1
            while step < win:
                s = s + pltpu.roll(s, step, 0)
                step *= 2
            cnt = jnp.minimum(pos + 1, win).astype(F32)
            pooled = (s[HALO_POOL:, :] / cnt - px[:, cols]).astype(BF16)
            pooled_ref[:, cols] = pooled
            mixed = _dot(pooled, poolw_ref[j].astype(BF16))
            z_ref[:, cols] = (mixed * vec_ref[0:1, cols] * _silu(p_ref[:, BW + j * GW:BW + (j + 1) * GW])).astype(BF16)

        uext[HALO_CONV:, :] = piece(2) * _sigmoid(piece(3))
        ue = uext[...]
        acc = jnp.zeros((t, BW), F32)
        for b in range(8):
            rb = ue if b == 0 else pltpu.roll(ue, b, 0)
            for a in range((CONV_K - b + 7) // 8):
                j = 8 * a + b
                acc = acc + convw_ref[CONV_K - 1 - j:CONV_K - j, :] * rb[HALO_CONV - 8 * a:HALO_CONV - 8 * a + t, :]
        y = acc + vec_ref[1:2, :]
        y_ref[...] = y
        yhat, _ = _layer_norm_stats(y)
        act = _silu(yhat * vec_ref[2:3, :] + vec_ref[3:4, :])
        z_ref[:, BW:2 * BW] = (act * _silu(piece(4))).astype(BF16)

        vhat, _ = _layer_norm_stats(piece(6))
        vn = (vhat * vec_ref[4:5, :] + vec_ref[5:6, :]).astype(BF16)
        for g in range(NG):
            ws = _tril_bf16(sguw_ref[g])
            cols = slice(g * GW, (g + 1) * GW)
            for c in range(t // GW):
                rows = slice(c * GW, (c + 1) * GW)
                sp = _dot(ws, vn[rows, cols]) + sgub_ref[g]
                gate = _silu(p_ref[rows, 7 * BW + g * GW:7 * BW + (g + 1) * GW])
                z_ref[rows, 2 * BW + g * GW:2 * BW + (g + 1) * GW] = (
                    p_ref[rows, 5 * BW + g * GW:5 * BW + (g + 1) * GW] * sp * gate).astype(BF16)

        vext[HALO_SC:, :] = piece(9) * piece(10)
        ve = vext[...]
        q = jnp.zeros((t, BW), F32)
        for j in range(SHORT_K):
            sh = ve if j == 0 else pltpu.roll(ve, j, 0)
            q = q + scw_ref[SHORT_K - 1 - j:SHORT_K - j, :] * sh[HALO_SC:, :]
        q_ref[...] = q
        z_ref[:, 3 * BW:] = (piece(8) * q * _silu(piece(11))).astype(BF16)

    row = lambda b, i: (b * nt + i, 0)
    full2 = lambda b, i: (0, 0)
    full3 = lambda b, i: (0, 0, 0)
    return _carried_call(
        body, comm, (proj, pool_w, vecs, conv_w, sgu_w, sgu_bc, sc_w), name="mixers_fwd", grid=(nb, nt),
        in_specs=[pl.BlockSpec((t, N_PIECE_COLS), row),
                  pl.BlockSpec((NG, GW, GW), full3),
                  pl.BlockSpec((8, BW), full2),
                  pl.BlockSpec((32, BW), full2),
                  pl.BlockSpec((NG, GW, GW), full3),
                  pl.BlockSpec((NG, GW, 1), full3),
                  pl.BlockSpec((8, BW), full2)],
        out_specs=[pl.BlockSpec((t, 4 * BW), row), pl.BlockSpec((t, BW), row),
                   pl.BlockSpec((t, BW), row), pl.BlockSpec((t, BW), row)],
        out_shape=[SDS((n, 4 * BW), BF16), SDS((n, BW), F32), SDS((n, BW), F32), SDS((n, BW), BF16)],
        scratch_shapes=[pltpu.VMEM((t + HALO_POOL, BW), F32), pltpu.VMEM((t + HALO_CONV, BW), F32),
                        pltpu.VMEM((t + HALO_SC, BW), F32)],
        semantics=("arbitrary", "arbitrary"))


def _merge_out(z, proj, x2d, wb, wo, tm, comm=None):
    n = x2d.shape[0]

    def body(z_ref, mga_ref, mgb_ref, x_ref, wb_ref, wo_ref, xo_ref, m_ref):
        merged = jnp.zeros((tm, D), F32)
        for k in range(4):
            bo = _dot(z_ref[:, k * BW:(k + 1) * BW], wb_ref[k])
            mg_ref = mga_ref if k < 2 else mgb_ref
            merged = merged + _sigmoid(mg_ref[:, (k % 2) * D:(k % 2 + 1) * D]) * bo
        mb = merged.astype(BF16)
        m_ref[...] = mb
        xo_ref[...] = x_ref[...] + _dot(mb, wo_ref[...])

    return _carried_call(
        body, comm, (z, proj, proj, x2d, wb, wo), name="merge_out", grid=(n // tm,),
        in_specs=[pl.BlockSpec((tm, 4 * BW), lambda i: (i, 0)),
                  pl.BlockSpec((tm, 2 * D), lambda i: (i, 3)),
                  pl.BlockSpec((tm, 2 * D), lambda i: (i, 4)),
                  pl.BlockSpec((tm, D), lambda i: (i, 0)),
                  pl.BlockSpec((4, BW, D), lambda i: (0, 0, 0)),
                  pl.BlockSpec((D, D), lambda i: (0, 0))],
        out_specs=[pl.BlockSpec((tm, D), lambda i: (i, 0)), pl.BlockSpec((tm, D), lambda i: (i, 0))],
        out_shape=[SDS((n, D), F32), SDS((n, D), BF16)], semantics=("parallel",))


def _loss_head(x2d, tgt2d, g_row, tm):
    n = x2d.shape[0]

    def body(x_ref, t_ref, g_ref, dx_ref, loss_ref, dg_ref):
        @pl.when(pl.program_id(0) == 0)
        def _():
            loss_ref[...] = jnp.zeros_like(loss_ref)
            dg_ref[...] = jnp.zeros_like(dg_ref)

        x = x_ref[...]
        g = g_ref[...]
        r = lax.rsqrt(jnp.mean(x * x, axis=-1, keepdims=True) + RMS_EPS)
        xn = x * r
        e = xn * g - t_ref[...]
        loss_ref[...] += 0.5 * jnp.sum(jnp.mean(e * e, axis=-1, keepdims=True))
        dy = e * (1.0 / D)
        dg_ref[...] += _colsum(dy * xn)
        gy = dy * g
        dx_ref[...] = r * gy - xn * (r * jnp.mean(gy * xn, axis=-1, keepdims=True))

    return pl.pallas_call(
        body, name="loss_head", grid=(n // tm,),
        in_specs=[pl.BlockSpec((tm, D), lambda i: (i, 0)), pl.BlockSpec((tm, D), lambda i: (i, 0)),
                  pl.BlockSpec((1, D), lambda i: (0, 0))],
        out_specs=[pl.BlockSpec((tm, D), lambda i: (i, 0)), pl.BlockSpec((8, 128), lambda i: (0, 0)),
                   pl.BlockSpec((1, D), lambda i: (0, 0))],
        out_shape=[SDS((n, D), F32), SDS((8, 128), F32), SDS((1, D), F32)],
        compiler_params=_params("arbitrary"),
    )(x2d, tgt2d, g_row)


def _merge_out_bwd(dxo, z, proj, wb, wo, tm):
    n = dxo.shape[0]

    def body(dx_ref, z_ref, mg_ref, wb_ref, wo_ref, dz_ref, dbo_ref, dp_ref, dm_ref):
        @pl.when(pl.program_id(1) == 0)
        def _():
            dm_ref[...] = _dot_nt(dx_ref[...].astype(BF16), wo_ref[...])

        dm = dm_ref[...]
        for k in range(2):
            bo = _dot(z_ref[:, k * BW:(k + 1) * BW], wb_ref[k])
            sig = _sigmoid(mg_ref[:, k * D:(k + 1) * D])
            dbo = (dm * sig).astype(BF16)
            dbo_ref[:, k * D:(k + 1) * D] = dbo
            dp_ref[:, k * D:(k + 1) * D] = (dm * bo * sig * (1.0 - sig)).astype(BF16)
            dz_ref[:, k * BW:(k + 1) * BW] = _dot_nt(dbo, wb_ref[k])

    return pl.pallas_call(
        body, name="merge_out_bwd", grid=(n // tm, 2),
        in_specs=[pl.BlockSpec((tm, D), lambda i, h: (i, 0)),
                  pl.BlockSpec((tm, 2 * BW), lambda i, h: (i, h)),
                  pl.BlockSpec((tm, 2 * D), lambda i, h: (i, 3 + h)),
                  pl.BlockSpec((2, BW, D), lambda i, h: (h, 0, 0)),
                  pl.BlockSpec((D, D), lambda i, h: (0, 0))],
        out_specs=[pl.BlockSpec((tm, 2 * BW), lambda i, h: (i, h)),
                   pl.BlockSpec((tm, 2 * D), lambda i, h: (i, h)),
                   pl.BlockSpec((tm, 2 * D), lambda i, h: (i, 3 + h))],
        out_shape=[SDS((n, 4 * BW), F32), SDS((n, 4 * D), BF16), SDS((n, IN_COLS), BF16)],
        scratch_shapes=[pltpu.VMEM((tm, D), F32)],
        compiler_params=_params("parallel", "arbitrary"),
    )(dxo, z, proj, wb, wo)


def _matmul_tn(a, b, nblk, a_cols, b_cols, a_follows, tk, name, by_device_cols=False):
    kdim = a.shape[0]
    nk = kdim // tk
    sub = b_cols // NDEV

    def body(a_ref, b_ref, o_ref, ob_ref, *acc):
        k = pl.program_id(1)
        acc_ref = acc[0] if by_device_cols else o_ref

        @pl.when(k == 0)
        def _():
            acc_ref[...] = jnp.zeros_like(acc_ref)

        acc_ref[...] += _dot_tn(a_ref[...].astype(BF16), b_ref[...].astype(BF16))

        @pl.when(k == nk - 1)
        def _():
            if by_device_cols:
                for d in range(NDEV):
                    part = acc_ref[:, d * sub:(d + 1) * sub]
                    o_ref[d] = part
                    ob_ref[d] = part.astype(BF16)
            else:
                ob_ref[...] = o_ref[...].astype(BF16)

    a_map = (lambda j, k: (k, j)) if a_follows else (lambda j, k: (k, 0))
    if by_device_cols:
        shape = (NDEV, nblk, a_cols, sub)
        out_spec = pl.BlockSpec((NDEV, None, a_cols, sub), lambda j, k: (0, j, 0, 0))
        scratch = [pltpu.VMEM((a_cols, b_cols), F32)]
    else:
        shape = (nblk, a_cols, b_cols)
        out_spec = pl.BlockSpec((None, a_cols, b_cols), lambda j, k: (j, 0, 0))
        scratch = []
    return pl.pallas_call(
        body, name=name, grid=(nblk, nk),
        in_specs=[pl.BlockSpec((tk, a_cols), a_map), pl.BlockSpec((tk, b_cols), lambda j, k: (k, j))],
        out_specs=[out_spec, out_spec], out_shape=[SDS(shape, F32), SDS(shape, BF16)],
        scratch_shapes=scratch, compiler_params=_params("parallel", "arbitrary"),
    )(a, b)


def _mixers_bwd(proj, dz, y_conv, q_sc, pooled, d_proj, pool_w, vecs, conv_w, sgu_w, sgu_bc, sc_w, nb, seq, t, comm=None):
    n = nb * seq
    nt = seq // t

    def body(p_ref, dz_ref, y_ref, q_ref, pooled_ref, dpin_ref,
             poolw_ref, vec_ref, convw_ref, sguw_ref, sgub_ref, scw_ref,
             dp_ref, dvec_ref, dconvw_ref, dscw_ref, dpoolw_ref, dsguw_ref, dsgub_ref,
             rext, dyext, dqext, bacc):
        b = pl.program_id(0)
        i = pl.program_id(1)
        first = jnp.logical_and(b == 0, i == 0)
        last = jnp.logical_and(b == nb - 1, i == nt - 1)

        @pl.when(first)
        def _():
            dvec_ref[...] = jnp.zeros_like(dvec_ref)
            dconvw_ref[...] = jnp.zeros_like(dconvw_ref)
            dscw_ref[...] = jnp.zeros_like(dscw_ref)
            dpoolw_ref[...] = jnp.zeros_like(dpoolw_ref)
            dsguw_ref[...] = jnp.zeros_like(dsguw_ref)
            dsgub_ref[...] = jnp.zeros_like(dsgub_ref)
            bacc[...] = jnp.zeros_like(bacc)

        @pl.when(i == 0)
        def _():
            rext[t:, :] = jnp.zeros((HALO_POOL, BW), F32)
            dyext[t:, :] = jnp.zeros((HALO_CONV, BW), F32)
            dqext[t:, :] = jnp.zeros((HALO_SC, BW), F32)

        @pl.when(i > 0)
        def _():
            rext[t:, :] = rext[0:HALO_POOL, :]
            dyext[t:, :] = dyext[0:HALO_CONV, :]
            dqext[t:, :] = dqext[0:HALO_SC, :]

        def piece(k):
            return p_ref[:, k * BW:(k + 1) * BW]

        def put(k, v):
            dp_ref[:, k * BW:(k + 1) * BW] = v.astype(BF16)

        pos = lax.broadcasted_iota(jnp.int32, (t, 1), 0) + (nt - 1 - i) * t

        dzp = dz_ref[:, 0:BW]
        pg = piece(1)
        d_pm = dzp * _silu(pg)
        for j, win in enumerate(POOL_WINDOWS):
            cols = slice(j * GW, (j + 1) * GW)
            pw = poolw_ref[j].astype(BF16)
            pooled = pooled_ref[:, cols]
            mixed = _dot(pooled, pw)
            scale = vec_ref[0:1, cols]
            dvec_ref[0:1, cols] += _colsum(d_pm[:, cols] * mixed)
            dp_ref[:, BW + j * GW:BW + (j + 1) * GW] = (dzp[:, cols] * (mixed * scale) * _dsilu(pg[:, cols])).astype(BF16)
            d_mixed = (d_pm[:, cols] * scale).astype(BF16)
            dpoolw_ref[j] += _dot_tn(pooled, d_mixed)
            d_pooled = _dot_nt(d_mixed, pw)
            cnt = jnp.minimum(pos + 1, win).astype(F32)
            rext[0:t, cols] = d_pooled / cnt
            s = rext[:, cols]
            step = 1
            while step < win:
                s = s + pltpu.roll(s, t + HALO_POOL - step, 0)
                step *= 2
            dp_ref[:, cols] = (s[0:t, :] - d_pooled).astype(BF16)

        dzc = dz_ref[:, BW:2 * BW]
        cgate = piece(4)
        yhat, rstd = _layer_norm_stats(y_ref[...])
        ln = yhat * vec_ref[2:3, :] + vec_ref[3:4, :]
        put(4, dzc * _silu(ln) * _dsilu(cgate))
        d_ln = dzc * _silu(cgate) * _dsilu(ln)
        dvec_ref[2:3, :] += _colsum(d_ln * yhat)
        dvec_ref[3:4, :] += _colsum(d_ln)
        dy = _layer_norm_bwd(d_ln * vec_ref[2:3, :], yhat, rstd)
        dvec_ref[1:2, :] += _colsum(dy)
        dyext[0:t, :] = dy
        dye = dyext[...]
        ca = piece(2)
        sig_cb = _sigmoid(piece(3))
        u = ca * sig_cb
        du = jnp.zeros((t, BW), F32)
        for b in range(8):
            rb = dye if b == 0 else pltpu.roll(dye, t + HALO_CONV - b, 0)
            for a in range((CONV_K - b + 7) // 8):
                j = 8 * a + b
                sh = rb[8 * a:8 * a + t, :]
                du = du + convw_ref[CONV_K - 1 - j:CONV_K - j, :] * sh
                dconvw_ref[CONV_K - 1 - j:CONV_K - j, :] += _colsum(u * sh)
        put(2, du * sig_cb)
        put(3, du * u * (1.0 - sig_cb))

        dzg = dz_ref[:, 2 * BW:3 * BW]
        gu = piece(5)
        ggate = piece(7)
        vhat, vrstd = _layer_norm_stats(piece(6))
        vn = (vhat * vec_ref[4:5, :] + vec_ref[5:6, :]).astype(BF16)
        t1 = dzg * _silu(ggate)
        d_sp_all = t1 * gu
        d_vn_cols = []
        sp_cols = []
        for g in range(NG):
            ws = _tril_bf16(sguw_ref[g])
            cols = slice(g * GW, (g + 1) * GW)
            d_vn_rows = []
            sp_rows = []
            for c in range(t // GW):
                rows = slice(c * GW, (c + 1) * GW)
                sp_rows.append(_dot(ws, vn[rows, cols]) + sgub_ref[g])
                d_sp = d_sp_all[rows, cols]
                bacc[g] += d_sp
                d_spb = d_sp.astype(BF16)
                d_vn_rows.append(_dot_tn(ws, d_spb))
                dsguw_ref[g] += _dot_nt(d_spb, vn[rows, cols])
            d_vn_cols.append(jnp.concatenate(d_vn_rows, axis=0))
            sp_cols.append(jnp.concatenate(sp_rows, axis=0))
        d_vn = jnp.concatenate(d_vn_cols, axis=1)
        sp = jnp.concatenate(sp_cols, axis=1)
        put(5, t1 * sp)
        put(7, dzg * gu * sp * _dsilu(ggate))
        dvec_ref[4:5, :] += _colsum(d_vn * vhat)
        dvec_ref[5:6, :] += _colsum(d_vn)
        put(6, _layer_norm_bwd(d_vn * vec_ref[4:5, :], vhat, vrstd))

        dzs = dz_ref[:, 3 * BW:]
        sb = piece(8)
        scp = piece(9)
        sx = piece(10)
        sgate = piece(11)
        q = q_ref[...]
        t2 = dzs * _silu(sgate)
        put(8, t2 * q)
        put(11, dzs * sb * q * _dsilu(sgate))
        dqext[0:t, :] = t2 * sb
        dqe = dqext[...]
        vv = scp * sx
        dv = jnp.zeros((t, BW), F32)
        for j in range(SHORT_K):
            sh = (dqe if j == 0 else pltpu.roll(dqe, t + HALO_SC - j, 0))[0:t, :]
            dv = dv + scw_ref[SHORT_K - 1 - j:SHORT_K - j, :] * sh
            dscw_ref[SHORT_K - 1 - j:SHORT_K - j, :] += _colsum(vv * sh)
        put(9, dv * sx)
        put(10, dv * scp)

        @pl.when(last)
        def _():
            r = lax.broadcasted_iota(jnp.int32, (GW, GW), 0)
            c = lax.broadcasted_iota(jnp.int32, (GW, GW), 1)
            for g in range(NG):
                dsguw_ref[g] = jnp.where(r >= c, dsguw_ref[g], 0.0)
                dsgub_ref[g:g + 1, :] = _colsum(bacc[g].T)

    row = lambda b, i: (b * nt + (nt - 1 - i), 0)
    full2 = lambda b, i: (0, 0)
    full3 = lambda b, i: (0, 0, 0)
    return _carried_call(
        body, comm, (proj, dz, y_conv, q_sc, pooled, d_proj, pool_w, vecs, conv_w, sgu_w, sgu_bc, sc_w),
        name="mixers_bwd", grid=(nb, nt),
        in_specs=[pl.BlockSpec((t, N_PIECE_COLS), row),
                  pl.BlockSpec((t, 4 * BW), row),
                  pl.BlockSpec((t, BW), row), pl.BlockSpec((t, BW), row), pl.BlockSpec((t, BW), row),
                  pl.BlockSpec(memory_space=pl.ANY),
                  pl.BlockSpec((NG, GW, GW), full3),
                  pl.BlockSpec((8, BW), full2),
                  pl.BlockSpec((32, BW), full2),
                  pl.BlockSpec((NG, GW, GW), full3),
                  pl.BlockSpec((NG, GW, 1), full3),
                  pl.BlockSpec((8, BW), full2)],
        out_specs=[pl.BlockSpec((t, N_PIECE_COLS), row),
                   pl.BlockSpec((8, BW), full2), pl.BlockSpec((32, BW), full2), pl.BlockSpec((8, BW), full2),
                   pl.BlockSpec((NG, GW, GW), full3), pl.BlockSpec((NG, GW, GW), full3),
                   pl.BlockSpec((8, GW), full2)],
        out_shape=[SDS((n, IN_COLS), BF16), SDS((8, BW), F32), SDS((32, BW), F32), SDS((8, BW), F32),
                   SDS((NG, GW, GW), F32), SDS((NG, GW, GW), F32), SDS((8, GW), F32)],
        scratch_shapes=[pltpu.VMEM((t + HALO_POOL, BW), F32), pltpu.VMEM((t + HALO_CONV, BW), F32),
                        pltpu.VMEM((t + HALO_SC, BW), F32), pltpu.VMEM((NG, GW, GW), F32)],
        input_output_aliases={5: 0}, semantics=("arbitrary", "arbitrary"))


def _in_proj_bwd(d_proj, w_g, x2d, g_row, dxo, tm, comm=None):
    n = x2d.shape[0]

    def body(dp_ref, w_ref, x_ref, g_ref, dxo_ref, dx_ref, dg_ref, acc):
        k = pl.program_id(1)

        @pl.when(jnp.logical_and(pl.program_id(0) == 0, k == 0))
        def _():
            dg_ref[...] = jnp.zeros_like(dg_ref)

        @pl.when(k == 0)
        def _():
            acc[...] = jnp.zeros_like(acc)

        acc[...] += _dot_nt(dp_ref[...], w_ref[...])

        @pl.when(k == NDEV - 1)
        def _():
            dh = acc[...]
            x = x_ref[...]
            r = lax.rsqrt(jnp.mean(x * x, axis=-1, keepdims=True) + RMS_EPS)
            xn = x * r
            dg_ref[...] += _colsum(dh * xn)
            gy = dh * g_ref[...]
            dx_ref[...] = dxo_ref[...] + r * gy - xn * (r * jnp.mean(gy * xn, axis=-1, keepdims=True))

    return _carried_call(
        body, comm, (d_proj, w_g, x2d, g_row, dxo), name="in_proj_bwd", grid=(n // tm, NDEV),
        in_specs=[pl.BlockSpec((tm, SHARD), lambda i, k: (i, k)),
                  pl.BlockSpec((None, D, SHARD), lambda i, k: (k, 0, 0)),
                  pl.BlockSpec((tm, D), lambda i, k: (i, 0)),
                  pl.BlockSpec((1, D), lambda i, k: (0, 0)),
                  pl.BlockSpec((tm, D), lambda i, k: (i, 0))],
        out_specs=[pl.BlockSpec((tm, D), lambda i, k: (i, 0)), pl.BlockSpec((1, D), lambda i, k: (0, 0))],
        out_shape=[SDS((n, D), F32), SDS((1, D), F32)],
        scratch_shapes=[pltpu.VMEM((tm, D), F32)], semantics=("arbitrary", "arbitrary"))


def _all_gather(arrs, name):
    na = len(arrs)

    def body(*refs):
        ins, outs = refs[:na], refs[na:2 * na]
        send_sems, recv_sems, local_sems = refs[2 * na:]
        x, y, c = _position()
        me, sibling = (x, y, c), (x, y, 1 - c)
        chips = [(1 - x, y), (x, 1 - y), (1 - x, 1 - y)]

        def copy(a, k, block, to, from_input=False):
            px, py, pc = block
            dst = outs[a].at[4 * px + 2 * py + pc]
            return pltpu.make_async_remote_copy(
                src_ref=ins[a] if from_input else dst, dst_ref=dst,
                send_sem=send_sems.at[7 * a + k], recv_sem=recv_sems.at[7 * a + k],
                device_id=to, device_id_type=MESH)

        mine = [pltpu.make_async_copy(ins[a], outs[a].at[4 * x + 2 * y + c], local_sems.at[a]) for a in range(na)]
        for cp in mine:
            cp.start()
        first = []
        for a in range(na):
            first.append(copy(a, 0, me, sibling, from_input=True))
            first += [copy(a, 1 + j, me, (*chip, c), from_input=True) for j, chip in enumerate(chips)]
        for cp in first:
            cp.start()
        passed = []
        for j, chip in enumerate(chips):
            for a in range(na):
                copy(a, 1 + j, (*chip, c), me).wait_recv()
                fwd = copy(a, 4 + j, (*chip, c), sibling)
                fwd.start()
                passed.append(fwd)
        for a in range(na):
            copy(a, 0, sibling, me).wait_recv()
            for j, chip in enumerate(chips):
                copy(a, 4 + j, (*chip, 1 - c), me).wait_recv()
        for cp in first + passed:
            cp.wait_send()
        for cp in mine:
            cp.wait()

    any_spec = pl.BlockSpec(memory_space=pl.ANY)
    return pl.pallas_call(
        body, name=name,
        in_specs=[any_spec] * na, out_specs=[any_spec] * na,
        out_shape=[SDS((NDEV,) + a.shape, a.dtype) for a in arrs],
        scratch_shapes=[pltpu.SemaphoreType.DMA((7 * na,)), pltpu.SemaphoreType.DMA((7 * na,)),
                        pltpu.SemaphoreType.DMA((na,))],
    )(*arrs)


def _adamw_math(g, w, m, v):
    m = ADAM_B1 * m + (1.0 - ADAM_B1) * g
    v = ADAM_B2 * v + (1.0 - ADAM_B2) * (g * g)
    m_hat = m / (1.0 - ADAM_B1 ** ADAM_STEP)
    v_hat = v / (1.0 - ADAM_B2 ** ADAM_STEP)
    delta = -ADAM_LR * (m_hat / (jnp.sqrt(v_hat) + ADAM_EPS) + ADAM_WD * w)
    return delta, m, v


def _adamw_layer(own, landed, w, m, v, layer, earlier, tr, name):
    _, rows, cols = w.shape

    def body(o_ref, l_ref, w_ref, m_ref, v_ref, *rest):
        g_out, d_out, m_out, v_out = rest[-4:]
        g = o_ref[...]
        for r in range(NDEV - 1):
            g = g + l_ref[r].astype(F32)
        g_out[...] = g
        d_out[...], m_out[...], v_out[...] = _adamw_math(g, w_ref[...], m_ref[...], v_ref[...])

    lay = pl.BlockSpec((None, tr, cols), lambda i: (layer, i, 0))
    any_spec = pl.BlockSpec(memory_space=pl.ANY)
    prior = list(earlier) if earlier is not None else []
    return pl.pallas_call(
        body, name=name, grid=(rows // tr,),
        in_specs=[pl.BlockSpec((tr, cols), lambda i: (i, 0)),
                  pl.BlockSpec((NDEV - 1, tr, cols), lambda i: (0, i, 0)), lay, lay, lay] + [any_spec] * len(prior),
        out_specs=[lay] * 4, out_shape=[SDS(w.shape, F32)] * 4,
        input_output_aliases={5 + k: k for k in range(len(prior))},
        compiler_params=_params("parallel"),
    )(own, landed, w, m, v, *prior)


def _adamw_gathered(parts, w, m, v, tr):
    rows, cols = w.shape

    def body(p_ref, w_ref, m_ref, v_ref, g_out, d_out, m_out, v_out):
        g = p_ref[0]
        for k in range(1, NDEV):
            g = g + p_ref[k]
        g_out[...] = g
        d_out[...], m_out[...], v_out[...] = _adamw_math(g, w_ref[...], m_ref[...], v_ref[...])

    blk = pl.BlockSpec((tr, cols), lambda i: (i, 0))
    return pl.pallas_call(
        body, name="adamw_small", grid=(rows // tr,),
        in_specs=[pl.BlockSpec((NDEV, tr, cols), lambda i: (0, i, 0)), blk, blk, blk],
        out_specs=[blk] * 4, out_shape=[SDS((rows, cols), F32)] * 4,
        compiler_params=_params("parallel"),
    )(parts, w, m, v)


def _adamw_plain(g, w, m, v):
    def body(g_ref, w_ref, m_ref, v_ref, d_out, m_out, v_out):
        d_out[...], m_out[...], v_out[...] = _adamw_math(g_ref[...], w_ref[...], m_ref[...], v_ref[...])

    return pl.pallas_call(body, name="adamw_conv_shards", out_shape=[SDS(w.shape, F32)] * 3)(g, w, m, v)


SMALL = ("norm_g", "pool_w", "pool_scale", "conv_w", "conv_b", "conv_ln_g", "conv_ln_b",
         "sgu_ln_g", "sgu_ln_b", "sgu_w", "sgu_b", "sc_w")
SMALL_SHAPES = {"norm_g": (D,), "pool_w": (NG, GW, GW), "pool_scale": (BW,), "conv_w": (CONV_K, BW), "conv_b": (BW,),
                "conv_ln_g": (BW,), "conv_ln_b": (BW,), "sgu_ln_g": (BW,), "sgu_ln_b": (BW,),
                "sgu_w": (NG, GW, GW), "sgu_b": (NG, GW), "sc_w": (SHORT_K, BW)}
REPLICATED = tuple(k for k in SMALL if k not in ("conv_w", "sc_w"))
PACK_UNIT = 8 * 128


def _size(shape):
    out = 1
    for s in shape:
        out *= s
    return out


def _padded(a):
    flat = a.reshape(-1)
    pad = -flat.shape[0] % PACK_UNIT
    return jnp.pad(flat, (0, pad)) if pad else flat


def _pack(arrays):
    return jnp.concatenate([_padded(a) for a in arrays]).reshape(-1, 128)


def _unpack(pack, shapes):
    flat = pack.reshape(-1)
    out, off = [], 0
    for shape in shapes:
        size = _size(shape)
        out.append(flat[off:off + size].reshape(shape))
        off += size + (-size % PACK_UNIT)
    return out


def _gathered_weights(wb, wo, conv):
    wb = wb.reshape(NDEV, 4, BW, D // NDEV).transpose(1, 2, 0, 3).reshape(4, BW, D)
    wo = wo.reshape(D, D)
    cw = CONV_K * (BW // NDEV)
    sw = SHORT_K * (BW // NDEV)
    sc_at = cw + (-cw % PACK_UNIT)
    flat = conv.reshape(NDEV, -1)
    conv_w = flat[:, :cw].reshape(NDEV, CONV_K, BW // NDEV).transpose(1, 0, 2).reshape(CONV_K, BW)
    sc_w = flat[:, sc_at:sc_at + sw].reshape(NDEV, SHORT_K, BW // NDEV).transpose(1, 0, 2).reshape(SHORT_K, BW)
    return wb, wo, conv_w, sc_w


def _layer_small_inputs(l, rep, conv_w, sc_w):
    vecs = jnp.stack([rep["pool_scale"][l], rep["conv_b"][l], rep["conv_ln_g"][l], rep["conv_ln_b"][l],
                      rep["sgu_ln_g"][l], rep["sgu_ln_b"][l], jnp.zeros((BW,), F32), jnp.zeros((BW,), F32)])
    conv_w = jnp.pad(conv_w, ((0, 32 - CONV_K), (0, 0)))
    sc_w = jnp.pad(sc_w, ((0, 8 - SHORT_K), (0, 0)))
    return (rep["pool_w"][l], vecs, conv_w, rep["sgu_w"][l], rep["sgu_b"][l].reshape(NG, GW, 1), sc_w)


def kernel(x, norm_g, w_in, pool_w, pool_scale, conv_w, conv_b, conv_ln_g, conv_ln_b, sgu_ln_g, sgu_ln_b, sgu_w, sgu_b, sc_w, w_branch, w_o, final_g, loss_target, m_norm_g, m_w_in, m_pool_w, m_pool_scale, m_conv_w, m_conv_b, m_conv_ln_g, m_conv_ln_b, m_sgu_ln_g, m_sgu_ln_b, m_sgu_w, m_sgu_b, m_sc_w, m_w_branch, m_w_o, m_final_g, v_norm_g, v_w_in, v_pool_w, v_pool_scale, v_conv_w, v_conv_b, v_conv_ln_g, v_conv_ln_b, v_sgu_ln_g, v_sgu_ln_b, v_sgu_w, v_sgu_b, v_sc_w, v_w_branch, v_w_o, v_final_g):
    args = dict(locals())
    w = {k: args[k] for k in SMALL + ("w_in", "w_branch", "w_o", "final_g")}
    mom = {k: args["m_" + k] for k in w}
    var = {k: args["v_" + k] for k in w}
    rep = {k: w[k] for k in REPLICATED}
    xi, yi, ci = _position()
    me = 4 * xi + 2 * yi + ci

    nb, seq, _ = x.shape
    n = nb * seq
    tm = min(512, n)
    tm_in = min(1024, n)
    tk = min(2048, n)
    tm2 = min(256, n)
    t_fwd = min(256, seq)
    t_bwd = min(128, seq)

    w_in_b = [w_in[l].astype(BF16) for l in range(DEPTH)]
    wb_b = [w_branch[l].astype(BF16).reshape(4 * BW, D // NDEV) for l in range(DEPTH)]
    wo_b = [w_o[l].astype(BF16) for l in range(DEPTH)]
    conv_b_ = [_pack([conv_w[l], sc_w[l]]) for l in range(DEPTH)]

    (w_in_g0,) = _all_gather([w_in_b[0]], "w_in_all_gather")
    x2d = x.reshape(n, D)
    saved = []
    ride = _Exchange(gathers=[wb_b[0], wo_b[0], conv_b_[0], w_in_b[1]])
    (proj, h), (wb_g, wo_g, conv_g, w_in_g1) = _in_proj(x2d, norm_g[0:1], w_in_g0, tm_in, ride)
    w_in_g = [w_in_g0, w_in_g1]
    for l in range(DEPTH):
        wb_l, wo_l, conv_l, sc_l = _gathered_weights(wb_g, wo_g, conv_g)
        small = _layer_small_inputs(l, rep, conv_l, sc_l)
        if l == 0:
            ride = _Exchange(gathers=[wb_b[1], wo_b[1], conv_b_[1]])
            (z, y_conv, q_sc, pooled), (wb_g, wo_g, conv_g) = _mixers_fwd(proj, *small, nb, seq, t_fwd, ride)
        else:
            (proj, h), _ = _in_proj(x2d, norm_g[l:l + 1], w_in_g[l], tm_in)
            (z, y_conv, q_sc, pooled), _ = _mixers_fwd(proj, *small, nb, seq, t_fwd)
        (x_new, merged), _ = _merge_out(z, proj, x2d, wb_l, wo_l, tm2)
        saved.append((x2d, proj, h, z, y_conv, q_sc, pooled, merged, small, wb_l, wo_l))
        x2d = x_new

    dx, loss_acc, d_final_g = _loss_head(x2d, loss_target.reshape(n, D), final_g.reshape(1, D), tm)
    loss = lax.psum(loss_acc[0, 0], ("x", "y", "c"))

    grads = {k: [None] * DEPTH for k in SMALL}
    big = {"w_in": None, "w_branch": None, "w_o": None}
    big_w = {"w_in": (w_in, m_w_in, v_w_in, (DEPTH, D, SHARD), 256),
             "w_branch": (w_branch, m_w_branch, v_w_branch, (DEPTH, 4 * BW, D // NDEV), 512),
             "w_o": (w_o, m_w_o, v_w_o, (DEPTH, D // NDEV, D), 64)}

    def adamw(k, own, landed, l):
        wk, mk, vk, shape, tr = big_w[k]
        big[k] = _adamw_layer(own, landed, wk.reshape(shape), mk.reshape(shape), vk.reshape(shape), l, big[k], tr,
                              "adamw_" + k)

    for l in reversed(range(DEPTH)):
        x_in, proj, h, z, y_conv, q_sc, pooled, merged, small, wb_l, wo_l = saved[l]
        dz, d_bo, d_proj = _merge_out_bwd(dx, z, proj, wb_l, wo_l, tm2)
        d_wo, d_wo_b = _matmul_tn(merged, dx, 1, D, D, False, tk, "grad_w_o")
        d_wb, d_wb_b = _matmul_tn(z, d_bo, 4, BW, D, True, tk, "grad_w_branch", by_device_cols=True)
        ride = _Exchange(scatters=[(d_wo_b.reshape(NDEV, D // NDEV, D), d_wo.reshape(NDEV, D // NDEV, D)),
                                   (d_wb_b.reshape(NDEV, 4 * BW, D // NDEV), d_wb.reshape(NDEV, 4 * BW, D // NDEV))])
        (d_proj, dvec, dconvw, dscw, dpoolw, dsguw, dsgub), (wo_landed, wo_own, wb_landed, wb_own) = _mixers_bwd(
            proj, dz, y_conv, q_sc, pooled, d_proj, *small, nb, seq, t_bwd, ride)
        d_w_in, d_w_in_b = _matmul_tn(h, d_proj, NDEV, D, SHARD, False, tk, "grad_w_in")
        ride = _Exchange(scatters=[(d_w_in_b, d_w_in)])
        (dx, d_norm_g), (in_landed, in_own) = _in_proj_bwd(d_proj, w_in_g[l], x_in, norm_g[l:l + 1], dx, tm_in, ride)
        adamw("w_o", wo_own, wo_landed, l)
        adamw("w_branch", wb_own, wb_landed, l)
        adamw("w_in", in_own, in_landed, l)
        grads["norm_g"][l] = d_norm_g[0]
        grads["pool_w"][l] = dpoolw
        grads["pool_scale"][l] = dvec[0]
        grads["conv_w"][l] = dconvw[:CONV_K]
        grads["conv_b"][l] = dvec[1]
        grads["conv_ln_g"][l] = dvec[2]
        grads["conv_ln_b"][l] = dvec[3]
        grads["sgu_ln_g"][l] = dvec[4]
        grads["sgu_ln_b"][l] = dvec[5]
        grads["sgu_w"][l] = dsguw
        grads["sgu_b"][l] = dsgub[:NG]
        grads["sc_w"][l] = dscw[:SHORT_K]
    grad_x = dx.reshape(nb, seq, D)
    small_grads = {k: jnp.stack(v) for k, v in grads.items()}

    names = list(SMALL) + ["final_g"]
    shapes = [(DEPTH,) + SMALL_SHAPES[k] for k in SMALL] + [(D,)]
    whole = REPLICATED + ("final_g",)
    part_pack = _pack([small_grads[k] for k in SMALL] + [d_final_g[0]])
    w_pack = _pack([w[k] if k in whole else jnp.zeros(s, F32) for k, s in zip(names, shapes)])
    m_pack = _pack([mom[k] if k in whole else jnp.zeros(s, F32) for k, s in zip(names, shapes)])
    v_pack = _pack([var[k] if k in whole else jnp.ones(s, F32) for k, s in zip(names, shapes)])
    (parts,) = _all_gather([part_pack], "small_grads_all_gather")
    g_pack, d_pack, nm_pack, nv_pack = _adamw_gathered(parts, w_pack, m_pack, v_pack, part_pack.shape[0] // 4)
    out_g = dict(zip(names, _unpack(g_pack, shapes)))
    out_d = dict(zip(names, _unpack(d_pack, shapes)))
    out_m = dict(zip(names, _unpack(nm_pack, shapes)))
    out_v = dict(zip(names, _unpack(nv_pack, shapes)))

    col = me * (BW // NDEV)
    for k in ("conv_w", "sc_w"):
        out_g[k] = lax.dynamic_slice_in_dim(out_g[k], col, BW // NDEV, axis=2)
    sharded_shapes = [w["conv_w"].shape, w["sc_w"].shape]
    d2, m2, v2 = _adamw_plain(_pack([out_g["conv_w"], out_g["sc_w"]]), _pack([w["conv_w"], w["sc_w"]]),
                              _pack([mom["conv_w"], mom["sc_w"]]), _pack([var["conv_w"], var["sc_w"]]))
    for dst, pk in ((out_d, d2), (out_m, m2), (out_v, v2)):
        dst["conv_w"], dst["sc_w"] = _unpack(pk, sharded_shapes)

    for k in big:
        out_g[k], out_d[k], out_m[k], out_v[k] = [o.reshape(w[k].shape) for o in big[k]]

    order = ("norm_g", "w_in", "pool_w", "pool_scale", "conv_w", "conv_b", "conv_ln_g", "conv_ln_b", "sgu_ln_g",
             "sgu_ln_b", "sgu_w", "sgu_b", "sc_w", "w_branch", "w_o", "final_g")
    return (loss, grad_x, *[out_g[k] for k in order], *[out_d[k] for k in order],
            *[out_m[k] for k in order], *[out_v[k] for k in order])
```

```python
import functools

import jax
import jax.numpy as jnp
from jax import lax
from jax.experimental import pallas as pl
from jax.experimental.pallas import tpu as pltpu

F32 = jnp.float32
BF16 = jnp.bfloat16
SDS = jax.ShapeDtypeStruct
MESH = pl.DeviceIdType.MESH

D = 1024
BW = 512
NG = 4
GW = 128
CONV_K = 31
SHORT_K = 3
POOL_WINDOWS = (2, 4, 8, 16)
N_PIECE_COLS = 12 * BW
IN_COLS = N_PIECE_COLS + 4 * D
NDEV = 8
SHARD = IN_COLS // NDEV
DEPTH = 2
RMS_EPS = 1e-6
LN_EPS = 1e-5
HALO_POOL, HALO_CONV, HALO_SC = 16, 32, 8

ADAM_LR = 0.001
ADAM_B1 = 0.9
ADAM_B2 = 0.999
ADAM_EPS = 1e-08
ADAM_WD = 0.01
ADAM_STEP = 10

VMEM_LIMIT = 56 * 1024 * 1024


def _params(*sem):
    return pltpu.CompilerParams(dimension_semantics=sem, vmem_limit_bytes=VMEM_LIMIT)


def _sigmoid(x):
    return jax.nn.sigmoid(x)


def _silu(x):
    return x * _sigmoid(x)


def _dsilu(x):
    s = _sigmoid(x)
    return s * (1.0 + x * (1.0 - s))


def _colsum(x):
    return jnp.sum(x, axis=0, keepdims=True)


def _dot(a, b):
    return jnp.dot(a, b, preferred_element_type=F32)


def _dot_nt(a, b):
    return lax.dot_general(a, b, (((1,), (1,)), ((), ())), preferred_element_type=F32)


def _dot_tn(a, b):
    return lax.dot_general(a, b, (((0,), (0,)), ((), ())), preferred_element_type=F32)


def _layer_norm_stats(y):
    mu = jnp.mean(y, axis=-1, keepdims=True)
    yc = y - mu
    var = jnp.mean(yc * yc, axis=-1, keepdims=True)
    rstd = lax.rsqrt(var + LN_EPS)
    return yc * rstd, rstd


def _layer_norm_bwd(d_hat, hat, rstd):
    return rstd * (d_hat - jnp.mean(d_hat, axis=-1, keepdims=True)
                   - hat * jnp.mean(d_hat * hat, axis=-1, keepdims=True))


def _position():
    return lax.axis_index("x"), lax.axis_index("y"), lax.axis_index("c")


def _flip(v, bit):
    return 1 - v if bit else v


class _Exchange:
    def __init__(self, gathers=(), scatters=()):
        self.gathers = list(gathers)
        self.scatters = list(scatters)
        self.inputs = self.gathers + [a for pair in self.scatters for a in pair]
        self.out_shapes = [SDS((NDEV,) + a.shape, a.dtype) for a in self.gathers]
        for send, keep in self.scatters:
            self.out_shapes += [SDS((NDEV - 1,) + send.shape[1:], send.dtype), SDS(keep.shape[1:], keep.dtype)]
        n = len(self.gathers) + len(self.scatters)
        self.scratch = [pltpu.SemaphoreType.DMA((7 * n,)), pltpu.SemaphoreType.DMA((7 * n,)),
                        pltpu.SemaphoreType.DMA((n,))]

    def _copies(self, ins, outs, sems):
        send_sems, recv_sems, local_sems = sems
        x, y, c = _position()
        me = 4 * x + 2 * y + c
        ng = len(self.gathers)
        local, remote = [], []
        for a in range(ng + len(self.scatters)):
            if a < ng:
                local.append(pltpu.make_async_copy(ins[a], outs[a].at[me], local_sems.at[a]))
            else:
                s = a - ng
                send, keep = ins[ng + 2 * s], ins[ng + 2 * s + 1]
                landed, own = outs[ng + 2 * s], outs[ng + 2 * s + 1]
                local.append(pltpu.make_async_copy(keep.at[me], own, local_sems.at[a]))
            for r in range(1, NDEV):
                peer = (_flip(x, r & 4), _flip(y, r & 2), _flip(c, r & 1))
                if a < ng:
                    src, dst = ins[a], outs[a].at[me]
                else:
                    src, dst = send.at[jnp.bitwise_xor(me, r)], landed.at[r - 1]
                remote.append(pltpu.make_async_remote_copy(
                    src_ref=src, dst_ref=dst, send_sem=send_sems.at[7 * a + r - 1],
                    recv_sem=recv_sems.at[7 * a + r - 1], device_id=peer, device_id_type=MESH))
        return local, remote

    def start(self, ins, outs, sems):
        local, remote = self._copies(ins, outs, sems)
        for cp in remote + local:
            cp.start()

    def finish(self, ins, outs, sems):
        local, remote = self._copies(ins, outs, sems)
        for cp in remote + local:
            cp.wait()


def _carried_call(body, comm, args, *, name, grid, in_specs, out_specs, out_shape, scratch_shapes=(),
                  semantics, input_output_aliases=None):
    aliases = input_output_aliases or {}
    if comm is None:
        out = pl.pallas_call(body, name=name, grid=grid, in_specs=in_specs, out_specs=out_specs, out_shape=out_shape,
                             scratch_shapes=list(scratch_shapes), input_output_aliases=aliases,
                             compiler_params=_params(*semantics))(*args)
        return list(out), []
    n_in, n_out, n_scr = len(in_specs), len(out_specs), len(scratch_shapes)
    c_in, c_out = len(comm.inputs), len(comm.out_shapes)

    def full_body(*refs):
        refs = list(refs)
        ins, refs = refs[:n_in], refs[n_in:]
        cins, refs = refs[:c_in], refs[c_in:]
        outs, refs = refs[:n_out], refs[n_out:]
        couts, refs = refs[:c_out], refs[c_out:]
        scr, sems = refs[:n_scr], refs[n_scr:]
        ids = [pl.program_id(d) for d in range(len(grid))]
        first = functools.reduce(jnp.logical_and, [i == 0 for i in ids])
        last = functools.reduce(jnp.logical_and, [i == g - 1 for i, g in zip(ids, grid)])

        @pl.when(first)
        def _():
            comm.start(cins, couts, sems)

        body(*ins, *outs, *scr)

        @pl.when(last)
        def _():
            comm.finish(cins, couts, sems)

    any_spec = pl.BlockSpec(memory_space=pl.ANY)
    out = pl.pallas_call(
        full_body, name=name, grid=grid,
        in_specs=list(in_specs) + [any_spec] * c_in, out_specs=list(out_specs) + [any_spec] * c_out,
        out_shape=list(out_shape) + comm.out_shapes, scratch_shapes=list(scratch_shapes) + comm.scratch,
        input_output_aliases=aliases, compiler_params=_params(*["arbitrary"] * len(grid)),
    )(*args, *comm.inputs)
    return list(out[:n_out]), list(out[n_out:])


def _in_proj(x2d, g_row, w_g, tm, comm=None):
    n = x2d.shape[0]
    half = D // 2

    def body(x_ref, g_ref, wu_ref, wl_ref, proj_ref, h_ref):
        @pl.when(pl.program_id(1) == 0)
        def _():
            x = x_ref[...]
            r = lax.rsqrt(jnp.mean(x * x, axis=-1, keepdims=True) + RMS_EPS)
            h_ref[...] = (x * r * g_ref[...]).astype(BF16)

        proj_ref[...] = _dot(h_ref[:, :half], wu_ref[...]) + _dot(h_ref[:, half:], wl_ref[...])

    return _carried_call(
        body, comm, (x2d, g_row, *w_g), name="in_proj", grid=(n // tm, NDEV),
        in_specs=[pl.BlockSpec((tm, D), lambda i, j: (i, 0)),
                  pl.BlockSpec((1, D), lambda i, j: (0, 0)),
                  pl.BlockSpec((None, half, SHARD), lambda i, j: (j, 0, 0)),
                  pl.BlockSpec((None, half, SHARD), lambda i, j: (j, 0, 0))],
        out_specs=[pl.BlockSpec((tm, SHARD), lambda i, j: (i, j)),
                   pl.BlockSpec((tm, D), lambda i, j: (i, 0))],
        out_shape=[SDS((n, IN_COLS), F32), SDS((n, D), BF16)],
        semantics=("parallel", "arbitrary"))


def _tril_bf16(w):
    r = lax.broadcasted_iota(jnp.int32, (GW, GW), 0)
    c = lax.broadcasted_iota(jnp.int32, (GW, GW), 1)
    return jnp.where(r >= c, w, 0.0).astype(BF16)


def _mixers_fwd(proj, pool_w, vecs, conv_w, sgu_w, sgu_bc, sc_w, nb, seq, t, comm=None):
    n = nb * seq
    nt = seq // t

    def body(p_ref, poolw_ref, vec_ref, convw_ref, sguw_ref, sgub_ref, scw_ref,
             z_ref, y_ref, q_ref, pooled_ref, pext, uext, vext):
        i = pl.program_id(1)

        @pl.when(i == 0)
        def _():
            pext[0:HALO_POOL, :] = jnp.zeros((HALO_POOL, BW), F32)
            uext[0:HALO_CONV, :] = jnp.zeros((HALO_CONV, BW), F32)
            vext[0:HALO_SC, :] = jnp.zeros((HALO_SC, BW), F32)

        @pl.when(i > 0)
        def _():
            pext[0:HALO_POOL, :] = pext[t:t + HALO_POOL, :]
            uext[0:HALO_CONV, :] = uext[t:t + HALO_CONV, :]
            vext[0:HALO_SC, :] = vext[t:t + HALO_SC, :]

        def piece(k):
            return p_ref[:, k * BW:(k + 1) * BW]

        pos = lax.broadcasted_iota(jnp.int32, (t, 1), 0) + i * t

        px = piece(0)
        pext[HALO_POOL:, :] = px
        for j, win in enumerate(POOL_WINDOWS):
            cols = slice(j * GW, (j + 1) * GW)
            s = pext[:, cols]
            step = 1
            while step < win:
                s = s + pltpu.roll(s, step, 0)
                step *= 2
            cnt = jnp.minimum(pos + 1, win).astype(F32)
            pooled = (s[HALO_POOL:, :] / cnt - px[:, cols]).astype(BF16)
            pooled_ref[:, cols] = pooled
            mixed = _dot(pooled, poolw_ref[j].astype(BF16))
            z_ref[:, cols] = (mixed * vec_ref[0:1, cols] * _silu(p_ref[:, BW + j * GW:BW + (j + 1) * GW])).astype(BF16)

        uext[HALO_CONV:, :] = piece(2) * _sigmoid(piece(3))
        ue = uext[...]
        acc = jnp.zeros((t, BW), F32)
        for b in range(8):
            rb = ue if b == 0 else pltpu.roll(ue, b, 0)
            for a in range((CONV_K - b + 7) // 8):
                j = 8 * a + b
                acc = acc + convw_ref[CONV_K - 1 - j:CONV_K - j, :] * rb[HALO_CONV - 8 * a:HALO_CONV - 8 * a + t, :]
        y = acc + vec_ref[1:2, :]
        y_ref[...] = y
        yhat, _ = _layer_norm_stats(y)
        act = _silu(yhat * vec_ref[2:3, :] + vec_ref[3:4, :])
        z_ref[:, BW:2 * BW] = (act * _silu(piece(4))).astype(BF16)

        vhat, _ = _layer_norm_stats(piece(6))
        vn = (vhat * vec_ref[4:5, :] + vec_ref[5:6, :]).astype(BF16)
        for g in range(NG):
            ws = _tril_bf16(sguw_ref[g])
            cols = slice(g * GW, (g + 1) * GW)
            for c in range(t // GW):
                rows = slice(c * GW, (c + 1) * GW)
                sp = _dot(ws, vn[rows, cols]) + sgub_ref[g]
                gate = _silu(p_ref[rows, 7 * BW + g * GW:7 * BW + (g + 1) * GW])
                z_ref[rows, 2 * BW + g * GW:2 * BW + (g + 1) * GW] = (
                    p_ref[rows, 5 * BW + g * GW:5 * BW + (g + 1) * GW] * sp * gate).astype(BF16)

        vext[HALO_SC:, :] = piece(9) * piece(10)
        ve = vext[...]
        q = jnp.zeros((t, BW), F32)
        for j in range(SHORT_K):
            sh = ve if j == 0 else pltpu.roll(ve, j, 0)
            q = q + scw_ref[SHORT_K - 1 - j:SHORT_K - j, :] * sh[HALO_SC:, :]
        q_ref[...] = q
        z_ref[:, 3 * BW:] = (piece(8) * q * _silu(piece(11))).astype(BF16)

    row = lambda b, i: (b * nt + i, 0)
    full2 = lambda b, i: (0, 0)
    full3 = lambda b, i: (0, 0, 0)
    return _carried_call(
        body, comm, (proj, pool_w, vecs, conv_w, sgu_w, sgu_bc, sc_w), name="mixers_fwd", grid=(nb, nt),
        in_specs=[pl.BlockSpec((t, N_PIECE_COLS), row),
                  pl.BlockSpec((NG, GW, GW), full3),
                  pl.BlockSpec((8, BW), full2),
                  pl.BlockSpec((32, BW), full2),
                  pl.BlockSpec((NG, GW, GW), full3),
                  pl.BlockSpec((NG, GW, 1), full3),
                  pl.BlockSpec((8, BW), full2)],
        out_specs=[pl.BlockSpec((t, 4 * BW), row), pl.BlockSpec((t, BW), row),
                   pl.BlockSpec((t, BW), row), pl.BlockSpec((t, BW), row)],
        out_shape=[SDS((n, 4 * BW), BF16), SDS((n, BW), F32), SDS((n, BW), F32), SDS((n, BW), BF16)],
        scratch_shapes=[pltpu.VMEM((t + HALO_POOL, BW), F32), pltpu.VMEM((t + HALO_CONV, BW), F32),
                        pltpu.VMEM((t + HALO_SC, BW), F32)],
        semantics=("arbitrary", "arbitrary"))


def _merge_out(z, proj, x2d, wb, wo, tm, comm=None):
    n = x2d.shape[0]

    def body(z_ref, mga_ref, mgb_ref, x_ref, wb_ref, wo_ref, xo_ref, m_ref):
        merged = jnp.zeros((tm, D), F32)
        for k in range(4):
            bo = _dot(z_ref[:, k * BW:(k + 1) * BW], wb_ref[k])
            mg_ref = mga_ref if k < 2 else mgb_ref
            merged = merged + _sigmoid(mg_ref[:, (k % 2) * D:(k % 2 + 1) * D]) * bo
        mb = merged.astype(BF16)
        m_ref[...] = mb
        xo_ref[...] = x_ref[...] + _dot(mb, wo_ref[...])

    return _carried_call(
        body, comm, (z, proj, proj, x2d, wb, wo), name="merge_out", grid=(n // tm,),
        in_specs=[pl.BlockSpec((tm, 4 * BW), lambda i: (i, 0)),
                  pl.BlockSpec((tm, 2 * D), lambda i: (i, 3)),
                  pl.BlockSpec((tm, 2 * D), lambda i: (i, 4)),
                  pl.BlockSpec((tm, D), lambda i: (i, 0)),
                  pl.BlockSpec((4, BW, D), lambda i: (0, 0, 0)),
                  pl.BlockSpec((D, D), lambda i: (0, 0))],
        out_specs=[pl.BlockSpec((tm, D), lambda i: (i, 0)), pl.BlockSpec((tm, D), lambda i: (i, 0))],
        out_shape=[SDS((n, D), F32), SDS((n, D), BF16)], semantics=("parallel",))


def _loss_head(x2d, tgt2d, g_row, tm):
    n = x2d.shape[0]

    def body(x_ref, t_ref, g_ref, dx_ref, loss_ref, dg_ref):
        @pl.when(pl.program_id(0) == 0)
        def _():
            loss_ref[...] = jnp.zeros_like(loss_ref)
            dg_ref[...] = jnp.zeros_like(dg_ref)

        x = x_ref[...]
        g = g_ref[...]
        r = lax.rsqrt(jnp.mean(x * x, axis=-1, keepdims=True) + RMS_EPS)
        xn = x * r
        e = xn * g - t_ref[...]
        loss_ref[...] += 0.5 * jnp.sum(jnp.mean(e * e, axis=-1, keepdims=True))
        dy = e * (1.0 / D)
        dg_ref[...] += _colsum(dy * xn)
        gy = dy * g
        dx_ref[...] = r * gy - xn * (r * jnp.mean(gy * xn, axis=-1, keepdims=True))

    return pl.pallas_call(
        body, name="loss_head", grid=(n // tm,),
        in_specs=[pl.BlockSpec((tm, D), lambda i: (i, 0)), pl.BlockSpec((tm, D), lambda i: (i, 0)),
                  pl.BlockSpec((1, D), lambda i: (0, 0))],
        out_specs=[pl.BlockSpec((tm, D), lambda i: (i, 0)), pl.BlockSpec((8, 128), lambda i: (0, 0)),
                   pl.BlockSpec((1, D), lambda i: (0, 0))],
        out_shape=[SDS((n, D), F32), SDS((8, 128), F32), SDS((1, D), F32)],
        compiler_params=_params("arbitrary"),
    )(x2d, tgt2d, g_row)


def _merge_out_bwd(dxo, z, proj, wb, wo, tm):
    n = dxo.shape[0]

    def body(dx_ref, z_ref, mg_ref, wb_ref, wo_ref, dz_ref, dbo_ref, dp_ref, dm_ref):
        @pl.when(pl.program_id(1) == 0)
        def _():
            dm_ref[...] = _dot_nt(dx_ref[...].astype(BF16), wo_ref[...])

        dm = dm_ref[...]
        for k in range(2):
            bo = _dot(z_ref[:, k * BW:(k + 1) * BW], wb_ref[k])
            sig = _sigmoid(mg_ref[:, k * D:(k + 1) * D])
            dbo = (dm * sig).astype(BF16)
            dbo_ref[:, k * D:(k + 1) * D] = dbo
            dp_ref[:, k * D:(k + 1) * D] = (dm * bo * sig * (1.0 - sig)).astype(BF16)
            dz_ref[:, k * BW:(k + 1) * BW] = _dot_nt(dbo, wb_ref[k])

    return pl.pallas_call(
        body, name="merge_out_bwd", grid=(n // tm, 2),
        in_specs=[pl.BlockSpec((tm, D), lambda i, h: (i, 0)),
                  pl.BlockSpec((tm, 2 * BW), lambda i, h: (i, h)),
                  pl.BlockSpec((tm, 2 * D), lambda i, h: (i, 3 + h)),
                  pl.BlockSpec((2, BW, D), lambda i, h: (h, 0, 0)),
                  pl.BlockSpec((D, D), lambda i, h: (0, 0))],
        out_specs=[pl.BlockSpec((tm, 2 * BW), lambda i, h: (i, h)),
                   pl.BlockSpec((tm, 2 * D), lambda i, h: (i, h)),
                   pl.BlockSpec((tm, 2 * D), lambda i, h: (i, 3 + h))],
        out_shape=[SDS((n, 4 * BW), F32), SDS((n, 4 * D), BF16), SDS((n, IN_COLS), BF16)],
        scratch_shapes=[pltpu.VMEM((tm, D), F32)],
        compiler_params=_params("parallel", "arbitrary"),
    )(dxo, z, proj, wb, wo)


def _matmul_tn(a, b, nblk, a_cols, b_cols, a_follows, tk, name, by_device_cols=False):
    kdim = a.shape[0]
    nk = kdim // tk
    sub = b_cols // NDEV

    def body(a_ref, b_ref, o_ref, ob_ref, *acc):
        k = pl.program_id(1)
        acc_ref = acc[0] if by_device_cols else o_ref

        @pl.when(k == 0)
        def _():
            acc_ref[...] = jnp.zeros_like(acc_ref)

        acc_ref[...] += _dot_tn(a_ref[...].astype(BF16), b_ref[...].astype(BF16))

        @pl.when(k == nk - 1)
        def _():
            if by_device_cols:
                for d in range(NDEV):
                    part = acc_ref[:, d * sub:(d + 1) * sub]
                    o_ref[d] = part
                    ob_ref[d] = part.astype(BF16)
            else:
                ob_ref[...] = o_ref[...].astype(BF16)

    a_map = (lambda j, k: (k, j)) if a_follows else (lambda j, k: (k, 0))
    if by_device_cols:
        shape = (NDEV, nblk, a_cols, sub)
        out_spec = pl.BlockSpec((NDEV, None, a_cols, sub), lambda j, k: (0, j, 0, 0))
        scratch = [pltpu.VMEM((a_cols, b_cols), F32)]
    else:
        shape = (nblk, a_cols, b_cols)
        out_spec = pl.BlockSpec((None, a_cols, b_cols), lambda j, k: (j, 0, 0))
        scratch = []
    return pl.pallas_call(
        body, name=name, grid=(nblk, nk),
        in_specs=[pl.BlockSpec((tk, a_cols), a_map), pl.BlockSpec((tk, b_cols), lambda j, k: (k, j))],
        out_specs=[out_spec, out_spec], out_shape=[SDS(shape, F32), SDS(shape, BF16)],
        scratch_shapes=scratch, compiler_params=_params("parallel", "arbitrary"),
    )(a, b)


def _mixers_bwd(proj, dz, y_conv, q_sc, pooled, d_proj, pool_w, vecs, conv_w, sgu_w, sgu_bc, sc_w, nb, seq, t, comm=None):
    n = nb * seq
    nt = seq // t

    def body(p_ref, dz_ref, y_ref, q_ref, pooled_ref, dpin_ref,
             poolw_ref, vec_ref, convw_ref, sguw_ref, sgub_ref, scw_ref,
             dp_ref, dvec_ref, dconvw_ref, dscw_ref, dpoolw_ref, dsguw_ref, dsgub_ref,
             rext, dyext, dqext, bacc):
        b = pl.program_id(0)
        i = pl.program_id(1)
        first = jnp.logical_and(b == 0, i == 0)
        last = jnp.logical_and(b == nb - 1, i == nt - 1)

        @pl.when(first)
        def _():
            dvec_ref[...] = jnp.zeros_like(dvec_ref)
            dconvw_ref[...] = jnp.zeros_like(dconvw_ref)
            dscw_ref[...] = jnp.zeros_like(dscw_ref)
            dpoolw_ref[...] = jnp.zeros_like(dpoolw_ref)
            dsguw_ref[...] = jnp.zeros_like(dsguw_ref)
            dsgub_ref[...] = jnp.zeros_like(dsgub_ref)
            bacc[...] = jnp.zeros_like(bacc)

        @pl.when(i == 0)
        def _():
            rext[t:, :] = jnp.zeros((HALO_POOL, BW), F32)
            dyext[t:, :] = jnp.zeros((HALO_CONV, BW), F32)
            dqext[t:, :] = jnp.zeros((HALO_SC, BW), F32)

        @pl.when(i > 0)
        def _():
            rext[t:, :] = rext[0:HALO_POOL, :]
            dyext[t:, :] = dyext[0:HALO_CONV, :]
            dqext[t:, :] = dqext[0:HALO_SC, :]

        def piece(k):
            return p_ref[:, k * BW:(k + 1) * BW]

        def put(k, v):
            dp_ref[:, k * BW:(k + 1) * BW] = v.astype(BF16)

        pos = lax.broadcasted_iota(jnp.int32, (t, 1), 0) + (nt - 1 - i) * t

        dzp = dz_ref[:, 0:BW]
        pg = piece(1)
        d_pm = dzp * _silu(pg)
        for j, win in enumerate(POOL_WINDOWS):
            cols = slice(j * GW, (j + 1) * GW)
            pw = poolw_ref[j].astype(BF16)
            pooled = pooled_ref[:, cols]
            mixed = _dot(pooled, pw)
            scale = vec_ref[0:1, cols]
            dvec_ref[0:1, cols] += _colsum(d_pm[:, cols] * mixed)
            dp_ref[:, BW + j * GW:BW + (j + 1) * GW] = (dzp[:, cols] * (mixed * scale) * _dsilu(pg[:, cols])).astype(BF16)
            d_mixed = (d_pm[:, cols] * scale).astype(BF16)
            dpoolw_ref[j] += _dot_tn(pooled, d_mixed)
            d_pooled = _dot_nt(d_mixed, pw)
            cnt = jnp.minimum(pos + 1, win).astype(F32)
            rext[0:t, cols] = d_pooled / cnt
            s = rext[:, cols]
            step = 1
            while step < win:
                s = s + pltpu.roll(s, t + HALO_POOL - step, 0)
                step *= 2
            dp_ref[:, cols] = (s[0:t, :] - d_pooled).astype(BF16)

        dzc = dz_ref[:, BW:2 * BW]
        cgate = piece(4)
        yhat, rstd = _layer_norm_stats(y_ref[...])
        ln = yhat * vec_ref[2:3, :] + vec_ref[3:4, :]
        put(4, dzc * _silu(ln) * _dsilu(cgate))
        d_ln = dzc * _silu(cgate) * _dsilu(ln)
        dvec_ref[2:3, :] += _colsum(d_ln * yhat)
        dvec_ref[3:4, :] += _colsum(d_ln)
        dy = _layer_norm_bwd(d_ln * vec_ref[2:3, :], yhat, rstd)
        dvec_ref[1:2, :] += _colsum(dy)
        dyext[0:t, :] = dy
        dye = dyext[...]
        ca = piece(2)
        sig_cb = _sigmoid(piece(3))
        u = ca * sig_cb
        du = jnp.zeros((t, BW), F32)
        for b in range(8):
            rb = dye if b == 0 else pltpu.roll(dye, t + HALO_CONV - b, 0)
            for a in range((CONV_K - b + 7) // 8):
                j = 8 * a + b
                sh = rb[8 * a:8 * a + t, :]
                du = du + convw_ref[CONV_K - 1 - j:CONV_K - j, :] * sh
                dconvw_ref[CONV_K - 1 - j:CONV_K - j, :] += _colsum(u * sh)
        put(2, du * sig_cb)
        put(3, du * u * (1.0 - sig_cb))

        dzg = dz_ref[:, 2 * BW:3 * BW]
        gu = piece(5)
        ggate = piece(7)
        vhat, vrstd = _layer_norm_stats(piece(6))
        vn = (vhat * vec_ref[4:5, :] + vec_ref[5:6, :]).astype(BF16)
        t1 = dzg * _silu(ggate)
        d_sp_all = t1 * gu
        d_vn_cols = []
        sp_cols = []
        for g in range(NG):
            ws = _tril_bf16(sguw_ref[g])
            cols = slice(g * GW, (g + 1) * GW)
            d_vn_rows = []
            sp_rows = []
            for c in range(t // GW):
                rows = slice(c * GW, (c + 1) * GW)
                sp_rows.append(_dot(ws, vn[rows, cols]) + sgub_ref[g])
                d_sp = d_sp_all[rows, cols]
                bacc[g] += d_sp
                d_spb = d_sp.astype(BF16)
                d_vn_rows.append(_dot_tn(ws, d_spb))
                dsguw_ref[g] += _dot_nt(d_spb, vn[rows, cols])
            d_vn_cols.append(jnp.concatenate(d_vn_rows, axis=0))
            sp_cols.append(jnp.concatenate(sp_rows, axis=0))
        d_vn = jnp.concatenate(d_vn_cols, axis=1)
        sp = jnp.concatenate(sp_cols, axis=1)
        put(5, t1 * sp)
        put(7, dzg * gu * sp * _dsilu(ggate))
        dvec_ref[4:5, :] += _colsum(d_vn * vhat)
        dvec_ref[5:6, :] += _colsum(d_vn)
        put(6, _layer_norm_bwd(d_vn * vec_ref[4:5, :], vhat, vrstd))

        dzs = dz_ref[:, 3 * BW:]
        sb = piece(8)
        scp = piece(9)
        sx = piece(10)
        sgate = piece(11)
        q = q_ref[...]
        t2 = dzs * _silu(sgate)
        put(8, t2 * q)
        put(11, dzs * sb * q * _dsilu(sgate))
        dqext[0:t, :] = t2 * sb
        dqe = dqext[...]
        vv = scp * sx
        dv = jnp.zeros((t, BW), F32)
        for j in range(SHORT_K):
            sh = (dqe if j == 0 else pltpu.roll(dqe, t + HALO_SC - j, 0))[0:t, :]
            dv = dv + scw_ref[SHORT_K - 1 - j:SHORT_K - j, :] * sh
            dscw_ref[SHORT_K - 1 - j:SHORT_K - j, :] += _colsum(vv * sh)
        put(9, dv * sx)
        put(10, dv * scp)

        @pl.when(last)
        def _():
            r = lax.broadcasted_iota(jnp.int32, (GW, GW), 0)
            c = lax.broadcasted_iota(jnp.int32, (GW, GW), 1)
            for g in range(NG):
                dsguw_ref[g] = jnp.where(r >= c, dsguw_ref[g], 0.0)
                dsgub_ref[g:g + 1, :] = _colsum(bacc[g].T)

    row = lambda b, i: (b * nt + (nt - 1 - i), 0)
    full2 = lambda b, i: (0, 0)
    full3 = lambda b, i: (0, 0, 0)
    return _carried_call(
        body, comm, (proj, dz, y_conv, q_sc, pooled, d_proj, pool_w, vecs, conv_w, sgu_w, sgu_bc, sc_w),
        name="mixers_bwd", grid=(nb, nt),
        in_specs=[pl.BlockSpec((t, N_PIECE_COLS), row),
                  pl.BlockSpec((t, 4 * BW), row),
                  pl.BlockSpec((t, BW), row), pl.BlockSpec((t, BW), row), pl.BlockSpec((t, BW), row),
                  pl.BlockSpec(memory_space=pl.ANY),
                  pl.BlockSpec((NG, GW, GW), full3),
                  pl.BlockSpec((8, BW), full2),
                  pl.BlockSpec((32, BW), full2),
                  pl.BlockSpec((NG, GW, GW), full3),
                  pl.BlockSpec((NG, GW, 1), full3),
                  pl.BlockSpec((8, BW), full2)],
        out_specs=[pl.BlockSpec((t, N_PIECE_COLS), row),
                   pl.BlockSpec((8, BW), full2), pl.BlockSpec((32, BW), full2), pl.BlockSpec((8, BW), full2),
                   pl.BlockSpec((NG, GW, GW), full3), pl.BlockSpec((NG, GW, GW), full3),
                   pl.BlockSpec((8, GW), full2)],
        out_shape=[SDS((n, IN_COLS), BF16), SDS((8, BW), F32), SDS((32, BW), F32), SDS((8, BW), F32),
                   SDS((NG, GW, GW), F32), SDS((NG, GW, GW), F32), SDS((8, GW), F32)],
        scratch_shapes=[pltpu.VMEM((t + HALO_POOL, BW), F32), pltpu.VMEM((t + HALO_CONV, BW), F32),
                        pltpu.VMEM((t + HALO_SC, BW), F32), pltpu.VMEM((NG, GW, GW), F32)],
        input_output_aliases={5: 0}, semantics=("arbitrary", "arbitrary"))


def _in_proj_bwd(d_proj, w_g, x2d, g_row, dxo, tm, comm=None):
    n = x2d.shape[0]
    half = D // 2

    def body(dp_ref, wu_ref, wl_ref, x_ref, g_ref, dxo_ref, dx_ref, dg_ref, acc):
        k = pl.program_id(1)

        @pl.when(jnp.logical_and(pl.program_id(0) == 0, k == 0))
        def _():
            dg_ref[...] = jnp.zeros_like(dg_ref)

        @pl.when(k == 0)
        def _():
            acc[...] = jnp.zeros_like(acc)

        dp = dp_ref[...]
        acc[:, :half] += _dot_nt(dp, wu_ref[...])
        acc[:, half:] += _dot_nt(dp, wl_ref[...])

        @pl.when(k == NDEV - 1)
        def _():
            dh = acc[...]
            x = x_ref[...]
            r = lax.rsqrt(jnp.mean(x * x, axis=-1, keepdims=True) + RMS_EPS)
            xn = x * r
            dg_ref[...] += _colsum(dh * xn)
            gy = dh * g_ref[...]
            dx_ref[...] = dxo_ref[...] + r * gy - xn * (r * jnp.mean(gy * xn, axis=-1, keepdims=True))

    return _carried_call(
        body, comm, (d_proj, *w_g, x2d, g_row, dxo), name="in_proj_bwd", grid=(n // tm, NDEV),
        in_specs=[pl.BlockSpec((tm, SHARD), lambda i, k: (i, k)),
                  pl.BlockSpec((None, half, SHARD), lambda i, k: (k, 0, 0)),
                  pl.BlockSpec((None, half, SHARD), lambda i, k: (k, 0, 0)),
                  pl.BlockSpec((tm, D), lambda i, k: (i, 0)),
                  pl.BlockSpec((1, D), lambda i, k: (0, 0)),
                  pl.BlockSpec((tm, D), lambda i, k: (i, 0))],
        out_specs=[pl.BlockSpec((tm, D), lambda i, k: (i, 0)), pl.BlockSpec((1, D), lambda i, k: (0, 0))],
        out_shape=[SDS((n, D), F32), SDS((1, D), F32)],
        scratch_shapes=[pltpu.VMEM((tm, D), F32)], semantics=("arbitrary", "arbitrary"))


def _all_gather(arrs, name):
    na = len(arrs)

    def body(*refs):
        ins, outs = refs[:na], refs[na:2 * na]
        send_sems, recv_sems, local_sems = refs[2 * na:]
        x, y, c = _position()
        me, sibling = (x, y, c), (x, y, 1 - c)
        chips = [(1 - x, y), (x, 1 - y), (1 - x, 1 - y)]

        def copy(a, k, block, to, from_input=False):
            px, py, pc = block
            dst = outs[a].at[4 * px + 2 * py + pc]
            return pltpu.make_async_remote_copy(
                src_ref=ins[a] if from_input else dst, dst_ref=dst,
                send_sem=send_sems.at[7 * a + k], recv_sem=recv_sems.at[7 * a + k],
                device_id=to, device_id_type=MESH)

        mine = [pltpu.make_async_copy(ins[a], outs[a].at[4 * x + 2 * y + c], local_sems.at[a]) for a in range(na)]
        for cp in mine:
            cp.start()
        first = []
        for a in range(na):
            first.append(copy(a, 0, me, sibling, from_input=True))
            first += [copy(a, 1 + j, me, (*chip, c), from_input=True) for j, chip in enumerate(chips)]
        for cp in first:
            cp.start()
        passed = []
        for j, chip in enumerate(chips):
            for a in range(na):
                copy(a, 1 + j, (*chip, c), me).wait_recv()
                fwd = copy(a, 4 + j, (*chip, c), sibling)
                fwd.start()
                passed.append(fwd)
        for a in range(na):
            copy(a, 0, sibling, me).wait_recv()
            for j, chip in enumerate(chips):
                copy(a, 4 + j, (*chip, 1 - c), me).wait_recv()
        for cp in first + passed:
            cp.wait_send()
        for cp in mine:
            cp.wait()

    any_spec = pl.BlockSpec(memory_space=pl.ANY)
    return pl.pallas_call(
        body, name=name,
        in_specs=[any_spec] * na, out_specs=[any_spec] * na,
        out_shape=[SDS((NDEV,) + a.shape, a.dtype) for a in arrs],
        scratch_shapes=[pltpu.SemaphoreType.DMA((7 * na,)), pltpu.SemaphoreType.DMA((7 * na,)),
                        pltpu.SemaphoreType.DMA((na,))],
    )(*arrs)


def _adamw_math(g, w, m, v):
    m = ADAM_B1 * m + (1.0 - ADAM_B1) * g
    v = ADAM_B2 * v + (1.0 - ADAM_B2) * (g * g)
    m_hat = m / (1.0 - ADAM_B1 ** ADAM_STEP)
    v_hat = v / (1.0 - ADAM_B2 ** ADAM_STEP)
    delta = -ADAM_LR * (m_hat / (jnp.sqrt(v_hat) + ADAM_EPS) + ADAM_WD * w)
    return delta, m, v


def _adamw_layer(own, landed, w, m, v, layer, earlier, tr, name):
    _, rows, cols = w.shape

    def body(o_ref, l_ref, w_ref, m_ref, v_ref, *rest):
        g_out, d_out, m_out, v_out = rest[-4:]
        g = o_ref[...]
        for r in range(NDEV - 1):
            g = g + l_ref[r].astype(F32)
        g_out[...] = g
        d_out[...], m_out[...], v_out[...] = _adamw_math(g, w_ref[...], m_ref[...], v_ref[...])

    lay = pl.BlockSpec((None, tr, cols), lambda i: (layer, i, 0))
    any_spec = pl.BlockSpec(memory_space=pl.ANY)
    prior = list(earlier) if earlier is not None else []
    return pl.pallas_call(
        body, name=name, grid=(rows // tr,),
        in_specs=[pl.BlockSpec((tr, cols), lambda i: (i, 0)),
                  pl.BlockSpec((NDEV - 1, tr, cols), lambda i: (0, i, 0)), lay, lay, lay] + [any_spec] * len(prior),
        out_specs=[lay] * 4, out_shape=[SDS(w.shape, F32)] * 4,
        input_output_aliases={5 + k: k for k in range(len(prior))},
        compiler_params=_params("parallel"),
    )(own, landed, w, m, v, *prior)


def _adamw_gathered(parts, w, m, v, tr):
    rows, cols = w.shape

    def body(p_ref, w_ref, m_ref, v_ref, g_out, d_out, m_out, v_out):
        g = p_ref[0]
        for k in range(1, NDEV):
            g = g + p_ref[k]
        g_out[...] = g
        d_out[...], m_out[...], v_out[...] = _adamw_math(g, w_ref[...], m_ref[...], v_ref[...])

    blk = pl.BlockSpec((tr, cols), lambda i: (i, 0))
    return pl.pallas_call(
        body, name="adamw_small", grid=(rows // tr,),
        in_specs=[pl.BlockSpec((NDEV, tr, cols), lambda i: (0, i, 0)), blk, blk, blk],
        out_specs=[blk] * 4, out_shape=[SDS((rows, cols), F32)] * 4,
        compiler_params=_params("parallel"),
    )(parts, w, m, v)


def _adamw_plain(g, w, m, v):
    def body(g_ref, w_ref, m_ref, v_ref, d_out, m_out, v_out):
        d_out[...], m_out[...], v_out[...] = _adamw_math(g_ref[...], w_ref[...], m_ref[...], v_ref[...])

    return pl.pallas_call(body, name="adamw_conv_shards", out_shape=[SDS(w.shape, F32)] * 3)(g, w, m, v)


SMALL = ("norm_g", "pool_w", "pool_scale", "conv_w", "conv_b", "conv_ln_g", "conv_ln_b",
         "sgu_ln_g", "sgu_ln_b", "sgu_w", "sgu_b", "sc_w")
SMALL_SHAPES = {"norm_g": (D,), "pool_w": (NG, GW, GW), "pool_scale": (BW,), "conv_w": (CONV_K, BW), "conv_b": (BW,),
                "conv_ln_g": (BW,), "conv_ln_b": (BW,), "sgu_ln_g": (BW,), "sgu_ln_b": (BW,),
                "sgu_w": (NG, GW, GW), "sgu_b": (NG, GW), "sc_w": (SHORT_K, BW)}
REPLICATED = tuple(k for k in SMALL if k not in ("conv_w", "sc_w"))
PACK_UNIT = 8 * 128


def _size(shape):
    out = 1
    for s in shape:
        out *= s
    return out


def _padded(a):
    flat = a.reshape(-1)
    pad = -flat.shape[0] % PACK_UNIT
    return jnp.pad(flat, (0, pad)) if pad else flat


def _pack(arrays):
    return jnp.concatenate([_padded(a) for a in arrays]).reshape(-1, 128)


def _unpack(pack, shapes):
    flat = pack.reshape(-1)
    out, off = [], 0
    for shape in shapes:
        size = _size(shape)
        out.append(flat[off:off + size].reshape(shape))
        off += size + (-size % PACK_UNIT)
    return out


def _gathered_weights(wb, wo, conv):
    wb = wb.reshape(NDEV, 4, BW, D // NDEV).transpose(1, 2, 0, 3).reshape(4, BW, D)
    wo = wo.reshape(D, D)
    cw = CONV_K * (BW // NDEV)
    sw = SHORT_K * (BW // NDEV)
    sc_at = cw + (-cw % PACK_UNIT)
    flat = conv.reshape(NDEV, -1)
    conv_w = flat[:, :cw].reshape(NDEV, CONV_K, BW // NDEV).transpose(1, 0, 2).reshape(CONV_K, BW)
    sc_w = flat[:, sc_at:sc_at + sw].reshape(NDEV, SHORT_K, BW // NDEV).transpose(1, 0, 2).reshape(SHORT_K, BW)
    return wb, wo, conv_w, sc_w


def _layer_small_inputs(l, rep, conv_w, sc_w):
    vecs = jnp.stack([rep["pool_scale"][l], rep["conv_b"][l], rep["conv_ln_g"][l], rep["conv_ln_b"][l],
                      rep["sgu_ln_g"][l], rep["sgu_ln_b"][l], jnp.zeros((BW,), F32), jnp.zeros((BW,), F32)])
    conv_w = jnp.pad(conv_w, ((0, 32 - CONV_K), (0, 0)))
    sc_w = jnp.pad(sc_w, ((0, 8 - SHORT_K), (0, 0)))
    return (rep["pool_w"][l], vecs, conv_w, rep["sgu_w"][l], rep["sgu_b"][l].reshape(NG, GW, 1), sc_w)


def kernel(x, norm_g, w_in, pool_w, pool_scale, conv_w, conv_b, conv_ln_g, conv_ln_b, sgu_ln_g, sgu_ln_b, sgu_w, sgu_b, sc_w, w_branch, w_o, final_g, loss_target, m_norm_g, m_w_in, m_pool_w, m_pool_scale, m_conv_w, m_conv_b, m_conv_ln_g, m_conv_ln_b, m_sgu_ln_g, m_sgu_ln_b, m_sgu_w, m_sgu_b, m_sc_w, m_w_branch, m_w_o, m_final_g, v_norm_g, v_w_in, v_pool_w, v_pool_scale, v_conv_w, v_conv_b, v_conv_ln_g, v_conv_ln_b, v_sgu_ln_g, v_sgu_ln_b, v_sgu_w, v_sgu_b, v_sc_w, v_w_branch, v_w_o, v_final_g):
    args = dict(locals())
    w = {k: args[k] for k in SMALL + ("w_in", "w_branch", "w_o", "final_g")}
    mom = {k: args["m_" + k] for k in w}
    var = {k: args["v_" + k] for k in w}
    rep = {k: w[k] for k in REPLICATED}
    xi, yi, ci = _position()
    me = 4 * xi + 2 * yi + ci

    nb, seq, _ = x.shape
    n = nb * seq
    tm = min(512, n)
    tm_in = min(1024, n)
    tk = min(2048, n)
    tm2 = min(256, n)
    t_fwd = min(256, seq)
    t_bwd = min(256, seq)

    w_in_b = [(w_in[l, :D // 2].astype(BF16), w_in[l, D // 2:].astype(BF16)) for l in range(DEPTH)]
    wb_b = [w_branch[l].astype(BF16).reshape(4 * BW, D // NDEV) for l in range(DEPTH)]
    wo_b = [w_o[l].astype(BF16) for l in range(DEPTH)]
    conv_b_ = [_pack([conv_w[l], sc_w[l]]) for l in range(DEPTH)]

    w_in_g = [_all_gather(list(w_in_b[0]), "w_in_all_gather"), None]
    x2d = x.reshape(n, D)
    saved = []
    ride = _Exchange(gathers=[wb_b[0], wo_b[0], conv_b_[0], w_in_b[1][0]])
    (proj, h), (wb_g, wo_g, conv_g, w_in_upper) = _in_proj(x2d, norm_g[0:1], w_in_g[0], tm_in, ride)
    for l in range(DEPTH):
        wb_l, wo_l, conv_l, sc_l = _gathered_weights(wb_g, wo_g, conv_g)
        small = _layer_small_inputs(l, rep, conv_l, sc_l)
        if l == 0:
            ride = _Exchange(gathers=[w_in_b[1][1]])
            (z, y_conv, q_sc, pooled), (w_in_lower,) = _mixers_fwd(proj, *small, nb, seq, t_fwd, ride)
            w_in_g[1] = [w_in_upper, w_in_lower]
            ride = _Exchange(gathers=[wb_b[1], wo_b[1], conv_b_[1]])
        else:
            (proj, h), _ = _in_proj(x2d, norm_g[l:l + 1], w_in_g[l], tm_in)
            (z, y_conv, q_sc, pooled), _ = _mixers_fwd(proj, *small, nb, seq, t_fwd)
            ride = None
        (x_new, merged), gathered = _merge_out(z, proj, x2d, wb_l, wo_l, tm2, ride)
        if l == 0:
            wb_g, wo_g, conv_g = gathered
        saved.append((x2d, proj, h, z, y_conv, q_sc, pooled, merged, small, wb_l, wo_l))
        x2d = x_new

    dx, loss_acc, d_final_g = _loss_head(x2d, loss_target.reshape(n, D), final_g.reshape(1, D), tm)
    loss = lax.psum(loss_acc[0, 0], ("x", "y", "c"))

    grads = {k: [None] * DEPTH for k in SMALL}
    big = {"w_in": None, "w_branch": None, "w_o": None}
    big_w = {"w_in": (w_in, m_w_in, v_w_in, (DEPTH, D, SHARD), 256),
             "w_branch": (w_branch, m_w_branch, v_w_branch, (DEPTH, 4 * BW, D // NDEV), 512),
             "w_o": (w_o, m_w_o, v_w_o, (DEPTH, D // NDEV, D), 64)}

    def adamw(k, own, landed, l):
        wk, mk, vk, shape, tr = big_w[k]
        big[k] = _adamw_layer(own, landed, wk.reshape(shape), mk.reshape(shape), vk.reshape(shape), l, big[k], tr,
                              "adamw_" + k)

    for l in reversed(range(DEPTH)):
        x_in, proj, h, z, y_conv, q_sc, pooled, merged, small, wb_l, wo_l = saved[l]
        dz, d_bo, d_proj = _merge_out_bwd(dx, z, proj, wb_l, wo_l, tm2)
        d_wo, d_wo_b = _matmul_tn(merged, dx, 1, D, D, False, tk, "grad_w_o")
        d_wb, d_wb_b = _matmul_tn(z, d_bo, 4, BW, D, True, tk, "grad_w_branch", by_device_cols=True)
        ride = _Exchange(scatters=[(d_wo_b.reshape(NDEV, D // NDEV, D), d_wo.reshape(NDEV, D // NDEV, D)),
                                   (d_wb_b.reshape(NDEV, 4 * BW, D // NDEV), d_wb.reshape(NDEV, 4 * BW, D // NDEV))])
        (d_proj, dvec, dconvw, dscw, dpoolw, dsguw, dsgub), (wo_landed, wo_own, wb_landed, wb_own) = _mixers_bwd(
            proj, dz, y_conv, q_sc, pooled, d_proj, *small, nb, seq, t_bwd, ride)
        d_w_in, d_w_in_b = _matmul_tn(h, d_proj, NDEV, D, SHARD, False, tk, "grad_w_in")
        ride = _Exchange(scatters=[(d_w_in_b, d_w_in)])
        (dx, d_norm_g), (in_landed, in_own) = _in_proj_bwd(d_proj, w_in_g[l], x_in, norm_g[l:l + 1], dx, tm_in, ride)
        adamw("w_o", wo_own, wo_landed, l)
        adamw("w_branch", wb_own, wb_landed, l)
        adamw("w_in", in_own, in_landed, l)
        grads["norm_g"][l] = d_norm_g[0]
        grads["pool_w"][l] = dpoolw
        grads["pool_scale"][l] = dvec[0]
        grads["conv_w"][l] = dconvw[:CONV_K]
        grads["conv_b"][l] = dvec[1]
        grads["conv_ln_g"][l] = dvec[2]
        grads["conv_ln_b"][l] = dvec[3]
        grads["sgu_ln_g"][l] = dvec[4]
        grads["sgu_ln_b"][l] = dvec[5]
        grads["sgu_w"][l] = dsguw
        grads["sgu_b"][l] = dsgub[:NG]
        grads["sc_w"][l] = dscw[:SHORT_K]
    grad_x = dx.reshape(nb, seq, D)
    small_grads = {k: jnp.stack(v) for k, v in grads.items()}

    names = list(SMALL) + ["final_g"]
    shapes = [(DEPTH,) + SMALL_SHAPES[k] for k in SMALL] + [(D,)]
    whole = REPLICATED + ("final_g",)
    part_pack = _pack([small_grads[k] for k in SMALL] + [d_final_g[0]])
    w_pack = _pack([w[k] if k in whole else jnp.zeros(s, F32) for k, s in zip(names, shapes)])
    m_pack = _pack([mom[k] if k in whole else jnp.zeros(s, F32) for k, s in zip(names, shapes)])
    v_pack = _pack([var[k] if k in whole else jnp.ones(s, F32) for k, s in zip(names, shapes)])
    (parts,) = _all_gather([part_pack], "small_grads_all_gather")
    g_pack, d_pack, nm_pack, nv_pack = _adamw_gathered(parts, w_pack, m_pack, v_pack, part_pack.shape[0] // 4)
    out_g = dict(zip(names, _unpack(g_pack, shapes)))
    out_d = dict(zip(names, _unpack(d_pack, shapes)))
    out_m = dict(zip(names, _unpack(nm_pack, shapes)))
    out_v = dict(zip(names, _unpack(nv_pack, shapes)))

    col = me * (BW // NDEV)
    for k in ("conv_w", "sc_w"):
        out_g[k] = lax.dynamic_slice_in_dim(out_g[k], col, BW // NDEV, axis=2)
    sharded_shapes = [w["conv_w"].shape, w["sc_w"].shape]
    d2, m2, v2 = _adamw_plain(_pack([out_g["conv_w"], out_g["sc_w"]]), _pack([w["conv_w"], w["sc_w"]]),
                              _pack([mom["conv_w"], mom["sc_w"]]), _pack([var["conv_w"], var["sc_w"]]))
    for dst, pk in ((out_d, d2), (out_m, m2), (out_v, v2)):
        dst["conv_w"], dst["sc_w"] = _unpack(pk, sharded_shapes)

    for k in big:
        out_g[k], out_d[k], out_m[k], out_v[k] = [o.reshape(w[k].shape) for o in big[k]]

    order = ("norm_g", "w_in", "pool_w", "pool_scale", "conv_w", "conv_b", "conv_ln_g", "conv_ln_b", "sgu_ln_g",
             "sgu_ln_b", "sgu_w", "sgu_b", "sc_w", "w_branch", "w_o", "final_g")
    return (loss, grad_x, *[out_g[k] for k in order], *[out_d[k] for k in order],
            *[out_m[k] for k in order], *[out_v[k] for k in order])
```

```python
import functools

import jax
import jax.numpy as jnp
from jax import lax
from jax.experimental import pallas as pl
from jax.experimental.pallas import tpu as pltpu

F32 = jnp.float32
BF16 = jnp.bfloat16
SDS = jax.ShapeDtypeStruct
MESH = pl.DeviceIdType.MESH

D = 1024
BW = 512
NG = 4
GW = 128
CONV_K = 31
SHORT_K = 3
POOL_WINDOWS = (2, 4, 8, 16)
N_PIECE_COLS = 12 * BW
IN_COLS = N_PIECE_COLS + 4 * D
NDEV = 8
SHARD = IN_COLS // NDEV
DEPTH = 2
RMS_EPS = 1e-6
LN_EPS = 1e-5
HALO_POOL, HALO_CONV, HALO_SC = 16, 32, 8

ADAM_LR = 0.001
ADAM_B1 = 0.9
ADAM_B2 = 0.999
ADAM_EPS = 1e-08
ADAM_WD = 0.01
ADAM_STEP = 10

VMEM_LIMIT = 56 * 1024 * 1024


def _params(*sem):
    return pltpu.CompilerParams(dimension_semantics=sem, vmem_limit_bytes=VMEM_LIMIT)


def _sigmoid(x):
    return jax.nn.sigmoid(x)


def _silu(x):
    return x * _sigmoid(x)


def _dsilu(x):
    s = _sigmoid(x)
    return s * (1.0 + x * (1.0 - s))


def _colsum(x):
    return jnp.sum(x, axis=0, keepdims=True)


def _dot(a, b):
    return jnp.dot(a, b, preferred_element_type=F32)


def _dot_nt(a, b):
    return lax.dot_general(a, b, (((1,), (1,)), ((), ())), preferred_element_type=F32)


def _dot_tn(a, b):
    return lax.dot_general(a, b, (((0,), (0,)), ((), ())), preferred_element_type=F32)


def _layer_norm_stats(y):
    mu = jnp.mean(y, axis=-1, keepdims=True)
    yc = y - mu
    var = jnp.mean(yc * yc, axis=-1, keepdims=True)
    rstd = lax.rsqrt(var + LN_EPS)
    return yc * rstd, rstd


def _layer_norm_bwd(d_hat, hat, rstd):
    return rstd * (d_hat - jnp.mean(d_hat, axis=-1, keepdims=True)
                   - hat * jnp.mean(d_hat * hat, axis=-1, keepdims=True))


def _position():
    return lax.axis_index("x"), lax.axis_index("y"), lax.axis_index("c")


def _flip(v, bit):
    return 1 - v if bit else v


class _Exchange:
    def __init__(self, gathers=(), scatters=()):
        self.gathers = list(gathers)
        self.scatters = list(scatters)
        self.inputs = self.gathers + [a for pair in self.scatters for a in pair]
        self.out_shapes = [SDS((NDEV,) + a.shape, a.dtype) for a in self.gathers]
        for send, keep in self.scatters:
            self.out_shapes += [SDS((NDEV - 1,) + send.shape[1:], send.dtype), SDS(keep.shape[1:], keep.dtype)]
        n = len(self.gathers) + len(self.scatters)
        self.scratch = [pltpu.SemaphoreType.DMA((7 * n,)), pltpu.SemaphoreType.DMA((7 * n,)),
                        pltpu.SemaphoreType.DMA((n,))]

    def _copies(self, ins, outs, sems):
        send_sems, recv_sems, local_sems = sems
        x, y, c = _position()
        me = 4 * x + 2 * y + c
        ng = len(self.gathers)
        local, remote = [], []
        for a in range(ng + len(self.scatters)):
            if a < ng:
                local.append(pltpu.make_async_copy(ins[a], outs[a].at[me], local_sems.at[a]))
            else:
                s = a - ng
                send, keep = ins[ng + 2 * s], ins[ng + 2 * s + 1]
                landed, own = outs[ng + 2 * s], outs[ng + 2 * s + 1]
                local.append(pltpu.make_async_copy(keep.at[me], own, local_sems.at[a]))
            for r in range(1, NDEV):
                peer = (_flip(x, r & 4), _flip(y, r & 2), _flip(c, r & 1))
                if a < ng:
                    src, dst = ins[a], outs[a].at[me]
                else:
                    src, dst = send.at[jnp.bitwise_xor(me, r)], landed.at[r - 1]
                remote.append(pltpu.make_async_remote_copy(
                    src_ref=src, dst_ref=dst, send_sem=send_sems.at[7 * a + r - 1],
                    recv_sem=recv_sems.at[7 * a + r - 1], device_id=peer, device_id_type=MESH))
        return local, remote

    def start(self, ins, outs, sems):
        local, remote = self._copies(ins, outs, sems)
        for cp in remote + local:
            cp.start()

    def finish(self, ins, outs, sems):
        local, remote = self._copies(ins, outs, sems)
        for cp in remote + local:
            cp.wait()


def _carried_call(body, comm, args, *, name, grid, in_specs, out_specs, out_shape, scratch_shapes=(),
                  semantics, input_output_aliases=None):
    aliases = input_output_aliases or {}
    if comm is None:
        out = pl.pallas_call(body, name=name, grid=grid, in_specs=in_specs, out_specs=out_specs, out_shape=out_shape,
                             scratch_shapes=list(scratch_shapes), input_output_aliases=aliases,
                             compiler_params=_params(*semantics))(*args)
        return list(out), []
    n_in, n_out, n_scr = len(in_specs), len(out_specs), len(scratch_shapes)
    c_in, c_out = len(comm.inputs), len(comm.out_shapes)

    def full_body(*refs):
        refs = list(refs)
        ins, refs = refs[:n_in], refs[n_in:]
        cins, refs = refs[:c_in], refs[c_in:]
        outs, refs = refs[:n_out], refs[n_out:]
        couts, refs = refs[:c_out], refs[c_out:]
        scr, sems = refs[:n_scr], refs[n_scr:]
        ids = [pl.program_id(d) for d in range(len(grid))]
        first = functools.reduce(jnp.logical_and, [i == 0 for i in ids])
        last = functools.reduce(jnp.logical_and, [i == g - 1 for i, g in zip(ids, grid)])

        @pl.when(first)
        def _():
            comm.start(cins, couts, sems)

        body(*ins, *outs, *scr)

        @pl.when(last)
        def _():
            comm.finish(cins, couts, sems)

    any_spec = pl.BlockSpec(memory_space=pl.ANY)
    out = pl.pallas_call(
        full_body, name=name, grid=grid,
        in_specs=list(in_specs) + [any_spec] * c_in, out_specs=list(out_specs) + [any_spec] * c_out,
        out_shape=list(out_shape) + comm.out_shapes, scratch_shapes=list(scratch_shapes) + comm.scratch,
        input_output_aliases=aliases, compiler_params=_params(*["arbitrary"] * len(grid)),
    )(*args, *comm.inputs)
    return list(out[:n_out]), list(out[n_out:])


def _in_proj(x2d, g_row, w_g, tm, comm=None):
    n = x2d.shape[0]
    half = D // 2

    def body(x_ref, g_ref, wu_ref, wl_ref, proj_ref, h_ref):
        @pl.when(pl.program_id(1) == 0)
        def _():
            x = x_ref[...]
            r = lax.rsqrt(jnp.mean(x * x, axis=-1, keepdims=True) + RMS_EPS)
            h_ref[...] = (x * r * g_ref[...]).astype(BF16)

        proj_ref[...] = _dot(h_ref[:, :half], wu_ref[...]) + _dot(h_ref[:, half:], wl_ref[...])

    return _carried_call(
        body, comm, (x2d, g_row, *w_g), name="in_proj", grid=(n // tm, NDEV),
        in_specs=[pl.BlockSpec((tm, D), lambda i, j: (i, 0)),
                  pl.BlockSpec((1, D), lambda i, j: (0, 0)),
                  pl.BlockSpec((None, half, SHARD), lambda i, j: (j, 0, 0)),
                  pl.BlockSpec((None, half, SHARD), lambda i, j: (j, 0, 0))],
        out_specs=[pl.BlockSpec((tm, SHARD), lambda i, j: (i, j)),
                   pl.BlockSpec((tm, D), lambda i, j: (i, 0))],
        out_shape=[SDS((n, IN_COLS), F32), SDS((n, D), BF16)],
        semantics=("parallel", "arbitrary"))


def _tril_bf16(w):
    r = lax.broadcasted_iota(jnp.int32, (GW, GW), 0)
    c = lax.broadcasted_iota(jnp.int32, (GW, GW), 1)
    return jnp.where(r >= c, w, 0.0).astype(BF16)


def _mixers_fwd(proj, pool_w, vecs, conv_w, sgu_w, sgu_bc, sc_w, nb, seq, t, comm=None):
    n = nb * seq
    nt = seq // t

    def body(p_ref, poolw_ref, vec_ref, convw_ref, sguw_ref, sgub_ref, scw_ref,
             z_ref, y_ref, q_ref, pooled_ref, pext, uext, vext):
        i = pl.program_id(1)

        @pl.when(i == 0)
        def _():
            pext[0:HALO_POOL, :] = jnp.zeros((HALO_POOL, BW), F32)
            uext[0:HALO_CONV, :] = jnp.zeros((HALO_CONV, BW), F32)
            vext[0:HALO_SC, :] = jnp.zeros((HALO_SC, BW), F32)

        @pl.when(i > 0)
        def _():
            pext[0:HALO_POOL, :] = pext[t:t + HALO_POOL, :]
            uext[0:HALO_CONV, :] = uext[t:t + HALO_CONV, :]
            vext[0:HALO_SC, :] = vext[t:t + HALO_SC, :]

        def piece(k):
            return p_ref[:, k * BW:(k + 1) * BW]

        pos = lax.broadcasted_iota(jnp.int32, (t, 1), 0) + i * t

        px = piece(0)
        pext[HALO_POOL:, :] = px
        for j, win in enumerate(POOL_WINDOWS):
            cols = slice(j * GW, (j + 1) * GW)
            s = pext[:, cols]
            step = 1
            while step < win:
                s = s + pltpu.roll(s, step, 0)
                step *= 2
            cnt = jnp.minimum(pos + 1, win).astype(F32)
            pooled = (s[HALO_POOL:, :] / cnt - px[:, cols]).astype(BF16)
            pooled_ref[:, cols] = pooled
            mixed = _dot(pooled, poolw_ref[j].astype(BF16))
            z_ref[:, cols] = (mixed * vec_ref[0:1, cols] * _silu(p_ref[:, BW + j * GW:BW + (j + 1) * GW])).astype(BF16)

        uext[HALO_CONV:, :] = piece(2) * _sigmoid(piece(3))
        ue = uext[...]
        acc = jnp.zeros((t, BW), F32)
        for b in range(8):
            rb = ue if b == 0 else pltpu.roll(ue, b, 0)
            for a in range((CONV_K - b + 7) // 8):
                j = 8 * a + b
                acc = acc + convw_ref[CONV_K - 1 - j:CONV_K - j, :] * rb[HALO_CONV - 8 * a:HALO_CONV - 8 * a + t, :]
        y = acc + vec_ref[1:2, :]
        y_ref[...] = y
        yhat, _ = _layer_norm_stats(y)
        act = _silu(yhat * vec_ref[2:3, :] + vec_ref[3:4, :])
        z_ref[:, BW:2 * BW] = (act * _silu(piece(4))).astype(BF16)

        vhat, _ = _layer_norm_stats(piece(6))
        vn = (vhat * vec_ref[4:5, :] + vec_ref[5:6, :]).astype(BF16)
        for g in range(NG):
            ws = _tril_bf16(sguw_ref[g])
            cols = slice(g * GW, (g + 1) * GW)
            for c in range(t // GW):
                rows = slice(c * GW, (c + 1) * GW)
                sp = _dot(ws, vn[rows, cols]) + sgub_ref[g]
                gate = _silu(p_ref[rows, 7 * BW + g * GW:7 * BW + (g + 1) * GW])
                z_ref[rows, 2 * BW + g * GW:2 * BW + (g + 1) * GW] = (
                    p_ref[rows, 5 * BW + g * GW:5 * BW + (g + 1) * GW] * sp * gate).astype(BF16)

        vext[HALO_SC:, :] = piece(9) * piece(10)
        ve = vext[...]
        q = jnp.zeros((t, BW), F32)
        for j in range(SHORT_K):
            sh = ve if j == 0 else pltpu.roll(ve, j, 0)
            q = q + scw_ref[SHORT_K - 1 - j:SHORT_K - j, :] * sh[HALO_SC:, :]
        q_ref[...] = q
        z_ref[:, 3 * BW:] = (piece(8) * q * _silu(piece(11))).astype(BF16)

    row = lambda b, i: (b * nt + i, 0)
    full2 = lambda b, i: (0, 0)
    full3 = lambda b, i: (0, 0, 0)
    return _carried_call(
        body, comm, (proj, pool_w, vecs, conv_w, sgu_w, sgu_bc, sc_w), name="mixers_fwd", grid=(nb, nt),
        in_specs=[pl.BlockSpec((t, N_PIECE_COLS), row),
                  pl.BlockSpec((NG, GW, GW), full3),
                  pl.BlockSpec((8, BW), full2),
                  pl.BlockSpec((32, BW), full2),
                  pl.BlockSpec((NG, GW, GW), full3),
                  pl.BlockSpec((NG, GW, 1), full3),
                  pl.BlockSpec((8, BW), full2)],
        out_specs=[pl.BlockSpec((t, 4 * BW), row), pl.BlockSpec((t, BW), row),
                   pl.BlockSpec((t, BW), row), pl.BlockSpec((t, BW), row)],
        out_shape=[SDS((n, 4 * BW), BF16), SDS((n, BW), F32), SDS((n, BW), F32), SDS((n, BW), BF16)],
        scratch_shapes=[pltpu.VMEM((t + HALO_POOL, BW), F32), pltpu.VMEM((t + HALO_CONV, BW), F32),
                        pltpu.VMEM((t + HALO_SC, BW), F32)],
        semantics=("arbitrary", "arbitrary"))


def _merge_out(z, proj, x2d, wb, wo, tm, comm=None):
    n = x2d.shape[0]

    def body(z_ref, mga_ref, mgb_ref, x_ref, wb_ref, wo_ref, xo_ref, m_ref):
        merged = jnp.zeros((tm, D), F32)
        for k in range(4):
            bo = _dot(z_ref[:, k * BW:(k + 1) * BW], wb_ref[k])
            mg_ref = mga_ref if k < 2 else mgb_ref
            merged = merged + _sigmoid(mg_ref[:, (k % 2) * D:(k % 2 + 1) * D]) * bo
        mb = merged.astype(BF16)
        m_ref[...] = mb
        xo_ref[...] = x_ref[...] + _dot(mb, wo_ref[...])

    return _carried_call(
        body, comm, (z, proj, proj, x2d, wb, wo), name="merge_out", grid=(n // tm,),
        in_specs=[pl.BlockSpec((tm, 4 * BW), lambda i: (i, 0)),
                  pl.BlockSpec((tm, 2 * D), lambda i: (i, 3)),
                  pl.BlockSpec((tm, 2 * D), lambda i: (i, 4)),
                  pl.BlockSpec((tm, D), lambda i: (i, 0)),
                  pl.BlockSpec((4, BW, D), lambda i: (0, 0, 0)),
                  pl.BlockSpec((D, D), lambda i: (0, 0))],
        out_specs=[pl.BlockSpec((tm, D), lambda i: (i, 0)), pl.BlockSpec((tm, D), lambda i: (i, 0))],
        out_shape=[SDS((n, D), F32), SDS((n, D), BF16)], semantics=("parallel",))


def _loss_head(x2d, tgt2d, g_row, tm):
    n = x2d.shape[0]

    def body(x_ref, t_ref, g_ref, dx_ref, loss_ref, dg_ref):
        @pl.when(pl.program_id(0) == 0)
        def _():
            loss_ref[...] = jnp.zeros_like(loss_ref)
            dg_ref[...] = jnp.zeros_like(dg_ref)

        x = x_ref[...]
        g = g_ref[...]
        r = lax.rsqrt(jnp.mean(x * x, axis=-1, keepdims=True) + RMS_EPS)
        xn = x * r
        e = xn * g - t_ref[...]
        loss_ref[...] += 0.5 * jnp.sum(jnp.mean(e * e, axis=-1, keepdims=True))
        dy = e * (1.0 / D)
        dg_ref[...] += _colsum(dy * xn)
        gy = dy * g
        dx_ref[...] = r * gy - xn * (r * jnp.mean(gy * xn, axis=-1, keepdims=True))

    return pl.pallas_call(
        body, name="loss_head", grid=(n // tm,),
        in_specs=[pl.BlockSpec((tm, D), lambda i: (i, 0)), pl.BlockSpec((tm, D), lambda i: (i, 0)),
                  pl.BlockSpec((1, D), lambda i: (0, 0))],
        out_specs=[pl.BlockSpec((tm, D), lambda i: (i, 0)), pl.BlockSpec((8, 128), lambda i: (0, 0)),
                   pl.BlockSpec((1, D), lambda i: (0, 0))],
        out_shape=[SDS((n, D), F32), SDS((8, 128), F32), SDS((1, D), F32)],
        compiler_params=_params("arbitrary"),
    )(x2d, tgt2d, g_row)


def _merge_out_bwd(dxo, z, proj, merged, wb, wo, tm):
    n = dxo.shape[0]
    ni = n // tm
    sub = D // NDEV

    def body(dx_ref, z_ref, mg_ref, m_ref, wb_ref, wo_ref,
             dz_ref, dp_ref, gwo_ref, gwo_b_ref, gwb_ref, gwb_b_ref,
             dm_ref, acc_wo, acc_wb, half_wo, half_wb, sems):
        i = pl.program_id(0)
        h = pl.program_id(1)

        @pl.when(jnp.logical_and(i == 0, h == 0))
        def _():
            acc_wo[...] = jnp.zeros_like(acc_wo)
            acc_wb[...] = jnp.zeros_like(acc_wb)

        @pl.when(h == 0)
        def _():
            dxb = dx_ref[...].astype(BF16)
            dm_ref[...] = _dot_nt(dxb, wo_ref[...])
            acc_wo[...] += _dot_tn(m_ref[...], dxb)

        dm = dm_ref[...]
        for k in range(2):
            zk = z_ref[:, k * BW:(k + 1) * BW]
            w = wb_ref[2 * h + k]
            bo = _dot(zk, w)
            sig = _sigmoid(mg_ref[:, k * D:(k + 1) * D])
            dbo = (dm * sig).astype(BF16)
            dp_ref[:, k * D:(k + 1) * D] = (dm * bo * sig * (1.0 - sig)).astype(BF16)
            dz_ref[:, k * BW:(k + 1) * BW] = _dot_nt(dbo, w)
            acc_wb[2 * h + k] += _dot_tn(zk, dbo)

        @pl.when(jnp.logical_and(i == ni - 1, h == 1))
        def _():
            half_wo[...] = acc_wo[...].astype(BF16)
            half_wb[...] = acc_wb[...].astype(BF16)
            copies = [pltpu.make_async_copy(acc_wo, gwo_ref, sems.at[0]),
                      pltpu.make_async_copy(half_wo, gwo_b_ref, sems.at[1])]
            for d in range(NDEV):
                cols = pl.ds(d * sub, sub)
                copies.append(pltpu.make_async_copy(acc_wb.at[:, :, cols], gwb_ref.at[d], sems.at[2 + 2 * d]))
                copies.append(pltpu.make_async_copy(half_wb.at[:, :, cols], gwb_b_ref.at[d], sems.at[3 + 2 * d]))
            for cp in copies:
                cp.start()
            for cp in copies:
                cp.wait()

    any_spec = pl.BlockSpec(memory_space=pl.ANY)
    return pl.pallas_call(
        body, name="merge_out_bwd", grid=(ni, 2),
        in_specs=[pl.BlockSpec((tm, D), lambda i, h: (i, 0)),
                  pl.BlockSpec((tm, 2 * BW), lambda i, h: (i, h)),
                  pl.BlockSpec((tm, 2 * D), lambda i, h: (i, 3 + h)),
                  pl.BlockSpec((tm, D), lambda i, h: (i, 0)),
                  pl.BlockSpec((4, BW, D), lambda i, h: (0, 0, 0)),
                  pl.BlockSpec((D, D), lambda i, h: (0, 0))],
        out_specs=[pl.BlockSpec((tm, 2 * BW), lambda i, h: (i, h)),
                   pl.BlockSpec((tm, 2 * D), lambda i, h: (i, 3 + h)),
                   any_spec, any_spec, any_spec, any_spec],
        out_shape=[SDS((n, 4 * BW), F32), SDS((n, IN_COLS), BF16), SDS((D, D), F32), SDS((D, D), BF16),
                   SDS((NDEV, 4, BW, sub), F32), SDS((NDEV, 4, BW, sub), BF16)],
        scratch_shapes=[pltpu.VMEM((tm, D), F32), pltpu.VMEM((D, D), F32), pltpu.VMEM((4, BW, D), F32),
                        pltpu.VMEM((D, D), BF16), pltpu.VMEM((4, BW, D), BF16),
                        pltpu.SemaphoreType.DMA((2 + 2 * NDEV,))],
        compiler_params=_params("arbitrary", "arbitrary"),
    )(dxo, z, proj, merged, wb, wo)


def _grad_w_in(h, d_proj, tk):
    nk = h.shape[0] // tk

    def body(h_ref, dp_ref, o_ref, ob_ref):
        k = pl.program_id(1)

        @pl.when(k == 0)
        def _():
            o_ref[...] = jnp.zeros_like(o_ref)

        o_ref[...] += _dot_tn(h_ref[...], dp_ref[...])

        @pl.when(k == nk - 1)
        def _():
            ob_ref[...] = o_ref[...].astype(BF16)

    out_spec = pl.BlockSpec((None, D, SHARD), lambda j, k: (j, 0, 0))
    return pl.pallas_call(
        body, name="grad_w_in", grid=(NDEV, nk),
        in_specs=[pl.BlockSpec((tk, D), lambda j, k: (k, 0)), pl.BlockSpec((tk, SHARD), lambda j, k: (k, j))],
        out_specs=[out_spec, out_spec], out_shape=[SDS((NDEV, D, SHARD), F32), SDS((NDEV, D, SHARD), BF16)],
        compiler_params=_params("parallel", "arbitrary"),
    )(h, d_proj)


def _mixers_bwd(proj, dz, y_conv, q_sc, pooled, d_proj, pool_w, vecs, conv_w, sgu_w, sgu_bc, sc_w, nb, seq, t, comm=None):
    n = nb * seq
    nt = seq // t

    def body(p_ref, dz_ref, y_ref, q_ref, pooled_ref, dpin_ref,
             poolw_ref, vec_ref, convw_ref, sguw_ref, sgub_ref, scw_ref,
             dp_ref, dvec_ref, dconvw_ref, dscw_ref, dpoolw_ref, dsguw_ref, dsgub_ref,
             rext, dyext, dqext, bacc):
        b = pl.program_id(0)
        i = pl.program_id(1)
        first = jnp.logical_and(b == 0, i == 0)
        last = jnp.logical_and(b == nb - 1, i == nt - 1)

        @pl.when(first)
        def _():
            dvec_ref[...] = jnp.zeros_like(dvec_ref)
            dconvw_ref[...] = jnp.zeros_like(dconvw_ref)
            dscw_ref[...] = jnp.zeros_like(dscw_ref)
            dpoolw_ref[...] = jnp.zeros_like(dpoolw_ref)
            dsguw_ref[...] = jnp.zeros_like(dsguw_ref)
            dsgub_ref[...] = jnp.zeros_like(dsgub_ref)
            bacc[...] = jnp.zeros_like(bacc)

        @pl.when(i == 0)
        def _():
            rext[t:, :] = jnp.zeros((HALO_POOL, BW), F32)
            dyext[t:, :] = jnp.zeros((HALO_CONV, BW), F32)
            dqext[t:, :] = jnp.zeros((HALO_SC, BW), F32)

        @pl.when(i > 0)
        def _():
            rext[t:, :] = rext[0:HALO_POOL, :]
            dyext[t:, :] = dyext[0:HALO_CONV, :]
            dqext[t:, :] = dqext[0:HALO_SC, :]

        def piece(k):
            return p_ref[:, k * BW:(k + 1) * BW]

        def put(k, v):
            dp_ref[:, k * BW:(k + 1) * BW] = v.astype(BF16)

        pos = lax.broadcasted_iota(jnp.int32, (t, 1), 0) + (nt - 1 - i) * t

        dzp = dz_ref[:, 0:BW]
        pg = piece(1)
        d_pm = dzp * _silu(pg)
        for j, win in enumerate(POOL_WINDOWS):
            cols = slice(j * GW, (j + 1) * GW)
            pw = poolw_ref[j].astype(BF16)
            pooled = pooled_ref[:, cols]
            mixed = _dot(pooled, pw)
            scale = vec_ref[0:1, cols]
            dvec_ref[0:1, cols] += _colsum(d_pm[:, cols] * mixed)
            dp_ref[:, BW + j * GW:BW + (j + 1) * GW] = (dzp[:, cols] * (mixed * scale) * _dsilu(pg[:, cols])).astype(BF16)
            d_mixed = (d_pm[:, cols] * scale).astype(BF16)
            dpoolw_ref[j] += _dot_tn(pooled, d_mixed)
            d_pooled = _dot_nt(d_mixed, pw)
            cnt = jnp.minimum(pos + 1, win).astype(F32)
            rext[0:t, cols] = d_pooled / cnt
            s = rext[:, cols]
            step = 1
            while step < win:
                s = s + pltpu.roll(s, t + HALO_POOL - step, 0)
                step *= 2
            dp_ref[:, cols] = (s[0:t, :] - d_pooled).astype(BF16)

        dzc = dz_ref[:, BW:2 * BW]
        cgate = piece(4)
        yhat, rstd = _layer_norm_stats(y_ref[...])
        ln = yhat * vec_ref[2:3, :] + vec_ref[3:4, :]
        put(4, dzc * _silu(ln) * _dsilu(cgate))
        d_ln = dzc * _silu(cgate) * _dsilu(ln)
        dvec_ref[2:3, :] += _colsum(d_ln * yhat)
        dvec_ref[3:4, :] += _colsum(d_ln)
        dy = _layer_norm_bwd(d_ln * vec_ref[2:3, :], yhat, rstd)
        dvec_ref[1:2, :] += _colsum(dy)
        dyext[0:t, :] = dy
        dye = dyext[...]
        ca = piece(2)
        sig_cb = _sigmoid(piece(3))
        u = ca * sig_cb
        du = jnp.zeros((t, BW), F32)
        for b in range(8):
            rb = dye if b == 0 else pltpu.roll(dye, t + HALO_CONV - b, 0)
            for a in range((CONV_K - b + 7) // 8):
                j = 8 * a + b
                sh = rb[8 * a:8 * a + t, :]
                du = du + convw_ref[CONV_K - 1 - j:CONV_K - j, :] * sh
                dconvw_ref[CONV_K - 1 - j:CONV_K - j, :] += _colsum(u * sh)
        put(2, du * sig_cb)
        put(3, du * u * (1.0 - sig_cb))

        dzg = dz_ref[:, 2 * BW:3 * BW]
        gu = piece(5)
        ggate = piece(7)
        vhat, vrstd = _layer_norm_stats(piece(6))
        vn = (vhat * vec_ref[4:5, :] + vec_ref[5:6, :]).astype(BF16)
        t1 = dzg * _silu(ggate)
        d_sp_all = t1 * gu
        d_vn_cols = []
        sp_cols = []
        for g in range(NG):
            ws = _tril_bf16(sguw_ref[g])
            cols = slice(g * GW, (g + 1) * GW)
            d_vn_rows = []
            sp_rows = []
            for c in range(t // GW):
                rows = slice(c * GW, (c + 1) * GW)
                sp_rows.append(_dot(ws, vn[rows, cols]) + sgub_ref[g])
                d_sp = d_sp_all[rows, cols]
                bacc[g] += d_sp
                d_spb = d_sp.astype(BF16)
                d_vn_rows.append(_dot_tn(ws, d_spb))
                dsguw_ref[g] += _dot_nt(d_spb, vn[rows, cols])
            d_vn_cols.append(jnp.concatenate(d_vn_rows, axis=0))
            sp_cols.append(jnp.concatenate(sp_rows, axis=0))
        d_vn = jnp.concatenate(d_vn_cols, axis=1)
        sp = jnp.concatenate(sp_cols, axis=1)
        put(5, t1 * sp)
        put(7, dzg * gu * sp * _dsilu(ggate))
        dvec_ref[4:5, :] += _colsum(d_vn * vhat)
        dvec_ref[5:6, :] += _colsum(d_vn)
        put(6, _layer_norm_bwd(d_vn * vec_ref[4:5, :], vhat, vrstd))

        dzs = dz_ref[:, 3 * BW:]
        sb = piece(8)
        scp = piece(9)
        sx = piece(10)
        sgate = piece(11)
        q = q_ref[...]
        t2 = dzs * _silu(sgate)
        put(8, t2 * q)
        put(11, dzs * sb * q * _dsilu(sgate))
        dqext[0:t, :] = t2 * sb
        dqe = dqext[...]
        vv = scp * sx
        dv = jnp.zeros((t, BW), F32)
        for j in range(SHORT_K):
            sh = (dqe if j == 0 else pltpu.roll(dqe, t + HALO_SC - j, 0))[0:t, :]
            dv = dv + scw_ref[SHORT_K - 1 - j:SHORT_K - j, :] * sh
            dscw_ref[SHORT_K - 1 - j:SHORT_K - j, :] += _colsum(vv * sh)
        put(9, dv * sx)
        put(10, dv * scp)

        @pl.when(last)
        def _():
            r = lax.broadcasted_iota(jnp.int32, (GW, GW), 0)
            c = lax.broadcasted_iota(jnp.int32, (GW, GW), 1)
            for g in range(NG):
                dsguw_ref[g] = jnp.where(r >= c, dsguw_ref[g], 0.0)
                dsgub_ref[g:g + 1, :] = _colsum(bacc[g].T)

    row = lambda b, i: (b * nt + (nt - 1 - i), 0)
    full2 = lambda b, i: (0, 0)
    full3 = lambda b, i: (0, 0, 0)
    return _carried_call(
        body, comm, (proj, dz, y_conv, q_sc, pooled, d_proj, pool_w, vecs, conv_w, sgu_w, sgu_bc, sc_w),
        name="mixers_bwd", grid=(nb, nt),
        in_specs=[pl.BlockSpec((t, N_PIECE_COLS), row),
                  pl.BlockSpec((t, 4 * BW), row),
                  pl.BlockSpec((t, BW), row), pl.BlockSpec((t, BW), row), pl.BlockSpec((t, BW), row),
                  pl.BlockSpec(memory_space=pl.ANY),
                  pl.BlockSpec((NG, GW, GW), full3),
                  pl.BlockSpec((8, BW), full2),
                  pl.BlockSpec((32, BW), full2),
                  pl.BlockSpec((NG, GW, GW), full3),
                  pl.BlockSpec((NG, GW, 1), full3),
                  pl.BlockSpec((8, BW), full2)],
        out_specs=[pl.BlockSpec((t, N_PIECE_COLS), row),
                   pl.BlockSpec((8, BW), full2), pl.BlockSpec((32, BW), full2), pl.BlockSpec((8, BW), full2),
                   pl.BlockSpec((NG, GW, GW), full3), pl.BlockSpec((NG, GW, GW), full3),
                   pl.BlockSpec((8, GW), full2)],
        out_shape=[SDS((n, IN_COLS), BF16), SDS((8, BW), F32), SDS((32, BW), F32), SDS((8, BW), F32),
                   SDS((NG, GW, GW), F32), SDS((NG, GW, GW), F32), SDS((8, GW), F32)],
        scratch_shapes=[pltpu.VMEM((t + HALO_POOL, BW), F32), pltpu.VMEM((t + HALO_CONV, BW), F32),
                        pltpu.VMEM((t + HALO_SC, BW), F32), pltpu.VMEM((NG, GW, GW), F32)],
        input_output_aliases={5: 0}, semantics=("arbitrary", "arbitrary"))


def _in_proj_bwd(d_proj, w_g, x2d, g_row, dxo, tm, comm=None):
    n = x2d.shape[0]
    half = D // 2

    def body(dp_ref, wu_ref, wl_ref, x_ref, g_ref, dxo_ref, dx_ref, dg_ref, acc):
        k = pl.program_id(1)

        @pl.when(jnp.logical_and(pl.program_id(0) == 0, k == 0))
        def _():
            dg_ref[...] = jnp.zeros_like(dg_ref)

        @pl.when(k == 0)
        def _():
            acc[...] = jnp.zeros_like(acc)

        dp = dp_ref[...]
        acc[:, :half] += _dot_nt(dp, wu_ref[...])
        acc[:, half:] += _dot_nt(dp, wl_ref[...])

        @pl.when(k == NDEV - 1)
        def _():
            dh = acc[...]
            x = x_ref[...]
            r = lax.rsqrt(jnp.mean(x * x, axis=-1, keepdims=True) + RMS_EPS)
            xn = x * r
            dg_ref[...] += _colsum(dh * xn)
            gy = dh * g_ref[...]
            dx_ref[...] = dxo_ref[...] + r * gy - xn * (r * jnp.mean(gy * xn, axis=-1, keepdims=True))

    return _carried_call(
        body, comm, (d_proj, *w_g, x2d, g_row, dxo), name="in_proj_bwd", grid=(n // tm, NDEV),
        in_specs=[pl.BlockSpec((tm, SHARD), lambda i, k: (i, k)),
                  pl.BlockSpec((None, half, SHARD), lambda i, k: (k, 0, 0)),
                  pl.BlockSpec((None, half, SHARD), lambda i, k: (k, 0, 0)),
                  pl.BlockSpec((tm, D), lambda i, k: (i, 0)),
                  pl.BlockSpec((1, D), lambda i, k: (0, 0)),
                  pl.BlockSpec((tm, D), lambda i, k: (i, 0))],
        out_specs=[pl.BlockSpec((tm, D), lambda i, k: (i, 0)), pl.BlockSpec((1, D), lambda i, k: (0, 0))],
        out_shape=[SDS((n, D), F32), SDS((1, D), F32)],
        scratch_shapes=[pltpu.VMEM((tm, D), F32)], semantics=("arbitrary", "arbitrary"))


def _all_gather(arrs, name):
    na = len(arrs)

    def body(*refs):
        ins, outs = refs[:na], refs[na:2 * na]
        send_sems, recv_sems, local_sems = refs[2 * na:]
        x, y, c = _position()
        me, sibling = (x, y, c), (x, y, 1 - c)
        chips = [(1 - x, y), (x, 1 - y), (1 - x, 1 - y)]

        def copy(a, k, block, to, from_input=False):
            px, py, pc = block
            dst = outs[a].at[4 * px + 2 * py + pc]
            return pltpu.make_async_remote_copy(
                src_ref=ins[a] if from_input else dst, dst_ref=dst,
                send_sem=send_sems.at[7 * a + k], recv_sem=recv_sems.at[7 * a + k],
                device_id=to, device_id_type=MESH)

        mine = [pltpu.make_async_copy(ins[a], outs[a].at[4 * x + 2 * y + c], local_sems.at[a]) for a in range(na)]
        for cp in mine:
            cp.start()
        first = []
        for a in range(na):
            first.append(copy(a, 0, me, sibling, from_input=True))
            first += [copy(a, 1 + j, me, (*chip, c), from_input=True) for j, chip in enumerate(chips)]
        for cp in first:
            cp.start()
        passed = []
        for j, chip in enumerate(chips):
            for a in range(na):
                copy(a, 1 + j, (*chip, c), me).wait_recv()
                fwd = copy(a, 4 + j, (*chip, c), sibling)
                fwd.start()
                passed.append(fwd)
        for a in range(na):
            copy(a, 0, sibling, me).wait_recv()
            for j, chip in enumerate(chips):
                copy(a, 4 + j, (*chip, 1 - c), me).wait_recv()
        for cp in first + passed:
            cp.wait_send()
        for cp in mine:
            cp.wait()

    any_spec = pl.BlockSpec(memory_space=pl.ANY)
    return pl.pallas_call(
        body, name=name,
        in_specs=[any_spec] * na, out_specs=[any_spec] * na,
        out_shape=[SDS((NDEV,) + a.shape, a.dtype) for a in arrs],
        scratch_shapes=[pltpu.SemaphoreType.DMA((7 * na,)), pltpu.SemaphoreType.DMA((7 * na,)),
                        pltpu.SemaphoreType.DMA((na,))],
    )(*arrs)


def _adamw_math(g, w, m, v):
    m = ADAM_B1 * m + (1.0 - ADAM_B1) * g
    v = ADAM_B2 * v + (1.0 - ADAM_B2) * (g * g)
    m_hat = m / (1.0 - ADAM_B1 ** ADAM_STEP)
    v_hat = v / (1.0 - ADAM_B2 ** ADAM_STEP)
    delta = -ADAM_LR * (m_hat / (jnp.sqrt(v_hat) + ADAM_EPS) + ADAM_WD * w)
    return delta, m, v


def _adamw_layer(own, landed, w, m, v, layer, earlier, tr, name):
    _, rows, cols = w.shape

    def body(o_ref, l_ref, w_ref, m_ref, v_ref, *rest):
        g_out, d_out, m_out, v_out = rest[-4:]
        g = o_ref[...]
        for r in range(NDEV - 1):
            g = g + l_ref[r].astype(F32)
        g_out[...] = g
        d_out[...], m_out[...], v_out[...] = _adamw_math(g, w_ref[...], m_ref[...], v_ref[...])

    lay = pl.BlockSpec((None, tr, cols), lambda i: (layer, i, 0))
    any_spec = pl.BlockSpec(memory_space=pl.ANY)
    prior = list(earlier) if earlier is not None else []
    return pl.pallas_call(
        body, name=name, grid=(rows // tr,),
        in_specs=[pl.BlockSpec((tr, cols), lambda i: (i, 0)),
                  pl.BlockSpec((NDEV - 1, tr, cols), lambda i: (0, i, 0)), lay, lay, lay] + [any_spec] * len(prior),
        out_specs=[lay] * 4, out_shape=[SDS(w.shape, F32)] * 4,
        input_output_aliases={5 + k: k for k in range(len(prior))},
        compiler_params=_params("parallel"),
    )(own, landed, w, m, v, *prior)


def _adamw_gathered(parts, w, m, v, tr):
    rows, cols = w.shape

    def body(p_ref, w_ref, m_ref, v_ref, g_out, d_out, m_out, v_out):
        g = p_ref[0]
        for k in range(1, NDEV):
            g = g + p_ref[k]
        g_out[...] = g
        d_out[...], m_out[...], v_out[...] = _adamw_math(g, w_ref[...], m_ref[...], v_ref[...])

    blk = pl.BlockSpec((tr, cols), lambda i: (i, 0))
    return pl.pallas_call(
        body, name="adamw_small", grid=(rows // tr,),
        in_specs=[pl.BlockSpec((NDEV, tr, cols), lambda i: (0, i, 0)), blk, blk, blk],
        out_specs=[blk] * 4, out_shape=[SDS((rows, cols), F32)] * 4,
        compiler_params=_params("parallel"),
    )(parts, w, m, v)


def _adamw_plain(g, w, m, v):
    def body(g_ref, w_ref, m_ref, v_ref, d_out, m_out, v_out):
        d_out[...], m_out[...], v_out[...] = _adamw_math(g_ref[...], w_ref[...], m_ref[...], v_ref[...])

    return pl.pallas_call(body, name="adamw_conv_shards", out_shape=[SDS(w.shape, F32)] * 3)(g, w, m, v)


SMALL = ("norm_g", "pool_w", "pool_scale", "conv_w", "conv_b", "conv_ln_g", "conv_ln_b",
         "sgu_ln_g", "sgu_ln_b", "sgu_w", "sgu_b", "sc_w")
SMALL_SHAPES = {"norm_g": (D,), "pool_w": (NG, GW, GW), "pool_scale": (BW,), "conv_w": (CONV_K, BW), "conv_b": (BW,),
                "conv_ln_g": (BW,), "conv_ln_b": (BW,), "sgu_ln_g": (BW,), "sgu_ln_b": (BW,),
                "sgu_w": (NG, GW, GW), "sgu_b": (NG, GW), "sc_w": (SHORT_K, BW)}
REPLICATED = tuple(k for k in SMALL if k not in ("conv_w", "sc_w"))
PACK_UNIT = 8 * 128


def _size(shape):
    out = 1
    for s in shape:
        out *= s
    return out


def _padded(a):
    flat = a.reshape(-1)
    pad = -flat.shape[0] % PACK_UNIT
    return jnp.pad(flat, (0, pad)) if pad else flat


def _pack(arrays):
    return jnp.concatenate([_padded(a) for a in arrays]).reshape(-1, 128)


def _unpack(pack, shapes):
    flat = pack.reshape(-1)
    out, off = [], 0
    for shape in shapes:
        size = _size(shape)
        out.append(flat[off:off + size].reshape(shape))
        off += size + (-size % PACK_UNIT)
    return out


def _gathered_weights(wb, wo, conv):
    wb = wb.reshape(NDEV, 4, BW, D // NDEV).transpose(1, 2, 0, 3).reshape(4, BW, D)
    wo = wo.reshape(D, D)
    cw = CONV_K * (BW // NDEV)
    sw = SHORT_K * (BW // NDEV)
    sc_at = cw + (-cw % PACK_UNIT)
    flat = conv.reshape(NDEV, -1)
    conv_w = flat[:, :cw].reshape(NDEV, CONV_K, BW // NDEV).transpose(1, 0, 2).reshape(CONV_K, BW)
    sc_w = flat[:, sc_at:sc_at + sw].reshape(NDEV, SHORT_K, BW // NDEV).transpose(1, 0, 2).reshape(SHORT_K, BW)
    return wb, wo, conv_w, sc_w


def _layer_small_inputs(l, rep, conv_w, sc_w):
    vecs = jnp.stack([rep["pool_scale"][l], rep["conv_b"][l], rep["conv_ln_g"][l], rep["conv_ln_b"][l],
                      rep["sgu_ln_g"][l], rep["sgu_ln_b"][l], jnp.zeros((BW,), F32), jnp.zeros((BW,), F32)])
    conv_w = jnp.pad(conv_w, ((0, 32 - CONV_K), (0, 0)))
    sc_w = jnp.pad(sc_w, ((0, 8 - SHORT_K), (0, 0)))
    return (rep["pool_w"][l], vecs, conv_w, rep["sgu_w"][l], rep["sgu_b"][l].reshape(NG, GW, 1), sc_w)


def kernel(x, norm_g, w_in, pool_w, pool_scale, conv_w, conv_b, conv_ln_g, conv_ln_b, sgu_ln_g, sgu_ln_b, sgu_w, sgu_b, sc_w, w_branch, w_o, final_g, loss_target, m_norm_g, m_w_in, m_pool_w, m_pool_scale, m_conv_w, m_conv_b, m_conv_ln_g, m_conv_ln_b, m_sgu_ln_g, m_sgu_ln_b, m_sgu_w, m_sgu_b, m_sc_w, m_w_branch, m_w_o, m_final_g, v_norm_g, v_w_in, v_pool_w, v_pool_scale, v_conv_w, v_conv_b, v_conv_ln_g, v_conv_ln_b, v_sgu_ln_g, v_sgu_ln_b, v_sgu_w, v_sgu_b, v_sc_w, v_w_branch, v_w_o, v_final_g):
    args = dict(locals())
    w = {k: args[k] for k in SMALL + ("w_in", "w_branch", "w_o", "final_g")}
    mom = {k: args["m_" + k] for k in w}
    var = {k: args["v_" + k] for k in w}
    rep = {k: w[k] for k in REPLICATED}
    xi, yi, ci = _position()
    me = 4 * xi + 2 * yi + ci

    nb, seq, _ = x.shape
    n = nb * seq
    tm = min(512, n)
    tm_in = min(1024, n)
    tk = min(2048, n)
    tm2 = min(256, n)
    t_fwd = min(256, seq)
    t_bwd = min(256, seq)

    w_in_b = [(w_in[l, :D // 2].astype(BF16), w_in[l, D // 2:].astype(BF16)) for l in range(DEPTH)]
    wb_b = [w_branch[l].astype(BF16).reshape(4 * BW, D // NDEV) for l in range(DEPTH)]
    wo_b = [w_o[l].astype(BF16) for l in range(DEPTH)]
    conv_b_ = [_pack([conv_w[l], sc_w[l]]) for l in range(DEPTH)]

    w_in_g = [_all_gather(list(w_in_b[0]), "w_in_all_gather"), None]
    x2d = x.reshape(n, D)
    saved = []
    ride = _Exchange(gathers=[wb_b[0], wo_b[0], conv_b_[0], w_in_b[1][0]])
    (proj, h), (wb_g, wo_g, conv_g, w_in_upper) = _in_proj(x2d, norm_g[0:1], w_in_g[0], tm_in, ride)
    for l in range(DEPTH):
        wb_l, wo_l, conv_l, sc_l = _gathered_weights(wb_g, wo_g, conv_g)
        small = _layer_small_inputs(l, rep, conv_l, sc_l)
        if l == 0:
            ride = _Exchange(gathers=[w_in_b[1][1]])
            (z, y_conv, q_sc, pooled), (w_in_lower,) = _mixers_fwd(proj, *small, nb, seq, t_fwd, ride)
            w_in_g[1] = [w_in_upper, w_in_lower]
            ride = _Exchange(gathers=[wb_b[1], wo_b[1], conv_b_[1]])
        else:
            (proj, h), _ = _in_proj(x2d, norm_g[l:l + 1], w_in_g[l], tm_in)
            (z, y_conv, q_sc, pooled), _ = _mixers_fwd(proj, *small, nb, seq, t_fwd)
            ride = None
        (x_new, merged), gathered = _merge_out(z, proj, x2d, wb_l, wo_l, tm2, ride)
        if l == 0:
            wb_g, wo_g, conv_g = gathered
        saved.append((x2d, proj, h, z, y_conv, q_sc, pooled, merged, small, wb_l, wo_l))
        x2d = x_new

    dx, loss_acc, d_final_g = _loss_head(x2d, loss_target.reshape(n, D), final_g.reshape(1, D), tm)
    loss = lax.psum(loss_acc[0, 0], ("x", "y", "c"))

    grads = {k: [None] * DEPTH for k in SMALL}
    big = {"w_in": None, "w_branch": None, "w_o": None}
    big_w = {"w_in": (w_in, m_w_in, v_w_in, (DEPTH, D, SHARD), 256),
             "w_branch": (w_branch, m_w_branch, v_w_branch, (DEPTH, 4 * BW, D // NDEV), 512),
             "w_o": (w_o, m_w_o, v_w_o, (DEPTH, D // NDEV, D), 64)}

    def adamw(k, own, landed, l):
        wk, mk, vk, shape, tr = big_w[k]
        big[k] = _adamw_layer(own, landed, wk.reshape(shape), mk.reshape(shape), vk.reshape(shape), l, big[k], tr,
                              "adamw_" + k)

    for l in reversed(range(DEPTH)):
        x_in, proj, h, z, y_conv, q_sc, pooled, merged, small, wb_l, wo_l = saved[l]
        dz, d_proj, d_wo, d_wo_b, d_wb, d_wb_b = _merge_out_bwd(dx, z, proj, merged, wb_l, wo_l, tm)
        ride = _Exchange(scatters=[(d_wo_b.reshape(NDEV, D // NDEV, D), d_wo.reshape(NDEV, D // NDEV, D)),
                                   (d_wb_b.reshape(NDEV, 4 * BW, D // NDEV), d_wb.reshape(NDEV, 4 * BW, D // NDEV))])
        (d_proj, dvec, dconvw, dscw, dpoolw, dsguw, dsgub), (wo_landed, wo_own, wb_landed, wb_own) = _mixers_bwd(
            proj, dz, y_conv, q_sc, pooled, d_proj, *small, nb, seq, t_bwd, ride)
        d_w_in, d_w_in_b = _grad_w_in(h, d_proj, tk)
        ride = _Exchange(scatters=[(d_w_in_b, d_w_in)])
        (dx, d_norm_g), (in_landed, in_own) = _in_proj_bwd(d_proj, w_in_g[l], x_in, norm_g[l:l + 1], dx, tm_in, ride)
        adamw("w_o", wo_own, wo_landed, l)
        adamw("w_branch", wb_own, wb_landed, l)
        adamw("w_in", in_own, in_landed, l)
        grads["norm_g"][l] = d_norm_g[0]
        grads["pool_w"][l] = dpoolw
        grads["pool_scale"][l] = dvec[0]
        grads["conv_w"][l] = dconvw[:CONV_K]
        grads["conv_b"][l] = dvec[1]
        grads["conv_ln_g"][l] = dvec[2]
        grads["conv_ln_b"][l] = dvec[3]
        grads["sgu_ln_g"][l] = dvec[4]
        grads["sgu_ln_b"][l] = dvec[5]
        grads["sgu_w"][l] = dsguw
        grads["sgu_b"][l] = dsgub[:NG]
        grads["sc_w"][l] = dscw[:SHORT_K]
    grad_x = dx.reshape(nb, seq, D)
    small_grads = {k: jnp.stack(v) for k, v in grads.items()}

    names = list(SMALL) + ["final_g"]
    shapes = [(DEPTH,) + SMALL_SHAPES[k] for k in SMALL] + [(D,)]
    whole = REPLICATED + ("final_g",)
    part_pack = _pack([small_grads[k] for k in SMALL] + [d_final_g[0]])
    w_pack = _pack([w[k] if k in whole else jnp.zeros(s, F32) for k, s in zip(names, shapes)])
    m_pack = _pack([mom[k] if k in whole else jnp.zeros(s, F32) for k, s in zip(names, shapes)])
    v_pack = _pack([var[k] if k in whole else jnp.ones(s, F32) for k, s in zip(names, shapes)])
    (parts,) = _all_gather([part_pack], "small_grads_all_gather")
    g_pack, d_pack, nm_pack, nv_pack = _adamw_gathered(parts, w_pack, m_pack, v_pack, part_pack.shape[0] // 4)
    out_g = dict(zip(names, _unpack(g_pack, shapes)))
    out_d = dict(zip(names, _unpack(d_pack, shapes)))
    out_m = dict(zip(names, _unpack(nm_pack, shapes)))
    out_v = dict(zip(names, _unpack(nv_pack, shapes)))

    col = me * (BW // NDEV)
    for k in ("conv_w", "sc_w"):
        out_g[k] = lax.dynamic_slice_in_dim(out_g[k], col, BW // NDEV, axis=2)
    sharded_shapes = [w["conv_w"].shape, w["sc_w"].shape]
    d2, m2, v2 = _adamw_plain(_pack([out_g["conv_w"], out_g["sc_w"]]), _pack([w["conv_w"], w["sc_w"]]),
                              _pack([mom["conv_w"], mom["sc_w"]]), _pack([var["conv_w"], var["sc_w"]]))
    for dst, pk in ((out_d, d2), (out_m, m2), (out_v, v2)):
        dst["conv_w"], dst["sc_w"] = _unpack(pk, sharded_shapes)

    for k in big:
        out_g[k], out_d[k], out_m[k], out_v[k] = [o.reshape(w[k].shape) for o in big[k]]

    order = ("norm_g", "w_in", "pool_w", "pool_scale", "conv_w", "conv_b", "conv_ln_g", "conv_ln_b", "sgu_ln_g",
             "sgu_ln_b", "sgu_w", "sgu_b", "sc_w", "w_branch", "w_o", "final_g")
    return (loss, grad_x, *[out_g[k] for k in order], *[out_d[k] for k in order],
            *[out_m[k] for k in order], *[out_v[k] for k in order])
```

```python
import functools

import jax
import jax.numpy as jnp
from jax import lax
from jax.experimental import pallas as pl
from jax.experimental.pallas import tpu as pltpu

F32 = jnp.float32
BF16 = jnp.bfloat16
SDS = jax.ShapeDtypeStruct
MESH = pl.DeviceIdType.MESH

D = 1024
BW = 512
NG = 4
GW = 128
CONV_K = 31
SHORT_K = 3
POOL_WINDOWS = (2, 4, 8, 16)
N_PIECE_COLS = 12 * BW
IN_COLS = N_PIECE_COLS + 4 * D
NDEV = 8
SHARD = IN_COLS // NDEV
DEPTH = 2
RMS_EPS = 1e-6
LN_EPS = 1e-5
HALO_POOL, HALO_CONV, HALO_SC = 16, 32, 8

ADAM_LR = 0.001
ADAM_B1 = 0.9
ADAM_B2 = 0.999
ADAM_EPS = 1e-08
ADAM_WD = 0.01
ADAM_STEP = 10

VMEM_LIMIT = 56 * 1024 * 1024


def _params(*sem):
    return pltpu.CompilerParams(dimension_semantics=sem, vmem_limit_bytes=VMEM_LIMIT)


def _sigmoid(x):
    return jax.nn.sigmoid(x)


def _silu(x):
    return x * _sigmoid(x)


def _dsilu(x):
    s = _sigmoid(x)
    return s * (1.0 + x * (1.0 - s))


def _colsum(x):
    return jnp.sum(x, axis=0, keepdims=True)


def _dot(a, b):
    return jnp.dot(a, b, preferred_element_type=F32)


def _dot_nt(a, b):
    return lax.dot_general(a, b, (((1,), (1,)), ((), ())), preferred_element_type=F32)


def _dot_tn(a, b):
    return lax.dot_general(a, b, (((0,), (0,)), ((), ())), preferred_element_type=F32)


def _layer_norm_stats(y):
    mu = jnp.mean(y, axis=-1, keepdims=True)
    yc = y - mu
    var = jnp.mean(yc * yc, axis=-1, keepdims=True)
    rstd = lax.rsqrt(var + LN_EPS)
    return yc * rstd, rstd


def _layer_norm_bwd(d_hat, hat, rstd):
    return rstd * (d_hat - jnp.mean(d_hat, axis=-1, keepdims=True)
                   - hat * jnp.mean(d_hat * hat, axis=-1, keepdims=True))


def _position():
    return lax.axis_index("x"), lax.axis_index("y"), lax.axis_index("c")


def _flip(v, bit):
    return 1 - v if bit else v


class _Exchange:
    def __init__(self, gathers=(), scatters=()):
        self.gathers = list(gathers)
        self.scatters = list(scatters)
        self.inputs = self.gathers + [a for pair in self.scatters for a in pair]
        self.out_shapes = [SDS((NDEV,) + a.shape, a.dtype) for a in self.gathers]
        for send, keep in self.scatters:
            self.out_shapes += [SDS((NDEV - 1,) + send.shape[1:], send.dtype), SDS(keep.shape[1:], keep.dtype)]
        n = len(self.gathers) + len(self.scatters)
        self.scratch = [pltpu.SemaphoreType.DMA((7 * n,)), pltpu.SemaphoreType.DMA((7 * n,)),
                        pltpu.SemaphoreType.DMA((n,))]

    def _copies(self, ins, outs, sems):
        send_sems, recv_sems, local_sems = sems
        x, y, c = _position()
        me = 4 * x + 2 * y + c
        ng = len(self.gathers)
        local, remote = [], []
        for a in range(ng + len(self.scatters)):
            if a < ng:
                local.append(pltpu.make_async_copy(ins[a], outs[a].at[me], local_sems.at[a]))
            else:
                s = a - ng
                send, keep = ins[ng + 2 * s], ins[ng + 2 * s + 1]
                landed, own = outs[ng + 2 * s], outs[ng + 2 * s + 1]
                local.append(pltpu.make_async_copy(keep.at[me], own, local_sems.at[a]))
            for r in range(1, NDEV):
                peer = (_flip(x, r & 4), _flip(y, r & 2), _flip(c, r & 1))
                if a < ng:
                    src, dst = ins[a], outs[a].at[me]
                else:
                    src, dst = send.at[jnp.bitwise_xor(me, r)], landed.at[r - 1]
                remote.append(pltpu.make_async_remote_copy(
                    src_ref=src, dst_ref=dst, send_sem=send_sems.at[7 * a + r - 1],
                    recv_sem=recv_sems.at[7 * a + r - 1], device_id=peer, device_id_type=MESH))
        return local, remote

    def start(self, ins, outs, sems):
        local, remote = self._copies(ins, outs, sems)
        for cp in remote + local:
            cp.start()

    def finish(self, ins, outs, sems):
        local, remote = self._copies(ins, outs, sems)
        for cp in remote + local:
            cp.wait()


def _carried_call(body, comm, args, *, name, grid, in_specs, out_specs, out_shape, scratch_shapes=(),
                  semantics, input_output_aliases=None):
    aliases = input_output_aliases or {}
    if comm is None:
        out = pl.pallas_call(body, name=name, grid=grid, in_specs=in_specs, out_specs=out_specs, out_shape=out_shape,
                             scratch_shapes=list(scratch_shapes), input_output_aliases=aliases,
                             compiler_params=_params(*semantics))(*args)
        return list(out), []
    n_in, n_out, n_scr = len(in_specs), len(out_specs), len(scratch_shapes)
    c_in, c_out = len(comm.inputs), len(comm.out_shapes)

    def full_body(*refs):
        refs = list(refs)
        ins, refs = refs[:n_in], refs[n_in:]
        cins, refs = refs[:c_in], refs[c_in:]
        outs, refs = refs[:n_out], refs[n_out:]
        couts, refs = refs[:c_out], refs[c_out:]
        scr, sems = refs[:n_scr], refs[n_scr:]
        ids = [pl.program_id(d) for d in range(len(grid))]
        first = functools.reduce(jnp.logical_and, [i == 0 for i in ids])
        last = functools.reduce(jnp.logical_and, [i == g - 1 for i, g in zip(ids, grid)])

        @pl.when(first)
        def _():
            comm.start(cins, couts, sems)

        body(*ins, *outs, *scr)

        @pl.when(last)
        def _():
            comm.finish(cins, couts, sems)

    any_spec = pl.BlockSpec(memory_space=pl.ANY)
    out = pl.pallas_call(
        full_body, name=name, grid=grid,
        in_specs=list(in_specs) + [any_spec] * c_in, out_specs=list(out_specs) + [any_spec] * c_out,
        out_shape=list(out_shape) + comm.out_shapes, scratch_shapes=list(scratch_shapes) + comm.scratch,
        input_output_aliases=aliases, compiler_params=_params(*["arbitrary"] * len(grid)),
    )(*args, *comm.inputs)
    return list(out[:n_out]), list(out[n_out:])


def _in_proj(x2d, g_row, w_g, tm, comm=None):
    n = x2d.shape[0]
    half = D // 2

    def body(x_ref, g_ref, wu_ref, wl_ref, proj_ref, h_ref):
        @pl.when(pl.program_id(1) == 0)
        def _():
            x = x_ref[...]
            r = lax.rsqrt(jnp.mean(x * x, axis=-1, keepdims=True) + RMS_EPS)
            h_ref[...] = (x * r * g_ref[...]).astype(BF16)

        proj_ref[...] = (_dot(h_ref[:, :half], wu_ref[...]) + _dot(h_ref[:, half:], wl_ref[...])).astype(BF16)

    return _carried_call(
        body, comm, (x2d, g_row, *w_g), name="in_proj", grid=(n // tm, NDEV),
        in_specs=[pl.BlockSpec((tm, D), lambda i, j: (i, 0)),
                  pl.BlockSpec((1, D), lambda i, j: (0, 0)),
                  pl.BlockSpec((None, half, SHARD), lambda i, j: (j, 0, 0)),
                  pl.BlockSpec((None, half, SHARD), lambda i, j: (j, 0, 0))],
        out_specs=[pl.BlockSpec((tm, SHARD), lambda i, j: (i, j)),
                   pl.BlockSpec((tm, D), lambda i, j: (i, 0))],
        out_shape=[SDS((n, IN_COLS), BF16), SDS((n, D), BF16)],
        semantics=("parallel", "arbitrary"))


def _tril_bf16(w):
    r = lax.broadcasted_iota(jnp.int32, (GW, GW), 0)
    c = lax.broadcasted_iota(jnp.int32, (GW, GW), 1)
    return jnp.where(r >= c, w, 0.0).astype(BF16)


def _mixers_fwd(proj, pool_w, vecs, conv_w, sgu_w, sgu_bc, sc_w, nb, seq, t, comm=None):
    n = nb * seq
    nt = seq // t

    def body(p_ref, poolw_ref, vec_ref, convw_ref, sguw_ref, sgub_ref, scw_ref,
             z_ref, y_ref, q_ref, pooled_ref, pext, uext, vext):
        i = pl.program_id(1)

        @pl.when(i == 0)
        def _():
            pext[0:HALO_POOL, :] = jnp.zeros((HALO_POOL, BW), F32)
            uext[0:HALO_CONV, :] = jnp.zeros((HALO_CONV, BW), F32)
            vext[0:HALO_SC, :] = jnp.zeros((HALO_SC, BW), F32)

        @pl.when(i > 0)
        def _():
            pext[0:HALO_POOL, :] = pext[t:t + HALO_POOL, :]
            uext[0:HALO_CONV, :] = uext[t:t + HALO_CONV, :]
            vext[0:HALO_SC, :] = vext[t:t + HALO_SC, :]

        def piece(k):
            return p_ref[:, k * BW:(k + 1) * BW].astype(F32)

        pos = lax.broadcasted_iota(jnp.int32, (t, 1), 0) + i * t

        px = piece(0)
        pext[HALO_POOL:, :] = px
        for j, win in enumerate(POOL_WINDOWS):
            cols = slice(j * GW, (j + 1) * GW)
            s = pext[:, cols]
            step = 1
            while step < win:
                s = s + pltpu.roll(s, step, 0)
                step *= 2
            cnt = jnp.minimum(pos + 1, win).astype(F32)
            pooled = (s[HALO_POOL:, :] / cnt - px[:, cols]).astype(BF16)
            pooled_ref[:, cols] = pooled
            mixed = _dot(pooled, poolw_ref[j].astype(BF16))
            z_ref[:, cols] = (mixed * vec_ref[0:1, cols] * _silu(p_ref[:, BW + j * GW:BW + (j + 1) * GW].astype(F32))).astype(BF16)

        uext[HALO_CONV:, :] = piece(2) * _sigmoid(piece(3))
        ue = uext[...]
        acc = jnp.zeros((t, BW), F32)
        for b in range(8):
            rb = ue if b == 0 else pltpu.roll(ue, b, 0)
            for a in range((CONV_K - b + 7) // 8):
                j = 8 * a + b
                acc = acc + convw_ref[CONV_K - 1 - j:CONV_K - j, :] * rb[HALO_CONV - 8 * a:HALO_CONV - 8 * a + t, :]
        y = acc + vec_ref[1:2, :]
        y_ref[...] = y
        yhat, _ = _layer_norm_stats(y)
        act = _silu(yhat * vec_ref[2:3, :] + vec_ref[3:4, :])
        z_ref[:, BW:2 * BW] = (act * _silu(piece(4))).astype(BF16)

        vhat, _ = _layer_norm_stats(piece(6))
        vn = (vhat * vec_ref[4:5, :] + vec_ref[5:6, :]).astype(BF16)
        for g in range(NG):
            ws = _tril_bf16(sguw_ref[g])
            cols = slice(g * GW, (g + 1) * GW)
            for c in range(t // GW):
                rows = slice(c * GW, (c + 1) * GW)
                sp = _dot(ws, vn[rows, cols]) + sgub_ref[g]
                gate = _silu(p_ref[rows, 7 * BW + g * GW:7 * BW + (g + 1) * GW].astype(F32))
                z_ref[rows, 2 * BW + g * GW:2 * BW + (g + 1) * GW] = (
                    p_ref[rows, 5 * BW + g * GW:5 * BW + (g + 1) * GW].astype(F32) * sp * gate).astype(BF16)

        vext[HALO_SC:, :] = piece(9) * piece(10)
        ve = vext[...]
        q = jnp.zeros((t, BW), F32)
        for j in range(SHORT_K):
            sh = ve if j == 0 else pltpu.roll(ve, j, 0)
            q = q + scw_ref[SHORT_K - 1 - j:SHORT_K - j, :] * sh[HALO_SC:, :]
        q_ref[...] = q
        z_ref[:, 3 * BW:] = (piece(8) * q * _silu(piece(11))).astype(BF16)

    row = lambda b, i: (b * nt + i, 0)
    full2 = lambda b, i: (0, 0)
    full3 = lambda b, i: (0, 0, 0)
    return _carried_call(
        body, comm, (proj, pool_w, vecs, conv_w, sgu_w, sgu_bc, sc_w), name="mixers_fwd", grid=(nb, nt),
        in_specs=[pl.BlockSpec((t, N_PIECE_COLS), row),
                  pl.BlockSpec((NG, GW, GW), full3),
                  pl.BlockSpec((8, BW), full2),
                  pl.BlockSpec((32, BW), full2),
                  pl.BlockSpec((NG, GW, GW), full3),
                  pl.BlockSpec((NG, GW, 1), full3),
                  pl.BlockSpec((8, BW), full2)],
        out_specs=[pl.BlockSpec((t, 4 * BW), row), pl.BlockSpec((t, BW), row),
                   pl.BlockSpec((t, BW), row), pl.BlockSpec((t, BW), row)],
        out_shape=[SDS((n, 4 * BW), BF16), SDS((n, BW), F32), SDS((n, BW), F32), SDS((n, BW), BF16)],
        scratch_shapes=[pltpu.VMEM((t + HALO_POOL, BW), F32), pltpu.VMEM((t + HALO_CONV, BW), F32),
                        pltpu.VMEM((t + HALO_SC, BW), F32)],
        semantics=("arbitrary", "arbitrary"))


def _merge_out(z, proj, x2d, wb, wo, tm, comm=None):
    n = x2d.shape[0]

    def body(z_ref, mga_ref, mgb_ref, x_ref, wb_ref, wo_ref, xo_ref, m_ref):
        merged = jnp.zeros((tm, D), F32)
        for k in range(4):
            bo = _dot(z_ref[:, k * BW:(k + 1) * BW], wb_ref[k])
            mg_ref = mga_ref if k < 2 else mgb_ref
            merged = merged + _sigmoid(mg_ref[:, (k % 2) * D:(k % 2 + 1) * D].astype(F32)) * bo
        mb = merged.astype(BF16)
        m_ref[...] = mb
        xo_ref[...] = x_ref[...] + _dot(mb, wo_ref[...])

    return _carried_call(
        body, comm, (z, proj, proj, x2d, wb, wo), name="merge_out", grid=(n // tm,),
        in_specs=[pl.BlockSpec((tm, 4 * BW), lambda i: (i, 0)),
                  pl.BlockSpec((tm, 2 * D), lambda i: (i, 3)),
                  pl.BlockSpec((tm, 2 * D), lambda i: (i, 4)),
                  pl.BlockSpec((tm, D), lambda i: (i, 0)),
                  pl.BlockSpec((4, BW, D), lambda i: (0, 0, 0)),
                  pl.BlockSpec((D, D), lambda i: (0, 0))],
        out_specs=[pl.BlockSpec((tm, D), lambda i: (i, 0)), pl.BlockSpec((tm, D), lambda i: (i, 0))],
        out_shape=[SDS((n, D), F32), SDS((n, D), BF16)], semantics=("parallel",))


def _loss_head(x2d, tgt2d, g_row, tm):
    n = x2d.shape[0]

    def body(x_ref, t_ref, g_ref, dx_ref, loss_ref, dg_ref):
        @pl.when(pl.program_id(0) == 0)
        def _():
            loss_ref[...] = jnp.zeros_like(loss_ref)
            dg_ref[...] = jnp.zeros_like(dg_ref)

        x = x_ref[...]
        g = g_ref[...]
        r = lax.rsqrt(jnp.mean(x * x, axis=-1, keepdims=True) + RMS_EPS)
        xn = x * r
        e = xn * g - t_ref[...]
        loss_ref[...] += 0.5 * jnp.sum(jnp.mean(e * e, axis=-1, keepdims=True))
        dy = e * (1.0 / D)
        dg_ref[...] += _colsum(dy * xn)
        gy = dy * g
        dx_ref[...] = r * gy - xn * (r * jnp.mean(gy * xn, axis=-1, keepdims=True))

    return pl.pallas_call(
        body, name="loss_head", grid=(n // tm,),
        in_specs=[pl.BlockSpec((tm, D), lambda i: (i, 0)), pl.BlockSpec((tm, D), lambda i: (i, 0)),
                  pl.BlockSpec((1, D), lambda i: (0, 0))],
        out_specs=[pl.BlockSpec((tm, D), lambda i: (i, 0)), pl.BlockSpec((8, 128), lambda i: (0, 0)),
                   pl.BlockSpec((1, D), lambda i: (0, 0))],
        out_shape=[SDS((n, D), F32), SDS((8, 128), F32), SDS((1, D), F32)],
        compiler_params=_params("arbitrary"),
    )(x2d, tgt2d, g_row)


def _merge_out_bwd(dxo, z, proj, merged, wb, wo, tm):
    n = dxo.shape[0]
    ni = n // tm
    sub = D // NDEV

    def body(dx_ref, z_ref, mg_ref, m_ref, wb_ref, wo_ref,
             dz_ref, dp_ref, gwo_ref, gwo_b_ref, gwb_ref, gwb_b_ref,
             dm_ref, acc_wo, acc_wb, half_wo, half_wb, sems):
        i = pl.program_id(0)
        h = pl.program_id(1)

        @pl.when(jnp.logical_and(i == 0, h == 0))
        def _():
            acc_wo[...] = jnp.zeros_like(acc_wo)
            acc_wb[...] = jnp.zeros_like(acc_wb)

        @pl.when(h == 0)
        def _():
            dxb = dx_ref[...].astype(BF16)
            dm_ref[...] = _dot_nt(dxb, wo_ref[...])
            acc_wo[...] += _dot_tn(m_ref[...], dxb)

        dm = dm_ref[...]
        for k in range(2):
            zk = z_ref[:, k * BW:(k + 1) * BW]
            w = wb_ref[2 * h + k]
            bo = _dot(zk, w)
            sig = _sigmoid(mg_ref[:, k * D:(k + 1) * D].astype(F32))
            dbo = (dm * sig).astype(BF16)
            dp_ref[:, k * D:(k + 1) * D] = (dm * bo * sig * (1.0 - sig)).astype(BF16)
            dz_ref[:, k * BW:(k + 1) * BW] = _dot_nt(dbo, w)
            acc_wb[2 * h + k] += _dot_tn(zk, dbo)

        @pl.when(jnp.logical_and(i == ni - 1, h == 1))
        def _():
            half_wo[...] = acc_wo[...].astype(BF16)
            half_wb[...] = acc_wb[...].astype(BF16)
            copies = [pltpu.make_async_copy(acc_wo, gwo_ref, sems.at[0]),
                      pltpu.make_async_copy(half_wo, gwo_b_ref, sems.at[1])]
            for d in range(NDEV):
                cols = pl.ds(d * sub, sub)
                copies.append(pltpu.make_async_copy(acc_wb.at[:, :, cols], gwb_ref.at[d], sems.at[2 + 2 * d]))
                copies.append(pltpu.make_async_copy(half_wb.at[:, :, cols], gwb_b_ref.at[d], sems.at[3 + 2 * d]))
            for cp in copies:
                cp.start()
            for cp in copies:
                cp.wait()

    any_spec = pl.BlockSpec(memory_space=pl.ANY)
    return pl.pallas_call(
        body, name="merge_out_bwd", grid=(ni, 2),
        in_specs=[pl.BlockSpec((tm, D), lambda i, h: (i, 0)),
                  pl.BlockSpec((tm, 2 * BW), lambda i, h: (i, h)),
                  pl.BlockSpec((tm, 2 * D), lambda i, h: (i, 3 + h)),
                  pl.BlockSpec((tm, D), lambda i, h: (i, 0)),
                  pl.BlockSpec((4, BW, D), lambda i, h: (0, 0, 0)),
                  pl.BlockSpec((D, D), lambda i, h: (0, 0))],
        out_specs=[pl.BlockSpec((tm, 2 * BW), lambda i, h: (i, h)),
                   pl.BlockSpec((tm, 2 * D), lambda i, h: (i, 3 + h)),
                   any_spec, any_spec, any_spec, any_spec],
        out_shape=[SDS((n, 4 * BW), F32), SDS((n, IN_COLS), BF16), SDS((D, D), F32), SDS((D, D), BF16),
                   SDS((NDEV, 4, BW, sub), F32), SDS((NDEV, 4, BW, sub), BF16)],
        scratch_shapes=[pltpu.VMEM((tm, D), F32), pltpu.VMEM((D, D), F32), pltpu.VMEM((4, BW, D), F32),
                        pltpu.VMEM((D, D), BF16), pltpu.VMEM((4, BW, D), BF16),
                        pltpu.SemaphoreType.DMA((2 + 2 * NDEV,))],
        compiler_params=_params("arbitrary", "arbitrary"),
    )(dxo, z, proj, merged, wb, wo)


def _grad_w_in(h, d_proj, tk):
    nk = h.shape[0] // tk

    def body(h_ref, dp_ref, o_ref, ob_ref):
        k = pl.program_id(1)

        @pl.when(k == 0)
        def _():
            o_ref[...] = jnp.zeros_like(o_ref)

        o_ref[...] += _dot_tn(h_ref[...], dp_ref[...])

        @pl.when(k == nk - 1)
        def _():
            ob_ref[...] = o_ref[...].astype(BF16)

    out_spec = pl.BlockSpec((None, D, SHARD), lambda j, k: (j, 0, 0))
    return pl.pallas_call(
        body, name="grad_w_in", grid=(NDEV, nk),
        in_specs=[pl.BlockSpec((tk, D), lambda j, k: (k, 0)), pl.BlockSpec((tk, SHARD), lambda j, k: (k, j))],
        out_specs=[out_spec, out_spec], out_shape=[SDS((NDEV, D, SHARD), F32), SDS((NDEV, D, SHARD), BF16)],
        compiler_params=_params("parallel", "arbitrary"),
    )(h, d_proj)


def _mixers_bwd(proj, dz, y_conv, q_sc, pooled, d_proj, pool_w, vecs, conv_w, sgu_w, sgu_bc, sc_w, nb, seq, t, comm=None):
    n = nb * seq
    nt = seq // t

    def body(p_ref, dz_ref, y_ref, q_ref, pooled_ref, dpin_ref,
             poolw_ref, vec_ref, convw_ref, sguw_ref, sgub_ref, scw_ref,
             dp_ref, dvec_ref, dconvw_ref, dscw_ref, dpoolw_ref, dsguw_ref, dsgub_ref,
             rext, dyext, dqext, bacc):
        b = pl.program_id(0)
        i = pl.program_id(1)
        first = jnp.logical_and(b == 0, i == 0)
        last = jnp.logical_and(b == nb - 1, i == nt - 1)

        @pl.when(first)
        def _():
            dvec_ref[...] = jnp.zeros_like(dvec_ref)
            dconvw_ref[...] = jnp.zeros_like(dconvw_ref)
            dscw_ref[...] = jnp.zeros_like(dscw_ref)
            dpoolw_ref[...] = jnp.zeros_like(dpoolw_ref)
            dsguw_ref[...] = jnp.zeros_like(dsguw_ref)
            dsgub_ref[...] = jnp.zeros_like(dsgub_ref)
            bacc[...] = jnp.zeros_like(bacc)

        @pl.when(i == 0)
        def _():
            rext[t:, :] = jnp.zeros((HALO_POOL, BW), F32)
            dyext[t:, :] = jnp.zeros((HALO_CONV, BW), F32)
            dqext[t:, :] = jnp.zeros((HALO_SC, BW), F32)

        @pl.when(i > 0)
        def _():
            rext[t:, :] = rext[0:HALO_POOL, :]
            dyext[t:, :] = dyext[0:HALO_CONV, :]
            dqext[t:, :] = dqext[0:HALO_SC, :]

        def piece(k):
            return p_ref[:, k * BW:(k + 1) * BW].astype(F32)

        def put(k, v):
            dp_ref[:, k * BW:(k + 1) * BW] = v.astype(BF16)

        pos = lax.broadcasted_iota(jnp.int32, (t, 1), 0) + (nt - 1 - i) * t

        dzp = dz_ref[:, 0:BW]
        pg = piece(1)
        d_pm = dzp * _silu(pg)
        for j, win in enumerate(POOL_WINDOWS):
            cols = slice(j * GW, (j + 1) * GW)
            pw = poolw_ref[j].astype(BF16)
            pooled = pooled_ref[:, cols]
            mixed = _dot(pooled, pw)
            scale = vec_ref[0:1, cols]
            dvec_ref[0:1, cols] += _colsum(d_pm[:, cols] * mixed)
            dp_ref[:, BW + j * GW:BW + (j + 1) * GW] = (dzp[:, cols] * (mixed * scale) * _dsilu(pg[:, cols])).astype(BF16)
            d_mixed = (d_pm[:, cols] * scale).astype(BF16)
            dpoolw_ref[j] += _dot_tn(pooled, d_mixed)
            d_pooled = _dot_nt(d_mixed, pw)
            cnt = jnp.minimum(pos + 1, win).astype(F32)
            rext[0:t, cols] = d_pooled / cnt
            s = rext[:, cols]
            step = 1
            while step < win:
                s = s + pltpu.roll(s, t + HALO_POOL - step, 0)
                step *= 2
            dp_ref[:, cols] = (s[0:t, :] - d_pooled).astype(BF16)

        dzc = dz_ref[:, BW:2 * BW]
        cgate = piece(4)
        yhat, rstd = _layer_norm_stats(y_ref[...])
        ln = yhat * vec_ref[2:3, :] + vec_ref[3:4, :]
        put(4, dzc * _silu(ln) * _dsilu(cgate))
        d_ln = dzc * _silu(cgate) * _dsilu(ln)
        dvec_ref[2:3, :] += _colsum(d_ln * yhat)
        dvec_ref[3:4, :] += _colsum(d_ln)
        dy = _layer_norm_bwd(d_ln * vec_ref[2:3, :], yhat, rstd)
        dvec_ref[1:2, :] += _colsum(dy)
        dyext[0:t, :] = dy
        dye = dyext[...]
        ca = piece(2)
        sig_cb = _sigmoid(piece(3))
        u = ca * sig_cb
        du = jnp.zeros((t, BW), F32)
        for b in range(8):
            rb = dye if b == 0 else pltpu.roll(dye, t + HALO_CONV - b, 0)
            for a in range((CONV_K - b + 7) // 8):
                j = 8 * a + b
                sh = rb[8 * a:8 * a + t, :]
                du = du + convw_ref[CONV_K - 1 - j:CONV_K - j, :] * sh
                dconvw_ref[CONV_K - 1 - j:CONV_K - j, :] += _colsum(u * sh)
        put(2, du * sig_cb)
        put(3, du * u * (1.0 - sig_cb))

        dzg = dz_ref[:, 2 * BW:3 * BW]
        gu = piece(5)
        ggate = piece(7)
        vhat, vrstd = _layer_norm_stats(piece(6))
        vn = (vhat * vec_ref[4:5, :] + vec_ref[5:6, :]).astype(BF16)
        t1 = dzg * _silu(ggate)
        d_sp_all = t1 * gu
        d_vn_cols = []
        sp_cols = []
        for g in range(NG):
            ws = _tril_bf16(sguw_ref[g])
            cols = slice(g * GW, (g + 1) * GW)
            d_vn_rows = []
            sp_rows = []
            for c in range(t // GW):
                rows = slice(c * GW, (c + 1) * GW)
                sp_rows.append(_dot(ws, vn[rows, cols]) + sgub_ref[g])
                d_sp = d_sp_all[rows, cols]
                bacc[g] += d_sp
                d_spb = d_sp.astype(BF16)
                d_vn_rows.append(_dot_tn(ws, d_spb))
                dsguw_ref[g] += _dot_nt(d_spb, vn[rows, cols])
            d_vn_cols.append(jnp.concatenate(d_vn_rows, axis=0))
            sp_cols.append(jnp.concatenate(sp_rows, axis=0))
        d_vn = jnp.concatenate(d_vn_cols, axis=1)
        sp = jnp.concatenate(sp_cols, axis=1)
        put(5, t1 * sp)
        put(7, dzg * gu * sp * _dsilu(ggate))
        dvec_ref[4:5, :] += _colsum(d_vn * vhat)
        dvec_ref[5:6, :] += _colsum(d_vn)
        put(6, _layer_norm_bwd(d_vn * vec_ref[4:5, :], vhat, vrstd))

        dzs = dz_ref[:, 3 * BW:]
        sb = piece(8)
        scp = piece(9)
        sx = piece(10)
        sgate = piece(11)
        q = q_ref[...]
        t2 = dzs * _silu(sgate)
        put(8, t2 * q)
        put(11, dzs * sb * q * _dsilu(sgate))
        dqext[0:t, :] = t2 * sb
        dqe = dqext[...]
        vv = scp * sx
        dv = jnp.zeros((t, BW), F32)
        for j in range(SHORT_K):
            sh = (dqe if j == 0 else pltpu.roll(dqe, t + HALO_SC - j, 0))[0:t, :]
            dv = dv + scw_ref[SHORT_K - 1 - j:SHORT_K - j, :] * sh
            dscw_ref[SHORT_K - 1 - j:SHORT_K - j, :] += _colsum(vv * sh)
        put(9, dv * sx)
        put(10, dv * scp)

        @pl.when(last)
        def _():
            r = lax.broadcasted_iota(jnp.int32, (GW, GW), 0)
            c = lax.broadcasted_iota(jnp.int32, (GW, GW), 1)
            for g in range(NG):
                dsguw_ref[g] = jnp.where(r >= c, dsguw_ref[g], 0.0)
                dsgub_ref[g:g + 1, :] = _colsum(bacc[g].T)

    row = lambda b, i: (b * nt + (nt - 1 - i), 0)
    full2 = lambda b, i: (0, 0)
    full3 = lambda b, i: (0, 0, 0)
    return _carried_call(
        body, comm, (proj, dz, y_conv, q_sc, pooled, d_proj, pool_w, vecs, conv_w, sgu_w, sgu_bc, sc_w),
        name="mixers_bwd", grid=(nb, nt),
        in_specs=[pl.BlockSpec((t, N_PIECE_COLS), row),
                  pl.BlockSpec((t, 4 * BW), row),
                  pl.BlockSpec((t, BW), row), pl.BlockSpec((t, BW), row), pl.BlockSpec((t, BW), row),
                  pl.BlockSpec(memory_space=pl.ANY),
                  pl.BlockSpec((NG, GW, GW), full3),
                  pl.BlockSpec((8, BW), full2),
                  pl.BlockSpec((32, BW), full2),
                  pl.BlockSpec((NG, GW, GW), full3),
                  pl.BlockSpec((NG, GW, 1), full3),
                  pl.BlockSpec((8, BW), full2)],
        out_specs=[pl.BlockSpec((t, N_PIECE_COLS), row),
                   pl.BlockSpec((8, BW), full2), pl.BlockSpec((32, BW), full2), pl.BlockSpec((8, BW), full2),
                   pl.BlockSpec((NG, GW, GW), full3), pl.BlockSpec((NG, GW, GW), full3),
                   pl.BlockSpec((8, GW), full2)],
        out_shape=[SDS((n, IN_COLS), BF16), SDS((8, BW), F32), SDS((32, BW), F32), SDS((8, BW), F32),
                   SDS((NG, GW, GW), F32), SDS((NG, GW, GW), F32), SDS((8, GW), F32)],
        scratch_shapes=[pltpu.VMEM((t + HALO_POOL, BW), F32), pltpu.VMEM((t + HALO_CONV, BW), F32),
                        pltpu.VMEM((t + HALO_SC, BW), F32), pltpu.VMEM((NG, GW, GW), F32)],
        input_output_aliases={5: 0}, semantics=("arbitrary", "arbitrary"))


def _in_proj_bwd(d_proj, w_g, x2d, g_row, dxo, tm, comm=None):
    n = x2d.shape[0]
    half = D // 2

    def body(dp_ref, wu_ref, wl_ref, x_hbm, g_ref, dxo_hbm, dx_ref, dg_ref, acc, x_buf, dxo_buf, sems):
        i = pl.program_id(0)
        k = pl.program_id(1)
        rows = pl.ds(pl.multiple_of(i * tm, tm), tm)
        late = (pltpu.make_async_copy(x_hbm.at[rows], x_buf, sems.at[0]),
                pltpu.make_async_copy(dxo_hbm.at[rows], dxo_buf, sems.at[1]))

        @pl.when(jnp.logical_and(i == 0, k == 0))
        def _():
            dg_ref[...] = jnp.zeros_like(dg_ref)

        @pl.when(k == 0)
        def _():
            for cp in late:
                cp.start()
            acc[...] = jnp.zeros_like(acc)

        dp = dp_ref[...]
        acc[:, :half] += _dot_nt(dp, wu_ref[...])
        acc[:, half:] += _dot_nt(dp, wl_ref[...])

        @pl.when(k == NDEV - 1)
        def _():
            for cp in late:
                cp.wait()
            dh = acc[...]
            x = x_buf[...]
            r = lax.rsqrt(jnp.mean(x * x, axis=-1, keepdims=True) + RMS_EPS)
            xn = x * r
            dg_ref[...] += _colsum(dh * xn)
            gy = dh * g_ref[...]
            dx_ref[...] = dxo_buf[...] + r * gy - xn * (r * jnp.mean(gy * xn, axis=-1, keepdims=True))

    any_spec = pl.BlockSpec(memory_space=pl.ANY)
    return _carried_call(
        body, comm, (d_proj, *w_g, x2d, g_row, dxo), name="in_proj_bwd", grid=(n // tm, NDEV),
        in_specs=[pl.BlockSpec((tm, SHARD), lambda i, k: (i, k)),
                  pl.BlockSpec((None, half, SHARD), lambda i, k: (k, 0, 0)),
                  pl.BlockSpec((None, half, SHARD), lambda i, k: (k, 0, 0)),
                  any_spec,
                  pl.BlockSpec((1, D), lambda i, k: (0, 0)),
                  any_spec],
        out_specs=[pl.BlockSpec((tm, D), lambda i, k: (i, 0)), pl.BlockSpec((1, D), lambda i, k: (0, 0))],
        out_shape=[SDS((n, D), F32), SDS((1, D), F32)],
        scratch_shapes=[pltpu.VMEM((tm, D), F32), pltpu.VMEM((tm, D), F32), pltpu.VMEM((tm, D), F32),
                        pltpu.SemaphoreType.DMA((2,))],
        semantics=("arbitrary", "arbitrary"))


def _all_gather(arrs, name):
    na = len(arrs)

    def body(*refs):
        ins, outs = refs[:na], refs[na:2 * na]
        send_sems, recv_sems, local_sems = refs[2 * na:]
        x, y, c = _position()
        me, sibling = (x, y, c), (x, y, 1 - c)
        chips = [(1 - x, y), (x, 1 - y), (1 - x, 1 - y)]

        def copy(a, k, block, to, from_input=False):
            px, py, pc = block
            dst = outs[a].at[4 * px + 2 * py + pc]
            return pltpu.make_async_remote_copy(
                src_ref=ins[a] if from_input else dst, dst_ref=dst,
                send_sem=send_sems.at[7 * a + k], recv_sem=recv_sems.at[7 * a + k],
                device_id=to, device_id_type=MESH)

        mine = [pltpu.make_async_copy(ins[a], outs[a].at[4 * x + 2 * y + c], local_sems.at[a]) for a in range(na)]
        for cp in mine:
            cp.start()
        first = []
        for a in range(na):
            first.append(copy(a, 0, me, sibling, from_input=True))
            first += [copy(a, 1 + j, me, (*chip, c), from_input=True) for j, chip in enumerate(chips)]
        for cp in first:
            cp.start()
        passed = []
        for j, chip in enumerate(chips):
            for a in range(na):
                copy(a, 1 + j, (*chip, c), me).wait_recv()
                fwd = copy(a, 4 + j, (*chip, c), sibling)
                fwd.start()
                passed.append(fwd)
        for a in range(na):
            copy(a, 0, sibling, me).wait_recv()
            for j, chip in enumerate(chips):
                copy(a, 4 + j, (*chip, 1 - c), me).wait_recv()
        for cp in first + passed:
            cp.wait_send()
        for cp in mine:
            cp.wait()

    any_spec = pl.BlockSpec(memory_space=pl.ANY)
    return pl.pallas_call(
        body, name=name,
        in_specs=[any_spec] * na, out_specs=[any_spec] * na,
        out_shape=[SDS((NDEV,) + a.shape, a.dtype) for a in arrs],
        scratch_shapes=[pltpu.SemaphoreType.DMA((7 * na,)), pltpu.SemaphoreType.DMA((7 * na,)),
                        pltpu.SemaphoreType.DMA((na,))],
    )(*arrs)


def _adamw_math(g, w, m, v):
    m = ADAM_B1 * m + (1.0 - ADAM_B1) * g
    v = ADAM_B2 * v + (1.0 - ADAM_B2) * (g * g)
    m_hat = m / (1.0 - ADAM_B1 ** ADAM_STEP)
    v_hat = v / (1.0 - ADAM_B2 ** ADAM_STEP)
    delta = -ADAM_LR * (m_hat / (jnp.sqrt(v_hat) + ADAM_EPS) + ADAM_WD * w)
    return delta, m, v


def _adamw_layer(own, landed, w, m, v, layer, earlier, tr, name):
    _, rows, cols = w.shape

    def body(o_ref, l_ref, w_ref, m_ref, v_ref, *rest):
        g_out, d_out, m_out, v_out = rest[-4:]
        g = o_ref[...]
        for r in range(NDEV - 1):
            g = g + l_ref[r].astype(F32)
        g_out[...] = g
        d_out[...], m_out[...], v_out[...] = _adamw_math(g, w_ref[...], m_ref[...], v_ref[...])

    lay = pl.BlockSpec((None, tr, cols), lambda i: (layer, i, 0))
    any_spec = pl.BlockSpec(memory_space=pl.ANY)
    prior = list(earlier) if earlier is not None else []
    return pl.pallas_call(
        body, name=name, grid=(rows // tr,),
        in_specs=[pl.BlockSpec((tr, cols), lambda i: (i, 0)),
                  pl.BlockSpec((NDEV - 1, tr, cols), lambda i: (0, i, 0)), lay, lay, lay] + [any_spec] * len(prior),
        out_specs=[lay] * 4, out_shape=[SDS(w.shape, F32)] * 4,
        input_output_aliases={5 + k: k for k in range(len(prior))},
        compiler_params=_params("parallel"),
    )(own, landed, w, m, v, *prior)


def _adamw_gathered(parts, w, m, v, tr):
    rows, cols = w.shape

    def body(p_ref, w_ref, m_ref, v_ref, g_out, d_out, m_out, v_out):
        g = p_ref[0]
        for k in range(1, NDEV):
            g = g + p_ref[k]
        g_out[...] = g
        d_out[...], m_out[...], v_out[...] = _adamw_math(g, w_ref[...], m_ref[...], v_ref[...])

    blk = pl.BlockSpec((tr, cols), lambda i: (i, 0))
    return pl.pallas_call(
        body, name="adamw_small", grid=(rows // tr,),
        in_specs=[pl.BlockSpec((NDEV, tr, cols), lambda i: (0, i, 0)), blk, blk, blk],
        out_specs=[blk] * 4, out_shape=[SDS((rows, cols), F32)] * 4,
        compiler_params=_params("parallel"),
    )(parts, w, m, v)


def _adamw_plain(g, w, m, v):
    def body(g_ref, w_ref, m_ref, v_ref, d_out, m_out, v_out):
        d_out[...], m_out[...], v_out[...] = _adamw_math(g_ref[...], w_ref[...], m_ref[...], v_ref[...])

    return pl.pallas_call(body, name="adamw_conv_shards", out_shape=[SDS(w.shape, F32)] * 3)(g, w, m, v)


SMALL = ("norm_g", "pool_w", "pool_scale", "conv_w", "conv_b", "conv_ln_g", "conv_ln_b",
         "sgu_ln_g", "sgu_ln_b", "sgu_w", "sgu_b", "sc_w")
SMALL_SHAPES = {"norm_g": (D,), "pool_w": (NG, GW, GW), "pool_scale": (BW,), "conv_w": (CONV_K, BW), "conv_b": (BW,),
                "conv_ln_g": (BW,), "conv_ln_b": (BW,), "sgu_ln_g": (BW,), "sgu_ln_b": (BW,),
                "sgu_w": (NG, GW, GW), "sgu_b": (NG, GW), "sc_w": (SHORT_K, BW)}
REPLICATED = tuple(k for k in SMALL if k not in ("conv_w", "sc_w"))
PACK_UNIT = 8 * 128


def _size(shape):
    out = 1
    for s in shape:
        out *= s
    return out


def _padded(a):
    flat = a.reshape(-1)
    pad = -flat.shape[0] % PACK_UNIT
    return jnp.pad(flat, (0, pad)) if pad else flat


def _pack(arrays):
    return jnp.concatenate([_padded(a) for a in arrays]).reshape(-1, 128)


def _unpack(pack, shapes):
    flat = pack.reshape(-1)
    out, off = [], 0
    for shape in shapes:
        size = _size(shape)
        out.append(flat[off:off + size].reshape(shape))
        off += size + (-size % PACK_UNIT)
    return out


def _gathered_weights(wb, wo, conv):
    wb = wb.reshape(NDEV, 4, BW, D // NDEV).transpose(1, 2, 0, 3).reshape(4, BW, D)
    wo = wo.reshape(D, D)
    cw = CONV_K * (BW // NDEV)
    sw = SHORT_K * (BW // NDEV)
    sc_at = cw + (-cw % PACK_UNIT)
    flat = conv.reshape(NDEV, -1)
    conv_w = flat[:, :cw].reshape(NDEV, CONV_K, BW // NDEV).transpose(1, 0, 2).reshape(CONV_K, BW)
    sc_w = flat[:, sc_at:sc_at + sw].reshape(NDEV, SHORT_K, BW // NDEV).transpose(1, 0, 2).reshape(SHORT_K, BW)
    return wb, wo, conv_w, sc_w


def _layer_small_inputs(l, rep, conv_w, sc_w):
    vecs = jnp.stack([rep["pool_scale"][l], rep["conv_b"][l], rep["conv_ln_g"][l], rep["conv_ln_b"][l],
                      rep["sgu_ln_g"][l], rep["sgu_ln_b"][l], jnp.zeros((BW,), F32), jnp.zeros((BW,), F32)])
    conv_w = jnp.pad(conv_w, ((0, 32 - CONV_K), (0, 0)))
    sc_w = jnp.pad(sc_w, ((0, 8 - SHORT_K), (0, 0)))
    return (rep["pool_w"][l], vecs, conv_w, rep["sgu_w"][l], rep["sgu_b"][l].reshape(NG, GW, 1), sc_w)


def kernel(x, norm_g, w_in, pool_w, pool_scale, conv_w, conv_b, conv_ln_g, conv_ln_b, sgu_ln_g, sgu_ln_b, sgu_w, sgu_b, sc_w, w_branch, w_o, final_g, loss_target, m_norm_g, m_w_in, m_pool_w, m_pool_scale, m_conv_w, m_conv_b, m_conv_ln_g, m_conv_ln_b, m_sgu_ln_g, m_sgu_ln_b, m_sgu_w, m_sgu_b, m_sc_w, m_w_branch, m_w_o, m_final_g, v_norm_g, v_w_in, v_pool_w, v_pool_scale, v_conv_w, v_conv_b, v_conv_ln_g, v_conv_ln_b, v_sgu_ln_g, v_sgu_ln_b, v_sgu_w, v_sgu_b, v_sc_w, v_w_branch, v_w_o, v_final_g):
    args = dict(locals())
    w = {k: args[k] for k in SMALL + ("w_in", "w_branch", "w_o", "final_g")}
    mom = {k: args["m_" + k] for k in w}
    var = {k: args["v_" + k] for k in w}
    rep = {k: w[k] for k in REPLICATED}
    xi, yi, ci = _position()
    me = 4 * xi + 2 * yi + ci

    nb, seq, _ = x.shape
    n = nb * seq
    tm = min(512, n)
    tm_in = min(1024, n)
    tm_fwd = min(2048, n)
    tk = min(2048, n)
    tm2 = min(512, n)
    t_fwd = min(256, seq)
    t_bwd = min(256, seq)

    w_in_b = [(w_in[l, :D // 2].astype(BF16), w_in[l, D // 2:].astype(BF16)) for l in range(DEPTH)]
    wb_b = [w_branch[l].astype(BF16).reshape(4 * BW, D // NDEV) for l in range(DEPTH)]
    wo_b = [w_o[l].astype(BF16) for l in range(DEPTH)]
    conv_b_ = [_pack([conv_w[l], sc_w[l]]) for l in range(DEPTH)]

    w_in_g = [_all_gather(list(w_in_b[0]), "w_in_all_gather"), None]
    x2d = x.reshape(n, D)
    saved = []
    ride = _Exchange(gathers=[wb_b[0], wo_b[0], conv_b_[0], w_in_b[1][0]])
    (proj, h), (wb_g, wo_g, conv_g, w_in_upper) = _in_proj(x2d, norm_g[0:1], w_in_g[0], tm_fwd, ride)
    for l in range(DEPTH):
        wb_l, wo_l, conv_l, sc_l = _gathered_weights(wb_g, wo_g, conv_g)
        small = _layer_small_inputs(l, rep, conv_l, sc_l)
        if l == 0:
            ride = _Exchange(gathers=[w_in_b[1][1]])
            (z, y_conv, q_sc, pooled), (w_in_lower,) = _mixers_fwd(proj, *small, nb, seq, t_fwd, ride)
            w_in_g[1] = [w_in_upper, w_in_lower]
            ride = _Exchange(gathers=[wb_b[1], wo_b[1], conv_b_[1]])
        else:
            (proj, h), _ = _in_proj(x2d, norm_g[l:l + 1], w_in_g[l], tm_fwd)
            (z, y_conv, q_sc, pooled), _ = _mixers_fwd(proj, *small, nb, seq, t_fwd)
            ride = None
        (x_new, merged), gathered = _merge_out(z, proj, x2d, wb_l, wo_l, tm2, ride)
        if l == 0:
            wb_g, wo_g, conv_g = gathered
        saved.append((x2d, proj, h, z, y_conv, q_sc, pooled, merged, small, wb_l, wo_l))
        x2d = x_new

    dx, loss_acc, d_final_g = _loss_head(x2d, loss_target.reshape(n, D), final_g.reshape(1, D), tm)
    loss = lax.psum(loss_acc[0, 0], ("x", "y", "c"))

    grads = {k: [None] * DEPTH for k in SMALL}
    big = {"w_in": None, "w_branch": None, "w_o": None}
    big_w = {"w_in": (w_in, m_w_in, v_w_in, (DEPTH, D, SHARD), 256),
             "w_branch": (w_branch, m_w_branch, v_w_branch, (DEPTH, 4 * BW, D // NDEV), 512),
             "w_o": (w_o, m_w_o, v_w_o, (DEPTH, D // NDEV, D), 64)}

    def adamw(k, own, landed, l):
        wk, mk, vk, shape, tr = big_w[k]
        big[k] = _adamw_layer(own, landed, wk.reshape(shape), mk.reshape(shape), vk.reshape(shape), l, big[k], tr,
                              "adamw_" + k)

    for l in reversed(range(DEPTH)):
        x_in, proj, h, z, y_conv, q_sc, pooled, merged, small, wb_l, wo_l = saved[l]
        dz, d_proj, d_wo, d_wo_b, d_wb, d_wb_b = _merge_out_bwd(dx, z, proj, merged, wb_l, wo_l, tm)
        ride = _Exchange(scatters=[(d_wo_b.reshape(NDEV, D // NDEV, D), d_wo.reshape(NDEV, D // NDEV, D)),
                                   (d_wb_b.reshape(NDEV, 4 * BW, D // NDEV), d_wb.reshape(NDEV, 4 * BW, D // NDEV))])
        (d_proj, dvec, dconvw, dscw, dpoolw, dsguw, dsgub), (wo_landed, wo_own, wb_landed, wb_own) = _mixers_bwd(
            proj, dz, y_conv, q_sc, pooled, d_proj, *small, nb, seq, t_bwd, ride)
        d_w_in, d_w_in_b = _grad_w_in(h, d_proj, tk)
        ride = _Exchange(scatters=[(d_w_in_b, d_w_in)])
        (dx, d_norm_g), (in_landed, in_own) = _in_proj_bwd(d_proj, w_in_g[l], x_in, norm_g[l:l + 1], dx, tm_in, ride)
        adamw("w_o", wo_own, wo_landed, l)
        adamw("w_branch", wb_own, wb_landed, l)
        adamw("w_in", in_own, in_landed, l)
        grads["norm_g"][l] = d_norm_g[0]
        grads["pool_w"][l] = dpoolw
        grads["pool_scale"][l] = dvec[0]
        grads["conv_w"][l] = dconvw[:CONV_K]
        grads["conv_b"][l] = dvec[1]
        grads["conv_ln_g"][l] = dvec[2]
        grads["conv_ln_b"][l] = dvec[3]
        grads["sgu_ln_g"][l] = dvec[4]
        grads["sgu_ln_b"][l] = dvec[5]
        grads["sgu_w"][l] = dsguw
        grads["sgu_b"][l] = dsgub[:NG]
        grads["sc_w"][l] = dscw[:SHORT_K]
    grad_x = dx.reshape(nb, seq, D)
    small_grads = {k: jnp.stack(v) for k, v in grads.items()}

    names = list(SMALL) + ["final_g"]
    shapes = [(DEPTH,) + SMALL_SHAPES[k] for k in SMALL] + [(D,)]
    whole = REPLICATED + ("final_g",)
    part_pack = _pack([small_grads[k] for k in SMALL] + [d_final_g[0]])
    w_pack = _pack([w[k] if k in whole else jnp.zeros(s, F32) for k, s in zip(names, shapes)])
    m_pack = _pack([mom[k] if k in whole else jnp.zeros(s, F32) for k, s in zip(names, shapes)])
    v_pack = _pack([var[k] if k in whole else jnp.ones(s, F32) for k, s in zip(names, shapes)])
    (parts,) = _all_gather([part_pack], "small_grads_all_gather")
    g_pack, d_pack, nm_pack, nv_pack = _adamw_gathered(parts, w_pack, m_pack, v_pack, part_pack.shape[0] // 4)
    out_g = dict(zip(names, _unpack(g_pack, shapes)))
    out_d = dict(zip(names, _unpack(d_pack, shapes)))
    out_m = dict(zip(names, _unpack(nm_pack, shapes)))
    out_v = dict(zip(names, _unpack(nv_pack, shapes)))

    col = me * (BW // NDEV)
    for k in ("conv_w", "sc_w"):
        out_g[k] = lax.dynamic_slice_in_dim(out_g[k], col, BW // NDEV, axis=2)
    sharded_shapes = [w["conv_w"].shape, w["sc_w"].shape]
    d2, m2, v2 = _adamw_plain(_pack([out_g["conv_w"], out_g["sc_w"]]), _pack([w["conv_w"], w["sc_w"]]),
                              _pack([mom["conv_w"], mom["sc_w"]]), _pack([var["conv_w"], var["sc_w"]]))
    for dst, pk in ((out_d, d2), (out_m, m2), (out_v, v2)):
        dst["conv_w"], dst["sc_w"] = _unpack(pk, sharded_shapes)

    for k in big:
        out_g[k], out_d[k], out_m[k], out_v[k] = [o.reshape(w[k].shape) for o in big[k]]

    order = ("norm_g", "w_in", "pool_w", "pool_scale", "conv_w", "conv_b", "conv_ln_g", "conv_ln_b", "sgu_ln_g",
             "sgu_ln_b", "sgu_w", "sgu_b", "sc_w", "w_branch", "w_o", "final_g")
    return (loss, grad_x, *[out_g[k] for k in order], *[out_d[k] for k in order],
            *[out_m[k] for k in order], *[out_v[k] for k in order])
```

```python
import functools

import jax
import jax.numpy as jnp
from jax import lax
from jax.experimental import pallas as pl
from jax.experimental.pallas import tpu as pltpu

F32 = jnp.float32
BF16 = jnp.bfloat16
SDS = jax.ShapeDtypeStruct
MESH = pl.DeviceIdType.MESH

D = 1024
BW = 512
NG = 4
GW = 128
CONV_K = 31
SHORT_K = 3
POOL_WINDOWS = (2, 4, 8, 16)
N_PIECE_COLS = 12 * BW
IN_COLS = N_PIECE_COLS + 4 * D
NDEV = 8
SHARD = IN_COLS // NDEV
DEPTH = 2
RMS_EPS = 1e-6
LN_EPS = 1e-5
HALO_POOL, HALO_CONV, HALO_SC = 16, 32, 8

ADAM_LR = 0.001
ADAM_B1 = 0.9
ADAM_B2 = 0.999
ADAM_EPS = 1e-08
ADAM_WD = 0.01
ADAM_STEP = 10

VMEM_LIMIT = 56 * 1024 * 1024


def _params(*sem):
    return pltpu.CompilerParams(dimension_semantics=sem, vmem_limit_bytes=VMEM_LIMIT)


def _sigmoid(x):
    return jax.nn.sigmoid(x)


def _silu(x):
    return x * _sigmoid(x)


def _dsilu(x):
    s = _sigmoid(x)
    return s * (1.0 + x * (1.0 - s))


def _colsum(x):
    return jnp.sum(x, axis=0, keepdims=True)


def _dot(a, b):
    return jnp.dot(a, b, preferred_element_type=F32)


def _dot_nt(a, b):
    return lax.dot_general(a, b, (((1,), (1,)), ((), ())), preferred_element_type=F32)


def _dot_tn(a, b):
    return lax.dot_general(a, b, (((0,), (0,)), ((), ())), preferred_element_type=F32)


def _layer_norm_stats(y):
    mu = jnp.mean(y, axis=-1, keepdims=True)
    yc = y - mu
    var = jnp.mean(yc * yc, axis=-1, keepdims=True)
    rstd = lax.rsqrt(var + LN_EPS)
    return yc * rstd, rstd


def _layer_norm_bwd(d_hat, hat, rstd):
    return rstd * (d_hat - jnp.mean(d_hat, axis=-1, keepdims=True)
                   - hat * jnp.mean(d_hat * hat, axis=-1, keepdims=True))


def _position():
    return lax.axis_index("x"), lax.axis_index("y"), lax.axis_index("c")


def _flip(v, bit):
    return 1 - v if bit else v


class _Exchange:
    def __init__(self, gathers=(), scatters=()):
        self.gathers = list(gathers)
        self.scatters = list(scatters)
        self.inputs = self.gathers + [a for pair in self.scatters for a in pair]
        self.out_shapes = [SDS((NDEV,) + a.shape, a.dtype) for a in self.gathers]
        for send, keep in self.scatters:
            self.out_shapes += [SDS((NDEV - 1,) + send.shape[1:], send.dtype), SDS(keep.shape[1:], keep.dtype)]
        n = len(self.gathers) + len(self.scatters)
        self.scratch = [pltpu.SemaphoreType.DMA((7 * n,)), pltpu.SemaphoreType.DMA((7 * n,)),
                        pltpu.SemaphoreType.DMA((n,))]

    def _copies(self, ins, outs, sems):
        send_sems, recv_sems, local_sems = sems
        x, y, c = _position()
        me = 4 * x + 2 * y + c
        ng = len(self.gathers)
        local, remote = [], []
        for a in range(ng + len(self.scatters)):
            if a < ng:
                local.append(pltpu.make_async_copy(ins[a], outs[a].at[me], local_sems.at[a]))
            else:
                s = a - ng
                send, keep = ins[ng + 2 * s], ins[ng + 2 * s + 1]
                landed, own = outs[ng + 2 * s], outs[ng + 2 * s + 1]
                local.append(pltpu.make_async_copy(keep.at[me], own, local_sems.at[a]))
            for r in range(1, NDEV):
                peer = (_flip(x, r & 4), _flip(y, r & 2), _flip(c, r & 1))
                if a < ng:
                    src, dst = ins[a], outs[a].at[me]
                else:
                    src, dst = send.at[jnp.bitwise_xor(me, r)], landed.at[r - 1]
                remote.append(pltpu.make_async_remote_copy(
                    src_ref=src, dst_ref=dst, send_sem=send_sems.at[7 * a + r - 1],
                    recv_sem=recv_sems.at[7 * a + r - 1], device_id=peer, device_id_type=MESH))
        return local, remote

    def start(self, ins, outs, sems):
        local, remote = self._copies(ins, outs, sems)
        for cp in remote + local:
            cp.start()

    def finish(self, ins, outs, sems):
        local, remote = self._copies(ins, outs, sems)
        for cp in remote + local:
            cp.wait()


def _carried_call(body, comm, args, *, name, grid, in_specs, out_specs, out_shape, scratch_shapes=(),
                  semantics, input_output_aliases=None):
    aliases = input_output_aliases or {}
    if comm is None:
        out = pl.pallas_call(body, name=name, grid=grid, in_specs=in_specs, out_specs=out_specs, out_shape=out_shape,
                             scratch_shapes=list(scratch_shapes), input_output_aliases=aliases,
                             compiler_params=_params(*semantics))(*args)
        return list(out), []
    n_in, n_out, n_scr = len(in_specs), len(out_specs), len(scratch_shapes)
    c_in, c_out = len(comm.inputs), len(comm.out_shapes)

    def full_body(*refs):
        refs = list(refs)
        ins, refs = refs[:n_in], refs[n_in:]
        cins, refs = refs[:c_in], refs[c_in:]
        outs, refs = refs[:n_out], refs[n_out:]
        couts, refs = refs[:c_out], refs[c_out:]
        scr, sems = refs[:n_scr], refs[n_scr:]
        ids = [pl.program_id(d) for d in range(len(grid))]
        first = functools.reduce(jnp.logical_and, [i == 0 for i in ids])
        last = functools.reduce(jnp.logical_and, [i == g - 1 for i, g in zip(ids, grid)])

        @pl.when(first)
        def _():
            comm.start(cins, couts, sems)

        body(*ins, *outs, *scr)

        @pl.when(last)
        def _():
            comm.finish(cins, couts, sems)

    any_spec = pl.BlockSpec(memory_space=pl.ANY)
    out = pl.pallas_call(
        full_body, name=name, grid=grid,
        in_specs=list(in_specs) + [any_spec] * c_in, out_specs=list(out_specs) + [any_spec] * c_out,
        out_shape=list(out_shape) + comm.out_shapes, scratch_shapes=list(scratch_shapes) + comm.scratch,
        input_output_aliases=aliases, compiler_params=_params(*["arbitrary"] * len(grid)),
    )(*args, *comm.inputs)
    return list(out[:n_out]), list(out[n_out:])


def _in_proj(x2d, g_row, w_g, tm, comm=None):
    n = x2d.shape[0]
    half = D // 2

    def body(x_ref, g_ref, wu_ref, wl_ref, proj_ref, h_ref):
        @pl.when(pl.program_id(1) == 0)
        def _():
            x = x_ref[...]
            r = lax.rsqrt(jnp.mean(x * x, axis=-1, keepdims=True) + RMS_EPS)
            h_ref[...] = (x * r * g_ref[...]).astype(BF16)

        proj_ref[...] = (_dot(h_ref[:, :half], wu_ref[...]) + _dot(h_ref[:, half:], wl_ref[...])).astype(BF16)

    return _carried_call(
        body, comm, (x2d, g_row, *w_g), name="in_proj", grid=(n // tm, NDEV),
        in_specs=[pl.BlockSpec((tm, D), lambda i, j: (i, 0)),
                  pl.BlockSpec((1, D), lambda i, j: (0, 0)),
                  pl.BlockSpec((None, half, SHARD), lambda i, j: (j, 0, 0)),
                  pl.BlockSpec((None, half, SHARD), lambda i, j: (j, 0, 0))],
        out_specs=[pl.BlockSpec((tm, SHARD), lambda i, j: (i, j)),
                   pl.BlockSpec((tm, D), lambda i, j: (i, 0))],
        out_shape=[SDS((n, IN_COLS), BF16), SDS((n, D), BF16)],
        semantics=("parallel", "arbitrary"))


def _tril_bf16(w):
    r = lax.broadcasted_iota(jnp.int32, (GW, GW), 0)
    c = lax.broadcasted_iota(jnp.int32, (GW, GW), 1)
    return jnp.where(r >= c, w, 0.0).astype(BF16)


def _mixers_fwd(proj, x2d, wb, wo, pool_w, vecs, conv_w, sgu_w, sgu_bc, sc_w, nb, seq, t, comm=None):
    n = nb * seq
    nt = seq // t

    def body(p_ref, x_ref, wb_ref, wo_ref, poolw_ref, vec_ref, convw_ref, sguw_ref, sgub_ref, scw_ref,
             z_ref, y_ref, q_ref, pooled_ref, xo_ref, m_ref, pext, uext, vext):
        i = pl.program_id(1)

        def gated(k):
            gate = _sigmoid(p_ref[:, N_PIECE_COLS + k * D:N_PIECE_COLS + (k + 1) * D].astype(F32))
            return gate * _dot(z_ref[:, k * BW:(k + 1) * BW], wb_ref[k])

        @pl.when(i == 0)
        def _():
            pext[0:HALO_POOL, :] = jnp.zeros((HALO_POOL, BW), F32)
            uext[0:HALO_CONV, :] = jnp.zeros((HALO_CONV, BW), F32)
            vext[0:HALO_SC, :] = jnp.zeros((HALO_SC, BW), F32)

        @pl.when(i > 0)
        def _():
            pext[0:HALO_POOL, :] = pext[t:t + HALO_POOL, :]
            uext[0:HALO_CONV, :] = uext[t:t + HALO_CONV, :]
            vext[0:HALO_SC, :] = vext[t:t + HALO_SC, :]

        def piece(k):
            return p_ref[:, k * BW:(k + 1) * BW].astype(F32)

        pos = lax.broadcasted_iota(jnp.int32, (t, 1), 0) + i * t

        px = piece(0)
        pext[HALO_POOL:, :] = px
        for j, win in enumerate(POOL_WINDOWS):
            cols = slice(j * GW, (j + 1) * GW)
            s = pext[:, cols]
            step = 1
            while step < win:
                s = s + pltpu.roll(s, step, 0)
                step *= 2
            cnt = jnp.minimum(pos + 1, win).astype(F32)
            pooled = (s[HALO_POOL:, :] / cnt - px[:, cols]).astype(BF16)
            pooled_ref[:, cols] = pooled
            mixed = _dot(pooled, poolw_ref[j].astype(BF16))
            z_ref[:, cols] = (mixed * vec_ref[0:1, cols] * _silu(p_ref[:, BW + j * GW:BW + (j + 1) * GW].astype(F32))).astype(BF16)
        merged = gated(0)

        uext[HALO_CONV:, :] = piece(2) * _sigmoid(piece(3))
        ue = uext[...]
        acc = jnp.zeros((t, BW), F32)
        for b in range(8):
            rb = ue if b == 0 else pltpu.roll(ue, b, 0)
            for a in range((CONV_K - b + 7) // 8):
                j = 8 * a + b
                acc = acc + convw_ref[CONV_K - 1 - j:CONV_K - j, :] * rb[HALO_CONV - 8 * a:HALO_CONV - 8 * a + t, :]
        y = acc + vec_ref[1:2, :]
        y_ref[...] = y
        yhat, _ = _layer_norm_stats(y)
        act = _silu(yhat * vec_ref[2:3, :] + vec_ref[3:4, :])
        z_ref[:, BW:2 * BW] = (act * _silu(piece(4))).astype(BF16)
        merged = merged + gated(1)

        vhat, _ = _layer_norm_stats(piece(6))
        vn = (vhat * vec_ref[4:5, :] + vec_ref[5:6, :]).astype(BF16)
        for g in range(NG):
            ws = _tril_bf16(sguw_ref[g])
            cols = slice(g * GW, (g + 1) * GW)
            for c in range(t // GW):
                rows = slice(c * GW, (c + 1) * GW)
                sp = _dot(ws, vn[rows, cols]) + sgub_ref[g]
                gate = _silu(p_ref[rows, 7 * BW + g * GW:7 * BW + (g + 1) * GW].astype(F32))
                z_ref[rows, 2 * BW + g * GW:2 * BW + (g + 1) * GW] = (
                    p_ref[rows, 5 * BW + g * GW:5 * BW + (g + 1) * GW].astype(F32) * sp * gate).astype(BF16)
        merged = merged + gated(2)

        vext[HALO_SC:, :] = piece(9) * piece(10)
        ve = vext[...]
        q = jnp.zeros((t, BW), F32)
        for j in range(SHORT_K):
            sh = ve if j == 0 else pltpu.roll(ve, j, 0)
            q = q + scw_ref[SHORT_K - 1 - j:SHORT_K - j, :] * sh[HALO_SC:, :]
        q_ref[...] = q
        z_ref[:, 3 * BW:] = (piece(8) * q * _silu(piece(11))).astype(BF16)
        mb = (merged + gated(3)).astype(BF16)
        m_ref[...] = mb
        xo_ref[...] = x_ref[...] + _dot(mb, wo_ref[...])

    row = lambda b, i: (b * nt + i, 0)
    full2 = lambda b, i: (0, 0)
    full3 = lambda b, i: (0, 0, 0)
    return _carried_call(
        body, comm, (proj, x2d, wb, wo, pool_w, vecs, conv_w, sgu_w, sgu_bc, sc_w), name="mixers_fwd", grid=(nb, nt),
        in_specs=[pl.BlockSpec((t, IN_COLS), row),
                  pl.BlockSpec((t, D), row),
                  pl.BlockSpec((4, BW, D), full3),
                  pl.BlockSpec((D, D), full2),
                  pl.BlockSpec((NG, GW, GW), full3),
                  pl.BlockSpec((8, BW), full2),
                  pl.BlockSpec((32, BW), full2),
                  pl.BlockSpec((NG, GW, GW), full3),
                  pl.BlockSpec((NG, GW, 1), full3),
                  pl.BlockSpec((8, BW), full2)],
        out_specs=[pl.BlockSpec((t, 4 * BW), row), pl.BlockSpec((t, BW), row),
                   pl.BlockSpec((t, BW), row), pl.BlockSpec((t, BW), row),
                   pl.BlockSpec((t, D), row), pl.BlockSpec((t, D), row)],
        out_shape=[SDS((n, 4 * BW), BF16), SDS((n, BW), F32), SDS((n, BW), F32), SDS((n, BW), BF16),
                   SDS((n, D), F32), SDS((n, D), BF16)],
        scratch_shapes=[pltpu.VMEM((t + HALO_POOL, BW), F32), pltpu.VMEM((t + HALO_CONV, BW), F32),
                        pltpu.VMEM((t + HALO_SC, BW), F32)],
        semantics=("arbitrary", "arbitrary"))


def _loss_head(x2d, tgt2d, g_row, tm):
    n = x2d.shape[0]

    def body(x_ref, t_ref, g_ref, dx_ref, loss_ref, dg_ref):
        @pl.when(pl.program_id(0) == 0)
        def _():
            loss_ref[...] = jnp.zeros_like(loss_ref)
            dg_ref[...] = jnp.zeros_like(dg_ref)

        x = x_ref[...]
        g = g_ref[...]
        r = lax.rsqrt(jnp.mean(x * x, axis=-1, keepdims=True) + RMS_EPS)
        xn = x * r
        e = xn * g - t_ref[...]
        loss_ref[...] += 0.5 * jnp.sum(jnp.mean(e * e, axis=-1, keepdims=True))
        dy = e * (1.0 / D)
        dg_ref[...] += _colsum(dy * xn)
        gy = dy * g
        dx_ref[...] = r * gy - xn * (r * jnp.mean(gy * xn, axis=-1, keepdims=True))

    return pl.pallas_call(
        body, name="loss_head", grid=(n // tm,),
        in_specs=[pl.BlockSpec((tm, D), lambda i: (i, 0)), pl.BlockSpec((tm, D), lambda i: (i, 0)),
                  pl.BlockSpec((1, D), lambda i: (0, 0))],
        out_specs=[pl.BlockSpec((tm, D), lambda i: (i, 0)), pl.BlockSpec((8, 128), lambda i: (0, 0)),
                   pl.BlockSpec((1, D), lambda i: (0, 0))],
        out_shape=[SDS((n, D), F32), SDS((8, 128), F32), SDS((1, D), F32)],
        compiler_params=_params("arbitrary"),
    )(x2d, tgt2d, g_row)


def _merge_out_bwd(dxo, z, proj, merged, wb, wo, tm):
    n = dxo.shape[0]
    ni = n // tm
    sub = D // NDEV

    def body(dx_ref, z_ref, mg_ref, m_ref, wb_ref, wo_ref,
             dbo_ref, dp_ref, gwo_ref, gwo_b_ref, gwb_ref, gwb_b_ref,
             dm_ref, acc_wo, acc_wb, half_wo, half_wb, sems):
        i = pl.program_id(0)
        h = pl.program_id(1)

        @pl.when(jnp.logical_and(i == 0, h == 0))
        def _():
            acc_wo[...] = jnp.zeros_like(acc_wo)
            acc_wb[...] = jnp.zeros_like(acc_wb)

        @pl.when(h == 0)
        def _():
            dxb = dx_ref[...].astype(BF16)
            dm_ref[...] = _dot_nt(dxb, wo_ref[...])
            acc_wo[...] += _dot_tn(m_ref[...], dxb)

        dm = dm_ref[...]
        for k in range(2):
            zk = z_ref[:, k * BW:(k + 1) * BW]
            w = wb_ref[2 * h + k]
            bo = _dot(zk, w)
            sig = _sigmoid(mg_ref[:, k * D:(k + 1) * D].astype(F32))
            dbo = (dm * sig).astype(BF16)
            dp_ref[:, k * D:(k + 1) * D] = (dm * bo * sig * (1.0 - sig)).astype(BF16)
            dbo_ref[:, k * D:(k + 1) * D] = dbo
            acc_wb[2 * h + k] += _dot_tn(zk, dbo)

        @pl.when(jnp.logical_and(i == ni - 1, h == 1))
        def _():
            half_wo[...] = acc_wo[...].astype(BF16)
            half_wb[...] = acc_wb[...].astype(BF16)
            copies = [pltpu.make_async_copy(acc_wo, gwo_ref, sems.at[0]),
                      pltpu.make_async_copy(half_wo, gwo_b_ref, sems.at[1])]
            for d in range(NDEV):
                cols = pl.ds(d * sub, sub)
                copies.append(pltpu.make_async_copy(acc_wb.at[:, :, cols], gwb_ref.at[d], sems.at[2 + 2 * d]))
                copies.append(pltpu.make_async_copy(half_wb.at[:, :, cols], gwb_b_ref.at[d], sems.at[3 + 2 * d]))
            for cp in copies:
                cp.start()
            for cp in copies:
                cp.wait()

    any_spec = pl.BlockSpec(memory_space=pl.ANY)
    return pl.pallas_call(
        body, name="merge_out_bwd", grid=(ni, 2),
        in_specs=[pl.BlockSpec((tm, D), lambda i, h: (i, 0)),
                  pl.BlockSpec((tm, 2 * BW), lambda i, h: (i, h)),
                  pl.BlockSpec((tm, 2 * D), lambda i, h: (i, 3 + h)),
                  pl.BlockSpec((tm, D), lambda i, h: (i, 0)),
                  pl.BlockSpec((4, BW, D), lambda i, h: (0, 0, 0)),
                  pl.BlockSpec((D, D), lambda i, h: (0, 0))],
        out_specs=[pl.BlockSpec((tm, 2 * D), lambda i, h: (i, h)),
                   pl.BlockSpec((tm, 2 * D), lambda i, h: (i, 3 + h)),
                   any_spec, any_spec, any_spec, any_spec],
        out_shape=[SDS((n, 4 * D), BF16), SDS((n, IN_COLS), BF16), SDS((D, D), F32), SDS((D, D), BF16),
                   SDS((NDEV, 4, BW, sub), F32), SDS((NDEV, 4, BW, sub), BF16)],
        scratch_shapes=[pltpu.VMEM((tm, D), F32), pltpu.VMEM((D, D), F32), pltpu.VMEM((4, BW, D), F32),
                        pltpu.VMEM((D, D), BF16), pltpu.VMEM((4, BW, D), BF16),
                        pltpu.SemaphoreType.DMA((2 + 2 * NDEV,))],
        compiler_params=_params("arbitrary", "arbitrary"),
    )(dxo, z, proj, merged, wb, wo)


def _grad_w_in(h, d_proj, tk, comm=None):
    nk = h.shape[0] // tk

    def body(h_ref, dp_ref, o_ref, ob_ref):
        k = pl.program_id(1)

        @pl.when(k == 0)
        def _():
            o_ref[...] = jnp.zeros_like(o_ref)

        o_ref[...] += _dot_tn(h_ref[...], dp_ref[...])

        @pl.when(k == nk - 1)
        def _():
            ob_ref[...] = o_ref[...].astype(BF16)

    out_spec = pl.BlockSpec((None, D, SHARD), lambda j, k: (j, 0, 0))
    return _carried_call(
        body, comm, (h, d_proj), name="grad_w_in", grid=(NDEV, nk),
        in_specs=[pl.BlockSpec((tk, D), lambda j, k: (k, 0)), pl.BlockSpec((tk, SHARD), lambda j, k: (k, j))],
        out_specs=[out_spec, out_spec], out_shape=[SDS((NDEV, D, SHARD), F32), SDS((NDEV, D, SHARD), BF16)],
        semantics=("parallel", "arbitrary"))


def _mixers_bwd(proj, d_bo, wb, y_conv, q_sc, pooled, d_proj, pool_w, vecs, conv_w, sgu_w, sgu_bc, sc_w, nb, seq, t, comm=None):
    n = nb * seq
    nt = seq // t

    def body(p_ref, dbo_ref, wb_ref, y_ref, q_ref, pooled_ref, dpin_ref,
             poolw_ref, vec_ref, convw_ref, sguw_ref, sgub_ref, scw_ref,
             dp_ref, dvec_ref, dconvw_ref, dscw_ref, dpoolw_ref, dsguw_ref, dsgub_ref,
             rext, dyext, dqext, bacc):
        b = pl.program_id(0)
        i = pl.program_id(1)
        first = jnp.logical_and(b == 0, i == 0)
        last = jnp.logical_and(b == nb - 1, i == nt - 1)

        @pl.when(first)
        def _():
            dvec_ref[...] = jnp.zeros_like(dvec_ref)
            dconvw_ref[...] = jnp.zeros_like(dconvw_ref)
            dscw_ref[...] = jnp.zeros_like(dscw_ref)
            dpoolw_ref[...] = jnp.zeros_like(dpoolw_ref)
            dsguw_ref[...] = jnp.zeros_like(dsguw_ref)
            dsgub_ref[...] = jnp.zeros_like(dsgub_ref)
            bacc[...] = jnp.zeros_like(bacc)

        @pl.when(i == 0)
        def _():
            rext[t:, :] = jnp.zeros((HALO_POOL, BW), F32)
            dyext[t:, :] = jnp.zeros((HALO_CONV, BW), F32)
            dqext[t:, :] = jnp.zeros((HALO_SC, BW), F32)

        @pl.when(i > 0)
        def _():
            rext[t:, :] = rext[0:HALO_POOL, :]
            dyext[t:, :] = dyext[0:HALO_CONV, :]
            dqext[t:, :] = dqext[0:HALO_SC, :]

        def piece(k):
            return p_ref[:, k * BW:(k + 1) * BW].astype(F32)

        def put(k, v):
            dp_ref[:, k * BW:(k + 1) * BW] = v.astype(BF16)

        def dz(k):
            return _dot_nt(dbo_ref[:, k * D:(k + 1) * D], wb_ref[k])

        pos = lax.broadcasted_iota(jnp.int32, (t, 1), 0) + (nt - 1 - i) * t

        dzp = dz(0)
        pg = piece(1)
        d_pm = dzp * _silu(pg)
        for j, win in enumerate(POOL_WINDOWS):
            cols = slice(j * GW, (j + 1) * GW)
            pw = poolw_ref[j].astype(BF16)
            pooled = pooled_ref[:, cols]
            mixed = _dot(pooled, pw)
            scale = vec_ref[0:1, cols]
            dvec_ref[0:1, cols] += _colsum(d_pm[:, cols] * mixed)
            dp_ref[:, BW + j * GW:BW + (j + 1) * GW] = (dzp[:, cols] * (mixed * scale) * _dsilu(pg[:, cols])).astype(BF16)
            d_mixed = (d_pm[:, cols] * scale).astype(BF16)
            dpoolw_ref[j] += _dot_tn(pooled, d_mixed)
            d_pooled = _dot_nt(d_mixed, pw)
            cnt = jnp.minimum(pos + 1, win).astype(F32)
            rext[0:t, cols] = d_pooled / cnt
            s = rext[:, cols]
            step = 1
            while step < win:
                s = s + pltpu.roll(s, t + HALO_POOL - step, 0)
                step *= 2
            dp_ref[:, cols] = (s[0:t, :] - d_pooled).astype(BF16)

        dzc = dz(1)
        cgate = piece(4)
        yhat, rstd = _layer_norm_stats(y_ref[...])
        ln = yhat * vec_ref[2:3, :] + vec_ref[3:4, :]
        put(4, dzc * _silu(ln) * _dsilu(cgate))
        d_ln = dzc * _silu(cgate) * _dsilu(ln)
        dvec_ref[2:3, :] += _colsum(d_ln * yhat)
        dvec_ref[3:4, :] += _colsum(d_ln)
        dy = _layer_norm_bwd(d_ln * vec_ref[2:3, :], yhat, rstd)
        dvec_ref[1:2, :] += _colsum(dy)
        dyext[0:t, :] = dy
        dye = dyext[...]
        ca = piece(2)
        sig_cb = _sigmoid(piece(3))
        u = ca * sig_cb
        du = jnp.zeros((t, BW), F32)
        for b in range(8):
            rb = dye if b == 0 else pltpu.roll(dye, t + HALO_CONV - b, 0)
            for a in range((CONV_K - b + 7) // 8):
                j = 8 * a + b
                sh = rb[8 * a:8 * a + t, :]
                du = du + convw_ref[CONV_K - 1 - j:CONV_K - j, :] * sh
                dconvw_ref[CONV_K - 1 - j:CONV_K - j, :] += _colsum(u * sh)
        put(2, du * sig_cb)
        put(3, du * u * (1.0 - sig_cb))

        dzg = dz(2)
        gu = piece(5)
        ggate = piece(7)
        vhat, vrstd = _layer_norm_stats(piece(6))
        vn = (vhat * vec_ref[4:5, :] + vec_ref[5:6, :]).astype(BF16)
        t1 = dzg * _silu(ggate)
        d_sp_all = t1 * gu
        d_vn_cols = []
        sp_cols = []
        for g in range(NG):
            ws = _tril_bf16(sguw_ref[g])
            cols = slice(g * GW, (g + 1) * GW)
            d_vn_rows = []
            sp_rows = []
            for c in range(t // GW):
                rows = slice(c * GW, (c + 1) * GW)
                sp_rows.append(_dot(ws, vn[rows, cols]) + sgub_ref[g])
                d_sp = d_sp_all[rows, cols]
                bacc[g] += d_sp
                d_spb = d_sp.astype(BF16)
                d_vn_rows.append(_dot_tn(ws, d_spb))
                dsguw_ref[g] += _dot_nt(d_spb, vn[rows, cols])
            d_vn_cols.append(jnp.concatenate(d_vn_rows, axis=0))
            sp_cols.append(jnp.concatenate(sp_rows, axis=0))
        d_vn = jnp.concatenate(d_vn_cols, axis=1)
        sp = jnp.concatenate(sp_cols, axis=1)
        put(5, t1 * sp)
        put(7, dzg * gu * sp * _dsilu(ggate))
        dvec_ref[4:5, :] += _colsum(d_vn * vhat)
        dvec_ref[5:6, :] += _colsum(d_vn)
        put(6, _layer_norm_bwd(d_vn * vec_ref[4:5, :], vhat, vrstd))

        dzs = dz(3)
        sb = piece(8)
        scp = piece(9)
        sx = piece(10)
        sgate = piece(11)
        q = q_ref[...]
        t2 = dzs * _silu(sgate)
        put(8, t2 * q)
        put(11, dzs * sb * q * _dsilu(sgate))
        dqext[0:t, :] = t2 * sb
        dqe = dqext[...]
        vv = scp * sx
        dv = jnp.zeros((t, BW), F32)
        for j in range(SHORT_K):
            sh = (dqe if j == 0 else pltpu.roll(dqe, t + HALO_SC - j, 0))[0:t, :]
            dv = dv + scw_ref[SHORT_K - 1 - j:SHORT_K - j, :] * sh
            dscw_ref[SHORT_K - 1 - j:SHORT_K - j, :] += _colsum(vv * sh)
        put(9, dv * sx)
        put(10, dv * scp)

        @pl.when(last)
        def _():
            r = lax.broadcasted_iota(jnp.int32, (GW, GW), 0)
            c = lax.broadcasted_iota(jnp.int32, (GW, GW), 1)
            for g in range(NG):
                dsguw_ref[g] = jnp.where(r >= c, dsguw_ref[g], 0.0)
                dsgub_ref[g:g + 1, :] = _colsum(bacc[g].T)

    row = lambda b, i: (b * nt + (nt - 1 - i), 0)
    full2 = lambda b, i: (0, 0)
    full3 = lambda b, i: (0, 0, 0)
    return _carried_call(
        body, comm, (proj, d_bo, wb, y_conv, q_sc, pooled, d_proj, pool_w, vecs, conv_w, sgu_w, sgu_bc, sc_w),
        name="mixers_bwd", grid=(nb, nt),
        in_specs=[pl.BlockSpec((t, N_PIECE_COLS), row),
                  pl.BlockSpec((t, 4 * D), row),
                  pl.BlockSpec((4, BW, D), full3),
                  pl.BlockSpec((t, BW), row), pl.BlockSpec((t, BW), row), pl.BlockSpec((t, BW), row),
                  pl.BlockSpec(memory_space=pl.ANY),
                  pl.BlockSpec((NG, GW, GW), full3),
                  pl.BlockSpec((8, BW), full2),
                  pl.BlockSpec((32, BW), full2),
                  pl.BlockSpec((NG, GW, GW), full3),
                  pl.BlockSpec((NG, GW, 1), full3),
                  pl.BlockSpec((8, BW), full2)],
        out_specs=[pl.BlockSpec((t, N_PIECE_COLS), row),
                   pl.BlockSpec((8, BW), full2), pl.BlockSpec((32, BW), full2), pl.BlockSpec((8, BW), full2),
                   pl.BlockSpec((NG, GW, GW), full3), pl.BlockSpec((NG, GW, GW), full3),
                   pl.BlockSpec((8, GW), full2)],
        out_shape=[SDS((n, IN_COLS), BF16), SDS((8, BW), F32), SDS((32, BW), F32), SDS((8, BW), F32),
                   SDS((NG, GW, GW), F32), SDS((NG, GW, GW), F32), SDS((8, GW), F32)],
        scratch_shapes=[pltpu.VMEM((t + HALO_POOL, BW), F32), pltpu.VMEM((t + HALO_CONV, BW), F32),
                        pltpu.VMEM((t + HALO_SC, BW), F32), pltpu.VMEM((NG, GW, GW), F32)],
        input_output_aliases={6: 0}, semantics=("arbitrary", "arbitrary"))


def _in_proj_bwd(d_proj, w_g, x2d, g_row, dxo, tm, comm=None):
    n = x2d.shape[0]
    half = D // 2

    def body(dp_ref, wu_ref, wl_ref, x_hbm, g_ref, dxo_hbm, dx_ref, dg_ref, acc, x_buf, dxo_buf, sems):
        i = pl.program_id(0)
        k = pl.program_id(1)
        rows = pl.ds(pl.multiple_of(i * tm, tm), tm)
        late = (pltpu.make_async_copy(x_hbm.at[rows], x_buf, sems.at[0]),
                pltpu.make_async_copy(dxo_hbm.at[rows], dxo_buf, sems.at[1]))

        @pl.when(jnp.logical_and(i == 0, k == 0))
        def _():
            dg_ref[...] = jnp.zeros_like(dg_ref)

        @pl.when(k == 0)
        def _():
            for cp in late:
                cp.start()
            acc[...] = jnp.zeros_like(acc)

        dp0 = dp_ref[:, :SHARD]
        dp1 = dp_ref[:, SHARD:]
        acc[:, :half] += _dot_nt(dp0, wu_ref[0]) + _dot_nt(dp1, wu_ref[1])
        acc[:, half:] += _dot_nt(dp0, wl_ref[0]) + _dot_nt(dp1, wl_ref[1])

        @pl.when(k == NDEV // 2 - 1)
        def _():
            for cp in late:
                cp.wait()
            dh = acc[...]
            x = x_buf[...]
            r = lax.rsqrt(jnp.mean(x * x, axis=-1, keepdims=True) + RMS_EPS)
            xn = x * r
            dg_ref[...] += _colsum(dh * xn)
            gy = dh * g_ref[...]
            dx_ref[...] = dxo_buf[...] + r * gy - xn * (r * jnp.mean(gy * xn, axis=-1, keepdims=True))

    any_spec = pl.BlockSpec(memory_space=pl.ANY)
    return _carried_call(
        body, comm, (d_proj, *w_g, x2d, g_row, dxo), name="in_proj_bwd", grid=(n // tm, NDEV // 2),
        in_specs=[pl.BlockSpec((tm, 2 * SHARD), lambda i, k: (i, k)),
                  pl.BlockSpec((2, half, SHARD), lambda i, k: (k, 0, 0)),
                  pl.BlockSpec((2, half, SHARD), lambda i, k: (k, 0, 0)),
                  any_spec,
                  pl.BlockSpec((1, D), lambda i, k: (0, 0)),
                  any_spec],
        out_specs=[pl.BlockSpec((tm, D), lambda i, k: (i, 0)), pl.BlockSpec((1, D), lambda i, k: (0, 0))],
        out_shape=[SDS((n, D), F32), SDS((1, D), F32)],
        scratch_shapes=[pltpu.VMEM((tm, D), F32), pltpu.VMEM((tm, D), F32), pltpu.VMEM((tm, D), F32),
                        pltpu.SemaphoreType.DMA((2,))],
        semantics=("arbitrary", "arbitrary"))


def _all_gather(arrs, name):
    na = len(arrs)

    def body(*refs):
        ins, outs = refs[:na], refs[na:2 * na]
        send_sems, recv_sems, local_sems = refs[2 * na:]
        x, y, c = _position()
        me, sibling = (x, y, c), (x, y, 1 - c)
        chips = [(1 - x, y), (x, 1 - y), (1 - x, 1 - y)]

        def copy(a, k, block, to, from_input=False):
            px, py, pc = block
            dst = outs[a].at[4 * px + 2 * py + pc]
            return pltpu.make_async_remote_copy(
                src_ref=ins[a] if from_input else dst, dst_ref=dst,
                send_sem=send_sems.at[7 * a + k], recv_sem=recv_sems.at[7 * a + k],
                device_id=to, device_id_type=MESH)

        mine = [pltpu.make_async_copy(ins[a], outs[a].at[4 * x + 2 * y + c], local_sems.at[a]) for a in range(na)]
        for cp in mine:
            cp.start()
        first = []
        for a in range(na):
            first.append(copy(a, 0, me, sibling, from_input=True))
            first += [copy(a, 1 + j, me, (*chip, c), from_input=True) for j, chip in enumerate(chips)]
        for cp in first:
            cp.start()
        passed = []
        for j, chip in enumerate(chips):
            for a in range(na):
                copy(a, 1 + j, (*chip, c), me).wait_recv()
                fwd = copy(a, 4 + j, (*chip, c), sibling)
                fwd.start()
                passed.append(fwd)
        for a in range(na):
            copy(a, 0, sibling, me).wait_recv()
            for j, chip in enumerate(chips):
                copy(a, 4 + j, (*chip, 1 - c), me).wait_recv()
        for cp in first + passed:
            cp.wait_send()
        for cp in mine:
            cp.wait()

    any_spec = pl.BlockSpec(memory_space=pl.ANY)
    return pl.pallas_call(
        body, name=name,
        in_specs=[any_spec] * na, out_specs=[any_spec] * na,
        out_shape=[SDS((NDEV,) + a.shape, a.dtype) for a in arrs],
        scratch_shapes=[pltpu.SemaphoreType.DMA((7 * na,)), pltpu.SemaphoreType.DMA((7 * na,)),
                        pltpu.SemaphoreType.DMA((na,))],
    )(*arrs)


def _adamw_math(g, w, m, v):
    m = ADAM_B1 * m + (1.0 - ADAM_B1) * g
    v = ADAM_B2 * v + (1.0 - ADAM_B2) * (g * g)
    m_hat = m / (1.0 - ADAM_B1 ** ADAM_STEP)
    v_hat = v / (1.0 - ADAM_B2 ** ADAM_STEP)
    delta = -ADAM_LR * (m_hat / (jnp.sqrt(v_hat) + ADAM_EPS) + ADAM_WD * w)
    return delta, m, v


def _adamw_layer(own, landed, w, m, v, layer, earlier, tr, name):
    _, rows, cols = w.shape

    def body(o_ref, l_ref, w_ref, m_ref, v_ref, *rest):
        g_out, d_out, m_out, v_out = rest[-4:]
        g = o_ref[...]
        for r in range(NDEV - 1):
            g = g + l_ref[r].astype(F32)
        g_out[...] = g
        d_out[...], m_out[...], v_out[...] = _adamw_math(g, w_ref[...], m_ref[...], v_ref[...])

    lay = pl.BlockSpec((None, tr, cols), lambda i: (layer, i, 0))
    any_spec = pl.BlockSpec(memory_space=pl.ANY)
    prior = list(earlier) if earlier is not None else []
    return pl.pallas_call(
        body, name=name, grid=(rows // tr,),
        in_specs=[pl.BlockSpec((tr, cols), lambda i: (i, 0)),
                  pl.BlockSpec((NDEV - 1, tr, cols), lambda i: (0, i, 0)), lay, lay, lay] + [any_spec] * len(prior),
        out_specs=[lay] * 4, out_shape=[SDS(w.shape, F32)] * 4,
        input_output_aliases={5 + k: k for k in range(len(prior))},
        compiler_params=_params("parallel"),
    )(own, landed, w, m, v, *prior)


def _adamw_gathered(parts, w, m, v, tr, name):
    rows, cols = w.shape

    def body(p_ref, w_ref, m_ref, v_ref, g_out, d_out, m_out, v_out):
        g = p_ref[0]
        for k in range(1, NDEV):
            g = g + p_ref[k]
        g_out[...] = g
        d_out[...], m_out[...], v_out[...] = _adamw_math(g, w_ref[...], m_ref[...], v_ref[...])

    blk = pl.BlockSpec((tr, cols), lambda i: (i, 0))
    return pl.pallas_call(
        body, name=name, grid=(rows // tr,),
        in_specs=[pl.BlockSpec((NDEV, tr, cols), lambda i: (0, i, 0)), blk, blk, blk],
        out_specs=[blk] * 4, out_shape=[SDS((rows, cols), F32)] * 4,
        compiler_params=_params("parallel"),
    )(parts, w, m, v)


def _adamw_plain(g, w, m, v):
    def body(g_ref, w_ref, m_ref, v_ref, d_out, m_out, v_out):
        d_out[...], m_out[...], v_out[...] = _adamw_math(g_ref[...], w_ref[...], m_ref[...], v_ref[...])

    return pl.pallas_call(body, name="adamw_conv_shards", out_shape=[SDS(w.shape, F32)] * 3)(g, w, m, v)


SMALL = ("norm_g", "pool_w", "pool_scale", "conv_w", "conv_b", "conv_ln_g", "conv_ln_b",
         "sgu_ln_g", "sgu_ln_b", "sgu_w", "sgu_b", "sc_w")
SMALL_SHAPES = {"norm_g": (D,), "pool_w": (NG, GW, GW), "pool_scale": (BW,), "conv_w": (CONV_K, BW), "conv_b": (BW,),
                "conv_ln_g": (BW,), "conv_ln_b": (BW,), "sgu_ln_g": (BW,), "sgu_ln_b": (BW,),
                "sgu_w": (NG, GW, GW), "sgu_b": (NG, GW), "sc_w": (SHORT_K, BW)}
REPLICATED = tuple(k for k in SMALL if k not in ("conv_w", "sc_w"))
PACK_UNIT = 8 * 128


def _size(shape):
    out = 1
    for s in shape:
        out *= s
    return out


def _padded(a):
    flat = a.reshape(-1)
    pad = -flat.shape[0] % PACK_UNIT
    return jnp.pad(flat, (0, pad)) if pad else flat


def _pack(arrays):
    return jnp.concatenate([_padded(a) for a in arrays]).reshape(-1, 128)


def _unpack(pack, shapes):
    flat = pack.reshape(-1)
    out, off = [], 0
    for shape in shapes:
        size = _size(shape)
        out.append(flat[off:off + size].reshape(shape))
        off += size + (-size % PACK_UNIT)
    return out


def _gathered_weights(wb, wo, conv):
    wb = wb.reshape(NDEV, 4, BW, D // NDEV).transpose(1, 2, 0, 3).reshape(4, BW, D)
    wo = wo.reshape(D, D)
    cw = CONV_K * (BW // NDEV)
    sw = SHORT_K * (BW // NDEV)
    sc_at = cw + (-cw % PACK_UNIT)
    flat = conv.reshape(NDEV, -1)
    conv_w = flat[:, :cw].reshape(NDEV, CONV_K, BW // NDEV).transpose(1, 0, 2).reshape(CONV_K, BW)
    sc_w = flat[:, sc_at:sc_at + sw].reshape(NDEV, SHORT_K, BW // NDEV).transpose(1, 0, 2).reshape(SHORT_K, BW)
    return wb, wo, conv_w, sc_w


def _layer_small_inputs(l, rep, conv_w, sc_w):
    vecs = jnp.stack([rep["pool_scale"][l], rep["conv_b"][l], rep["conv_ln_g"][l], rep["conv_ln_b"][l],
                      rep["sgu_ln_g"][l], rep["sgu_ln_b"][l], jnp.zeros((BW,), F32), jnp.zeros((BW,), F32)])
    conv_w = jnp.pad(conv_w, ((0, 32 - CONV_K), (0, 0)))
    sc_w = jnp.pad(sc_w, ((0, 8 - SHORT_K), (0, 0)))
    return (rep["pool_w"][l], vecs, conv_w, rep["sgu_w"][l], rep["sgu_b"][l].reshape(NG, GW, 1), sc_w)


def kernel(x, norm_g, w_in, pool_w, pool_scale, conv_w, conv_b, conv_ln_g, conv_ln_b, sgu_ln_g, sgu_ln_b, sgu_w, sgu_b, sc_w, w_branch, w_o, final_g, loss_target, m_norm_g, m_w_in, m_pool_w, m_pool_scale, m_conv_w, m_conv_b, m_conv_ln_g, m_conv_ln_b, m_sgu_ln_g, m_sgu_ln_b, m_sgu_w, m_sgu_b, m_sc_w, m_w_branch, m_w_o, m_final_g, v_norm_g, v_w_in, v_pool_w, v_pool_scale, v_conv_w, v_conv_b, v_conv_ln_g, v_conv_ln_b, v_sgu_ln_g, v_sgu_ln_b, v_sgu_w, v_sgu_b, v_sc_w, v_w_branch, v_w_o, v_final_g):
    args = dict(locals())
    w = {k: args[k] for k in SMALL + ("w_in", "w_branch", "w_o", "final_g")}
    mom = {k: args["m_" + k] for k in w}
    var = {k: args["v_" + k] for k in w}
    rep = {k: w[k] for k in REPLICATED}
    xi, yi, ci = _position()
    me = 4 * xi + 2 * yi + ci

    nb, seq, _ = x.shape
    n = nb * seq
    tm = min(512, n)
    tm_in = min(1024, n)
    tm_fwd = min(2048, n)
    tk = min(2048, n)
    t_fwd = min(256, seq)
    t_bwd = min(256, seq)

    w_in_b = [(w_in[l, :D // 2].astype(BF16), w_in[l, D // 2:].astype(BF16)) for l in range(DEPTH)]
    wb_b = [w_branch[l].astype(BF16).reshape(4 * BW, D // NDEV) for l in range(DEPTH)]
    wo_b = [w_o[l].astype(BF16) for l in range(DEPTH)]
    conv_b_ = [_pack([conv_w[l], sc_w[l]]) for l in range(DEPTH)]

    w_in_g = [_all_gather(list(w_in_b[0]), "w_in_all_gather"), None]
    x2d = x.reshape(n, D)
    saved = []
    ride = _Exchange(gathers=[wb_b[0], wo_b[0], conv_b_[0], w_in_b[1][0]])
    (proj, h), (wb_g, wo_g, conv_g, w_in_upper) = _in_proj(x2d, norm_g[0:1], w_in_g[0], tm_fwd, ride)
    for l in range(DEPTH):
        if l == 1:
            ride = _Exchange(gathers=[wb_b[1], wo_b[1], conv_b_[1]])
            (proj, h), (wb_g, wo_g, conv_g) = _in_proj(x2d, norm_g[l:l + 1], w_in_g[l], tm_fwd, ride)
        wb_l, wo_l, conv_l, sc_l = _gathered_weights(wb_g, wo_g, conv_g)
        small = _layer_small_inputs(l, rep, conv_l, sc_l)
        ride = _Exchange(gathers=[w_in_b[1][1]]) if l == 0 else None
        (z, y_conv, q_sc, pooled, x_new, merged), gathered = _mixers_fwd(
            proj, x2d, wb_l, wo_l, *small, nb, seq, t_fwd, ride)
        if l == 0:
            w_in_g[1] = [w_in_upper, gathered[0]]
        saved.append((x2d, proj, h, z, y_conv, q_sc, pooled, merged, small, wb_l, wo_l))
        x2d = x_new

    dx, loss_acc, d_final_g = _loss_head(x2d, loss_target.reshape(n, D), final_g.reshape(1, D), tm)

    grads = {k: [None] * DEPTH for k in SMALL}
    big = {"w_in": None, "w_branch": None, "w_o": None}
    big_w = {"w_in": (w_in, m_w_in, v_w_in, (DEPTH, D, SHARD), 256),
             "w_branch": (w_branch, m_w_branch, v_w_branch, (DEPTH, 4 * BW, D // NDEV), 512),
             "w_o": (w_o, m_w_o, v_w_o, (DEPTH, D // NDEV, D), 64)}

    def adamw(k, own, landed, l):
        wk, mk, vk, shape, tr = big_w[k]
        big[k] = _adamw_layer(own, landed, wk.reshape(shape), mk.reshape(shape), vk.reshape(shape), l, big[k], tr,
                              "adamw_" + k)

    for l in reversed(range(DEPTH)):
        x_in, proj, h, z, y_conv, q_sc, pooled, merged, small, wb_l, wo_l = saved[l]
        d_bo, d_proj, d_wo, d_wo_b, d_wb, d_wb_b = _merge_out_bwd(dx, z, proj, merged, wb_l, wo_l, tm)
        ride = _Exchange(scatters=[(d_wo_b.reshape(NDEV, D // NDEV, D), d_wo.reshape(NDEV, D // NDEV, D)),
                                   (d_wb_b.reshape(NDEV, 4 * BW, D // NDEV), d_wb.reshape(NDEV, 4 * BW, D // NDEV))])
        (d_proj, dvec, dconvw, dscw, dpoolw, dsguw, dsgub), (wo_landed, wo_own, wb_landed, wb_own) = _mixers_bwd(
            proj, d_bo, wb_l, y_conv, q_sc, pooled, d_proj, *small, nb, seq, t_bwd, ride)
        grads["pool_w"][l] = dpoolw
        grads["pool_scale"][l] = dvec[0]
        grads["conv_w"][l] = dconvw[:CONV_K]
        grads["conv_b"][l] = dvec[1]
        grads["conv_ln_g"][l] = dvec[2]
        grads["conv_ln_b"][l] = dvec[3]
        grads["sgu_ln_g"][l] = dvec[4]
        grads["sgu_ln_b"][l] = dvec[5]
        grads["sgu_w"][l] = dsguw
        grads["sgu_b"][l] = dsgub[:NG]
        grads["sc_w"][l] = dscw[:SHORT_K]
        if l == 0:
            grads["norm_g"][0] = jnp.zeros((D,), F32)
            part_pack = _pack([jnp.stack(grads[k]) for k in SMALL] + [d_final_g[0], loss_acc[0, 0:1]])
            ride = _Exchange(gathers=[part_pack])
        else:
            ride = None
        (d_w_in, d_w_in_b), gathered = _grad_w_in(h, d_proj, tk, ride)
        if l == 0:
            parts = gathered[0]
        ride = _Exchange(scatters=[(d_w_in_b, d_w_in)])
        (dx, d_norm_g), (in_landed, in_own) = _in_proj_bwd(d_proj, w_in_g[l], x_in, norm_g[l:l + 1], dx, tm_in, ride)
        adamw("w_o", wo_own, wo_landed, l)
        adamw("w_branch", wb_own, wb_landed, l)
        adamw("w_in", in_own, in_landed, l)
        if l == 1:
            grads["norm_g"][1] = d_norm_g[0]
    grad_x = dx.reshape(nb, seq, D)

    names = list(SMALL) + ["final_g", "loss"]
    shapes = [(DEPTH,) + SMALL_SHAPES[k] for k in SMALL] + [(D,), (1,)]
    whole = REPLICATED + ("final_g",)
    w_pack = _pack([w[k] if k in whole else jnp.zeros(s, F32) for k, s in zip(names, shapes)])
    m_pack = _pack([mom[k] if k in whole else jnp.zeros(s, F32) for k, s in zip(names, shapes)])
    v_pack = _pack([var[k] if k in whole else jnp.ones(s, F32) for k, s in zip(names, shapes)])
    rows = parts.shape[1]
    tr = max(r for r in range(8, 641, 8) if rows % r == 0)
    packs = _adamw_gathered(parts, w_pack, m_pack, v_pack, tr, "adamw_small")
    out_g, out_d, out_m, out_v = [dict(zip(names, _unpack(p, shapes))) for p in packs]
    loss = out_g["loss"][0]

    (tail,) = _all_gather([d_norm_g.reshape(8, 128)], "norm_g_all_gather")
    first = [a[0].reshape(8, 128) for a in (w["norm_g"], mom["norm_g"], var["norm_g"])]
    tails = _adamw_gathered(tail, *first, 8, "adamw_norm_g")
    for dst, t0 in zip((out_g, out_d, out_m, out_v), tails):
        dst["norm_g"] = jnp.concatenate([t0.reshape(1, D), dst["norm_g"][1:]])

    col = me * (BW // NDEV)
    for k in ("conv_w", "sc_w"):
        out_g[k] = lax.dynamic_slice_in_dim(out_g[k], col, BW // NDEV, axis=2)
    sharded_shapes = [w["conv_w"].shape, w["sc_w"].shape]
    d2, m2, v2 = _adamw_plain(_pack([out_g["conv_w"], out_g["sc_w"]]), _pack([w["conv_w"], w["sc_w"]]),
                              _pack([mom["conv_w"], mom["sc_w"]]), _pack([var["conv_w"], var["sc_w"]]))
    for dst, pk in ((out_d, d2), (out_m, m2), (out_v, v2)):
        dst["conv_w"], dst["sc_w"] = _unpack(pk, sharded_shapes)

    for k in big:
        out_g[k], out_d[k], out_m[k], out_v[k] = [o.reshape(w[k].shape) for o in big[k]]

    order = ("norm_g", "w_in", "pool_w", "pool_scale", "conv_w", "conv_b", "conv_ln_g", "conv_ln_b", "sgu_ln_g",
             "sgu_ln_b", "sgu_w", "sgu_b", "sc_w", "w_branch", "w_o", "final_g")
    return (loss, grad_x, *[out_g[k] for k in order], *[out_d[k] for k in order],
            *[out_m[k] for k in order], *[out_v[k] for k in order])
```

```python
import functools

import jax
import jax.numpy as jnp
from jax import lax
from jax.experimental import pallas as pl
from jax.experimental.pallas import tpu as pltpu

F32 = jnp.float32
BF16 = jnp.bfloat16
SDS = jax.ShapeDtypeStruct
MESH = pl.DeviceIdType.MESH

D = 1024
BW = 512
NG = 4
GW = 128
CONV_K = 31
SHORT_K = 3
POOL_WINDOWS = (2, 4, 8, 16)
N_PIECE_COLS = 12 * BW
IN_COLS = N_PIECE_COLS + 4 * D
NDEV = 8
SHARD = IN_COLS // NDEV
DEPTH = 2
RMS_EPS = 1e-6
LN_EPS = 1e-5
HALO_POOL, HALO_CONV, HALO_SC = 16, 32, 8
CONV_ROWS = 32

ADAM_LR = 0.001
ADAM_B1 = 0.9
ADAM_B2 = 0.999
ADAM_EPS = 1e-08
ADAM_WD = 0.01
ADAM_STEP = 10

VMEM_LIMIT = 56 * 1024 * 1024


def _params(*sem):
    return pltpu.CompilerParams(dimension_semantics=sem, vmem_limit_bytes=VMEM_LIMIT)


def _sigmoid(x):
    return jax.nn.sigmoid(x)


def _silu(x):
    return x * _sigmoid(x)


def _dsilu(x):
    s = _sigmoid(x)
    return s * (1.0 + x * (1.0 - s))


def _colsum(x):
    return jnp.sum(x, axis=0, keepdims=True)


def _dot(a, b):
    return jnp.dot(a, b, preferred_element_type=F32)


def _dot_nt(a, b):
    return lax.dot_general(a, b, (((1,), (1,)), ((), ())), preferred_element_type=F32)


def _dot_tn(a, b):
    return lax.dot_general(a, b, (((0,), (0,)), ((), ())), preferred_element_type=F32)


def _layer_norm_stats(y):
    mu = jnp.mean(y, axis=-1, keepdims=True)
    yc = y - mu
    var = jnp.mean(yc * yc, axis=-1, keepdims=True)
    rstd = lax.rsqrt(var + LN_EPS)
    return yc * rstd, rstd


def _layer_norm_bwd(d_hat, hat, rstd):
    return rstd * (d_hat - jnp.mean(d_hat, axis=-1, keepdims=True)
                   - hat * jnp.mean(d_hat * hat, axis=-1, keepdims=True))


def _position():
    return lax.axis_index("x"), lax.axis_index("y"), lax.axis_index("c")


def _flip(v, bit):
    return 1 - v if bit else v


class _Exchange:
    def __init__(self, gathers=(), scatters=()):
        self.gathers = list(gathers)
        self.scatters = list(scatters)
        self.inputs = self.gathers + [a for pair in self.scatters for a in pair]
        self.out_shapes = [SDS((NDEV,) + a.shape, a.dtype) for a in self.gathers]
        for send, keep in self.scatters:
            self.out_shapes += [SDS((NDEV - 1,) + send.shape[1:], send.dtype), SDS(keep.shape[1:], keep.dtype)]
        n = len(self.gathers) + len(self.scatters)
        self.scratch = [pltpu.SemaphoreType.DMA((7 * n,)), pltpu.SemaphoreType.DMA((7 * n,)),
                        pltpu.SemaphoreType.DMA((n,))]

    def _copies(self, ins, outs, sems):
        send_sems, recv_sems, local_sems = sems
        x, y, c = _position()
        me = 4 * x + 2 * y + c
        ng = len(self.gathers)
        local, remote = [], []
        for a in range(ng + len(self.scatters)):
            if a < ng:
                local.append(pltpu.make_async_copy(ins[a], outs[a].at[me], local_sems.at[a]))
            else:
                s = a - ng
                send, keep = ins[ng + 2 * s], ins[ng + 2 * s + 1]
                landed, own = outs[ng + 2 * s], outs[ng + 2 * s + 1]
                local.append(pltpu.make_async_copy(keep.at[me], own, local_sems.at[a]))
            for r in range(1, NDEV):
                peer = (_flip(x, r & 4), _flip(y, r & 2), _flip(c, r & 1))
                if a < ng:
                    src, dst = ins[a], outs[a].at[me]
                else:
                    src, dst = send.at[jnp.bitwise_xor(me, r)], landed.at[r - 1]
                remote.append(pltpu.make_async_remote_copy(
                    src_ref=src, dst_ref=dst, send_sem=send_sems.at[7 * a + r - 1],
                    recv_sem=recv_sems.at[7 * a + r - 1], device_id=peer, device_id_type=MESH))
        return local, remote

    def start(self, ins, outs, sems):
        local, remote = self._copies(ins, outs, sems)
        for cp in remote + local:
            cp.start()

    def finish(self, ins, outs, sems):
        local, remote = self._copies(ins, outs, sems)
        for cp in remote + local:
            cp.wait()


def _carried_call(body, comm, args, *, name, grid, in_specs, out_specs, out_shape, scratch_shapes=(),
                  semantics, input_output_aliases=None):
    aliases = input_output_aliases or {}
    if comm is None:
        out = pl.pallas_call(body, name=name, grid=grid, in_specs=in_specs, out_specs=out_specs, out_shape=out_shape,
                             scratch_shapes=list(scratch_shapes), input_output_aliases=aliases,
                             compiler_params=_params(*semantics))(*args)
        return list(out), []
    n_in, n_out, n_scr = len(in_specs), len(out_specs), len(scratch_shapes)
    c_in, c_out = len(comm.inputs), len(comm.out_shapes)

    def full_body(*refs):
        refs = list(refs)
        ins, refs = refs[:n_in], refs[n_in:]
        cins, refs = refs[:c_in], refs[c_in:]
        outs, refs = refs[:n_out], refs[n_out:]
        couts, refs = refs[:c_out], refs[c_out:]
        scr, sems = refs[:n_scr], refs[n_scr:]
        ids = [pl.program_id(d) for d in range(len(grid))]
        first = functools.reduce(jnp.logical_and, [i == 0 for i in ids])
        last = functools.reduce(jnp.logical_and, [i == g - 1 for i, g in zip(ids, grid)])

        @pl.when(first)
        def _():
            comm.start(cins, couts, sems)

        body(*ins, *outs, *scr)

        @pl.when(last)
        def _():
            comm.finish(cins, couts, sems)

    any_spec = pl.BlockSpec(memory_space=pl.ANY)
    out = pl.pallas_call(
        full_body, name=name, grid=grid,
        in_specs=list(in_specs) + [any_spec] * c_in, out_specs=list(out_specs) + [any_spec] * c_out,
        out_shape=list(out_shape) + comm.out_shapes, scratch_shapes=list(scratch_shapes) + comm.scratch,
        input_output_aliases=aliases, compiler_params=_params(*["arbitrary"] * len(grid)),
    )(*args, *comm.inputs)
    return list(out[:n_out]), list(out[n_out:])


def _in_proj(x2d, g_row, w_g, tm, comm=None):
    n = x2d.shape[0]
    half = D // 2

    def body(x_ref, g_ref, wu_ref, wl_ref, proj_ref, h_ref):
        @pl.when(pl.program_id(1) == 0)
        def _():
            x = x_ref[...]
            r = lax.rsqrt(jnp.mean(x * x, axis=-1, keepdims=True) + RMS_EPS)
            h_ref[...] = (x * r * g_ref[...]).astype(BF16)

        proj_ref[...] = (_dot(h_ref[:, :half], wu_ref[...]) + _dot(h_ref[:, half:], wl_ref[...])).astype(BF16)

    return _carried_call(
        body, comm, (x2d, g_row, *w_g), name="in_proj", grid=(n // tm, NDEV),
        in_specs=[pl.BlockSpec((tm, D), lambda i, j: (i, 0)),
                  pl.BlockSpec((1, D), lambda i, j: (0, 0)),
                  pl.BlockSpec((None, half, SHARD), lambda i, j: (j, 0, 0)),
                  pl.BlockSpec((None, half, SHARD), lambda i, j: (j, 0, 0))],
        out_specs=[pl.BlockSpec((tm, SHARD), lambda i, j: (i, j)),
                   pl.BlockSpec((tm, D), lambda i, j: (i, 0))],
        out_shape=[SDS((n, IN_COLS), BF16), SDS((n, D), BF16)],
        semantics=("parallel", "arbitrary"))


def _tril_bf16(w):
    r = lax.broadcasted_iota(jnp.int32, (GW, GW), 0)
    c = lax.broadcasted_iota(jnp.int32, (GW, GW), 1)
    return jnp.where(r >= c, w, 0.0).astype(BF16)


def _mixers_fwd(proj, x2d, wb, wo, pool_w, vecs, conv_w, sgu_w, sgu_bc, sc_w, nb, seq, t, comm=None):
    n = nb * seq
    nt = seq // t

    def body(p_ref, x_ref, wb_ref, wo_ref, poolw_ref, vec_ref, convw_ref, sguw_ref, sgub_ref, scw_ref,
             z_ref, y_ref, q_ref, pooled_ref, xo_ref, m_ref, pext, uext, vext):
        i = pl.program_id(1)

        def gated(k):
            gate = _sigmoid(p_ref[:, N_PIECE_COLS + k * D:N_PIECE_COLS + (k + 1) * D].astype(F32))
            return gate * _dot(z_ref[:, k * BW:(k + 1) * BW], wb_ref[k])

        @pl.when(i == 0)
        def _():
            pext[0:HALO_POOL, :] = jnp.zeros((HALO_POOL, BW), F32)
            uext[0:HALO_CONV, :] = jnp.zeros((HALO_CONV, BW), F32)
            vext[0:HALO_SC, :] = jnp.zeros((HALO_SC, BW), F32)

        @pl.when(i > 0)
        def _():
            pext[0:HALO_POOL, :] = pext[t:t + HALO_POOL, :]
            uext[0:HALO_CONV, :] = uext[t:t + HALO_CONV, :]
            vext[0:HALO_SC, :] = vext[t:t + HALO_SC, :]

        def piece(k):
            return p_ref[:, k * BW:(k + 1) * BW].astype(F32)

        pos = lax.broadcasted_iota(jnp.int32, (t, 1), 0) + i * t

        px = piece(0)
        pext[HALO_POOL:, :] = px
        for j, win in enumerate(POOL_WINDOWS):
            cols = slice(j * GW, (j + 1) * GW)
            s = pext[:, cols]
            step = 1
            while step < win:
                s = s + pltpu.roll(s, step, 0)
                step *= 2
            cnt = jnp.minimum(pos + 1, win).astype(F32)
            pooled = (s[HALO_POOL:, :] / cnt - px[:, cols]).astype(BF16)
            pooled_ref[:, cols] = pooled
            mixed = _dot(pooled, poolw_ref[j].astype(BF16))
            z_ref[:, cols] = (mixed * vec_ref[0:1, cols] * _silu(p_ref[:, BW + j * GW:BW + (j + 1) * GW].astype(F32))).astype(BF16)
        merged = gated(0)

        uext[HALO_CONV:, :] = piece(2) * _sigmoid(piece(3))
        ue = uext[...]
        acc = jnp.zeros((t, BW), F32)
        for b in range(8):
            rb = ue if b == 0 else pltpu.roll(ue, b, 0)
            for a in range((CONV_K - b + 7) // 8):
                j = 8 * a + b
                acc = acc + convw_ref[CONV_K - 1 - j:CONV_K - j, :] * rb[HALO_CONV - 8 * a:HALO_CONV - 8 * a + t, :]
        y = acc + vec_ref[1:2, :]
        y_ref[...] = y
        yhat, _ = _layer_norm_stats(y)
        act = _silu(yhat * vec_ref[2:3, :] + vec_ref[3:4, :])
        z_ref[:, BW:2 * BW] = (act * _silu(piece(4))).astype(BF16)
        merged = merged + gated(1)

        vhat, _ = _layer_norm_stats(piece(6))
        vn = (vhat * vec_ref[4:5, :] + vec_ref[5:6, :]).astype(BF16)
        for g in range(NG):
            ws = _tril_bf16(sguw_ref[g])
            cols = slice(g * GW, (g + 1) * GW)
            for c in range(t // GW):
                rows = slice(c * GW, (c + 1) * GW)
                sp = _dot(ws, vn[rows, cols]) + sgub_ref[g]
                gate = _silu(p_ref[rows, 7 * BW + g * GW:7 * BW + (g + 1) * GW].astype(F32))
                z_ref[rows, 2 * BW + g * GW:2 * BW + (g + 1) * GW] = (
                    p_ref[rows, 5 * BW + g * GW:5 * BW + (g + 1) * GW].astype(F32) * sp * gate).astype(BF16)
        merged = merged + gated(2)

        vext[HALO_SC:, :] = piece(9) * piece(10)
        ve = vext[...]
        q = jnp.zeros((t, BW), F32)
        for j in range(SHORT_K):
            sh = ve if j == 0 else pltpu.roll(ve, j, 0)
            q = q + scw_ref[SHORT_K - 1 - j:SHORT_K - j, :] * sh[HALO_SC:, :]
        q_ref[...] = q
        z_ref[:, 3 * BW:] = (piece(8) * q * _silu(piece(11))).astype(BF16)
        mb = (merged + gated(3)).astype(BF16)
        m_ref[...] = mb
        xo_ref[...] = x_ref[...] + _dot(mb, wo_ref[...])

    row = lambda b, i: (b * nt + i, 0)
    full2 = lambda b, i: (0, 0)
    full3 = lambda b, i: (0, 0, 0)
    return _carried_call(
        body, comm, (proj, x2d, wb, wo, pool_w, vecs, conv_w, sgu_w, sgu_bc, sc_w), name="mixers_fwd", grid=(nb, nt),
        in_specs=[pl.BlockSpec((t, IN_COLS), row),
                  pl.BlockSpec((t, D), row),
                  pl.BlockSpec((4, BW, D), full3),
                  pl.BlockSpec((D, D), full2),
                  pl.BlockSpec((NG, GW, GW), full3),
                  pl.BlockSpec((8, BW), full2),
                  pl.BlockSpec((32, BW), full2),
                  pl.BlockSpec((NG, GW, GW), full3),
                  pl.BlockSpec((NG, GW, 1), full3),
                  pl.BlockSpec((8, BW), full2)],
        out_specs=[pl.BlockSpec((t, 4 * BW), row), pl.BlockSpec((t, BW), row),
                   pl.BlockSpec((t, BW), row), pl.BlockSpec((t, BW), row),
                   pl.BlockSpec((t, D), row), pl.BlockSpec((t, D), row)],
        out_shape=[SDS((n, 4 * BW), BF16), SDS((n, BW), F32), SDS((n, BW), F32), SDS((n, BW), BF16),
                   SDS((n, D), F32), SDS((n, D), BF16)],
        scratch_shapes=[pltpu.VMEM((t + HALO_POOL, BW), F32), pltpu.VMEM((t + HALO_CONV, BW), F32),
                        pltpu.VMEM((t + HALO_SC, BW), F32)],
        semantics=("arbitrary", "arbitrary"))


def _loss_head(x2d, tgt2d, g_row, tm):
    n = x2d.shape[0]

    def body(x_ref, t_ref, g_ref, dx_ref, loss_ref, dg_ref):
        @pl.when(pl.program_id(0) == 0)
        def _():
            loss_ref[...] = jnp.zeros_like(loss_ref)
            dg_ref[...] = jnp.zeros_like(dg_ref)

        x = x_ref[...]
        g = g_ref[...]
        r = lax.rsqrt(jnp.mean(x * x, axis=-1, keepdims=True) + RMS_EPS)
        xn = x * r
        e = xn * g - t_ref[...]
        loss_ref[...] += 0.5 * jnp.sum(jnp.mean(e * e, axis=-1, keepdims=True))
        dy = e * (1.0 / D)
        dg_ref[...] += _colsum(dy * xn)
        gy = dy * g
        dx_ref[...] = r * gy - xn * (r * jnp.mean(gy * xn, axis=-1, keepdims=True))

    return pl.pallas_call(
        body, name="loss_head", grid=(n // tm,),
        in_specs=[pl.BlockSpec((tm, D), lambda i: (i, 0)), pl.BlockSpec((tm, D), lambda i: (i, 0)),
                  pl.BlockSpec((1, D), lambda i: (0, 0))],
        out_specs=[pl.BlockSpec((tm, D), lambda i: (i, 0)), pl.BlockSpec((8, 128), lambda i: (0, 0)),
                   pl.BlockSpec((1, D), lambda i: (0, 0))],
        out_shape=[SDS((n, D), F32), SDS((8, 128), F32), SDS((1, D), F32)],
        compiler_params=_params("arbitrary"),
    )(x2d, tgt2d, g_row)


def _merge_out_bwd(dxo, z, proj, merged, wb, wo, tm):
    n = dxo.shape[0]
    ni = n // tm
    sub = D // NDEV

    def body(dx_ref, z_ref, mg_ref, m_ref, wb_ref, wo_ref,
             dbo_ref, dp_ref, gwo_ref, gwo_b_ref, gwb_ref, gwb_b_ref,
             dm_ref, acc_wo, acc_wb, half_wo, half_wb, sems):
        i = pl.program_id(0)
        h = pl.program_id(1)

        @pl.when(jnp.logical_and(i == 0, h == 0))
        def _():
            acc_wo[...] = jnp.zeros_like(acc_wo)
            acc_wb[...] = jnp.zeros_like(acc_wb)

        @pl.when(h == 0)
        def _():
            dxb = dx_ref[...].astype(BF16)
            dm_ref[...] = _dot_nt(dxb, wo_ref[...])
            acc_wo[...] += _dot_tn(m_ref[...], dxb)

        dm = dm_ref[...]
        for k in range(2):
            zk = z_ref[:, k * BW:(k + 1) * BW]
            w = wb_ref[2 * h + k]
            bo = _dot(zk, w)
            sig = _sigmoid(mg_ref[:, k * D:(k + 1) * D].astype(F32))
            dbo = (dm * sig).astype(BF16)
            dp_ref[:, k * D:(k + 1) * D] = (dm * bo * sig * (1.0 - sig)).astype(BF16)
            dbo_ref[:, k * D:(k + 1) * D] = dbo
            acc_wb[2 * h + k] += _dot_tn(zk, dbo)

        @pl.when(jnp.logical_and(i == ni - 1, h == 1))
        def _():
            half_wo[...] = acc_wo[...].astype(BF16)
            half_wb[...] = acc_wb[...].astype(BF16)
            copies = [pltpu.make_async_copy(acc_wo, gwo_ref, sems.at[0]),
                      pltpu.make_async_copy(half_wo, gwo_b_ref, sems.at[1])]
            for d in range(NDEV):
                cols = pl.ds(d * sub, sub)
                copies.append(pltpu.make_async_copy(acc_wb.at[:, :, cols], gwb_ref.at[d], sems.at[2 + 2 * d]))
                copies.append(pltpu.make_async_copy(half_wb.at[:, :, cols], gwb_b_ref.at[d], sems.at[3 + 2 * d]))
            for cp in copies:
                cp.start()
            for cp in copies:
                cp.wait()

    any_spec = pl.BlockSpec(memory_space=pl.ANY)
    return pl.pallas_call(
        body, name="merge_out_bwd", grid=(ni, 2),
        in_specs=[pl.BlockSpec((tm, D), lambda i, h: (i, 0)),
                  pl.BlockSpec((tm, 2 * BW), lambda i, h: (i, h)),
                  pl.BlockSpec((tm, 2 * D), lambda i, h: (i, 3 + h)),
                  pl.BlockSpec((tm, D), lambda i, h: (i, 0)),
                  pl.BlockSpec((4, BW, D), lambda i, h: (0, 0, 0)),
                  pl.BlockSpec((D, D), lambda i, h: (0, 0))],
        out_specs=[pl.BlockSpec((tm, 2 * D), lambda i, h: (i, h)),
                   pl.BlockSpec((tm, 2 * D), lambda i, h: (i, 3 + h)),
                   any_spec, any_spec, any_spec, any_spec],
        out_shape=[SDS((n, 4 * D), BF16), SDS((n, IN_COLS), BF16), SDS((D, D), F32), SDS((D, D), BF16),
                   SDS((NDEV, 4, BW, sub), F32), SDS((NDEV, 4, BW, sub), BF16)],
        scratch_shapes=[pltpu.VMEM((tm, D), F32), pltpu.VMEM((D, D), F32), pltpu.VMEM((4, BW, D), F32),
                        pltpu.VMEM((D, D), BF16), pltpu.VMEM((4, BW, D), BF16),
                        pltpu.SemaphoreType.DMA((2 + 2 * NDEV,))],
        compiler_params=_params("arbitrary", "arbitrary"),
    )(dxo, z, proj, merged, wb, wo)


def _grad_w_in(h, d_proj, tk, comm=None):
    nk = h.shape[0] // tk

    def body(h_ref, dp_ref, o_ref, ob_ref):
        k = pl.program_id(1)

        @pl.when(k == 0)
        def _():
            o_ref[...] = jnp.zeros_like(o_ref)

        o_ref[...] += _dot_tn(h_ref[...], dp_ref[...])

        @pl.when(k == nk - 1)
        def _():
            ob_ref[...] = o_ref[...].astype(BF16)

    out_spec = pl.BlockSpec((None, D, SHARD), lambda j, k: (j, 0, 0))
    return _carried_call(
        body, comm, (h, d_proj), name="grad_w_in", grid=(NDEV, nk),
        in_specs=[pl.BlockSpec((tk, D), lambda j, k: (k, 0)), pl.BlockSpec((tk, SHARD), lambda j, k: (k, j))],
        out_specs=[out_spec, out_spec], out_shape=[SDS((NDEV, D, SHARD), F32), SDS((NDEV, D, SHARD), BF16)],
        semantics=("parallel", "arbitrary"))


def _mixers_bwd(proj, d_bo, wb, y_conv, q_sc, pooled, d_proj, pool_w, vecs, conv_w, sgu_w, sgu_bc, sc_w, nb, seq, t, comm=None):
    n = nb * seq
    nt = seq // t

    def body(p_ref, dbo_ref, wb_ref, y_ref, q_ref, pooled_ref, dpin_ref,
             poolw_ref, vec_ref, convw_ref, sguw_ref, sgub_ref, scw_ref,
             dp_ref, dvec_ref, dconvw_ref, dscw_ref, dpoolw_ref, dsguw_ref, dsgub_ref,
             rext, dyext, dqext, bacc, dyrot, dwacc):
        b = pl.program_id(0)
        i = pl.program_id(1)
        first = jnp.logical_and(b == 0, i == 0)
        last = jnp.logical_and(b == nb - 1, i == nt - 1)

        @pl.when(first)
        def _():
            dvec_ref[...] = jnp.zeros_like(dvec_ref)
            dconvw_ref[...] = jnp.zeros_like(dconvw_ref)
            dscw_ref[...] = jnp.zeros_like(dscw_ref)
            dpoolw_ref[...] = jnp.zeros_like(dpoolw_ref)
            dsguw_ref[...] = jnp.zeros_like(dsguw_ref)
            dsgub_ref[...] = jnp.zeros_like(dsgub_ref)
            bacc[...] = jnp.zeros_like(bacc)
            dwacc[...] = jnp.zeros_like(dwacc)

        @pl.when(i == 0)
        def _():
            rext[t:, :] = jnp.zeros((HALO_POOL, BW), F32)
            dyext[t:, :] = jnp.zeros((HALO_CONV, BW), F32)
            dqext[t:, :] = jnp.zeros((HALO_SC, BW), F32)

        @pl.when(i > 0)
        def _():
            rext[t:, :] = rext[0:HALO_POOL, :]
            dyext[t:, :] = dyext[0:HALO_CONV, :]
            dqext[t:, :] = dqext[0:HALO_SC, :]

        def piece(k):
            return p_ref[:, k * BW:(k + 1) * BW].astype(F32)

        def put(k, v):
            dp_ref[:, k * BW:(k + 1) * BW] = v.astype(BF16)

        def dz(k):
            return _dot_nt(dbo_ref[:, k * D:(k + 1) * D], wb_ref[k])

        pos = lax.broadcasted_iota(jnp.int32, (t, 1), 0) + (nt - 1 - i) * t

        dzp = dz(0)
        pg = piece(1)
        d_pm = dzp * _silu(pg)
        for j, win in enumerate(POOL_WINDOWS):
            cols = slice(j * GW, (j + 1) * GW)
            pw = poolw_ref[j].astype(BF16)
            pooled = pooled_ref[:, cols]
            mixed = _dot(pooled, pw)
            scale = vec_ref[0:1, cols]
            dvec_ref[0:1, cols] += _colsum(d_pm[:, cols] * mixed)
            dp_ref[:, BW + j * GW:BW + (j + 1) * GW] = (dzp[:, cols] * (mixed * scale) * _dsilu(pg[:, cols])).astype(BF16)
            d_mixed = (d_pm[:, cols] * scale).astype(BF16)
            dpoolw_ref[j] += _dot_tn(pooled, d_mixed)
            d_pooled = _dot_nt(d_mixed, pw)
            cnt = jnp.minimum(pos + 1, win).astype(F32)
            rext[0:t, cols] = d_pooled / cnt
            s = rext[:, cols]
            step = 1
            while step < win:
                s = s + pltpu.roll(s, t + HALO_POOL - step, 0)
                step *= 2
            dp_ref[:, cols] = (s[0:t, :] - d_pooled).astype(BF16)

        dzc = dz(1)
        cgate = piece(4)
        yhat, rstd = _layer_norm_stats(y_ref[...])
        ln = yhat * vec_ref[2:3, :] + vec_ref[3:4, :]
        put(4, dzc * _silu(ln) * _dsilu(cgate))
        d_ln = dzc * _silu(cgate) * _dsilu(ln)
        dvec_ref[2:3, :] += _colsum(d_ln * yhat)
        dvec_ref[3:4, :] += _colsum(d_ln)
        dy = _layer_norm_bwd(d_ln * vec_ref[2:3, :], yhat, rstd)
        dvec_ref[1:2, :] += _colsum(dy)
        dyext[0:t, :] = dy
        dye = dyext[...]
        for b in range(1, 8):
            dyrot[b - 1] = pltpu.roll(dye, t + HALO_CONV - b, 0)

        def conv_rows(c, carry):
            r0 = pl.multiple_of(c * CONV_ROWS, CONV_ROWS)
            rows = pl.ds(r0, CONV_ROWS)
            sig_cb = _sigmoid(p_ref[rows, 3 * BW:4 * BW].astype(F32))
            u = p_ref[rows, 2 * BW:3 * BW].astype(F32) * sig_cb
            du = jnp.zeros((CONV_ROWS, BW), F32)
            for b in range(8):
                for a in range((CONV_K - b + 7) // 8):
                    j = 8 * a + b
                    at = pl.ds(pl.multiple_of(r0 + 8 * a, 8), CONV_ROWS)
                    sh = dyext[at, :] if b == 0 else dyrot[b - 1, at, :]
                    du = du + convw_ref[CONV_K - 1 - j:CONV_K - j, :] * sh
                    dwacc[CONV_K - 1 - j] += jnp.sum((u * sh).reshape(CONV_ROWS // 8, 8, BW), axis=0)
            dp_ref[rows, 2 * BW:3 * BW] = (du * sig_cb).astype(BF16)
            dp_ref[rows, 3 * BW:4 * BW] = (du * u * (1.0 - sig_cb)).astype(BF16)
            return carry

        lax.fori_loop(0, t // CONV_ROWS, conv_rows, 0)

        dzg = dz(2)
        gu = piece(5)
        ggate = piece(7)
        vhat, vrstd = _layer_norm_stats(piece(6))
        vn = (vhat * vec_ref[4:5, :] + vec_ref[5:6, :]).astype(BF16)
        t1 = dzg * _silu(ggate)
        d_sp_all = t1 * gu
        d_vn_cols = []
        sp_cols = []
        for g in range(NG):
            ws = _tril_bf16(sguw_ref[g])
            cols = slice(g * GW, (g + 1) * GW)
            d_vn_rows = []
            sp_rows = []
            for c in range(t // GW):
                rows = slice(c * GW, (c + 1) * GW)
                sp_rows.append(_dot(ws, vn[rows, cols]) + sgub_ref[g])
                d_sp = d_sp_all[rows, cols]
                bacc[g] += d_sp
                d_spb = d_sp.astype(BF16)
                d_vn_rows.append(_dot_tn(ws, d_spb))
                dsguw_ref[g] += _dot_nt(d_spb, vn[rows, cols])
            d_vn_cols.append(jnp.concatenate(d_vn_rows, axis=0))
            sp_cols.append(jnp.concatenate(sp_rows, axis=0))
        d_vn = jnp.concatenate(d_vn_cols, axis=1)
        sp = jnp.concatenate(sp_cols, axis=1)
        put(5, t1 * sp)
        put(7, dzg * gu * sp * _dsilu(ggate))
        dvec_ref[4:5, :] += _colsum(d_vn * vhat)
        dvec_ref[5:6, :] += _colsum(d_vn)
        put(6, _layer_norm_bwd(d_vn * vec_ref[4:5, :], vhat, vrstd))

        dzs = dz(3)
        sb = piece(8)
        scp = piece(9)
        sx = piece(10)
        sgate = piece(11)
        q = q_ref[...]
        t2 = dzs * _silu(sgate)
        put(8, t2 * q)
        put(11, dzs * sb * q * _dsilu(sgate))
        dqext[0:t, :] = t2 * sb
        dqe = dqext[...]
        vv = scp * sx
        dv = jnp.zeros((t, BW), F32)
        for j in range(SHORT_K):
            sh = (dqe if j == 0 else pltpu.roll(dqe, t + HALO_SC - j, 0))[0:t, :]
            dv = dv + scw_ref[SHORT_K - 1 - j:SHORT_K - j, :] * sh
            dscw_ref[SHORT_K - 1 - j:SHORT_K - j, :] += _colsum(vv * sh)
        put(9, dv * sx)
        put(10, dv * scp)

        @pl.when(last)
        def _():
            r = lax.broadcasted_iota(jnp.int32, (GW, GW), 0)
            c = lax.broadcasted_iota(jnp.int32, (GW, GW), 1)
            for g in range(NG):
                dsguw_ref[g] = jnp.where(r >= c, dsguw_ref[g], 0.0)
                dsgub_ref[g:g + 1, :] = _colsum(bacc[g].T)
            dconvw_ref[...] = jnp.sum(dwacc[...], axis=1)

    row = lambda b, i: (b * nt + (nt - 1 - i), 0)
    full2 = lambda b, i: (0, 0)
    full3 = lambda b, i: (0, 0, 0)
    return _carried_call(
        body, comm, (proj, d_bo, wb, y_conv, q_sc, pooled, d_proj, pool_w, vecs, conv_w, sgu_w, sgu_bc, sc_w),
        name="mixers_bwd", grid=(nb, nt),
        in_specs=[pl.BlockSpec((t, N_PIECE_COLS), row),
                  pl.BlockSpec((t, 4 * D), row),
                  pl.BlockSpec((4, BW, D), full3),
                  pl.BlockSpec((t, BW), row), pl.BlockSpec((t, BW), row), pl.BlockSpec((t, BW), row),
                  pl.BlockSpec(memory_space=pl.ANY),
                  pl.BlockSpec((NG, GW, GW), full3),
                  pl.BlockSpec((8, BW), full2),
                  pl.BlockSpec((32, BW), full2),
                  pl.BlockSpec((NG, GW, GW), full3),
                  pl.BlockSpec((NG, GW, 1), full3),
                  pl.BlockSpec((8, BW), full2)],
        out_specs=[pl.BlockSpec((t, N_PIECE_COLS), row),
                   pl.BlockSpec((8, BW), full2), pl.BlockSpec((32, BW), full2), pl.BlockSpec((8, BW), full2),
                   pl.BlockSpec((NG, GW, GW), full3), pl.BlockSpec((NG, GW, GW), full3),
                   pl.BlockSpec((8, GW), full2)],
        out_shape=[SDS((n, IN_COLS), BF16), SDS((8, BW), F32), SDS((32, BW), F32), SDS((8, BW), F32),
                   SDS((NG, GW, GW), F32), SDS((NG, GW, GW), F32), SDS((8, GW), F32)],
        scratch_shapes=[pltpu.VMEM((t + HALO_POOL, BW), F32), pltpu.VMEM((t + HALO_CONV, BW), F32),
                        pltpu.VMEM((t + HALO_SC, BW), F32), pltpu.VMEM((NG, GW, GW), F32),
                        pltpu.VMEM((7, t + HALO_CONV, BW), F32), pltpu.VMEM((32, 8, BW), F32)],
        input_output_aliases={6: 0}, semantics=("arbitrary", "arbitrary"))


def _in_proj_bwd(d_proj, w_g, x2d, g_row, dxo, tm, comm=None):
    n = x2d.shape[0]
    half = D // 2

    def body(dp_ref, wu_ref, wl_ref, x_hbm, g_ref, dxo_hbm, dx_ref, dg_ref, acc, x_buf, dxo_buf, sems):
        i = pl.program_id(0)
        k = pl.program_id(1)
        rows = pl.ds(pl.multiple_of(i * tm, tm), tm)
        late = (pltpu.make_async_copy(x_hbm.at[rows], x_buf, sems.at[0]),
                pltpu.make_async_copy(dxo_hbm.at[rows], dxo_buf, sems.at[1]))

        @pl.when(jnp.logical_and(i == 0, k == 0))
        def _():
            dg_ref[...] = jnp.zeros_like(dg_ref)

        @pl.when(k == 0)
        def _():
            for cp in late:
                cp.start()
            acc[...] = jnp.zeros_like(acc)

        dp0 = dp_ref[:, :SHARD]
        dp1 = dp_ref[:, SHARD:]
        acc[:, :half] += _dot_nt(dp0, wu_ref[0]) + _dot_nt(dp1, wu_ref[1])
        acc[:, half:] += _dot_nt(dp0, wl_ref[0]) + _dot_nt(dp1, wl_ref[1])

        @pl.when(k == NDEV // 2 - 1)
        def _():
            for cp in late:
                cp.wait()
            dh = acc[...]
            x = x_buf[...]
            r = lax.rsqrt(jnp.mean(x * x, axis=-1, keepdims=True) + RMS_EPS)
            xn = x * r
            dg_ref[...] += _colsum(dh * xn)
            gy = dh * g_ref[...]
            dx_ref[...] = dxo_buf[...] + r * gy - xn * (r * jnp.mean(gy * xn, axis=-1, keepdims=True))

    any_spec = pl.BlockSpec(memory_space=pl.ANY)
    return _carried_call(
        body, comm, (d_proj, *w_g, x2d, g_row, dxo), name="in_proj_bwd", grid=(n // tm, NDEV // 2),
        in_specs=[pl.BlockSpec((tm, 2 * SHARD), lambda i, k: (i, k)),
                  pl.BlockSpec((2, half, SHARD), lambda i, k: (k, 0, 0)),
                  pl.BlockSpec((2, half, SHARD), lambda i, k: (k, 0, 0)),
                  any_spec,
                  pl.BlockSpec((1, D), lambda i, k: (0, 0)),
                  any_spec],
        out_specs=[pl.BlockSpec((tm, D), lambda i, k: (i, 0)), pl.BlockSpec((1, D), lambda i, k: (0, 0))],
        out_shape=[SDS((n, D), F32), SDS((1, D), F32)],
        scratch_shapes=[pltpu.VMEM((tm, D), F32), pltpu.VMEM((tm, D), F32), pltpu.VMEM((tm, D), F32),
                        pltpu.SemaphoreType.DMA((2,))],
        semantics=("arbitrary", "arbitrary"))


def _all_gather(arrs, name):
    na = len(arrs)

    def body(*refs):
        ins, outs = refs[:na], refs[na:2 * na]
        send_sems, recv_sems, local_sems = refs[2 * na:]
        x, y, c = _position()
        me, sibling = (x, y, c), (x, y, 1 - c)
        chips = [(1 - x, y), (x, 1 - y), (1 - x, 1 - y)]

        def copy(a, k, block, to, from_input=False):
            px, py, pc = block
            dst = outs[a].at[4 * px + 2 * py + pc]
            return pltpu.make_async_remote_copy(
                src_ref=ins[a] if from_input else dst, dst_ref=dst,
                send_sem=send_sems.at[7 * a + k], recv_sem=recv_sems.at[7 * a + k],
                device_id=to, device_id_type=MESH)

        mine = [pltpu.make_async_copy(ins[a], outs[a].at[4 * x + 2 * y + c], local_sems.at[a]) for a in range(na)]
        for cp in mine:
            cp.start()
        first = []
        for a in range(na):
            first.append(copy(a, 0, me, sibling, from_input=True))
            first += [copy(a, 1 + j, me, (*chip, c), from_input=True) for j, chip in enumerate(chips)]
        for cp in first:
            cp.start()
        passed = []
        for j, chip in enumerate(chips):
            for a in range(na):
                copy(a, 1 + j, (*chip, c), me).wait_recv()
                fwd = copy(a, 4 + j, (*chip, c), sibling)
                fwd.start()
                passed.append(fwd)
        for a in range(na):
            copy(a, 0, sibling, me).wait_recv()
            for j, chip in enumerate(chips):
                copy(a, 4 + j, (*chip, 1 - c), me).wait_recv()
        for cp in first + passed:
            cp.wait_send()
        for cp in mine:
            cp.wait()

    any_spec = pl.BlockSpec(memory_space=pl.ANY)
    return pl.pallas_call(
        body, name=name,
        in_specs=[any_spec] * na, out_specs=[any_spec] * na,
        out_shape=[SDS((NDEV,) + a.shape, a.dtype) for a in arrs],
        scratch_shapes=[pltpu.SemaphoreType.DMA((7 * na,)), pltpu.SemaphoreType.DMA((7 * na,)),
                        pltpu.SemaphoreType.DMA((na,))],
    )(*arrs)


def _adamw_math(g, w, m, v):
    m = ADAM_B1 * m + (1.0 - ADAM_B1) * g
    v = ADAM_B2 * v + (1.0 - ADAM_B2) * (g * g)
    m_hat = m / (1.0 - ADAM_B1 ** ADAM_STEP)
    v_hat = v / (1.0 - ADAM_B2 ** ADAM_STEP)
    delta = -ADAM_LR * (m_hat / (jnp.sqrt(v_hat) + ADAM_EPS) + ADAM_WD * w)
    return delta, m, v


def _adamw_layer(own, landed, w, m, v, layer, earlier, tr, name):
    _, rows, cols = w.shape

    def body(o_ref, l_ref, w_ref, m_ref, v_ref, *rest):
        g_out, d_out, m_out, v_out = rest[-4:]
        g = o_ref[...]
        for r in range(NDEV - 1):
            g = g + l_ref[r].astype(F32)
        g_out[...] = g
        d_out[...], m_out[...], v_out[...] = _adamw_math(g, w_ref[...], m_ref[...], v_ref[...])

    lay = pl.BlockSpec((None, tr, cols), lambda i: (layer, i, 0))
    any_spec = pl.BlockSpec(memory_space=pl.ANY)
    prior = list(earlier) if earlier is not None else []
    return pl.pallas_call(
        body, name=name, grid=(rows // tr,),
        in_specs=[pl.BlockSpec((tr, cols), lambda i: (i, 0)),
                  pl.BlockSpec((NDEV - 1, tr, cols), lambda i: (0, i, 0)), lay, lay, lay] + [any_spec] * len(prior),
        out_specs=[lay] * 4, out_shape=[SDS(w.shape, F32)] * 4,
        input_output_aliases={5 + k: k for k in range(len(prior))},
        compiler_params=_params("parallel"),
    )(own, landed, w, m, v, *prior)


def _adamw_gathered(parts, w, m, v, tr, name):
    rows, cols = w.shape

    def body(p_ref, w_ref, m_ref, v_ref, g_out, d_out, m_out, v_out):
        g = p_ref[0]
        for k in range(1, NDEV):
            g = g + p_ref[k]
        g_out[...] = g
        d_out[...], m_out[...], v_out[...] = _adamw_math(g, w_ref[...], m_ref[...], v_ref[...])

    blk = pl.BlockSpec((tr, cols), lambda i: (i, 0))
    return pl.pallas_call(
        body, name=name, grid=(rows // tr,),
        in_specs=[pl.BlockSpec((NDEV, tr, cols), lambda i: (0, i, 0)), blk, blk, blk],
        out_specs=[blk] * 4, out_shape=[SDS((rows, cols), F32)] * 4,
        compiler_params=_params("parallel"),
    )(parts, w, m, v)


def _adamw_plain(g, w, m, v):
    def body(g_ref, w_ref, m_ref, v_ref, d_out, m_out, v_out):
        d_out[...], m_out[...], v_out[...] = _adamw_math(g_ref[...], w_ref[...], m_ref[...], v_ref[...])

    return pl.pallas_call(body, name="adamw_conv_shards", out_shape=[SDS(w.shape, F32)] * 3)(g, w, m, v)


SMALL = ("norm_g", "pool_w", "pool_scale", "conv_w", "conv_b", "conv_ln_g", "conv_ln_b",
         "sgu_ln_g", "sgu_ln_b", "sgu_w", "sgu_b", "sc_w")
SMALL_SHAPES = {"norm_g": (D,), "pool_w": (NG, GW, GW), "pool_scale": (BW,), "conv_w": (CONV_K, BW), "conv_b": (BW,),
                "conv_ln_g": (BW,), "conv_ln_b": (BW,), "sgu_ln_g": (BW,), "sgu_ln_b": (BW,),
                "sgu_w": (NG, GW, GW), "sgu_b": (NG, GW), "sc_w": (SHORT_K, BW)}
REPLICATED = tuple(k for k in SMALL if k not in ("conv_w", "sc_w"))
PACK_UNIT = 8 * 128


def _size(shape):
    out = 1
    for s in shape:
        out *= s
    return out


def _padded(a):
    flat = a.reshape(-1)
    pad = -flat.shape[0] % PACK_UNIT
    return jnp.pad(flat, (0, pad)) if pad else flat


def _pack(arrays):
    return jnp.concatenate([_padded(a) for a in arrays]).reshape(-1, 128)


def _unpack(pack, shapes):
    flat = pack.reshape(-1)
    out, off = [], 0
    for shape in shapes:
        size = _size(shape)
        out.append(flat[off:off + size].reshape(shape))
        off += size + (-size % PACK_UNIT)
    return out


def _gathered_weights(wb, wo, conv):
    wb = wb.reshape(NDEV, 4, BW, D // NDEV).transpose(1, 2, 0, 3).reshape(4, BW, D)
    wo = wo.reshape(D, D)
    cw = CONV_K * (BW // NDEV)
    sw = SHORT_K * (BW // NDEV)
    sc_at = cw + (-cw % PACK_UNIT)
    flat = conv.reshape(NDEV, -1)
    conv_w = flat[:, :cw].reshape(NDEV, CONV_K, BW // NDEV).transpose(1, 0, 2).reshape(CONV_K, BW)
    sc_w = flat[:, sc_at:sc_at + sw].reshape(NDEV, SHORT_K, BW // NDEV).transpose(1, 0, 2).reshape(SHORT_K, BW)
    return wb, wo, conv_w, sc_w


def _layer_small_inputs(l, rep, conv_w, sc_w):
    vecs = jnp.stack([rep["pool_scale"][l], rep["conv_b"][l], rep["conv_ln_g"][l], rep["conv_ln_b"][l],
                      rep["sgu_ln_g"][l], rep["sgu_ln_b"][l], jnp.zeros((BW,), F32), jnp.zeros((BW,), F32)])
    conv_w = jnp.pad(conv_w, ((0, 32 - CONV_K), (0, 0)))
    sc_w = jnp.pad(sc_w, ((0, 8 - SHORT_K), (0, 0)))
    return (rep["pool_w"][l], vecs, conv_w, rep["sgu_w"][l], rep["sgu_b"][l].reshape(NG, GW, 1), sc_w)


def kernel(x, norm_g, w_in, pool_w, pool_scale, conv_w, conv_b, conv_ln_g, conv_ln_b, sgu_ln_g, sgu_ln_b, sgu_w, sgu_b, sc_w, w_branch, w_o, final_g, loss_target, m_norm_g, m_w_in, m_pool_w, m_pool_scale, m_conv_w, m_conv_b, m_conv_ln_g, m_conv_ln_b, m_sgu_ln_g, m_sgu_ln_b, m_sgu_w, m_sgu_b, m_sc_w, m_w_branch, m_w_o, m_final_g, v_norm_g, v_w_in, v_pool_w, v_pool_scale, v_conv_w, v_conv_b, v_conv_ln_g, v_conv_ln_b, v_sgu_ln_g, v_sgu_ln_b, v_sgu_w, v_sgu_b, v_sc_w, v_w_branch, v_w_o, v_final_g):
    args = dict(locals())
    w = {k: args[k] for k in SMALL + ("w_in", "w_branch", "w_o", "final_g")}
    mom = {k: args["m_" + k] for k in w}
    var = {k: args["v_" + k] for k in w}
    rep = {k: w[k] for k in REPLICATED}
    xi, yi, ci = _position()
    me = 4 * xi + 2 * yi + ci

    nb, seq, _ = x.shape
    n = nb * seq
    tm = min(512, n)
    tm_in = min(1024, n)
    tm_fwd = min(2048, n)
    tk = min(2048, n)
    t_fwd = min(256, seq)
    t_bwd = min(256, seq)

    w_in_b = [(w_in[l, :D // 2].astype(BF16), w_in[l, D // 2:].astype(BF16)) for l in range(DEPTH)]
    wb_b = [w_branch[l].astype(BF16).reshape(4 * BW, D // NDEV) for l in range(DEPTH)]
    wo_b = [w_o[l].astype(BF16) for l in range(DEPTH)]
    conv_b_ = [_pack([conv_w[l], sc_w[l]]) for l in range(DEPTH)]

    w_in_g = [_all_gather(list(w_in_b[0]), "w_in_all_gather"), None]
    x2d = x.reshape(n, D)
    saved = []
    ride = _Exchange(gathers=[wb_b[0], wo_b[0], conv_b_[0], w_in_b[1][0]])
    (proj, h), (wb_g, wo_g, conv_g, w_in_upper) = _in_proj(x2d, norm_g[0:1], w_in_g[0], tm_fwd, ride)
    for l in range(DEPTH):
        if l == 1:
            ride = _Exchange(gathers=[wb_b[1], wo_b[1], conv_b_[1]])
            (proj, h), (wb_g, wo_g, conv_g) = _in_proj(x2d, norm_g[l:l + 1], w_in_g[l], tm_fwd, ride)
        wb_l, wo_l, conv_l, sc_l = _gathered_weights(wb_g, wo_g, conv_g)
        small = _layer_small_inputs(l, rep, conv_l, sc_l)
        ride = _Exchange(gathers=[w_in_b[1][1]]) if l == 0 else None
        (z, y_conv, q_sc, pooled, x_new, merged), gathered = _mixers_fwd(
            proj, x2d, wb_l, wo_l, *small, nb, seq, t_fwd, ride)
        if l == 0:
            w_in_g[1] = [w_in_upper, gathered[0]]
        saved.append((x2d, proj, h, z, y_conv, q_sc, pooled, merged, small, wb_l, wo_l))
        x2d = x_new

    dx, loss_acc, d_final_g = _loss_head(x2d, loss_target.reshape(n, D), final_g.reshape(1, D), tm)

    grads = {k: [None] * DEPTH for k in SMALL}
    big = {"w_in": None, "w_branch": None, "w_o": None}
    big_w = {"w_in": (w_in, m_w_in, v_w_in, (DEPTH, D, SHARD), 256),
             "w_branch": (w_branch, m_w_branch, v_w_branch, (DEPTH, 4 * BW, D // NDEV), 512),
             "w_o": (w_o, m_w_o, v_w_o, (DEPTH, D // NDEV, D), 64)}

    def adamw(k, own, landed, l):
        wk, mk, vk, shape, tr = big_w[k]
        big[k] = _adamw_layer(own, landed, wk.reshape(shape), mk.reshape(shape), vk.reshape(shape), l, big[k], tr,
                              "adamw_" + k)

    for l in reversed(range(DEPTH)):
        x_in, proj, h, z, y_conv, q_sc, pooled, merged, small, wb_l, wo_l = saved[l]
        d_bo, d_proj, d_wo, d_wo_b, d_wb, d_wb_b = _merge_out_bwd(dx, z, proj, merged, wb_l, wo_l, tm)
        ride = _Exchange(scatters=[(d_wo_b.reshape(NDEV, D // NDEV, D), d_wo.reshape(NDEV, D // NDEV, D)),
                                   (d_wb_b.reshape(NDEV, 4 * BW, D // NDEV), d_wb.reshape(NDEV, 4 * BW, D // NDEV))])
        (d_proj, dvec, dconvw, dscw, dpoolw, dsguw, dsgub), (wo_landed, wo_own, wb_landed, wb_own) = _mixers_bwd(
            proj, d_bo, wb_l, y_conv, q_sc, pooled, d_proj, *small, nb, seq, t_bwd, ride)
        grads["pool_w"][l] = dpoolw
        grads["pool_scale"][l] = dvec[0]
        grads["conv_w"][l] = dconvw[:CONV_K]
        grads["conv_b"][l] = dvec[1]
        grads["conv_ln_g"][l] = dvec[2]
        grads["conv_ln_b"][l] = dvec[3]
        grads["sgu_ln_g"][l] = dvec[4]
        grads["sgu_ln_b"][l] = dvec[5]
        grads["sgu_w"][l] = dsguw
        grads["sgu_b"][l] = dsgub[:NG]
        grads["sc_w"][l] = dscw[:SHORT_K]
        if l == 0:
            grads["norm_g"][0] = jnp.zeros((D,), F32)
            part_pack = _pack([jnp.stack(grads[k]) for k in SMALL] + [d_final_g[0], loss_acc[0, 0:1]])
            ride = _Exchange(gathers=[part_pack])
        else:
            ride = None
        (d_w_in, d_w_in_b), gathered = _grad_w_in(h, d_proj, tk, ride)
        if l == 0:
            parts = gathered[0]
        ride = _Exchange(scatters=[(d_w_in_b, d_w_in)])
        (dx, d_norm_g), (in_landed, in_own) = _in_proj_bwd(d_proj, w_in_g[l], x_in, norm_g[l:l + 1], dx, tm_in, ride)
        adamw("w_o", wo_own, wo_landed, l)
        adamw("w_branch", wb_own, wb_landed, l)
        adamw("w_in", in_own, in_landed, l)
        if l == 1:
            grads["norm_g"][1] = d_norm_g[0]
    grad_x = dx.reshape(nb, seq, D)

    names = list(SMALL) + ["final_g", "loss"]
    shapes = [(DEPTH,) + SMALL_SHAPES[k] for k in SMALL] + [(D,), (1,)]
    whole = REPLICATED + ("final_g",)
    w_pack = _pack([w[k] if k in whole else jnp.zeros(s, F32) for k, s in zip(names, shapes)])
    m_pack = _pack([mom[k] if k in whole else jnp.zeros(s, F32) for k, s in zip(names, shapes)])
    v_pack = _pack([var[k] if k in whole else jnp.ones(s, F32) for k, s in zip(names, shapes)])
    rows = parts.shape[1]
    tr = max(r for r in range(8, 641, 8) if rows % r == 0)
    packs = _adamw_gathered(parts, w_pack, m_pack, v_pack, tr, "adamw_small")
    out_g, out_d, out_m, out_v = [dict(zip(names, _unpack(p, shapes))) for p in packs]
    loss = out_g["loss"][0]

    (tail,) = _all_gather([d_norm_g.reshape(8, 128)], "norm_g_all_gather")
    first = [a[0].reshape(8, 128) for a in (w["norm_g"], mom["norm_g"], var["norm_g"])]
    tails = _adamw_gathered(tail, *first, 8, "adamw_norm_g")
    for dst, t0 in zip((out_g, out_d, out_m, out_v), tails):
        dst["norm_g"] = jnp.concatenate([t0.reshape(1, D), dst["norm_g"][1:]])

    col = me * (BW // NDEV)
    for k in ("conv_w", "sc_w"):
        out_g[k] = lax.dynamic_slice_in_dim(out_g[k], col, BW // NDEV, axis=2)
    sharded_shapes = [w["conv_w"].shape, w["sc_w"].shape]
    d2, m2, v2 = _adamw_plain(_pack([out_g["conv_w"], out_g["sc_w"]]), _pack([w["conv_w"], w["sc_w"]]),
                              _pack([mom["conv_w"], mom["sc_w"]]), _pack([var["conv_w"], var["sc_w"]]))
    for dst, pk in ((out_d, d2), (out_m, m2), (out_v, v2)):
        dst["conv_w"], dst["sc_w"] = _unpack(pk, sharded_shapes)

    for k in big:
        out_g[k], out_d[k], out_m[k], out_v[k] = [o.reshape(w[k].shape) for o in big[k]]

    order = ("norm_g", "w_in", "pool_w", "pool_scale", "conv_w", "conv_b", "conv_ln_g", "conv_ln_b", "sgu_ln_g",
             "sgu_ln_b", "sgu_w", "sgu_b", "sc_w", "w_branch", "w_o", "final_g")
    return (loss, grad_x, *[out_g[k] for k in order], *[out_d[k] for k in order],
            *[out_m[k] for k in order], *[out_v[k] for k in order])
```

```python
import functools

import jax
import jax.numpy as jnp
from jax import lax
from jax.experimental import pallas as pl
from jax.experimental.pallas import tpu as pltpu

F32 = jnp.float32
BF16 = jnp.bfloat16
SDS = jax.ShapeDtypeStruct
MESH = pl.DeviceIdType.MESH

D = 1024
BW = 512
NG = 4
GW = 128
CONV_K = 31
SHORT_K = 3
POOL_WINDOWS = (2, 4, 8, 16)
N_PIECE_COLS = 12 * BW
IN_COLS = N_PIECE_COLS + 4 * D
NDEV = 8
SHARD = IN_COLS // NDEV
DEPTH = 2
RMS_EPS = 1e-6
LN_EPS = 1e-5
HALO_POOL, HALO_CONV, HALO_SC = 16, 32, 8
CONV_ROWS = 32

ADAM_LR = 0.001
ADAM_B1 = 0.9
ADAM_B2 = 0.999
ADAM_EPS = 1e-08
ADAM_WD = 0.01
ADAM_STEP = 10

VMEM_LIMIT = 56 * 1024 * 1024


def _params(*sem):
    return pltpu.CompilerParams(dimension_semantics=sem, vmem_limit_bytes=VMEM_LIMIT)


def _sigmoid(x):
    return jax.nn.sigmoid(x)


def _silu(x):
    return x * _sigmoid(x)


def _dsilu(x):
    s = _sigmoid(x)
    return s * (1.0 + x * (1.0 - s))


def _colsum(x):
    return jnp.sum(x, axis=0, keepdims=True)


def _dot(a, b):
    return jnp.dot(a, b, preferred_element_type=F32)


def _dot_nt(a, b):
    return lax.dot_general(a, b, (((1,), (1,)), ((), ())), preferred_element_type=F32)


def _dot_tn(a, b):
    return lax.dot_general(a, b, (((0,), (0,)), ((), ())), preferred_element_type=F32)


def _layer_norm_stats(y):
    mu = jnp.mean(y, axis=-1, keepdims=True)
    yc = y - mu
    var = jnp.mean(yc * yc, axis=-1, keepdims=True)
    rstd = lax.rsqrt(var + LN_EPS)
    return yc * rstd, rstd


def _layer_norm_bwd(d_hat, hat, rstd):
    return rstd * (d_hat - jnp.mean(d_hat, axis=-1, keepdims=True)
                   - hat * jnp.mean(d_hat * hat, axis=-1, keepdims=True))


def _position():
    return lax.axis_index("x"), lax.axis_index("y"), lax.axis_index("c")


def _flip(v, bit):
    return 1 - v if bit else v


class _Exchange:
    def __init__(self, gathers=(), scatters=()):
        self.gathers = list(gathers)
        self.scatters = list(scatters)
        self.inputs = self.gathers + [a for pair in self.scatters for a in pair]
        self.out_shapes = [SDS((NDEV,) + a.shape, a.dtype) for a in self.gathers]
        for send, keep in self.scatters:
            self.out_shapes += [SDS((NDEV - 1,) + send.shape[1:], send.dtype), SDS(keep.shape[1:], keep.dtype)]
        n = len(self.gathers) + len(self.scatters)
        self.scratch = [pltpu.SemaphoreType.DMA((7 * n,)), pltpu.SemaphoreType.DMA((7 * n,)),
                        pltpu.SemaphoreType.DMA((n,))]

    def _copies(self, ins, outs, sems):
        send_sems, recv_sems, local_sems = sems
        x, y, c = _position()
        me = 4 * x + 2 * y + c
        ng = len(self.gathers)
        local, remote = [], []
        for a in range(ng + len(self.scatters)):
            if a < ng:
                local.append(pltpu.make_async_copy(ins[a], outs[a].at[me], local_sems.at[a]))
            else:
                s = a - ng
                send, keep = ins[ng + 2 * s], ins[ng + 2 * s + 1]
                landed, own = outs[ng + 2 * s], outs[ng + 2 * s + 1]
                local.append(pltpu.make_async_copy(keep.at[me], own, local_sems.at[a]))
            for r in range(1, NDEV):
                peer = (_flip(x, r & 4), _flip(y, r & 2), _flip(c, r & 1))
                if a < ng:
                    src, dst = ins[a], outs[a].at[me]
                else:
                    src, dst = send.at[jnp.bitwise_xor(me, r)], landed.at[r - 1]
                remote.append(pltpu.make_async_remote_copy(
                    src_ref=src, dst_ref=dst, send_sem=send_sems.at[7 * a + r - 1],
                    recv_sem=recv_sems.at[7 * a + r - 1], device_id=peer, device_id_type=MESH))
        return local, remote

    def start(self, ins, outs, sems):
        local, remote = self._copies(ins, outs, sems)
        for cp in remote + local:
            cp.start()

    def finish(self, ins, outs, sems):
        local, remote = self._copies(ins, outs, sems)
        for cp in remote + local:
            cp.wait()


def _carried_call(body, comm, args, *, name, grid, in_specs, out_specs, out_shape, scratch_shapes=(),
                  semantics, input_output_aliases=None):
    aliases = input_output_aliases or {}
    if comm is None:
        out = pl.pallas_call(body, name=name, grid=grid, in_specs=in_specs, out_specs=out_specs, out_shape=out_shape,
                             scratch_shapes=list(scratch_shapes), input_output_aliases=aliases,
                             compiler_params=_params(*semantics))(*args)
        return list(out), []
    n_in, n_out, n_scr = len(in_specs), len(out_specs), len(scratch_shapes)
    c_in, c_out = len(comm.inputs), len(comm.out_shapes)

    def full_body(*refs):
        refs = list(refs)
        ins, refs = refs[:n_in], refs[n_in:]
        cins, refs = refs[:c_in], refs[c_in:]
        outs, refs = refs[:n_out], refs[n_out:]
        couts, refs = refs[:c_out], refs[c_out:]
        scr, sems = refs[:n_scr], refs[n_scr:]
        ids = [pl.program_id(d) for d in range(len(grid))]
        first = functools.reduce(jnp.logical_and, [i == 0 for i in ids])
        last = functools.reduce(jnp.logical_and, [i == g - 1 for i, g in zip(ids, grid)])

        @pl.when(first)
        def _():
            comm.start(cins, couts, sems)

        body(*ins, *outs, *scr)

        @pl.when(last)
        def _():
            comm.finish(cins, couts, sems)

    any_spec = pl.BlockSpec(memory_space=pl.ANY)
    out = pl.pallas_call(
        full_body, name=name, grid=grid,
        in_specs=list(in_specs) + [any_spec] * c_in, out_specs=list(out_specs) + [any_spec] * c_out,
        out_shape=list(out_shape) + comm.out_shapes, scratch_shapes=list(scratch_shapes) + comm.scratch,
        input_output_aliases=aliases, compiler_params=_params(*["arbitrary"] * len(grid)),
    )(*args, *comm.inputs)
    return list(out[:n_out]), list(out[n_out:])


def _in_proj(x2d, g_row, w_g, tm, comm=None):
    n = x2d.shape[0]
    half = D // 2

    def body(x_ref, g_ref, wu_ref, wl_ref, proj_ref, h_ref):
        @pl.when(pl.program_id(1) == 0)
        def _():
            x = x_ref[...]
            r = lax.rsqrt(jnp.mean(x * x, axis=-1, keepdims=True) + RMS_EPS)
            h_ref[...] = (x * r * g_ref[...]).astype(BF16)

        proj_ref[...] = (_dot(h_ref[:, :half], wu_ref[...]) + _dot(h_ref[:, half:], wl_ref[...])).astype(BF16)

    return _carried_call(
        body, comm, (x2d, g_row, *w_g), name="in_proj", grid=(n // tm, NDEV),
        in_specs=[pl.BlockSpec((tm, D), lambda i, j: (i, 0)),
                  pl.BlockSpec((1, D), lambda i, j: (0, 0)),
                  pl.BlockSpec((None, half, SHARD), lambda i, j: (j, 0, 0)),
                  pl.BlockSpec((None, half, SHARD), lambda i, j: (j, 0, 0))],
        out_specs=[pl.BlockSpec((tm, SHARD), lambda i, j: (i, j)),
                   pl.BlockSpec((tm, D), lambda i, j: (i, 0))],
        out_shape=[SDS((n, IN_COLS), BF16), SDS((n, D), BF16)],
        semantics=("parallel", "arbitrary"))


def _in_proj_first(x2d, g_row, w_own, order, tm, comm):
    n = x2d.shape[0]
    ni = n // tm
    half = D // 2
    c_in, c_out = len(comm.inputs), len(comm.out_shapes)

    def body(order_ref, x_ref, g_ref, wu_hbm, wl_hbm, *refs):
        refs = list(refs)
        cins, refs = refs[:c_in], refs[c_in:]
        proj_ref, h_hbm, gu_hbm, gl_hbm = refs[:4]
        couts, refs = refs[4:4 + c_out], refs[4 + c_out:]
        h_all, wbuf, send_sems, recv_sems, own_sems, load_sems, h_sems = refs[:7]
        csems = refs[7:]
        jp = pl.program_id(0)
        i = pl.program_id(1)
        x, y, c = _position()
        me = 4 * x + 2 * y + c
        sibling = (x, y, 1 - c)
        chips = [(1 - x, y), (x, 1 - y), (1 - x, 1 - y)]
        own = (wu_hbm, wl_hbm)
        gathered = (gu_hbm, gl_hbm)

        def copy(a, k, block, to, from_input=False):
            px, py, pc = block
            dst = gathered[a].at[4 * px + 2 * py + pc]
            return pltpu.make_async_remote_copy(
                src_ref=own[a] if from_input else dst, dst_ref=dst,
                send_sem=send_sems.at[7 * a + k], recv_sem=recv_sems.at[7 * a + k],
                device_id=to, device_id_type=MESH)

        def load(a, slot, block_id):
            return pltpu.make_async_copy(gathered[a].at[block_id], wbuf.at[slot, a], load_sems.at[2 * slot + a])

        def keep(a):
            return pltpu.make_async_copy(own[a], gathered[a].at[me], own_sems.at[a])

        def h_store(t):
            rows = pl.ds(t * tm, tm)
            return pltpu.make_async_copy(h_all.at[rows], h_hbm.at[rows], h_sems.at[t])

        first = jnp.logical_and(jp == 0, i == 0)
        last = jnp.logical_and(jp == NDEV - 1, i == ni - 1)

        @pl.when(first)
        def _():
            comm.start(cins, couts, csems)
            for a in range(2):
                keep(a).start()
                copy(a, 0, (x, y, c), sibling, from_input=True).start()
                for j, chip in enumerate(chips):
                    copy(a, 1 + j, (x, y, c), (*chip, c), from_input=True).start()
                own_block = pltpu.make_async_copy(own[a], wbuf.at[0, a], load_sems.at[a])
                own_block.start()
                own_block.wait()

        for nxt in range(1, NDEV):
            @pl.when(jnp.logical_and(jp == nxt - 1, i == ni - 1))
            def _(nxt=nxt):
                j, passed = (nxt - 2) // 2, nxt % 2 == 1
                for a in range(2):
                    if nxt == 1:
                        copy(a, 0, sibling, (x, y, c)).wait_recv()
                    elif passed:
                        copy(a, 4 + j, (*chips[j], 1 - c), (x, y, c)).wait_recv()
                    else:
                        copy(a, 1 + j, (*chips[j], c), (x, y, c)).wait_recv()
                        copy(a, 4 + j, (*chips[j], c), sibling).start()
                    load(a, nxt % 2, order_ref[nxt]).start()

        @pl.when(jnp.logical_and(jp > 0, i == 0))
        def _():
            for a in range(2):
                load(a, jp % 2, order_ref[jp]).wait()

        rows = pl.ds(pl.multiple_of(i * tm, tm), tm)

        @pl.when(jp == 0)
        def _():
            xt = x_ref[...]
            r = lax.rsqrt(jnp.mean(xt * xt, axis=-1, keepdims=True) + RMS_EPS)
            h_all[rows, :] = (xt * r * g_ref[...]).astype(BF16)
            for t in range(ni):
                @pl.when(i == t)
                def _(t=t):
                    h_store(t).start()

        slot = jp % 2
        proj_ref[...] = (_dot(h_all[rows, :half], wbuf[slot, 0]) + _dot(h_all[rows, half:], wbuf[slot, 1])).astype(BF16)

        @pl.when(last)
        def _():
            for a in range(2):
                keep(a).wait()
                copy(a, 0, (x, y, c), sibling, from_input=True).wait_send()
                for j, chip in enumerate(chips):
                    copy(a, 1 + j, (x, y, c), (*chip, c), from_input=True).wait_send()
                    copy(a, 4 + j, (*chip, c), sibling).wait_send()
            for t in range(ni):
                h_store(t).wait()
            comm.finish(cins, couts, csems)

    any_spec = pl.BlockSpec(memory_space=pl.ANY)
    blocks = SDS((NDEV, half, SHARD), BF16)
    out = pl.pallas_call(
        body, name="in_proj_first",
        grid_spec=pltpu.PrefetchScalarGridSpec(
            num_scalar_prefetch=1, grid=(NDEV, ni),
            in_specs=[pl.BlockSpec((tm, D), lambda jp, i, order: (jnp.where(jp == 0, i, ni - 1), 0)),
                      pl.BlockSpec((1, D), lambda jp, i, order: (0, 0)),
                      any_spec, any_spec] + [any_spec] * c_in,
            out_specs=[pl.BlockSpec((tm, SHARD), lambda jp, i, order: (i, order[jp])),
                       any_spec, any_spec, any_spec] + [any_spec] * c_out,
            scratch_shapes=[pltpu.VMEM((n, D), BF16), pltpu.VMEM((2, 2, half, SHARD), BF16),
                            pltpu.SemaphoreType.DMA((14,)), pltpu.SemaphoreType.DMA((14,)),
                            pltpu.SemaphoreType.DMA((2,)), pltpu.SemaphoreType.DMA((4,)),
                            pltpu.SemaphoreType.DMA((ni,))] + comm.scratch),
        out_shape=[SDS((n, IN_COLS), BF16), SDS((n, D), BF16), blocks, blocks] + comm.out_shapes,
        compiler_params=_params("arbitrary", "arbitrary"),
    )(order, x2d, g_row, *w_own, *comm.inputs)
    return list(out[:2]), list(out[2:4]), list(out[4:])


def _tril_bf16(w):
    r = lax.broadcasted_iota(jnp.int32, (GW, GW), 0)
    c = lax.broadcasted_iota(jnp.int32, (GW, GW), 1)
    return jnp.where(r >= c, w, 0.0).astype(BF16)


def _mixers_fwd(proj, x2d, wb, wo, pool_w, vecs, conv_w, sgu_w, sgu_bc, sc_w, nb, seq, t, comm=None):
    n = nb * seq
    nt = seq // t

    def body(p_ref, x_ref, wb_ref, wo_ref, poolw_ref, vec_ref, convw_ref, sguw_ref, sgub_ref, scw_ref,
             z_ref, y_ref, q_ref, pooled_ref, xo_ref, m_ref, pext, uext, vext):
        i = pl.program_id(1)

        def gated(k):
            gate = _sigmoid(p_ref[:, N_PIECE_COLS + k * D:N_PIECE_COLS + (k + 1) * D].astype(F32))
            return gate * _dot(z_ref[:, k * BW:(k + 1) * BW], wb_ref[k])

        @pl.when(i == 0)
        def _():
            pext[0:HALO_POOL, :] = jnp.zeros((HALO_POOL, BW), F32)
            uext[0:HALO_CONV, :] = jnp.zeros((HALO_CONV, BW), F32)
            vext[0:HALO_SC, :] = jnp.zeros((HALO_SC, BW), F32)

        @pl.when(i > 0)
        def _():
            pext[0:HALO_POOL, :] = pext[t:t + HALO_POOL, :]
            uext[0:HALO_CONV, :] = uext[t:t + HALO_CONV, :]
            vext[0:HALO_SC, :] = vext[t:t + HALO_SC, :]

        def piece(k):
            return p_ref[:, k * BW:(k + 1) * BW].astype(F32)

        pos = lax.broadcasted_iota(jnp.int32, (t, 1), 0) + i * t

        px = piece(0)
        pext[HALO_POOL:, :] = px
        for j, win in enumerate(POOL_WINDOWS):
            cols = slice(j * GW, (j + 1) * GW)
            s = pext[:, cols]
            step = 1
            while step < win:
                s = s + pltpu.roll(s, step, 0)
                step *= 2
            cnt = jnp.minimum(pos + 1, win).astype(F32)
            pooled = (s[HALO_POOL:, :] / cnt - px[:, cols]).astype(BF16)
            pooled_ref[:, cols] = pooled
            mixed = _dot(pooled, poolw_ref[j].astype(BF16))
            z_ref[:, cols] = (mixed * vec_ref[0:1, cols] * _silu(p_ref[:, BW + j * GW:BW + (j + 1) * GW].astype(F32))).astype(BF16)
        merged = gated(0)

        uext[HALO_CONV:, :] = piece(2) * _sigmoid(piece(3))
        ue = uext[...]
        acc = jnp.zeros((t, BW), F32)
        for b in range(8):
            rb = ue if b == 0 else pltpu.roll(ue, b, 0)
            for a in range((CONV_K - b + 7) // 8):
                j = 8 * a + b
                acc = acc + convw_ref[CONV_K - 1 - j:CONV_K - j, :] * rb[HALO_CONV - 8 * a:HALO_CONV - 8 * a + t, :]
        y = acc + vec_ref[1:2, :]
        y_ref[...] = y
        yhat, _ = _layer_norm_stats(y)
        act = _silu(yhat * vec_ref[2:3, :] + vec_ref[3:4, :])
        z_ref[:, BW:2 * BW] = (act * _silu(piece(4))).astype(BF16)
        merged = merged + gated(1)

        vhat, _ = _layer_norm_stats(piece(6))
        vn = (vhat * vec_ref[4:5, :] + vec_ref[5:6, :]).astype(BF16)
        for g in range(NG):
            ws = _tril_bf16(sguw_ref[g])
            cols = slice(g * GW, (g + 1) * GW)
            for c in range(t // GW):
                rows = slice(c * GW, (c + 1) * GW)
                sp = _dot(ws, vn[rows, cols]) + sgub_ref[g]
                gate = _silu(p_ref[rows, 7 * BW + g * GW:7 * BW + (g + 1) * GW].astype(F32))
                z_ref[rows, 2 * BW + g * GW:2 * BW + (g + 1) * GW] = (
                    p_ref[rows, 5 * BW + g * GW:5 * BW + (g + 1) * GW].astype(F32) * sp * gate).astype(BF16)
        merged = merged + gated(2)

        vext[HALO_SC:, :] = piece(9) * piece(10)
        ve = vext[...]
        q = jnp.zeros((t, BW), F32)
        for j in range(SHORT_K):
            sh = ve if j == 0 else pltpu.roll(ve, j, 0)
            q = q + scw_ref[SHORT_K - 1 - j:SHORT_K - j, :] * sh[HALO_SC:, :]
        q_ref[...] = q
        z_ref[:, 3 * BW:] = (piece(8) * q * _silu(piece(11))).astype(BF16)
        mb = (merged + gated(3)).astype(BF16)
        m_ref[...] = mb
        xo_ref[...] = x_ref[...] + _dot(mb, wo_ref[...])

    row = lambda b, i: (b * nt + i, 0)
    full2 = lambda b, i: (0, 0)
    full3 = lambda b, i: (0, 0, 0)
    return _carried_call(
        body, comm, (proj, x2d, wb, wo, pool_w, vecs, conv_w, sgu_w, sgu_bc, sc_w), name="mixers_fwd", grid=(nb, nt),
        in_specs=[pl.BlockSpec((t, IN_COLS), row),
                  pl.BlockSpec((t, D), row),
                  pl.BlockSpec((4, BW, D), full3),
                  pl.BlockSpec((D, D), full2),
                  pl.BlockSpec((NG, GW, GW), full3),
                  pl.BlockSpec((8, BW), full2),
                  pl.BlockSpec((32, BW), full2),
                  pl.BlockSpec((NG, GW, GW), full3),
                  pl.BlockSpec((NG, GW, 1), full3),
                  pl.BlockSpec((8, BW), full2)],
        out_specs=[pl.BlockSpec((t, 4 * BW), row), pl.BlockSpec((t, BW), row),
                   pl.BlockSpec((t, BW), row), pl.BlockSpec((t, BW), row),
                   pl.BlockSpec((t, D), row), pl.BlockSpec((t, D), row)],
        out_shape=[SDS((n, 4 * BW), BF16), SDS((n, BW), F32), SDS((n, BW), F32), SDS((n, BW), BF16),
                   SDS((n, D), F32), SDS((n, D), BF16)],
        scratch_shapes=[pltpu.VMEM((t + HALO_POOL, BW), F32), pltpu.VMEM((t + HALO_CONV, BW), F32),
                        pltpu.VMEM((t + HALO_SC, BW), F32)],
        semantics=("arbitrary", "arbitrary"))


def _loss_head(x2d, tgt2d, g_row, tm):
    n = x2d.shape[0]

    def body(x_ref, t_ref, g_ref, dx_ref, loss_ref, dg_ref):
        @pl.when(pl.program_id(0) == 0)
        def _():
            loss_ref[...] = jnp.zeros_like(loss_ref)
            dg_ref[...] = jnp.zeros_like(dg_ref)

        x = x_ref[...]
        g = g_ref[...]
        r = lax.rsqrt(jnp.mean(x * x, axis=-1, keepdims=True) + RMS_EPS)
        xn = x * r
        e = xn * g - t_ref[...]
        loss_ref[...] += 0.5 * jnp.sum(jnp.mean(e * e, axis=-1, keepdims=True))
        dy = e * (1.0 / D)
        dg_ref[...] += _colsum(dy * xn)
        gy = dy * g
        dx_ref[...] = r * gy - xn * (r * jnp.mean(gy * xn, axis=-1, keepdims=True))

    return pl.pallas_call(
        body, name="loss_head", grid=(n // tm,),
        in_specs=[pl.BlockSpec((tm, D), lambda i: (i, 0)), pl.BlockSpec((tm, D), lambda i: (i, 0)),
                  pl.BlockSpec((1, D), lambda i: (0, 0))],
        out_specs=[pl.BlockSpec((tm, D), lambda i: (i, 0)), pl.BlockSpec((8, 128), lambda i: (0, 0)),
                   pl.BlockSpec((1, D), lambda i: (0, 0))],
        out_shape=[SDS((n, D), F32), SDS((8, 128), F32), SDS((1, D), F32)],
        compiler_params=_params("arbitrary"),
    )(x2d, tgt2d, g_row)


def _merge_out_bwd(dxo, z, proj, merged, wb, wo, tm):
    n = dxo.shape[0]
    ni = n // tm
    sub = D // NDEV

    def body(dx_ref, z_ref, mg_ref, m_ref, wb_ref, wo_ref,
             dbo_ref, dp_ref, gwo_ref, gwo_b_ref, gwb_ref, gwb_b_ref,
             dm_ref, acc_wo, acc_wb, half_wo, half_wb, sems):
        i = pl.program_id(0)
        h = pl.program_id(1)

        @pl.when(jnp.logical_and(i == 0, h == 0))
        def _():
            acc_wo[...] = jnp.zeros_like(acc_wo)
            acc_wb[...] = jnp.zeros_like(acc_wb)

        @pl.when(h == 0)
        def _():
            dxb = dx_ref[...].astype(BF16)
            dm_ref[...] = _dot_nt(dxb, wo_ref[...])
            acc_wo[...] += _dot_tn(m_ref[...], dxb)

        dm = dm_ref[...]
        for k in range(2):
            zk = z_ref[:, k * BW:(k + 1) * BW]
            w = wb_ref[2 * h + k]
            bo = _dot(zk, w)
            sig = _sigmoid(mg_ref[:, k * D:(k + 1) * D].astype(F32))
            dbo = (dm * sig).astype(BF16)
            dp_ref[:, k * D:(k + 1) * D] = (dm * bo * sig * (1.0 - sig)).astype(BF16)
            dbo_ref[:, k * D:(k + 1) * D] = dbo
            acc_wb[2 * h + k] += _dot_tn(zk, dbo)

        @pl.when(jnp.logical_and(i == ni - 1, h == 1))
        def _():
            half_wo[...] = acc_wo[...].astype(BF16)
            half_wb[...] = acc_wb[...].astype(BF16)
            copies = [pltpu.make_async_copy(acc_wo, gwo_ref, sems.at[0]),
                      pltpu.make_async_copy(half_wo, gwo_b_ref, sems.at[1])]
            for d in range(NDEV):
                cols = pl.ds(d * sub, sub)
                copies.append(pltpu.make_async_copy(acc_wb.at[:, :, cols], gwb_ref.at[d], sems.at[2 + 2 * d]))
                copies.append(pltpu.make_async_copy(half_wb.at[:, :, cols], gwb_b_ref.at[d], sems.at[3 + 2 * d]))
            for cp in copies:
                cp.start()
            for cp in copies:
                cp.wait()

    any_spec = pl.BlockSpec(memory_space=pl.ANY)
    return pl.pallas_call(
        body, name="merge_out_bwd", grid=(ni, 2),
        in_specs=[pl.BlockSpec((tm, D), lambda i, h: (i, 0)),
                  pl.BlockSpec((tm, 2 * BW), lambda i, h: (i, h)),
                  pl.BlockSpec((tm, 2 * D), lambda i, h: (i, 3 + h)),
                  pl.BlockSpec((tm, D), lambda i, h: (i, 0)),
                  pl.BlockSpec((4, BW, D), lambda i, h: (0, 0, 0)),
                  pl.BlockSpec((D, D), lambda i, h: (0, 0))],
        out_specs=[pl.BlockSpec((tm, 2 * D), lambda i, h: (i, h)),
                   pl.BlockSpec((tm, 2 * D), lambda i, h: (i, 3 + h)),
                   any_spec, any_spec, any_spec, any_spec],
        out_shape=[SDS((n, 4 * D), BF16), SDS((n, IN_COLS), BF16), SDS((D, D), F32), SDS((D, D), BF16),
                   SDS((NDEV, 4, BW, sub), F32), SDS((NDEV, 4, BW, sub), BF16)],
        scratch_shapes=[pltpu.VMEM((tm, D), F32), pltpu.VMEM((D, D), F32), pltpu.VMEM((4, BW, D), F32),
                        pltpu.VMEM((D, D), BF16), pltpu.VMEM((4, BW, D), BF16),
                        pltpu.SemaphoreType.DMA((2 + 2 * NDEV,))],
        compiler_params=_params("arbitrary", "arbitrary"),
    )(dxo, z, proj, merged, wb, wo)


def _grad_w_in(h, d_proj, tk, comm=None):
    nk = h.shape[0] // tk

    def body(h_ref, dp_ref, o_ref, ob_ref):
        k = pl.program_id(1)

        @pl.when(k == 0)
        def _():
            o_ref[...] = jnp.zeros_like(o_ref)

        o_ref[...] += _dot_tn(h_ref[...], dp_ref[...])

        @pl.when(k == nk - 1)
        def _():
            ob_ref[...] = o_ref[...].astype(BF16)

    out_spec = pl.BlockSpec((None, D, SHARD), lambda j, k: (j, 0, 0))
    return _carried_call(
        body, comm, (h, d_proj), name="grad_w_in", grid=(NDEV, nk),
        in_specs=[pl.BlockSpec((tk, D), lambda j, k: (k, 0)), pl.BlockSpec((tk, SHARD), lambda j, k: (k, j))],
        out_specs=[out_spec, out_spec], out_shape=[SDS((NDEV, D, SHARD), F32), SDS((NDEV, D, SHARD), BF16)],
        semantics=("parallel", "arbitrary"))


def _mixers_bwd(proj, d_bo, wb, y_conv, q_sc, pooled, d_proj, pool_w, vecs, conv_w, sgu_w, sgu_bc, sc_w, nb, seq, t, comm=None):
    n = nb * seq
    nt = seq // t

    def body(p_ref, dbo_ref, wb_ref, y_ref, q_ref, pooled_ref, dpin_ref,
             poolw_ref, vec_ref, convw_ref, sguw_ref, sgub_ref, scw_ref,
             dp_ref, dvec_ref, dconvw_ref, dscw_ref, dpoolw_ref, dsguw_ref, dsgub_ref,
             rext, dyext, dqext, bacc, dyrot, dwacc):
        b = pl.program_id(0)
        i = pl.program_id(1)
        first = jnp.logical_and(b == 0, i == 0)
        last = jnp.logical_and(b == nb - 1, i == nt - 1)

        @pl.when(first)
        def _():
            dvec_ref[...] = jnp.zeros_like(dvec_ref)
            dconvw_ref[...] = jnp.zeros_like(dconvw_ref)
            dscw_ref[...] = jnp.zeros_like(dscw_ref)
            dpoolw_ref[...] = jnp.zeros_like(dpoolw_ref)
            dsguw_ref[...] = jnp.zeros_like(dsguw_ref)
            dsgub_ref[...] = jnp.zeros_like(dsgub_ref)
            bacc[...] = jnp.zeros_like(bacc)
            dwacc[...] = jnp.zeros_like(dwacc)

        @pl.when(i == 0)
        def _():
            rext[t:, :] = jnp.zeros((HALO_POOL, BW), F32)
            dyext[t:, :] = jnp.zeros((HALO_CONV, BW), F32)
            dqext[t:, :] = jnp.zeros((HALO_SC, BW), F32)

        @pl.when(i > 0)
        def _():
            rext[t:, :] = rext[0:HALO_POOL, :]
            dyext[t:, :] = dyext[0:HALO_CONV, :]
            dqext[t:, :] = dqext[0:HALO_SC, :]

        def piece(k):
            return p_ref[:, k * BW:(k + 1) * BW].astype(F32)

        def put(k, v):
            dp_ref[:, k * BW:(k + 1) * BW] = v.astype(BF16)

        def dz(k):
            return _dot_nt(dbo_ref[:, k * D:(k + 1) * D], wb_ref[k])

        pos = lax.broadcasted_iota(jnp.int32, (t, 1), 0) + (nt - 1 - i) * t

        dzp = dz(0)
        pg = piece(1)
        d_pm = dzp * _silu(pg)
        for j, win in enumerate(POOL_WINDOWS):
            cols = slice(j * GW, (j + 1) * GW)
            pw = poolw_ref[j].astype(BF16)
            pooled = pooled_ref[:, cols]
            mixed = _dot(pooled, pw)
            scale = vec_ref[0:1, cols]
            dvec_ref[0:1, cols] += _colsum(d_pm[:, cols] * mixed)
            dp_ref[:, BW + j * GW:BW + (j + 1) * GW] = (dzp[:, cols] * (mixed * scale) * _dsilu(pg[:, cols])).astype(BF16)
            d_mixed = (d_pm[:, cols] * scale).astype(BF16)
            dpoolw_ref[j] += _dot_tn(pooled, d_mixed)
            d_pooled = _dot_nt(d_mixed, pw)
            cnt = jnp.minimum(pos + 1, win).astype(F32)
            rext[0:t, cols] = d_pooled / cnt
            s = rext[:, cols]
            step = 1
            while step < win:
                s = s + pltpu.roll(s, t + HALO_POOL - step, 0)
                step *= 2
            dp_ref[:, cols] = (s[0:t, :] - d_pooled).astype(BF16)

        dzc = dz(1)
        cgate = piece(4)
        yhat, rstd = _layer_norm_stats(y_ref[...])
        ln = yhat * vec_ref[2:3, :] + vec_ref[3:4, :]
        put(4, dzc * _silu(ln) * _dsilu(cgate))
        d_ln = dzc * _silu(cgate) * _dsilu(ln)
        dvec_ref[2:3, :] += _colsum(d_ln * yhat)
        dvec_ref[3:4, :] += _colsum(d_ln)
        dy = _layer_norm_bwd(d_ln * vec_ref[2:3, :], yhat, rstd)
        dvec_ref[1:2, :] += _colsum(dy)
        dyext[0:t, :] = dy
        dye = dyext[...]
        for b in range(1, 8):
            dyrot[b - 1] = pltpu.roll(dye, t + HALO_CONV - b, 0)

        def conv_rows(c, carry):
            r0 = pl.multiple_of(c * CONV_ROWS, CONV_ROWS)
            rows = pl.ds(r0, CONV_ROWS)
            sig_cb = _sigmoid(p_ref[rows, 3 * BW:4 * BW].astype(F32))
            u = p_ref[rows, 2 * BW:3 * BW].astype(F32) * sig_cb
            du = jnp.zeros((CONV_ROWS, BW), F32)
            for b in range(8):
                for a in range((CONV_K - b + 7) // 8):
                    j = 8 * a + b
                    at = pl.ds(pl.multiple_of(r0 + 8 * a, 8), CONV_ROWS)
                    sh = dyext[at, :] if b == 0 else dyrot[b - 1, at, :]
                    du = du + convw_ref[CONV_K - 1 - j:CONV_K - j, :] * sh
                    dwacc[CONV_K - 1 - j] += jnp.sum((u * sh).reshape(CONV_ROWS // 8, 8, BW), axis=0)
            dp_ref[rows, 2 * BW:3 * BW] = (du * sig_cb).astype(BF16)
            dp_ref[rows, 3 * BW:4 * BW] = (du * u * (1.0 - sig_cb)).astype(BF16)
            return carry

        lax.fori_loop(0, t // CONV_ROWS, conv_rows, 0)

        dzg = dz(2)
        gu = piece(5)
        ggate = piece(7)
        vhat, vrstd = _layer_norm_stats(piece(6))
        vn = (vhat * vec_ref[4:5, :] + vec_ref[5:6, :]).astype(BF16)
        t1 = dzg * _silu(ggate)
        d_sp_all = t1 * gu
        d_vn_cols = []
        sp_cols = []
        for g in range(NG):
            ws = _tril_bf16(sguw_ref[g])
            cols = slice(g * GW, (g + 1) * GW)
            d_vn_rows = []
            sp_rows = []
            for c in range(t // GW):
                rows = slice(c * GW, (c + 1) * GW)
                sp_rows.append(_dot(ws, vn[rows, cols]) + sgub_ref[g])
                d_sp = d_sp_all[rows, cols]
                bacc[g] += d_sp
                d_spb = d_sp.astype(BF16)
                d_vn_rows.append(_dot_tn(ws, d_spb))
                dsguw_ref[g] += _dot_nt(d_spb, vn[rows, cols])
            d_vn_cols.append(jnp.concatenate(d_vn_rows, axis=0))
            sp_cols.append(jnp.concatenate(sp_rows, axis=0))
        d_vn = jnp.concatenate(d_vn_cols, axis=1)
        sp = jnp.concatenate(sp_cols, axis=1)
        put(5, t1 * sp)
        put(7, dzg * gu * sp * _dsilu(ggate))
        dvec_ref[4:5, :] += _colsum(d_vn * vhat)
        dvec_ref[5:6, :] += _colsum(d_vn)
        put(6, _layer_norm_bwd(d_vn * vec_ref[4:5, :], vhat, vrstd))

        dzs = dz(3)
        sb = piece(8)
        scp = piece(9)
        sx = piece(10)
        sgate = piece(11)
        q = q_ref[...]
        t2 = dzs * _silu(sgate)
        put(8, t2 * q)
        put(11, dzs * sb * q * _dsilu(sgate))
        dqext[0:t, :] = t2 * sb
        dqe = dqext[...]
        vv = scp * sx
        dv = jnp.zeros((t, BW), F32)
        for j in range(SHORT_K):
            sh = (dqe if j == 0 else pltpu.roll(dqe, t + HALO_SC - j, 0))[0:t, :]
            dv = dv + scw_ref[SHORT_K - 1 - j:SHORT_K - j, :] * sh
            dscw_ref[SHORT_K - 1 - j:SHORT_K - j, :] += _colsum(vv * sh)
        put(9, dv * sx)
        put(10, dv * scp)

        @pl.when(last)
        def _():
            r = lax.broadcasted_iota(jnp.int32, (GW, GW), 0)
            c = lax.broadcasted_iota(jnp.int32, (GW, GW), 1)
            for g in range(NG):
                dsguw_ref[g] = jnp.where(r >= c, dsguw_ref[g], 0.0)
                dsgub_ref[g:g + 1, :] = _colsum(bacc[g].T)
            dconvw_ref[...] = jnp.sum(dwacc[...], axis=1)

    row = lambda b, i: (b * nt + (nt - 1 - i), 0)
    full2 = lambda b, i: (0, 0)
    full3 = lambda b, i: (0, 0, 0)
    return _carried_call(
        body, comm, (proj, d_bo, wb, y_conv, q_sc, pooled, d_proj, pool_w, vecs, conv_w, sgu_w, sgu_bc, sc_w),
        name="mixers_bwd", grid=(nb, nt),
        in_specs=[pl.BlockSpec((t, N_PIECE_COLS), row),
                  pl.BlockSpec((t, 4 * D), row),
                  pl.BlockSpec((4, BW, D), full3),
                  pl.BlockSpec((t, BW), row), pl.BlockSpec((t, BW), row), pl.BlockSpec((t, BW), row),
                  pl.BlockSpec(memory_space=pl.ANY),
                  pl.BlockSpec((NG, GW, GW), full3),
                  pl.BlockSpec((8, BW), full2),
                  pl.BlockSpec((32, BW), full2),
                  pl.BlockSpec((NG, GW, GW), full3),
                  pl.BlockSpec((NG, GW, 1), full3),
                  pl.BlockSpec((8, BW), full2)],
        out_specs=[pl.BlockSpec((t, N_PIECE_COLS), row),
                   pl.BlockSpec((8, BW), full2), pl.BlockSpec((32, BW), full2), pl.BlockSpec((8, BW), full2),
                   pl.BlockSpec((NG, GW, GW), full3), pl.BlockSpec((NG, GW, GW), full3),
                   pl.BlockSpec((8, GW), full2)],
        out_shape=[SDS((n, IN_COLS), BF16), SDS((8, BW), F32), SDS((32, BW), F32), SDS((8, BW), F32),
                   SDS((NG, GW, GW), F32), SDS((NG, GW, GW), F32), SDS((8, GW), F32)],
        scratch_shapes=[pltpu.VMEM((t + HALO_POOL, BW), F32), pltpu.VMEM((t + HALO_CONV, BW), F32),
                        pltpu.VMEM((t + HALO_SC, BW), F32), pltpu.VMEM((NG, GW, GW), F32),
                        pltpu.VMEM((7, t + HALO_CONV, BW), F32), pltpu.VMEM((32, 8, BW), F32)],
        input_output_aliases={6: 0}, semantics=("arbitrary", "arbitrary"))


def _in_proj_bwd(d_proj, w_g, x2d, g_row, dxo, tm, comm=None):
    n = x2d.shape[0]
    half = D // 2

    def body(dp_ref, wu_ref, wl_ref, x_hbm, g_ref, dxo_hbm, dx_ref, dg_ref, acc, x_buf, dxo_buf, sems):
        i = pl.program_id(0)
        k = pl.program_id(1)
        rows = pl.ds(pl.multiple_of(i * tm, tm), tm)
        late = (pltpu.make_async_copy(x_hbm.at[rows], x_buf, sems.at[0]),
                pltpu.make_async_copy(dxo_hbm.at[rows], dxo_buf, sems.at[1]))

        @pl.when(jnp.logical_and(i == 0, k == 0))
        def _():
            dg_ref[...] = jnp.zeros_like(dg_ref)

        @pl.when(k == 0)
        def _():
            for cp in late:
                cp.start()
            acc[...] = jnp.zeros_like(acc)

        dp0 = dp_ref[:, :SHARD]
        dp1 = dp_ref[:, SHARD:]
        acc[:, :half] += _dot_nt(dp0, wu_ref[0]) + _dot_nt(dp1, wu_ref[1])
        acc[:, half:] += _dot_nt(dp0, wl_ref[0]) + _dot_nt(dp1, wl_ref[1])

        @pl.when(k == NDEV // 2 - 1)
        def _():
            for cp in late:
                cp.wait()
            dh = acc[...]
            x = x_buf[...]
            r = lax.rsqrt(jnp.mean(x * x, axis=-1, keepdims=True) + RMS_EPS)
            xn = x * r
            dg_ref[...] += _colsum(dh * xn)
            gy = dh * g_ref[...]
            dx_ref[...] = dxo_buf[...] + r * gy - xn * (r * jnp.mean(gy * xn, axis=-1, keepdims=True))

    any_spec = pl.BlockSpec(memory_space=pl.ANY)
    return _carried_call(
        body, comm, (d_proj, *w_g, x2d, g_row, dxo), name="in_proj_bwd", grid=(n // tm, NDEV // 2),
        in_specs=[pl.BlockSpec((tm, 2 * SHARD), lambda i, k: (i, k)),
                  pl.BlockSpec((2, half, SHARD), lambda i, k: (k, 0, 0)),
                  pl.BlockSpec((2, half, SHARD), lambda i, k: (k, 0, 0)),
                  any_spec,
                  pl.BlockSpec((1, D), lambda i, k: (0, 0)),
                  any_spec],
        out_specs=[pl.BlockSpec((tm, D), lambda i, k: (i, 0)), pl.BlockSpec((1, D), lambda i, k: (0, 0))],
        out_shape=[SDS((n, D), F32), SDS((1, D), F32)],
        scratch_shapes=[pltpu.VMEM((tm, D), F32), pltpu.VMEM((tm, D), F32), pltpu.VMEM((tm, D), F32),
                        pltpu.SemaphoreType.DMA((2,))],
        semantics=("arbitrary", "arbitrary"))


def _all_gather(arrs, name):
    na = len(arrs)

    def body(*refs):
        ins, outs = refs[:na], refs[na:2 * na]
        send_sems, recv_sems, local_sems = refs[2 * na:]
        x, y, c = _position()
        me, sibling = (x, y, c), (x, y, 1 - c)
        chips = [(1 - x, y), (x, 1 - y), (1 - x, 1 - y)]

        def copy(a, k, block, to, from_input=False):
            px, py, pc = block
            dst = outs[a].at[4 * px + 2 * py + pc]
            return pltpu.make_async_remote_copy(
                src_ref=ins[a] if from_input else dst, dst_ref=dst,
                send_sem=send_sems.at[7 * a + k], recv_sem=recv_sems.at[7 * a + k],
                device_id=to, device_id_type=MESH)

        mine = [pltpu.make_async_copy(ins[a], outs[a].at[4 * x + 2 * y + c], local_sems.at[a]) for a in range(na)]
        for cp in mine:
            cp.start()
        first = []
        for a in range(na):
            first.append(copy(a, 0, me, sibling, from_input=True))
            first += [copy(a, 1 + j, me, (*chip, c), from_input=True) for j, chip in enumerate(chips)]
        for cp in first:
            cp.start()
        passed = []
        for j, chip in enumerate(chips):
            for a in range(na):
                copy(a, 1 + j, (*chip, c), me).wait_recv()
                fwd = copy(a, 4 + j, (*chip, c), sibling)
                fwd.start()
                passed.append(fwd)
        for a in range(na):
            copy(a, 0, sibling, me).wait_recv()
            for j, chip in enumerate(chips):
                copy(a, 4 + j, (*chip, 1 - c), me).wait_recv()
        for cp in first + passed:
            cp.wait_send()
        for cp in mine:
            cp.wait()

    any_spec = pl.BlockSpec(memory_space=pl.ANY)
    return pl.pallas_call(
        body, name=name,
        in_specs=[any_spec] * na, out_specs=[any_spec] * na,
        out_shape=[SDS((NDEV,) + a.shape, a.dtype) for a in arrs],
        scratch_shapes=[pltpu.SemaphoreType.DMA((7 * na,)), pltpu.SemaphoreType.DMA((7 * na,)),
                        pltpu.SemaphoreType.DMA((na,))],
    )(*arrs)


def _adamw_math(g, w, m, v):
    m = ADAM_B1 * m + (1.0 - ADAM_B1) * g
    v = ADAM_B2 * v + (1.0 - ADAM_B2) * (g * g)
    m_hat = m / (1.0 - ADAM_B1 ** ADAM_STEP)
    v_hat = v / (1.0 - ADAM_B2 ** ADAM_STEP)
    delta = -ADAM_LR * (m_hat / (jnp.sqrt(v_hat) + ADAM_EPS) + ADAM_WD * w)
    return delta, m, v


def _adamw_layer(own, landed, w, m, v, layer, earlier, tr, name):
    _, rows, cols = w.shape

    def body(o_ref, l_ref, w_ref, m_ref, v_ref, *rest):
        g_out, d_out, m_out, v_out = rest[-4:]
        g = o_ref[...]
        for r in range(NDEV - 1):
            g = g + l_ref[r].astype(F32)
        g_out[...] = g
        d_out[...], m_out[...], v_out[...] = _adamw_math(g, w_ref[...], m_ref[...], v_ref[...])

    lay = pl.BlockSpec((None, tr, cols), lambda i: (layer, i, 0))
    any_spec = pl.BlockSpec(memory_space=pl.ANY)
    prior = list(earlier) if earlier is not None else []
    return pl.pallas_call(
        body, name=name, grid=(rows // tr,),
        in_specs=[pl.BlockSpec((tr, cols), lambda i: (i, 0)),
                  pl.BlockSpec((NDEV - 1, tr, cols), lambda i: (0, i, 0)), lay, lay, lay] + [any_spec] * len(prior),
        out_specs=[lay] * 4, out_shape=[SDS(w.shape, F32)] * 4,
        input_output_aliases={5 + k: k for k in range(len(prior))},
        compiler_params=_params("parallel"),
    )(own, landed, w, m, v, *prior)


def _adamw_gathered(parts, w, m, v, tr, name):
    rows, cols = w.shape

    def body(p_ref, w_ref, m_ref, v_ref, g_out, d_out, m_out, v_out):
        g = p_ref[0]
        for k in range(1, NDEV):
            g = g + p_ref[k]
        g_out[...] = g
        d_out[...], m_out[...], v_out[...] = _adamw_math(g, w_ref[...], m_ref[...], v_ref[...])

    blk = pl.BlockSpec((tr, cols), lambda i: (i, 0))
    return pl.pallas_call(
        body, name=name, grid=(rows // tr,),
        in_specs=[pl.BlockSpec((NDEV, tr, cols), lambda i: (0, i, 0)), blk, blk, blk],
        out_specs=[blk] * 4, out_shape=[SDS((rows, cols), F32)] * 4,
        compiler_params=_params("parallel"),
    )(parts, w, m, v)


def _adamw_plain(g, w, m, v):
    def body(g_ref, w_ref, m_ref, v_ref, d_out, m_out, v_out):
        d_out[...], m_out[...], v_out[...] = _adamw_math(g_ref[...], w_ref[...], m_ref[...], v_ref[...])

    return pl.pallas_call(body, name="adamw_conv_shards", out_shape=[SDS(w.shape, F32)] * 3)(g, w, m, v)


SMALL = ("norm_g", "pool_w", "pool_scale", "conv_w", "conv_b", "conv_ln_g", "conv_ln_b",
         "sgu_ln_g", "sgu_ln_b", "sgu_w", "sgu_b", "sc_w")
SMALL_SHAPES = {"norm_g": (D,), "pool_w": (NG, GW, GW), "pool_scale": (BW,), "conv_w": (CONV_K, BW), "conv_b": (BW,),
                "conv_ln_g": (BW,), "conv_ln_b": (BW,), "sgu_ln_g": (BW,), "sgu_ln_b": (BW,),
                "sgu_w": (NG, GW, GW), "sgu_b": (NG, GW), "sc_w": (SHORT_K, BW)}
REPLICATED = tuple(k for k in SMALL if k not in ("conv_w", "sc_w"))
PACK_UNIT = 8 * 128


def _size(shape):
    out = 1
    for s in shape:
        out *= s
    return out


def _padded(a):
    flat = a.reshape(-1)
    pad = -flat.shape[0] % PACK_UNIT
    return jnp.pad(flat, (0, pad)) if pad else flat


def _pack(arrays):
    return jnp.concatenate([_padded(a) for a in arrays]).reshape(-1, 128)


def _unpack(pack, shapes):
    flat = pack.reshape(-1)
    out, off = [], 0
    for shape in shapes:
        size = _size(shape)
        out.append(flat[off:off + size].reshape(shape))
        off += size + (-size % PACK_UNIT)
    return out


def _gathered_weights(wb, wo, conv):
    wb = wb.reshape(NDEV, 4, BW, D // NDEV).transpose(1, 2, 0, 3).reshape(4, BW, D)
    wo = wo.reshape(D, D)
    cw = CONV_K * (BW // NDEV)
    sw = SHORT_K * (BW // NDEV)
    sc_at = cw + (-cw % PACK_UNIT)
    flat = conv.reshape(NDEV, -1)
    conv_w = flat[:, :cw].reshape(NDEV, CONV_K, BW // NDEV).transpose(1, 0, 2).reshape(CONV_K, BW)
    sc_w = flat[:, sc_at:sc_at + sw].reshape(NDEV, SHORT_K, BW // NDEV).transpose(1, 0, 2).reshape(SHORT_K, BW)
    return wb, wo, conv_w, sc_w


def _layer_small_inputs(l, rep, conv_w, sc_w):
    vecs = jnp.stack([rep["pool_scale"][l], rep["conv_b"][l], rep["conv_ln_g"][l], rep["conv_ln_b"][l],
                      rep["sgu_ln_g"][l], rep["sgu_ln_b"][l], jnp.zeros((BW,), F32), jnp.zeros((BW,), F32)])
    conv_w = jnp.pad(conv_w, ((0, 32 - CONV_K), (0, 0)))
    sc_w = jnp.pad(sc_w, ((0, 8 - SHORT_K), (0, 0)))
    return (rep["pool_w"][l], vecs, conv_w, rep["sgu_w"][l], rep["sgu_b"][l].reshape(NG, GW, 1), sc_w)


def kernel(x, norm_g, w_in, pool_w, pool_scale, conv_w, conv_b, conv_ln_g, conv_ln_b, sgu_ln_g, sgu_ln_b, sgu_w, sgu_b, sc_w, w_branch, w_o, final_g, loss_target, m_norm_g, m_w_in, m_pool_w, m_pool_scale, m_conv_w, m_conv_b, m_conv_ln_g, m_conv_ln_b, m_sgu_ln_g, m_sgu_ln_b, m_sgu_w, m_sgu_b, m_sc_w, m_w_branch, m_w_o, m_final_g, v_norm_g, v_w_in, v_pool_w, v_pool_scale, v_conv_w, v_conv_b, v_conv_ln_g, v_conv_ln_b, v_sgu_ln_g, v_sgu_ln_b, v_sgu_w, v_sgu_b, v_sc_w, v_w_branch, v_w_o, v_final_g):
    args = dict(locals())
    w = {k: args[k] for k in SMALL + ("w_in", "w_branch", "w_o", "final_g")}
    mom = {k: args["m_" + k] for k in w}
    var = {k: args["v_" + k] for k in w}
    rep = {k: w[k] for k in REPLICATED}
    xi, yi, ci = _position()
    me = 4 * xi + 2 * yi + ci

    nb, seq, _ = x.shape
    n = nb * seq
    tm = min(512, n)
    tm_in = min(1024, n)
    tm_fwd = min(2048, n)
    tk = min(2048, n)
    t_fwd = min(256, seq)
    t_bwd = min(256, seq)

    w_in_b = [(w_in[l, :D // 2].astype(BF16), w_in[l, D // 2:].astype(BF16)) for l in range(DEPTH)]
    wb_b = [w_branch[l].astype(BF16).reshape(4 * BW, D // NDEV) for l in range(DEPTH)]
    wo_b = [w_o[l].astype(BF16) for l in range(DEPTH)]
    conv_b_ = [_pack([conv_w[l], sc_w[l]]) for l in range(DEPTH)]

    x2d = x.reshape(n, D)
    saved = []
    ride = _Exchange(gathers=[wb_b[0], wo_b[0], conv_b_[0], w_in_b[1][0]])
    order = jnp.bitwise_xor(me, jnp.array([0, 1, 4, 5, 2, 3, 6, 7], jnp.int32)).astype(jnp.int32)
    (proj, h), w_in_g0, (wb_g, wo_g, conv_g, w_in_upper) = _in_proj_first(
        x2d, norm_g[0:1], w_in_b[0], order, tm_in, ride)
    w_in_g = [w_in_g0, None]
    for l in range(DEPTH):
        if l == 1:
            ride = _Exchange(gathers=[wb_b[1], wo_b[1], conv_b_[1]])
            (proj, h), (wb_g, wo_g, conv_g) = _in_proj(x2d, norm_g[l:l + 1], w_in_g[l], tm_fwd, ride)
        wb_l, wo_l, conv_l, sc_l = _gathered_weights(wb_g, wo_g, conv_g)
        small = _layer_small_inputs(l, rep, conv_l, sc_l)
        ride = _Exchange(gathers=[w_in_b[1][1]]) if l == 0 else None
        (z, y_conv, q_sc, pooled, x_new, merged), gathered = _mixers_fwd(
            proj, x2d, wb_l, wo_l, *small, nb, seq, t_fwd, ride)
        if l == 0:
            w_in_g[1] = [w_in_upper, gathered[0]]
        saved.append((x2d, proj, h, z, y_conv, q_sc, pooled, merged, small, wb_l, wo_l))
        x2d = x_new

    dx, loss_acc, d_final_g = _loss_head(x2d, loss_target.reshape(n, D), final_g.reshape(1, D), tm)

    grads = {k: [None] * DEPTH for k in SMALL}
    big = {"w_in": None, "w_branch": None, "w_o": None}
    big_w = {"w_in": (w_in, m_w_in, v_w_in, (DEPTH, D, SHARD), 256),
             "w_branch": (w_branch, m_w_branch, v_w_branch, (DEPTH, 4 * BW, D // NDEV), 512),
             "w_o": (w_o, m_w_o, v_w_o, (DEPTH, D // NDEV, D), 64)}

    def adamw(k, own, landed, l):
        wk, mk, vk, shape, tr = big_w[k]
        big[k] = _adamw_layer(own, landed, wk.reshape(shape), mk.reshape(shape), vk.reshape(shape), l, big[k], tr,
                              "adamw_" + k)

    for l in reversed(range(DEPTH)):
        x_in, proj, h, z, y_conv, q_sc, pooled, merged, small, wb_l, wo_l = saved[l]
        d_bo, d_proj, d_wo, d_wo_b, d_wb, d_wb_b = _merge_out_bwd(dx, z, proj, merged, wb_l, wo_l, tm)
        ride = _Exchange(scatters=[(d_wo_b.reshape(NDEV, D // NDEV, D), d_wo.reshape(NDEV, D // NDEV, D)),
                                   (d_wb_b.reshape(NDEV, 4 * BW, D // NDEV), d_wb.reshape(NDEV, 4 * BW, D // NDEV))])
        (d_proj, dvec, dconvw, dscw, dpoolw, dsguw, dsgub), (wo_landed, wo_own, wb_landed, wb_own) = _mixers_bwd(
            proj, d_bo, wb_l, y_conv, q_sc, pooled, d_proj, *small, nb, seq, t_bwd, ride)
        grads["pool_w"][l] = dpoolw
        grads["pool_scale"][l] = dvec[0]
        grads["conv_w"][l] = dconvw[:CONV_K]
        grads["conv_b"][l] = dvec[1]
        grads["conv_ln_g"][l] = dvec[2]
        grads["conv_ln_b"][l] = dvec[3]
        grads["sgu_ln_g"][l] = dvec[4]
        grads["sgu_ln_b"][l] = dvec[5]
        grads["sgu_w"][l] = dsguw
        grads["sgu_b"][l] = dsgub[:NG]
        grads["sc_w"][l] = dscw[:SHORT_K]
        if l == 0:
            grads["norm_g"][0] = jnp.zeros((D,), F32)
            part_pack = _pack([jnp.stack(grads[k]) for k in SMALL] + [d_final_g[0], loss_acc[0, 0:1]])
            ride = _Exchange(gathers=[part_pack])
        else:
            ride = None
        (d_w_in, d_w_in_b), gathered = _grad_w_in(h, d_proj, tk, ride)
        if l == 0:
            parts = gathered[0]
        ride = _Exchange(scatters=[(d_w_in_b, d_w_in)])
        (dx, d_norm_g), (in_landed, in_own) = _in_proj_bwd(d_proj, w_in_g[l], x_in, norm_g[l:l + 1], dx, tm_in, ride)
        adamw("w_o", wo_own, wo_landed, l)
        adamw("w_branch", wb_own, wb_landed, l)
        adamw("w_in", in_own, in_landed, l)
        if l == 1:
            grads["norm_g"][1] = d_norm_g[0]
    grad_x = dx.reshape(nb, seq, D)

    names = list(SMALL) + ["final_g", "loss"]
    shapes = [(DEPTH,) + SMALL_SHAPES[k] for k in SMALL] + [(D,), (1,)]
    whole = REPLICATED + ("final_g",)
    w_pack = _pack([w[k] if k in whole else jnp.zeros(s, F32) for k, s in zip(names, shapes)])
    m_pack = _pack([mom[k] if k in whole else jnp.zeros(s, F32) for k, s in zip(names, shapes)])
    v_pack = _pack([var[k] if k in whole else jnp.ones(s, F32) for k, s in zip(names, shapes)])
    rows = parts.shape[1]
    tr = max(r for r in range(8, 641, 8) if rows % r == 0)
    packs = _adamw_gathered(parts, w_pack, m_pack, v_pack, tr, "adamw_small")
    out_g, out_d, out_m, out_v = [dict(zip(names, _unpack(p, shapes))) for p in packs]
    loss = out_g["loss"][0]

    (tail,) = _all_gather([d_norm_g.reshape(8, 128)], "norm_g_all_gather")
    first = [a[0].reshape(8, 128) for a in (w["norm_g"], mom["norm_g"], var["norm_g"])]
    tails = _adamw_gathered(tail, *first, 8, "adamw_norm_g")
    for dst, t0 in zip((out_g, out_d, out_m, out_v), tails):
        dst["norm_g"] = jnp.concatenate([t0.reshape(1, D), dst["norm_g"][1:]])

    col = me * (BW // NDEV)
    for k in ("conv_w", "sc_w"):
        out_g[k] = lax.dynamic_slice_in_dim(out_g[k], col, BW // NDEV, axis=2)
    sharded_shapes = [w["conv_w"].shape, w["sc_w"].shape]
    d2, m2, v2 = _adamw_plain(_pack([out_g["conv_w"], out_g["sc_w"]]), _pack([w["conv_w"], w["sc_w"]]),
                              _pack([mom["conv_w"], mom["sc_w"]]), _pack([var["conv_w"], var["sc_w"]]))
    for dst, pk in ((out_d, d2), (out_m, m2), (out_v, v2)):
        dst["conv_w"], dst["sc_w"] = _unpack(pk, sharded_shapes)

    for k in big:
        out_g[k], out_d[k], out_m[k], out_v[k] = [o.reshape(w[k].shape) for o in big[k]]

    order = ("norm_g", "w_in", "pool_w", "pool_scale", "conv_w", "conv_b", "conv_ln_g", "conv_ln_b", "sgu_ln_g",
             "sgu_ln_b", "sgu_w", "sgu_b", "sc_w", "w_branch", "w_o", "final_g")
    return (loss, grad_x, *[out_g[k] for k in order], *[out_d[k] for k in order],
            *[out_m[k] for k in order], *[out_v[k] for k in order])
```

```python
import functools

import jax
import jax.numpy as jnp
from jax import lax
from jax.experimental import pallas as pl
from jax.experimental.pallas import tpu as pltpu

F32 = jnp.float32
BF16 = jnp.bfloat16
SDS = jax.ShapeDtypeStruct
MESH = pl.DeviceIdType.MESH

D = 1024
BW = 512
NG = 4
GW = 128
CONV_K = 31
SHORT_K = 3
POOL_WINDOWS = (2, 4, 8, 16)
N_PIECE_COLS = 12 * BW
IN_COLS = N_PIECE_COLS + 4 * D
NDEV = 8
SHARD = IN_COLS // NDEV
DEPTH = 2
RMS_EPS = 1e-6
LN_EPS = 1e-5
HALO_POOL, HALO_CONV, HALO_SC = 16, 32, 8
CONV_ROWS = 32

ADAM_LR = 0.001
ADAM_B1 = 0.9
ADAM_B2 = 0.999
ADAM_EPS = 1e-08
ADAM_WD = 0.01
ADAM_STEP = 10

VMEM_LIMIT = 56 * 1024 * 1024


def _params(*sem):
    return pltpu.CompilerParams(dimension_semantics=sem, vmem_limit_bytes=VMEM_LIMIT)


def _sigmoid(x):
    return jax.nn.sigmoid(x)


def _silu(x):
    return x * _sigmoid(x)


def _dsilu(x):
    s = _sigmoid(x)
    return s * (1.0 + x * (1.0 - s))


def _colsum(x):
    return jnp.sum(x, axis=0, keepdims=True)


def _dot(a, b):
    return jnp.dot(a, b, preferred_element_type=F32)


def _dot_nt(a, b):
    return lax.dot_general(a, b, (((1,), (1,)), ((), ())), preferred_element_type=F32)


def _dot_tn(a, b):
    return lax.dot_general(a, b, (((0,), (0,)), ((), ())), preferred_element_type=F32)


def _layer_norm_stats(y):
    mu = jnp.mean(y, axis=-1, keepdims=True)
    yc = y - mu
    var = jnp.mean(yc * yc, axis=-1, keepdims=True)
    rstd = lax.rsqrt(var + LN_EPS)
    return yc * rstd, rstd


def _layer_norm_bwd(d_hat, hat, rstd):
    return rstd * (d_hat - jnp.mean(d_hat, axis=-1, keepdims=True)
                   - hat * jnp.mean(d_hat * hat, axis=-1, keepdims=True))


def _position():
    return lax.axis_index("x"), lax.axis_index("y"), lax.axis_index("c")


def _flip(v, bit):
    return 1 - v if bit else v


class _Exchange:
    def __init__(self, gathers=(), scatters=(), chip_gathers=()):
        self.gathers = list(gathers)
        self.scatters = list(scatters)
        self.chip_gathers = list(chip_gathers)
        self.inputs = self.gathers + [a for pair in self.scatters for a in pair] + self.chip_gathers
        self.out_shapes = [SDS((NDEV,) + a.shape, a.dtype) for a in self.gathers]
        for send, keep in self.scatters:
            self.out_shapes += [SDS((NDEV - 1,) + send.shape[1:], send.dtype), SDS(keep.shape[1:], keep.dtype)]
        self.out_shapes += [SDS((NDEV,) + a.shape, a.dtype) for a in self.chip_gathers]
        n = len(self.gathers) + len(self.scatters) + len(self.chip_gathers)
        self.scratch = [pltpu.SemaphoreType.DMA((7 * n,)), pltpu.SemaphoreType.DMA((7 * n,)),
                        pltpu.SemaphoreType.DMA((n,))]

    def _copies(self, ins, outs, sems):
        send_sems, recv_sems, local_sems = sems
        x, y, c = _position()
        me = 4 * x + 2 * y + c
        ng = len(self.gathers)
        local, remote = [], []
        for a in range(ng + len(self.scatters)):
            if a < ng:
                local.append(pltpu.make_async_copy(ins[a], outs[a].at[me], local_sems.at[a]))
            else:
                s = a - ng
                send, keep = ins[ng + 2 * s], ins[ng + 2 * s + 1]
                landed, own = outs[ng + 2 * s], outs[ng + 2 * s + 1]
                local.append(pltpu.make_async_copy(keep.at[me], own, local_sems.at[a]))
            for r in range(1, NDEV):
                peer = (_flip(x, r & 4), _flip(y, r & 2), _flip(c, r & 1))
                if a < ng:
                    src, dst = ins[a], outs[a].at[me]
                else:
                    src, dst = send.at[jnp.bitwise_xor(me, r)], landed.at[r - 1]
                remote.append(pltpu.make_async_remote_copy(
                    src_ref=src, dst_ref=dst, send_sem=send_sems.at[7 * a + r - 1],
                    recv_sem=recv_sems.at[7 * a + r - 1], device_id=peer, device_id_type=MESH))
        return local, remote

    def _chip_copy(self, ins, outs, sems, t):
        send_sems, recv_sems, local_sems = sems
        x, y, c = _position()
        at = len(self.gathers) + 2 * len(self.scatters) + t
        a = len(self.gathers) + len(self.scatters) + t

        def copy(k, block, to, from_input=False):
            px, py, pc = block
            dst = outs[at].at[4 * px + 2 * py + pc]
            return pltpu.make_async_remote_copy(
                src_ref=ins[at] if from_input else dst, dst_ref=dst, send_sem=send_sems.at[7 * a + k],
                recv_sem=recv_sems.at[7 * a + k], device_id=to, device_id_type=MESH)

        return copy, pltpu.make_async_copy(ins[at], outs[at].at[4 * x + 2 * y + c], local_sems.at[a])

    def start(self, ins, outs, sems):
        local, remote = self._copies(ins, outs, sems)
        for cp in remote + local:
            cp.start()
        x, y, c = _position()
        for t in range(len(self.chip_gathers)):
            copy, mine = self._chip_copy(ins, outs, sems, t)
            mine.start()
            copy(0, (x, y, c), (x, y, 1 - c), from_input=True).start()
            for j, chip in enumerate([(1 - x, y), (x, 1 - y), (1 - x, 1 - y)]):
                copy(1 + j, (x, y, c), (*chip, c), from_input=True).start()

    def finish(self, ins, outs, sems):
        x, y, c = _position()
        me, sibling = (x, y, c), (x, y, 1 - c)
        chips = [(1 - x, y), (x, 1 - y), (1 - x, 1 - y)]
        for t in range(len(self.chip_gathers)):
            copy, _ = self._chip_copy(ins, outs, sems, t)
            for j, chip in enumerate(chips):
                copy(1 + j, (*chip, c), me).wait_recv()
                copy(4 + j, (*chip, c), sibling).start()
        local, remote = self._copies(ins, outs, sems)
        for cp in remote + local:
            cp.wait()
        for t in range(len(self.chip_gathers)):
            copy, mine = self._chip_copy(ins, outs, sems, t)
            copy(0, sibling, me).wait_recv()
            copy(0, me, sibling, from_input=True).wait_send()
            for j, chip in enumerate(chips):
                copy(4 + j, (*chip, 1 - c), me).wait_recv()
                copy(1 + j, me, (*chip, c), from_input=True).wait_send()
                copy(4 + j, (*chip, c), sibling).wait_send()
            mine.wait()


def _carried_call(body, comm, args, *, name, grid, in_specs, out_specs, out_shape, scratch_shapes=(),
                  semantics, input_output_aliases=None):
    aliases = input_output_aliases or {}
    if comm is None:
        out = pl.pallas_call(body, name=name, grid=grid, in_specs=in_specs, out_specs=out_specs, out_shape=out_shape,
                             scratch_shapes=list(scratch_shapes), input_output_aliases=aliases,
                             compiler_params=_params(*semantics))(*args)
        return list(out), []
    n_in, n_out, n_scr = len(in_specs), len(out_specs), len(scratch_shapes)
    c_in, c_out = len(comm.inputs), len(comm.out_shapes)

    def full_body(*refs):
        refs = list(refs)
        ins, refs = refs[:n_in], refs[n_in:]
        cins, refs = refs[:c_in], refs[c_in:]
        outs, refs = refs[:n_out], refs[n_out:]
        couts, refs = refs[:c_out], refs[c_out:]
        scr, sems = refs[:n_scr], refs[n_scr:]
        ids = [pl.program_id(d) for d in range(len(grid))]
        first = functools.reduce(jnp.logical_and, [i == 0 for i in ids])
        last = functools.reduce(jnp.logical_and, [i == g - 1 for i, g in zip(ids, grid)])

        @pl.when(first)
        def _():
            comm.start(cins, couts, sems)

        body(*ins, *outs, *scr)

        @pl.when(last)
        def _():
            comm.finish(cins, couts, sems)

    any_spec = pl.BlockSpec(memory_space=pl.ANY)
    out = pl.pallas_call(
        full_body, name=name, grid=grid,
        in_specs=list(in_specs) + [any_spec] * c_in, out_specs=list(out_specs) + [any_spec] * c_out,
        out_shape=list(out_shape) + comm.out_shapes, scratch_shapes=list(scratch_shapes) + comm.scratch,
        input_output_aliases=aliases, compiler_params=_params(*["arbitrary"] * len(grid)),
    )(*args, *comm.inputs)
    return list(out[:n_out]), list(out[n_out:])


def _in_proj(x2d, g_row, w_g, tm, comm=None):
    n = x2d.shape[0]
    half = D // 2

    def body(x_ref, g_ref, wu_ref, wl_ref, proj_ref, h_ref):
        @pl.when(pl.program_id(1) == 0)
        def _():
            x = x_ref[...]
            r = lax.rsqrt(jnp.mean(x * x, axis=-1, keepdims=True) + RMS_EPS)
            h_ref[...] = (x * r * g_ref[...]).astype(BF16)

        proj_ref[...] = (_dot(h_ref[:, :half], wu_ref[...]) + _dot(h_ref[:, half:], wl_ref[...])).astype(BF16)

    return _carried_call(
        body, comm, (x2d, g_row, *w_g), name="in_proj", grid=(n // tm, NDEV),
        in_specs=[pl.BlockSpec((tm, D), lambda i, j: (i, 0)),
                  pl.BlockSpec((1, D), lambda i, j: (0, 0)),
                  pl.BlockSpec((None, half, SHARD), lambda i, j: (j, 0, 0)),
                  pl.BlockSpec((None, half, SHARD), lambda i, j: (j, 0, 0))],
        out_specs=[pl.BlockSpec((tm, SHARD), lambda i, j: (i, j)),
                   pl.BlockSpec((tm, D), lambda i, j: (i, 0))],
        out_shape=[SDS((n, IN_COLS), BF16), SDS((n, D), BF16)],
        semantics=("parallel", "arbitrary"))


def _in_proj_first(x2d, g_row, w_own, order, tm, comm):
    n = x2d.shape[0]
    ni = n // tm
    half = D // 2
    c_in, c_out = len(comm.inputs), len(comm.out_shapes)

    def body(order_ref, x_ref, g_ref, wu_hbm, wl_hbm, *refs):
        refs = list(refs)
        cins, refs = refs[:c_in], refs[c_in:]
        proj_ref, h_hbm, gu_hbm, gl_hbm = refs[:4]
        couts, refs = refs[4:4 + c_out], refs[4 + c_out:]
        h_all, wbuf, send_sems, recv_sems, own_sems, load_sems, h_sems = refs[:7]
        csems = refs[7:]
        jp = pl.program_id(0)
        i = pl.program_id(1)
        x, y, c = _position()
        me = 4 * x + 2 * y + c
        sibling = (x, y, 1 - c)
        chips = [(1 - x, y), (x, 1 - y), (1 - x, 1 - y)]
        own = (wu_hbm, wl_hbm)
        gathered = (gu_hbm, gl_hbm)

        def copy(a, k, block, to, from_input=False):
            px, py, pc = block
            dst = gathered[a].at[4 * px + 2 * py + pc]
            return pltpu.make_async_remote_copy(
                src_ref=own[a] if from_input else dst, dst_ref=dst,
                send_sem=send_sems.at[7 * a + k], recv_sem=recv_sems.at[7 * a + k],
                device_id=to, device_id_type=MESH)

        def load(a, slot, block_id):
            return pltpu.make_async_copy(gathered[a].at[block_id], wbuf.at[slot, a], load_sems.at[2 * slot + a])

        def keep(a):
            return pltpu.make_async_copy(own[a], gathered[a].at[me], own_sems.at[a])

        def h_store(t):
            rows = pl.ds(t * tm, tm)
            return pltpu.make_async_copy(h_all.at[rows], h_hbm.at[rows], h_sems.at[t])

        first = jnp.logical_and(jp == 0, i == 0)
        last = jnp.logical_and(jp == NDEV - 1, i == ni - 1)

        @pl.when(jnp.logical_and(jp == NDEV - 3, i == 0))
        def _():
            comm.start(cins, couts, csems)

        @pl.when(jnp.logical_and(jp == 2, i == 0))
        def _():
            for a in range(2):
                copy(a, 3, (x, y, c), (*chips[2], c), from_input=True).start()

        @pl.when(first)
        def _():
            for a in range(2):
                keep(a).start()
                copy(a, 0, (x, y, c), sibling, from_input=True).start()
                for j, chip in enumerate(chips[:2]):
                    copy(a, 1 + j, (x, y, c), (*chip, c), from_input=True).start()
                own_block = pltpu.make_async_copy(own[a], wbuf.at[0, a], load_sems.at[a])
                own_block.start()
                own_block.wait()

        for nxt in range(1, NDEV):
            @pl.when(jnp.logical_and(jp == nxt - 1, i == ni - 1))
            def _(nxt=nxt):
                j, passed = (nxt - 2) // 2, nxt % 2 == 1
                for a in range(2):
                    if nxt == 1:
                        copy(a, 0, sibling, (x, y, c)).wait_recv()
                    elif passed:
                        copy(a, 4 + j, (*chips[j], 1 - c), (x, y, c)).wait_recv()
                    else:
                        copy(a, 1 + j, (*chips[j], c), (x, y, c)).wait_recv()
                        copy(a, 4 + j, (*chips[j], c), sibling).start()
                    load(a, nxt % 2, order_ref[nxt]).start()

        @pl.when(jnp.logical_and(jp > 0, i == 0))
        def _():
            for a in range(2):
                load(a, jp % 2, order_ref[jp]).wait()

        rows = pl.ds(pl.multiple_of(i * tm, tm), tm)

        @pl.when(jp == 0)
        def _():
            xt = x_ref[...]
            r = lax.rsqrt(jnp.mean(xt * xt, axis=-1, keepdims=True) + RMS_EPS)
            h_all[rows, :] = (xt * r * g_ref[...]).astype(BF16)
            for t in range(ni):
                @pl.when(i == t)
                def _(t=t):
                    h_store(t).start()

        slot = jp % 2
        proj_ref[...] = (_dot(h_all[rows, :half], wbuf[slot, 0]) + _dot(h_all[rows, half:], wbuf[slot, 1])).astype(BF16)

        @pl.when(last)
        def _():
            for a in range(2):
                keep(a).wait()
                copy(a, 0, (x, y, c), sibling, from_input=True).wait_send()
                for j, chip in enumerate(chips):
                    copy(a, 1 + j, (x, y, c), (*chip, c), from_input=True).wait_send()
                    copy(a, 4 + j, (*chip, c), sibling).wait_send()
            for t in range(ni):
                h_store(t).wait()
            comm.finish(cins, couts, csems)

    any_spec = pl.BlockSpec(memory_space=pl.ANY)
    blocks = SDS((NDEV, half, SHARD), BF16)
    out = pl.pallas_call(
        body, name="in_proj_first",
        grid_spec=pltpu.PrefetchScalarGridSpec(
            num_scalar_prefetch=1, grid=(NDEV, ni),
            in_specs=[pl.BlockSpec((tm, D), lambda jp, i, order: (jnp.where(jp == 0, i, ni - 1), 0)),
                      pl.BlockSpec((1, D), lambda jp, i, order: (0, 0)),
                      any_spec, any_spec] + [any_spec] * c_in,
            out_specs=[pl.BlockSpec((tm, SHARD), lambda jp, i, order: (i, order[jp])),
                       any_spec, any_spec, any_spec] + [any_spec] * c_out,
            scratch_shapes=[pltpu.VMEM((n, D), BF16), pltpu.VMEM((2, 2, half, SHARD), BF16),
                            pltpu.SemaphoreType.DMA((14,)), pltpu.SemaphoreType.DMA((14,)),
                            pltpu.SemaphoreType.DMA((2,)), pltpu.SemaphoreType.DMA((4,)),
                            pltpu.SemaphoreType.DMA((ni,))] + comm.scratch),
        out_shape=[SDS((n, IN_COLS), BF16), SDS((n, D), BF16), blocks, blocks] + comm.out_shapes,
        compiler_params=_params("arbitrary", "arbitrary"),
    )(order, x2d, g_row, *w_own, *comm.inputs)
    return list(out[:2]), list(out[2:4]), list(out[4:])


def _tril_bf16(w):
    r = lax.broadcasted_iota(jnp.int32, (GW, GW), 0)
    c = lax.broadcasted_iota(jnp.int32, (GW, GW), 1)
    return jnp.where(r >= c, w, 0.0).astype(BF16)


def _mixers_fwd(proj, x2d, wb, wo, pool_w, vecs, conv_w, sgu_w, sgu_bc, sc_w, nb, seq, t, comm=None):
    n = nb * seq
    nt = seq // t

    def body(p_ref, x_ref, wb_ref, wo_ref, poolw_ref, vec_ref, convw_ref, sguw_ref, sgub_ref, scw_ref,
             z_ref, y_ref, q_ref, pooled_ref, xo_ref, m_ref, pext, uext, vext):
        i = pl.program_id(1)

        def gated(k):
            gate = _sigmoid(p_ref[:, N_PIECE_COLS + k * D:N_PIECE_COLS + (k + 1) * D].astype(F32))
            return gate * _dot(z_ref[:, k * BW:(k + 1) * BW], wb_ref[k])

        @pl.when(i == 0)
        def _():
            pext[0:HALO_POOL, :] = jnp.zeros((HALO_POOL, BW), F32)
            uext[0:HALO_CONV, :] = jnp.zeros((HALO_CONV, BW), F32)
            vext[0:HALO_SC, :] = jnp.zeros((HALO_SC, BW), F32)

        @pl.when(i > 0)
        def _():
            pext[0:HALO_POOL, :] = pext[t:t + HALO_POOL, :]
            uext[0:HALO_CONV, :] = uext[t:t + HALO_CONV, :]
            vext[0:HALO_SC, :] = vext[t:t + HALO_SC, :]

        def piece(k):
            return p_ref[:, k * BW:(k + 1) * BW].astype(F32)

        pos = lax.broadcasted_iota(jnp.int32, (t, 1), 0) + i * t

        px = piece(0)
        pext[HALO_POOL:, :] = px
        for j, win in enumerate(POOL_WINDOWS):
            cols = slice(j * GW, (j + 1) * GW)
            s = pext[:, cols]
            step = 1
            while step < win:
                s = s + pltpu.roll(s, step, 0)
                step *= 2
            cnt = jnp.minimum(pos + 1, win).astype(F32)
            pooled = (s[HALO_POOL:, :] / cnt - px[:, cols]).astype(BF16)
            pooled_ref[:, cols] = pooled
            mixed = _dot(pooled, poolw_ref[j].astype(BF16))
            z_ref[:, cols] = (mixed * vec_ref[0:1, cols] * _silu(p_ref[:, BW + j * GW:BW + (j + 1) * GW].astype(F32))).astype(BF16)
        merged = gated(0)

        uext[HALO_CONV:, :] = piece(2) * _sigmoid(piece(3))
        ue = uext[...]
        acc = jnp.zeros((t, BW), F32)
        for b in range(8):
            rb = ue if b == 0 else pltpu.roll(ue, b, 0)
            for a in range((CONV_K - b + 7) // 8):
                j = 8 * a + b
                acc = acc + convw_ref[CONV_K - 1 - j:CONV_K - j, :] * rb[HALO_CONV - 8 * a:HALO_CONV - 8 * a + t, :]
        y = acc + vec_ref[1:2, :]
        y_ref[...] = y
        yhat, _ = _layer_norm_stats(y)
        act = _silu(yhat * vec_ref[2:3, :] + vec_ref[3:4, :])
        z_ref[:, BW:2 * BW] = (act * _silu(piece(4))).astype(BF16)
        merged = merged + gated(1)

        vhat, _ = _layer_norm_stats(piece(6))
        vn = (vhat * vec_ref[4:5, :] + vec_ref[5:6, :]).astype(BF16)
        for g in range(NG):
            ws = _tril_bf16(sguw_ref[g])
            cols = slice(g * GW, (g + 1) * GW)
            for c in range(t // GW):
                rows = slice(c * GW, (c + 1) * GW)
                sp = _dot(ws, vn[rows, cols]) + sgub_ref[g]
                gate = _silu(p_ref[rows, 7 * BW + g * GW:7 * BW + (g + 1) * GW].astype(F32))
                z_ref[rows, 2 * BW + g * GW:2 * BW + (g + 1) * GW] = (
                    p_ref[rows, 5 * BW + g * GW:5 * BW + (g + 1) * GW].astype(F32) * sp * gate).astype(BF16)
        merged = merged + gated(2)

        vext[HALO_SC:, :] = piece(9) * piece(10)
        ve = vext[...]
        q = jnp.zeros((t, BW), F32)
        for j in range(SHORT_K):
            sh = ve if j == 0 else pltpu.roll(ve, j, 0)
            q = q + scw_ref[SHORT_K - 1 - j:SHORT_K - j, :] * sh[HALO_SC:, :]
        q_ref[...] = q
        z_ref[:, 3 * BW:] = (piece(8) * q * _silu(piece(11))).astype(BF16)
        mb = (merged + gated(3)).astype(BF16)
        m_ref[...] = mb
        xo_ref[...] = x_ref[...] + _dot(mb, wo_ref[...])

    row = lambda b, i: (b * nt + i, 0)
    full2 = lambda b, i: (0, 0)
    full3 = lambda b, i: (0, 0, 0)
    return _carried_call(
        body, comm, (proj, x2d, wb, wo, pool_w, vecs, conv_w, sgu_w, sgu_bc, sc_w), name="mixers_fwd", grid=(nb, nt),
        in_specs=[pl.BlockSpec((t, IN_COLS), row),
                  pl.BlockSpec((t, D), row),
                  pl.BlockSpec((4, BW, D), full3),
                  pl.BlockSpec((D, D), full2),
                  pl.BlockSpec((NG, GW, GW), full3),
                  pl.BlockSpec((8, BW), full2),
                  pl.BlockSpec((32, BW), full2),
                  pl.BlockSpec((NG, GW, GW), full3),
                  pl.BlockSpec((NG, GW, 1), full3),
                  pl.BlockSpec((8, BW), full2)],
        out_specs=[pl.BlockSpec((t, 4 * BW), row), pl.BlockSpec((t, BW), row),
                   pl.BlockSpec((t, BW), row), pl.BlockSpec((t, BW), row),
                   pl.BlockSpec((t, D), row), pl.BlockSpec((t, D), row)],
        out_shape=[SDS((n, 4 * BW), BF16), SDS((n, BW), F32), SDS((n, BW), F32), SDS((n, BW), BF16),
                   SDS((n, D), F32), SDS((n, D), BF16)],
        scratch_shapes=[pltpu.VMEM((t + HALO_POOL, BW), F32), pltpu.VMEM((t + HALO_CONV, BW), F32),
                        pltpu.VMEM((t + HALO_SC, BW), F32)],
        semantics=("arbitrary", "arbitrary"))


def _loss_head(x2d, tgt2d, g_row, tm):
    n = x2d.shape[0]

    def body(x_ref, t_ref, g_ref, dx_ref, loss_ref, dg_ref):
        @pl.when(pl.program_id(0) == 0)
        def _():
            loss_ref[...] = jnp.zeros_like(loss_ref)
            dg_ref[...] = jnp.zeros_like(dg_ref)

        x = x_ref[...]
        g = g_ref[...]
        r = lax.rsqrt(jnp.mean(x * x, axis=-1, keepdims=True) + RMS_EPS)
        xn = x * r
        e = xn * g - t_ref[...]
        loss_ref[...] += 0.5 * jnp.sum(jnp.mean(e * e, axis=-1, keepdims=True))
        dy = e * (1.0 / D)
        dg_ref[...] += _colsum(dy * xn)
        gy = dy * g
        dx_ref[...] = r * gy - xn * (r * jnp.mean(gy * xn, axis=-1, keepdims=True))

    return pl.pallas_call(
        body, name="loss_head", grid=(n // tm,),
        in_specs=[pl.BlockSpec((tm, D), lambda i: (i, 0)), pl.BlockSpec((tm, D), lambda i: (i, 0)),
                  pl.BlockSpec((1, D), lambda i: (0, 0))],
        out_specs=[pl.BlockSpec((tm, D), lambda i: (i, 0)), pl.BlockSpec((8, 128), lambda i: (0, 0)),
                   pl.BlockSpec((1, D), lambda i: (0, 0))],
        out_shape=[SDS((n, D), F32), SDS((8, 128), F32), SDS((1, D), F32)],
        compiler_params=_params("arbitrary"),
    )(x2d, tgt2d, g_row)


def _merge_out_bwd(dxo, z, proj, merged, wb, wo, tm):
    n = dxo.shape[0]
    ni = n // tm
    sub = D // NDEV

    def body(dx_ref, z_ref, mg_ref, m_ref, wb_ref, wo_ref,
             dbo_ref, dp_ref, gwo_ref, gwo_b_ref, gwb_ref, gwb_b_ref,
             dm_ref, acc_wo, acc_wb, half_wo, half_wb, sems):
        i = pl.program_id(0)
        h = pl.program_id(1)

        @pl.when(jnp.logical_and(i == 0, h == 0))
        def _():
            acc_wo[...] = jnp.zeros_like(acc_wo)
            acc_wb[...] = jnp.zeros_like(acc_wb)

        @pl.when(h == 0)
        def _():
            dxb = dx_ref[...].astype(BF16)
            dm_ref[...] = _dot_nt(dxb, wo_ref[...])
            acc_wo[...] += _dot_tn(m_ref[...], dxb)

        dm = dm_ref[...]
        for k in range(2):
            zk = z_ref[:, k * BW:(k + 1) * BW]
            w = wb_ref[2 * h + k]
            bo = _dot(zk, w)
            sig = _sigmoid(mg_ref[:, k * D:(k + 1) * D].astype(F32))
            dbo = (dm * sig).astype(BF16)
            dp_ref[:, k * D:(k + 1) * D] = (dm * bo * sig * (1.0 - sig)).astype(BF16)
            dbo_ref[:, k * D:(k + 1) * D] = dbo
            acc_wb[2 * h + k] += _dot_tn(zk, dbo)

        @pl.when(jnp.logical_and(i == ni - 1, h == 1))
        def _():
            half_wo[...] = acc_wo[...].astype(BF16)
            half_wb[...] = acc_wb[...].astype(BF16)
            copies = [pltpu.make_async_copy(acc_wo, gwo_ref, sems.at[0]),
                      pltpu.make_async_copy(half_wo, gwo_b_ref, sems.at[1])]
            for d in range(NDEV):
                cols = pl.ds(d * sub, sub)
                copies.append(pltpu.make_async_copy(acc_wb.at[:, :, cols], gwb_ref.at[d], sems.at[2 + 2 * d]))
                copies.append(pltpu.make_async_copy(half_wb.at[:, :, cols], gwb_b_ref.at[d], sems.at[3 + 2 * d]))
            for cp in copies:
                cp.start()
            for cp in copies:
                cp.wait()

    any_spec = pl.BlockSpec(memory_space=pl.ANY)
    return pl.pallas_call(
        body, name="merge_out_bwd", grid=(ni, 2),
        in_specs=[pl.BlockSpec((tm, D), lambda i, h: (i, 0)),
                  pl.BlockSpec((tm, 2 * BW), lambda i, h: (i, h)),
                  pl.BlockSpec((tm, 2 * D), lambda i, h: (i, 3 + h)),
                  pl.BlockSpec((tm, D), lambda i, h: (i, 0)),
                  pl.BlockSpec((4, BW, D), lambda i, h: (0, 0, 0)),
                  pl.BlockSpec((D, D), lambda i, h: (0, 0))],
        out_specs=[pl.BlockSpec((tm, 2 * D), lambda i, h: (i, h)),
                   pl.BlockSpec((tm, 2 * D), lambda i, h: (i, 3 + h)),
                   any_spec, any_spec, any_spec, any_spec],
        out_shape=[SDS((n, 4 * D), BF16), SDS((n, IN_COLS), BF16), SDS((D, D), F32), SDS((D, D), BF16),
                   SDS((NDEV, 4, BW, sub), F32), SDS((NDEV, 4, BW, sub), BF16)],
        scratch_shapes=[pltpu.VMEM((tm, D), F32), pltpu.VMEM((D, D), F32), pltpu.VMEM((4, BW, D), F32),
                        pltpu.VMEM((D, D), BF16), pltpu.VMEM((4, BW, D), BF16),
                        pltpu.SemaphoreType.DMA((2 + 2 * NDEV,))],
        compiler_params=_params("arbitrary", "arbitrary"),
    )(dxo, z, proj, merged, wb, wo)


def _grad_w_in(h, d_proj, tk, comm=None):
    nk = h.shape[0] // tk

    def body(h_ref, dp_ref, o_ref, ob_ref):
        k = pl.program_id(1)

        @pl.when(k == 0)
        def _():
            o_ref[...] = jnp.zeros_like(o_ref)

        o_ref[...] += _dot_tn(h_ref[...], dp_ref[...])

        @pl.when(k == nk - 1)
        def _():
            ob_ref[...] = o_ref[...].astype(BF16)

    out_spec = pl.BlockSpec((None, D, SHARD), lambda j, k: (j, 0, 0))
    return _carried_call(
        body, comm, (h, d_proj), name="grad_w_in", grid=(NDEV, nk),
        in_specs=[pl.BlockSpec((tk, D), lambda j, k: (k, 0)), pl.BlockSpec((tk, SHARD), lambda j, k: (k, j))],
        out_specs=[out_spec, out_spec], out_shape=[SDS((NDEV, D, SHARD), F32), SDS((NDEV, D, SHARD), BF16)],
        semantics=("parallel", "arbitrary"))


def _mixers_bwd(proj, d_bo, wb, y_conv, q_sc, pooled, d_proj, pool_w, vecs, conv_w, sgu_w, sgu_bc, sc_w, nb, seq, t, comm=None):
    n = nb * seq
    nt = seq // t

    def body(p_ref, dbo_ref, wb_ref, y_ref, q_ref, pooled_ref, dpin_ref,
             poolw_ref, vec_ref, convw_ref, sguw_ref, sgub_ref, scw_ref,
             dp_ref, dvec_ref, dconvw_ref, dscw_ref, dpoolw_ref, dsguw_ref, dsgub_ref,
             rext, dyext, dqext, bacc, dyrot, dwacc):
        b = pl.program_id(0)
        i = pl.program_id(1)
        first = jnp.logical_and(b == 0, i == 0)
        last = jnp.logical_and(b == nb - 1, i == nt - 1)

        @pl.when(first)
        def _():
            dvec_ref[...] = jnp.zeros_like(dvec_ref)
            dconvw_ref[...] = jnp.zeros_like(dconvw_ref)
            dscw_ref[...] = jnp.zeros_like(dscw_ref)
            dpoolw_ref[...] = jnp.zeros_like(dpoolw_ref)
            dsguw_ref[...] = jnp.zeros_like(dsguw_ref)
            dsgub_ref[...] = jnp.zeros_like(dsgub_ref)
            bacc[...] = jnp.zeros_like(bacc)
            dwacc[...] = jnp.zeros_like(dwacc)

        @pl.when(i == 0)
        def _():
            rext[t:, :] = jnp.zeros((HALO_POOL, BW), F32)
            dyext[t:, :] = jnp.zeros((HALO_CONV, BW), F32)
            dqext[t:, :] = jnp.zeros((HALO_SC, BW), F32)

        @pl.when(i > 0)
        def _():
            rext[t:, :] = rext[0:HALO_POOL, :]
            dyext[t:, :] = dyext[0:HALO_CONV, :]
            dqext[t:, :] = dqext[0:HALO_SC, :]

        def piece(k):
            return p_ref[:, k * BW:(k + 1) * BW].astype(F32)

        def put(k, v):
            dp_ref[:, k * BW:(k + 1) * BW] = v.astype(BF16)

        def dz(k):
            return _dot_nt(dbo_ref[:, k * D:(k + 1) * D], wb_ref[k])

        pos = lax.broadcasted_iota(jnp.int32, (t, 1), 0) + (nt - 1 - i) * t

        dzp = dz(0)
        pg = piece(1)
        d_pm = dzp * _silu(pg)
        for j, win in enumerate(POOL_WINDOWS):
            cols = slice(j * GW, (j + 1) * GW)
            pw = poolw_ref[j].astype(BF16)
            pooled = pooled_ref[:, cols]
            mixed = _dot(pooled, pw)
            scale = vec_ref[0:1, cols]
            dvec_ref[0:1, cols] += _colsum(d_pm[:, cols] * mixed)
            dp_ref[:, BW + j * GW:BW + (j + 1) * GW] = (dzp[:, cols] * (mixed * scale) * _dsilu(pg[:, cols])).astype(BF16)
            d_mixed = (d_pm[:, cols] * scale).astype(BF16)
            dpoolw_ref[j] += _dot_tn(pooled, d_mixed)
            d_pooled = _dot_nt(d_mixed, pw)
            cnt = jnp.minimum(pos + 1, win).astype(F32)
            rext[0:t, cols] = d_pooled / cnt
            s = rext[:, cols]
            step = 1
            while step < win:
                s = s + pltpu.roll(s, t + HALO_POOL - step, 0)
                step *= 2
            dp_ref[:, cols] = (s[0:t, :] - d_pooled).astype(BF16)

        dzc = dz(1)
        cgate = piece(4)
        yhat, rstd = _layer_norm_stats(y_ref[...])
        ln = yhat * vec_ref[2:3, :] + vec_ref[3:4, :]
        put(4, dzc * _silu(ln) * _dsilu(cgate))
        d_ln = dzc * _silu(cgate) * _dsilu(ln)
        dvec_ref[2:3, :] += _colsum(d_ln * yhat)
        dvec_ref[3:4, :] += _colsum(d_ln)
        dy = _layer_norm_bwd(d_ln * vec_ref[2:3, :], yhat, rstd)
        dvec_ref[1:2, :] += _colsum(dy)
        dyext[0:t, :] = dy
        dye = dyext[...]
        for b in range(1, 8):
            dyrot[b - 1] = pltpu.roll(dye, t + HALO_CONV - b, 0)

        def conv_rows(c, carry):
            r0 = pl.multiple_of(c * CONV_ROWS, CONV_ROWS)
            rows = pl.ds(r0, CONV_ROWS)
            sig_cb = _sigmoid(p_ref[rows, 3 * BW:4 * BW].astype(F32))
            u = p_ref[rows, 2 * BW:3 * BW].astype(F32) * sig_cb
            du = jnp.zeros((CONV_ROWS, BW), F32)
            for b in range(8):
                for a in range((CONV_K - b + 7) // 8):
                    j = 8 * a + b
                    at = pl.ds(pl.multiple_of(r0 + 8 * a, 8), CONV_ROWS)
                    sh = dyext[at, :] if b == 0 else dyrot[b - 1, at, :]
                    du = du + convw_ref[CONV_K - 1 - j:CONV_K - j, :] * sh
                    dwacc[CONV_K - 1 - j] += jnp.sum((u * sh).reshape(CONV_ROWS // 8, 8, BW), axis=0)
            dp_ref[rows, 2 * BW:3 * BW] = (du * sig_cb).astype(BF16)
            dp_ref[rows, 3 * BW:4 * BW] = (du * u * (1.0 - sig_cb)).astype(BF16)
            return carry

        lax.fori_loop(0, t // CONV_ROWS, conv_rows, 0)

        dzg = dz(2)
        gu = piece(5)
        ggate = piece(7)
        vhat, vrstd = _layer_norm_stats(piece(6))
        vn = (vhat * vec_ref[4:5, :] + vec_ref[5:6, :]).astype(BF16)
        t1 = dzg * _silu(ggate)
        d_sp_all = t1 * gu
        d_vn_cols = []
        sp_cols = []
        for g in range(NG):
            ws = _tril_bf16(sguw_ref[g])
            cols = slice(g * GW, (g + 1) * GW)
            d_vn_rows = []
            sp_rows = []
            for c in range(t // GW):
                rows = slice(c * GW, (c + 1) * GW)
                sp_rows.append(_dot(ws, vn[rows, cols]) + sgub_ref[g])
                d_sp = d_sp_all[rows, cols]
                bacc[g] += d_sp
                d_spb = d_sp.astype(BF16)
                d_vn_rows.append(_dot_tn(ws, d_spb))
                dsguw_ref[g] += _dot_nt(d_spb, vn[rows, cols])
            d_vn_cols.append(jnp.concatenate(d_vn_rows, axis=0))
            sp_cols.append(jnp.concatenate(sp_rows, axis=0))
        d_vn = jnp.concatenate(d_vn_cols, axis=1)
        sp = jnp.concatenate(sp_cols, axis=1)
        put(5, t1 * sp)
        put(7, dzg * gu * sp * _dsilu(ggate))
        dvec_ref[4:5, :] += _colsum(d_vn * vhat)
        dvec_ref[5:6, :] += _colsum(d_vn)
        put(6, _layer_norm_bwd(d_vn * vec_ref[4:5, :], vhat, vrstd))

        dzs = dz(3)
        sb = piece(8)
        scp = piece(9)
        sx = piece(10)
        sgate = piece(11)
        q = q_ref[...]
        t2 = dzs * _silu(sgate)
        put(8, t2 * q)
        put(11, dzs * sb * q * _dsilu(sgate))
        dqext[0:t, :] = t2 * sb
        dqe = dqext[...]
        vv = scp * sx
        dv = jnp.zeros((t, BW), F32)
        for j in range(SHORT_K):
            sh = (dqe if j == 0 else pltpu.roll(dqe, t + HALO_SC - j, 0))[0:t, :]
            dv = dv + scw_ref[SHORT_K - 1 - j:SHORT_K - j, :] * sh
            dscw_ref[SHORT_K - 1 - j:SHORT_K - j, :] += _colsum(vv * sh)
        put(9, dv * sx)
        put(10, dv * scp)

        @pl.when(last)
        def _():
            r = lax.broadcasted_iota(jnp.int32, (GW, GW), 0)
            c = lax.broadcasted_iota(jnp.int32, (GW, GW), 1)
            for g in range(NG):
                dsguw_ref[g] = jnp.where(r >= c, dsguw_ref[g], 0.0)
                dsgub_ref[g:g + 1, :] = _colsum(bacc[g].T)
            dconvw_ref[...] = jnp.sum(dwacc[...], axis=1)

    row = lambda b, i: (b * nt + (nt - 1 - i), 0)
    full2 = lambda b, i: (0, 0)
    full3 = lambda b, i: (0, 0, 0)
    return _carried_call(
        body, comm, (proj, d_bo, wb, y_conv, q_sc, pooled, d_proj, pool_w, vecs, conv_w, sgu_w, sgu_bc, sc_w),
        name="mixers_bwd", grid=(nb, nt),
        in_specs=[pl.BlockSpec((t, N_PIECE_COLS), row),
                  pl.BlockSpec((t, 4 * D), row),
                  pl.BlockSpec((4, BW, D), full3),
                  pl.BlockSpec((t, BW), row), pl.BlockSpec((t, BW), row), pl.BlockSpec((t, BW), row),
                  pl.BlockSpec(memory_space=pl.ANY),
                  pl.BlockSpec((NG, GW, GW), full3),
                  pl.BlockSpec((8, BW), full2),
                  pl.BlockSpec((32, BW), full2),
                  pl.BlockSpec((NG, GW, GW), full3),
                  pl.BlockSpec((NG, GW, 1), full3),
                  pl.BlockSpec((8, BW), full2)],
        out_specs=[pl.BlockSpec((t, N_PIECE_COLS), row),
                   pl.BlockSpec((8, BW), full2), pl.BlockSpec((32, BW), full2), pl.BlockSpec((8, BW), full2),
                   pl.BlockSpec((NG, GW, GW), full3), pl.BlockSpec((NG, GW, GW), full3),
                   pl.BlockSpec((8, GW), full2)],
        out_shape=[SDS((n, IN_COLS), BF16), SDS((8, BW), F32), SDS((32, BW), F32), SDS((8, BW), F32),
                   SDS((NG, GW, GW), F32), SDS((NG, GW, GW), F32), SDS((8, GW), F32)],
        scratch_shapes=[pltpu.VMEM((t + HALO_POOL, BW), F32), pltpu.VMEM((t + HALO_CONV, BW), F32),
                        pltpu.VMEM((t + HALO_SC, BW), F32), pltpu.VMEM((NG, GW, GW), F32),
                        pltpu.VMEM((7, t + HALO_CONV, BW), F32), pltpu.VMEM((32, 8, BW), F32)],
        input_output_aliases={6: 0}, semantics=("arbitrary", "arbitrary"))


def _in_proj_bwd(d_proj, w_g, x2d, g_row, dxo, tm, comm=None):
    n = x2d.shape[0]
    half = D // 2

    def body(dp_ref, wu_ref, wl_ref, x_hbm, g_ref, dxo_hbm, dx_ref, dg_ref, acc, x_buf, dxo_buf, sems):
        i = pl.program_id(0)
        k = pl.program_id(1)
        rows = pl.ds(pl.multiple_of(i * tm, tm), tm)
        late = (pltpu.make_async_copy(x_hbm.at[rows], x_buf, sems.at[0]),
                pltpu.make_async_copy(dxo_hbm.at[rows], dxo_buf, sems.at[1]))

        @pl.when(jnp.logical_and(i == 0, k == 0))
        def _():
            dg_ref[...] = jnp.zeros_like(dg_ref)

        @pl.when(k == 0)
        def _():
            for cp in late:
                cp.start()
            acc[...] = jnp.zeros_like(acc)

        dp0 = dp_ref[:, :SHARD]
        dp1 = dp_ref[:, SHARD:]
        acc[:, :half] += _dot_nt(dp0, wu_ref[0]) + _dot_nt(dp1, wu_ref[1])
        acc[:, half:] += _dot_nt(dp0, wl_ref[0]) + _dot_nt(dp1, wl_ref[1])

        @pl.when(k == NDEV // 2 - 1)
        def _():
            for cp in late:
                cp.wait()
            dh = acc[...]
            x = x_buf[...]
            r = lax.rsqrt(jnp.mean(x * x, axis=-1, keepdims=True) + RMS_EPS)
            xn = x * r
            dg_ref[...] += _colsum(dh * xn)
            gy = dh * g_ref[...]
            dx_ref[...] = dxo_buf[...] + r * gy - xn * (r * jnp.mean(gy * xn, axis=-1, keepdims=True))

    any_spec = pl.BlockSpec(memory_space=pl.ANY)
    return _carried_call(
        body, comm, (d_proj, *w_g, x2d, g_row, dxo), name="in_proj_bwd", grid=(n // tm, NDEV // 2),
        in_specs=[pl.BlockSpec((tm, 2 * SHARD), lambda i, k: (i, k)),
                  pl.BlockSpec((2, half, SHARD), lambda i, k: (k, 0, 0)),
                  pl.BlockSpec((2, half, SHARD), lambda i, k: (k, 0, 0)),
                  any_spec,
                  pl.BlockSpec((1, D), lambda i, k: (0, 0)),
                  any_spec],
        out_specs=[pl.BlockSpec((tm, D), lambda i, k: (i, 0)), pl.BlockSpec((1, D), lambda i, k: (0, 0))],
        out_shape=[SDS((n, D), F32), SDS((1, D), F32)],
        scratch_shapes=[pltpu.VMEM((tm, D), F32), pltpu.VMEM((tm, D), F32), pltpu.VMEM((tm, D), F32),
                        pltpu.SemaphoreType.DMA((2,))],
        semantics=("arbitrary", "arbitrary"))


def _all_gather(arrs, name):
    na = len(arrs)

    def body(*refs):
        ins, outs = refs[:na], refs[na:2 * na]
        send_sems, recv_sems, local_sems = refs[2 * na:]
        x, y, c = _position()
        me, sibling = (x, y, c), (x, y, 1 - c)
        chips = [(1 - x, y), (x, 1 - y), (1 - x, 1 - y)]

        def copy(a, k, block, to, from_input=False):
            px, py, pc = block
            dst = outs[a].at[4 * px + 2 * py + pc]
            return pltpu.make_async_remote_copy(
                src_ref=ins[a] if from_input else dst, dst_ref=dst,
                send_sem=send_sems.at[7 * a + k], recv_sem=recv_sems.at[7 * a + k],
                device_id=to, device_id_type=MESH)

        mine = [pltpu.make_async_copy(ins[a], outs[a].at[4 * x + 2 * y + c], local_sems.at[a]) for a in range(na)]
        for cp in mine:
            cp.start()
        first = []
        for a in range(na):
            first.append(copy(a, 0, me, sibling, from_input=True))
            first += [copy(a, 1 + j, me, (*chip, c), from_input=True) for j, chip in enumerate(chips)]
        for cp in first:
            cp.start()
        passed = []
        for j, chip in enumerate(chips):
            for a in range(na):
                copy(a, 1 + j, (*chip, c), me).wait_recv()
                fwd = copy(a, 4 + j, (*chip, c), sibling)
                fwd.start()
                passed.append(fwd)
        for a in range(na):
            copy(a, 0, sibling, me).wait_recv()
            for j, chip in enumerate(chips):
                copy(a, 4 + j, (*chip, 1 - c), me).wait_recv()
        for cp in first + passed:
            cp.wait_send()
        for cp in mine:
            cp.wait()

    any_spec = pl.BlockSpec(memory_space=pl.ANY)
    return pl.pallas_call(
        body, name=name,
        in_specs=[any_spec] * na, out_specs=[any_spec] * na,
        out_shape=[SDS((NDEV,) + a.shape, a.dtype) for a in arrs],
        scratch_shapes=[pltpu.SemaphoreType.DMA((7 * na,)), pltpu.SemaphoreType.DMA((7 * na,)),
                        pltpu.SemaphoreType.DMA((na,))],
    )(*arrs)


def _adamw_math(g, w, m, v):
    m = ADAM_B1 * m + (1.0 - ADAM_B1) * g
    v = ADAM_B2 * v + (1.0 - ADAM_B2) * (g * g)
    m_hat = m / (1.0 - ADAM_B1 ** ADAM_STEP)
    v_hat = v / (1.0 - ADAM_B2 ** ADAM_STEP)
    delta = -ADAM_LR * (m_hat / (jnp.sqrt(v_hat) + ADAM_EPS) + ADAM_WD * w)
    return delta, m, v


def _adamw_layer(own, landed, w, m, v, layer, earlier, tr, name):
    _, rows, cols = w.shape

    def body(o_ref, l_ref, w_ref, m_ref, v_ref, *rest):
        g_out, d_out, m_out, v_out = rest[-4:]
        g = o_ref[...]
        for r in range(NDEV - 1):
            g = g + l_ref[r].astype(F32)
        g_out[...] = g
        d_out[...], m_out[...], v_out[...] = _adamw_math(g, w_ref[...], m_ref[...], v_ref[...])

    lay = pl.BlockSpec((None, tr, cols), lambda i: (layer, i, 0))
    any_spec = pl.BlockSpec(memory_space=pl.ANY)
    prior = list(earlier) if earlier is not None else []
    return pl.pallas_call(
        body, name=name, grid=(rows // tr,),
        in_specs=[pl.BlockSpec((tr, cols), lambda i: (i, 0)),
                  pl.BlockSpec((NDEV - 1, tr, cols), lambda i: (0, i, 0)), lay, lay, lay] + [any_spec] * len(prior),
        out_specs=[lay] * 4, out_shape=[SDS(w.shape, F32)] * 4,
        input_output_aliases={5 + k: k for k in range(len(prior))},
        compiler_params=_params("parallel"),
    )(own, landed, w, m, v, *prior)


def _adamw_gathered(parts, w, m, v, tr, name):
    rows, cols = w.shape

    def body(p_ref, w_ref, m_ref, v_ref, g_out, d_out, m_out, v_out):
        g = p_ref[0]
        for k in range(1, NDEV):
            g = g + p_ref[k]
        g_out[...] = g
        d_out[...], m_out[...], v_out[...] = _adamw_math(g, w_ref[...], m_ref[...], v_ref[...])

    blk = pl.BlockSpec((tr, cols), lambda i: (i, 0))
    return pl.pallas_call(
        body, name=name, grid=(rows // tr,),
        in_specs=[pl.BlockSpec((NDEV, tr, cols), lambda i: (0, i, 0)), blk, blk, blk],
        out_specs=[blk] * 4, out_shape=[SDS((rows, cols), F32)] * 4,
        compiler_params=_params("parallel"),
    )(parts, w, m, v)


def _adamw_plain(g, w, m, v):
    def body(g_ref, w_ref, m_ref, v_ref, d_out, m_out, v_out):
        d_out[...], m_out[...], v_out[...] = _adamw_math(g_ref[...], w_ref[...], m_ref[...], v_ref[...])

    return pl.pallas_call(body, name="adamw_conv_shards", out_shape=[SDS(w.shape, F32)] * 3)(g, w, m, v)


SMALL = ("norm_g", "pool_w", "pool_scale", "conv_w", "conv_b", "conv_ln_g", "conv_ln_b",
         "sgu_ln_g", "sgu_ln_b", "sgu_w", "sgu_b", "sc_w")
SMALL_SHAPES = {"norm_g": (D,), "pool_w": (NG, GW, GW), "pool_scale": (BW,), "conv_w": (CONV_K, BW), "conv_b": (BW,),
                "conv_ln_g": (BW,), "conv_ln_b": (BW,), "sgu_ln_g": (BW,), "sgu_ln_b": (BW,),
                "sgu_w": (NG, GW, GW), "sgu_b": (NG, GW), "sc_w": (SHORT_K, BW)}
REPLICATED = tuple(k for k in SMALL if k not in ("conv_w", "sc_w"))
PACK_UNIT = 8 * 128


def _size(shape):
    out = 1
    for s in shape:
        out *= s
    return out


def _padded(a):
    flat = a.reshape(-1)
    pad = -flat.shape[0] % PACK_UNIT
    return jnp.pad(flat, (0, pad)) if pad else flat


def _pack(arrays):
    return jnp.concatenate([_padded(a) for a in arrays]).reshape(-1, 128)


def _unpack(pack, shapes):
    flat = pack.reshape(-1)
    out, off = [], 0
    for shape in shapes:
        size = _size(shape)
        out.append(flat[off:off + size].reshape(shape))
        off += size + (-size % PACK_UNIT)
    return out


def _gathered_weights(wb, wo, conv):
    wb = wb.reshape(NDEV, 4, BW, D // NDEV).transpose(1, 2, 0, 3).reshape(4, BW, D)
    wo = wo.reshape(D, D)
    cw = CONV_K * (BW // NDEV)
    sw = SHORT_K * (BW // NDEV)
    sc_at = cw + (-cw % PACK_UNIT)
    flat = conv.reshape(NDEV, -1)
    conv_w = flat[:, :cw].reshape(NDEV, CONV_K, BW // NDEV).transpose(1, 0, 2).reshape(CONV_K, BW)
    sc_w = flat[:, sc_at:sc_at + sw].reshape(NDEV, SHORT_K, BW // NDEV).transpose(1, 0, 2).reshape(SHORT_K, BW)
    return wb, wo, conv_w, sc_w


def _layer_small_inputs(l, rep, conv_w, sc_w):
    vecs = jnp.stack([rep["pool_scale"][l], rep["conv_b"][l], rep["conv_ln_g"][l], rep["conv_ln_b"][l],
                      rep["sgu_ln_g"][l], rep["sgu_ln_b"][l], jnp.zeros((BW,), F32), jnp.zeros((BW,), F32)])
    conv_w = jnp.pad(conv_w, ((0, 32 - CONV_K), (0, 0)))
    sc_w = jnp.pad(sc_w, ((0, 8 - SHORT_K), (0, 0)))
    return (rep["pool_w"][l], vecs, conv_w, rep["sgu_w"][l], rep["sgu_b"][l].reshape(NG, GW, 1), sc_w)


def kernel(x, norm_g, w_in, pool_w, pool_scale, conv_w, conv_b, conv_ln_g, conv_ln_b, sgu_ln_g, sgu_ln_b, sgu_w, sgu_b, sc_w, w_branch, w_o, final_g, loss_target, m_norm_g, m_w_in, m_pool_w, m_pool_scale, m_conv_w, m_conv_b, m_conv_ln_g, m_conv_ln_b, m_sgu_ln_g, m_sgu_ln_b, m_sgu_w, m_sgu_b, m_sc_w, m_w_branch, m_w_o, m_final_g, v_norm_g, v_w_in, v_pool_w, v_pool_scale, v_conv_w, v_conv_b, v_conv_ln_g, v_conv_ln_b, v_sgu_ln_g, v_sgu_ln_b, v_sgu_w, v_sgu_b, v_sc_w, v_w_branch, v_w_o, v_final_g):
    args = dict(locals())
    w = {k: args[k] for k in SMALL + ("w_in", "w_branch", "w_o", "final_g")}
    mom = {k: args["m_" + k] for k in w}
    var = {k: args["v_" + k] for k in w}
    rep = {k: w[k] for k in REPLICATED}
    xi, yi, ci = _position()
    me = 4 * xi + 2 * yi + ci

    nb, seq, _ = x.shape
    n = nb * seq
    tm = min(512, n)
    tm_in = min(1024, n)
    tm_fwd = min(2048, n)
    tk = min(2048, n)
    t_fwd = min(256, seq)
    t_bwd = min(256, seq)

    w_in_b = [(w_in[l, :D // 2].astype(BF16), w_in[l, D // 2:].astype(BF16)) for l in range(DEPTH)]
    wb_b = [w_branch[l].astype(BF16).reshape(4 * BW, D // NDEV) for l in range(DEPTH)]
    wo_b = [w_o[l].astype(BF16) for l in range(DEPTH)]
    conv_b_ = [_pack([conv_w[l], sc_w[l]]) for l in range(DEPTH)]

    x2d = x.reshape(n, D)
    saved = []
    ride = _Exchange(gathers=[wb_b[0], wo_b[0], conv_b_[0]])
    order = jnp.bitwise_xor(me, jnp.array([0, 1, 4, 5, 2, 3, 6, 7], jnp.int32)).astype(jnp.int32)
    (proj, h), w_in_g0, (wb_g, wo_g, conv_g) = _in_proj_first(
        x2d, norm_g[0:1], w_in_b[0], order, tm_in, ride)
    w_in_g = [w_in_g0, None]
    for l in range(DEPTH):
        if l == 1:
            ride = _Exchange(gathers=[wb_b[1], wo_b[1], conv_b_[1]])
            (proj, h), (wb_g, wo_g, conv_g) = _in_proj(x2d, norm_g[l:l + 1], w_in_g[l], tm_fwd, ride)
        wb_l, wo_l, conv_l, sc_l = _gathered_weights(wb_g, wo_g, conv_g)
        small = _layer_small_inputs(l, rep, conv_l, sc_l)
        ride = _Exchange(chip_gathers=list(w_in_b[1])) if l == 0 else None
        (z, y_conv, q_sc, pooled, x_new, merged), gathered = _mixers_fwd(
            proj, x2d, wb_l, wo_l, *small, nb, seq, t_fwd, ride)
        if l == 0:
            w_in_g[1] = gathered
        saved.append((x2d, proj, h, z, y_conv, q_sc, pooled, merged, small, wb_l, wo_l))
        x2d = x_new

    dx, loss_acc, d_final_g = _loss_head(x2d, loss_target.reshape(n, D), final_g.reshape(1, D), tm)

    grads = {k: [None] * DEPTH for k in SMALL}
    big = {"w_in": None, "w_branch": None, "w_o": None}
    big_w = {"w_in": (w_in, m_w_in, v_w_in, (DEPTH, D, SHARD), 256),
             "w_branch": (w_branch, m_w_branch, v_w_branch, (DEPTH, 4 * BW, D // NDEV), 512),
             "w_o": (w_o, m_w_o, v_w_o, (DEPTH, D // NDEV, D), 64)}

    def adamw(k, own, landed, l):
        wk, mk, vk, shape, tr = big_w[k]
        big[k] = _adamw_layer(own, landed, wk.reshape(shape), mk.reshape(shape), vk.reshape(shape), l, big[k], tr,
                              "adamw_" + k)

    for l in reversed(range(DEPTH)):
        x_in, proj, h, z, y_conv, q_sc, pooled, merged, small, wb_l, wo_l = saved[l]
        d_bo, d_proj, d_wo, d_wo_b, d_wb, d_wb_b = _merge_out_bwd(dx, z, proj, merged, wb_l, wo_l, tm)
        ride = _Exchange(scatters=[(d_wo_b.reshape(NDEV, D // NDEV, D), d_wo.reshape(NDEV, D // NDEV, D)),
                                   (d_wb_b.reshape(NDEV, 4 * BW, D // NDEV), d_wb.reshape(NDEV, 4 * BW, D // NDEV))])
        (d_proj, dvec, dconvw, dscw, dpoolw, dsguw, dsgub), (wo_landed, wo_own, wb_landed, wb_own) = _mixers_bwd(
            proj, d_bo, wb_l, y_conv, q_sc, pooled, d_proj, *small, nb, seq, t_bwd, ride)
        grads["pool_w"][l] = dpoolw
        grads["pool_scale"][l] = dvec[0]
        grads["conv_w"][l] = dconvw[:CONV_K]
        grads["conv_b"][l] = dvec[1]
        grads["conv_ln_g"][l] = dvec[2]
        grads["conv_ln_b"][l] = dvec[3]
        grads["sgu_ln_g"][l] = dvec[4]
        grads["sgu_ln_b"][l] = dvec[5]
        grads["sgu_w"][l] = dsguw
        grads["sgu_b"][l] = dsgub[:NG]
        grads["sc_w"][l] = dscw[:SHORT_K]
        if l == 0:
            grads["norm_g"][0] = jnp.zeros((D,), F32)
            part_pack = _pack([jnp.stack(grads[k]) for k in SMALL] + [d_final_g[0], loss_acc[0, 0:1]])
            ride = _Exchange(gathers=[part_pack])
        else:
            ride = None
        (d_w_in, d_w_in_b), gathered = _grad_w_in(h, d_proj, tk, ride)
        if l == 0:
            parts = gathered[0]
        ride = _Exchange(scatters=[(d_w_in_b, d_w_in)])
        (dx, d_norm_g), (in_landed, in_own) = _in_proj_bwd(d_proj, w_in_g[l], x_in, norm_g[l:l + 1], dx, tm_in, ride)
        adamw("w_o", wo_own, wo_landed, l)
        adamw("w_branch", wb_own, wb_landed, l)
        adamw("w_in", in_own, in_landed, l)
        if l == 1:
            grads["norm_g"][1] = d_norm_g[0]
    grad_x = dx.reshape(nb, seq, D)

    names = list(SMALL) + ["final_g", "loss"]
    shapes = [(DEPTH,) + SMALL_SHAPES[k] for k in SMALL] + [(D,), (1,)]
    whole = REPLICATED + ("final_g",)
    w_pack = _pack([w[k] if k in whole else jnp.zeros(s, F32) for k, s in zip(names, shapes)])
    m_pack = _pack([mom[k] if k in whole else jnp.zeros(s, F32) for k, s in zip(names, shapes)])
    v_pack = _pack([var[k] if k in whole else jnp.ones(s, F32) for k, s in zip(names, shapes)])
    rows = parts.shape[1]
    tr = max(r for r in range(8, 641, 8) if rows % r == 0)
    packs = _adamw_gathered(parts, w_pack, m_pack, v_pack, tr, "adamw_small")
    out_g, out_d, out_m, out_v = [dict(zip(names, _unpack(p, shapes))) for p in packs]
    loss = out_g["loss"][0]

    (tail,) = _all_gather([d_norm_g.reshape(8, 128)], "norm_g_all_gather")
    first = [a[0].reshape(8, 128) for a in (w["norm_g"], mom["norm_g"], var["norm_g"])]
    tails = _adamw_gathered(tail, *first, 8, "adamw_norm_g")
    for dst, t0 in zip((out_g, out_d, out_m, out_v), tails):
        dst["norm_g"] = jnp.concatenate([t0.reshape(1, D), dst["norm_g"][1:]])

    col = me * (BW // NDEV)
    for k in ("conv_w", "sc_w"):
        out_g[k] = lax.dynamic_slice_in_dim(out_g[k], col, BW // NDEV, axis=2)
    sharded_shapes = [w["conv_w"].shape, w["sc_w"].shape]
    d2, m2, v2 = _adamw_plain(_pack([out_g["conv_w"], out_g["sc_w"]]), _pack([w["conv_w"], w["sc_w"]]),
                              _pack([mom["conv_w"], mom["sc_w"]]), _pack([var["conv_w"], var["sc_w"]]))
    for dst, pk in ((out_d, d2), (out_m, m2), (out_v, v2)):
        dst["conv_w"], dst["sc_w"] = _unpack(pk, sharded_shapes)

    for k in big:
        out_g[k], out_d[k], out_m[k], out_v[k] = [o.reshape(w[k].shape) for o in big[k]]

    order = ("norm_g", "w_in", "pool_w", "pool_scale", "conv_w", "conv_b", "conv_ln_g", "conv_ln_b", "sgu_ln_g",
             "sgu_ln_b", "sgu_w", "sgu_b", "sc_w", "w_branch", "w_o", "final_g")
    return (loss, grad_x, *[out_g[k] for k in order], *[out_d[k] for k in order],
            *[out_m[k] for k in order], *[out_v[k] for k in order])
```

```python
import functools

import jax
import jax.numpy as jnp
from jax import lax
from jax.experimental import pallas as pl
from jax.experimental.pallas import tpu as pltpu

F32 = jnp.float32
BF16 = jnp.bfloat16
SDS = jax.ShapeDtypeStruct
MESH = pl.DeviceIdType.MESH

D = 1024
BW = 512
NG = 4
GW = 128
CONV_K = 31
SHORT_K = 3
POOL_WINDOWS = (2, 4, 8, 16)
N_PIECE_COLS = 12 * BW
IN_COLS = N_PIECE_COLS + 4 * D
NDEV = 8
SHARD = IN_COLS // NDEV
DEPTH = 2
RMS_EPS = 1e-6
LN_EPS = 1e-5
HALO_POOL, HALO_CONV, HALO_SC = 16, 32, 8
CONV_ROWS = 32

ADAM_LR = 0.001
ADAM_B1 = 0.9
ADAM_B2 = 0.999
ADAM_EPS = 1e-08
ADAM_WD = 0.01
ADAM_STEP = 10

VMEM_LIMIT = 56 * 1024 * 1024


def _params(*sem):
    return pltpu.CompilerParams(dimension_semantics=sem, vmem_limit_bytes=VMEM_LIMIT)


def _sigmoid(x):
    return jax.nn.sigmoid(x)


def _silu(x):
    return x * _sigmoid(x)


def _dsilu(x):
    s = _sigmoid(x)
    return s * (1.0 + x * (1.0 - s))


def _colsum(x):
    return jnp.sum(x, axis=0, keepdims=True)


def _dot(a, b):
    return jnp.dot(a, b, preferred_element_type=F32)


def _dot_nt(a, b):
    return lax.dot_general(a, b, (((1,), (1,)), ((), ())), preferred_element_type=F32)


def _dot_tn(a, b):
    return lax.dot_general(a, b, (((0,), (0,)), ((), ())), preferred_element_type=F32)


def _layer_norm_stats(y):
    mu = jnp.mean(y, axis=-1, keepdims=True)
    yc = y - mu
    var = jnp.mean(yc * yc, axis=-1, keepdims=True)
    rstd = lax.rsqrt(var + LN_EPS)
    return yc * rstd, rstd


def _layer_norm_bwd(d_hat, hat, rstd):
    return rstd * (d_hat - jnp.mean(d_hat, axis=-1, keepdims=True)
                   - hat * jnp.mean(d_hat * hat, axis=-1, keepdims=True))


def _position():
    return lax.axis_index("x"), lax.axis_index("y"), lax.axis_index("c")


def _flip(v, bit):
    return 1 - v if bit else v


class _Exchange:
    def __init__(self, gathers=(), scatters=(), chip_gathers=()):
        self.gathers = list(gathers)
        self.scatters = list(scatters)
        self.chip_gathers = list(chip_gathers)
        self.inputs = self.gathers + [a for pair in self.scatters for a in pair] + self.chip_gathers
        self.out_shapes = [SDS((NDEV,) + a.shape, a.dtype) for a in self.gathers]
        for send, keep in self.scatters:
            self.out_shapes += [SDS((NDEV - 1,) + send.shape[1:], send.dtype), SDS(keep.shape[1:], keep.dtype)]
        self.out_shapes += [SDS((NDEV,) + a.shape, a.dtype) for a in self.chip_gathers]
        n = len(self.gathers) + len(self.scatters) + len(self.chip_gathers)
        self.scratch = [pltpu.SemaphoreType.DMA((7 * n,)), pltpu.SemaphoreType.DMA((7 * n,)),
                        pltpu.SemaphoreType.DMA((n,))]

    def _copies(self, ins, outs, sems):
        send_sems, recv_sems, local_sems = sems
        x, y, c = _position()
        me = 4 * x + 2 * y + c
        ng = len(self.gathers)
        local, remote = [], []
        for a in range(ng + len(self.scatters)):
            if a < ng:
                local.append(pltpu.make_async_copy(ins[a], outs[a].at[me], local_sems.at[a]))
            else:
                s = a - ng
                send, keep = ins[ng + 2 * s], ins[ng + 2 * s + 1]
                landed, own = outs[ng + 2 * s], outs[ng + 2 * s + 1]
                local.append(pltpu.make_async_copy(keep.at[me], own, local_sems.at[a]))
            for r in range(1, NDEV):
                peer = (_flip(x, r & 4), _flip(y, r & 2), _flip(c, r & 1))
                if a < ng:
                    src, dst = ins[a], outs[a].at[me]
                else:
                    src, dst = send.at[jnp.bitwise_xor(me, r)], landed.at[r - 1]
                remote.append(pltpu.make_async_remote_copy(
                    src_ref=src, dst_ref=dst, send_sem=send_sems.at[7 * a + r - 1],
                    recv_sem=recv_sems.at[7 * a + r - 1], device_id=peer, device_id_type=MESH))
        return local, remote

    def _chip_copy(self, ins, outs, sems, t):
        send_sems, recv_sems, local_sems = sems
        x, y, c = _position()
        at = len(self.gathers) + 2 * len(self.scatters) + t
        a = len(self.gathers) + len(self.scatters) + t

        def copy(k, block, to, from_input=False):
            px, py, pc = block
            dst = outs[at].at[4 * px + 2 * py + pc]
            return pltpu.make_async_remote_copy(
                src_ref=ins[at] if from_input else dst, dst_ref=dst, send_sem=send_sems.at[7 * a + k],
                recv_sem=recv_sems.at[7 * a + k], device_id=to, device_id_type=MESH)

        return copy, pltpu.make_async_copy(ins[at], outs[at].at[4 * x + 2 * y + c], local_sems.at[a])

    def start(self, ins, outs, sems):
        local, remote = self._copies(ins, outs, sems)
        for cp in remote + local:
            cp.start()
        x, y, c = _position()
        for t in range(len(self.chip_gathers)):
            copy, mine = self._chip_copy(ins, outs, sems, t)
            mine.start()
            copy(0, (x, y, c), (x, y, 1 - c), from_input=True).start()
            for j, chip in enumerate([(1 - x, y), (x, 1 - y), (1 - x, 1 - y)]):
                copy(1 + j, (x, y, c), (*chip, c), from_input=True).start()

    def pass_on(self, ins, outs, sems):
        x, y, c = _position()
        for t in range(len(self.chip_gathers)):
            copy, _ = self._chip_copy(ins, outs, sems, t)
            for j, chip in enumerate([(1 - x, y), (x, 1 - y), (1 - x, 1 - y)]):
                copy(1 + j, (*chip, c), (x, y, c)).wait_recv()
                copy(4 + j, (*chip, c), (x, y, 1 - c)).start()

    def finish(self, ins, outs, sems):
        x, y, c = _position()
        me, sibling = (x, y, c), (x, y, 1 - c)
        chips = [(1 - x, y), (x, 1 - y), (1 - x, 1 - y)]
        local, remote = self._copies(ins, outs, sems)
        for cp in remote + local:
            cp.wait()
        for t in range(len(self.chip_gathers)):
            copy, mine = self._chip_copy(ins, outs, sems, t)
            copy(0, sibling, me).wait_recv()
            copy(0, me, sibling, from_input=True).wait_send()
            for j, chip in enumerate(chips):
                copy(4 + j, (*chip, 1 - c), me).wait_recv()
                copy(1 + j, me, (*chip, c), from_input=True).wait_send()
                copy(4 + j, (*chip, c), sibling).wait_send()
            mine.wait()


def _carried_call(body, comm, args, *, name, grid, in_specs, out_specs, out_shape, scratch_shapes=(),
                  semantics, input_output_aliases=None):
    aliases = input_output_aliases or {}
    if comm is None:
        out = pl.pallas_call(body, name=name, grid=grid, in_specs=in_specs, out_specs=out_specs, out_shape=out_shape,
                             scratch_shapes=list(scratch_shapes), input_output_aliases=aliases,
                             compiler_params=_params(*semantics))(*args)
        return list(out), []
    n_in, n_out, n_scr = len(in_specs), len(out_specs), len(scratch_shapes)
    c_in, c_out = len(comm.inputs), len(comm.out_shapes)

    def full_body(*refs):
        refs = list(refs)
        ins, refs = refs[:n_in], refs[n_in:]
        cins, refs = refs[:c_in], refs[c_in:]
        outs, refs = refs[:n_out], refs[n_out:]
        couts, refs = refs[:c_out], refs[c_out:]
        scr, sems = refs[:n_scr], refs[n_scr:]
        ids = [pl.program_id(d) for d in range(len(grid))]
        first = functools.reduce(jnp.logical_and, [i == 0 for i in ids])
        last = functools.reduce(jnp.logical_and, [i == g - 1 for i, g in zip(ids, grid)])

        @pl.when(first)
        def _():
            comm.start(cins, couts, sems)

        body(*ins, *outs, *scr)

        late = functools.reduce(jnp.logical_and, [ids[0] == grid[0] - 1] + [i == 0 for i in ids[1:]])

        @pl.when(late if len(grid) > 1 else last)
        def _():
            comm.pass_on(cins, couts, sems)

        @pl.when(last)
        def _():
            comm.finish(cins, couts, sems)

    any_spec = pl.BlockSpec(memory_space=pl.ANY)
    out = pl.pallas_call(
        full_body, name=name, grid=grid,
        in_specs=list(in_specs) + [any_spec] * c_in, out_specs=list(out_specs) + [any_spec] * c_out,
        out_shape=list(out_shape) + comm.out_shapes, scratch_shapes=list(scratch_shapes) + comm.scratch,
        input_output_aliases=aliases, compiler_params=_params(*["arbitrary"] * len(grid)),
    )(*args, *comm.inputs)
    return list(out[:n_out]), list(out[n_out:])


def _in_proj(x2d, g_row, w_g, tm, comm=None):
    n = x2d.shape[0]
    half = D // 2

    def body(x_ref, g_ref, wu_ref, wl_ref, proj_ref, h_ref):
        @pl.when(pl.program_id(1) == 0)
        def _():
            x = x_ref[...]
            r = lax.rsqrt(jnp.mean(x * x, axis=-1, keepdims=True) + RMS_EPS)
            h_ref[...] = (x * r * g_ref[...]).astype(BF16)

        proj_ref[...] = (_dot(h_ref[:, :half], wu_ref[...]) + _dot(h_ref[:, half:], wl_ref[...])).astype(BF16)

    return _carried_call(
        body, comm, (x2d, g_row, *w_g), name="in_proj", grid=(n // tm, NDEV),
        in_specs=[pl.BlockSpec((tm, D), lambda i, j: (i, 0)),
                  pl.BlockSpec((1, D), lambda i, j: (0, 0)),
                  pl.BlockSpec((None, half, SHARD), lambda i, j: (j, 0, 0)),
                  pl.BlockSpec((None, half, SHARD), lambda i, j: (j, 0, 0))],
        out_specs=[pl.BlockSpec((tm, SHARD), lambda i, j: (i, j)),
                   pl.BlockSpec((tm, D), lambda i, j: (i, 0))],
        out_shape=[SDS((n, IN_COLS), BF16), SDS((n, D), BF16)],
        semantics=("parallel", "arbitrary"))


def _in_proj_first(x2d, g_row, w_own, order, tm, comm):
    n = x2d.shape[0]
    ni = n // tm
    half = D // 2
    c_in, c_out = len(comm.inputs), len(comm.out_shapes)

    def body(order_ref, x_ref, g_ref, wu_hbm, wl_hbm, *refs):
        refs = list(refs)
        cins, refs = refs[:c_in], refs[c_in:]
        proj_ref, h_hbm, gu_hbm, gl_hbm = refs[:4]
        couts, refs = refs[4:4 + c_out], refs[4 + c_out:]
        h_all, wbuf, send_sems, recv_sems, own_sems, load_sems, h_sems = refs[:7]
        csems = refs[7:]
        jp = pl.program_id(0)
        i = pl.program_id(1)
        x, y, c = _position()
        me = 4 * x + 2 * y + c
        sibling = (x, y, 1 - c)
        chips = [(1 - x, y), (x, 1 - y), (1 - x, 1 - y)]
        own = (wu_hbm, wl_hbm)
        gathered = (gu_hbm, gl_hbm)

        def copy(a, k, block, to, from_input=False):
            px, py, pc = block
            dst = gathered[a].at[4 * px + 2 * py + pc]
            return pltpu.make_async_remote_copy(
                src_ref=own[a] if from_input else dst, dst_ref=dst,
                send_sem=send_sems.at[7 * a + k], recv_sem=recv_sems.at[7 * a + k],
                device_id=to, device_id_type=MESH)

        def load(a, slot, block_id):
            return pltpu.make_async_copy(gathered[a].at[block_id], wbuf.at[slot, a], load_sems.at[2 * slot + a])

        def keep(a):
            return pltpu.make_async_copy(own[a], gathered[a].at[me], own_sems.at[a])

        def h_store(t):
            rows = pl.ds(t * tm, tm)
            return pltpu.make_async_copy(h_all.at[rows], h_hbm.at[rows], h_sems.at[t])

        first = jnp.logical_and(jp == 0, i == 0)
        last = jnp.logical_and(jp == NDEV - 1, i == ni - 1)

        @pl.when(jnp.logical_and(jp == NDEV - 3, i == 0))
        def _():
            comm.start(cins, couts, csems)

        @pl.when(jnp.logical_and(jp == 2, i == 0))
        def _():
            for a in range(2):
                copy(a, 3, (x, y, c), (*chips[2], c), from_input=True).start()

        @pl.when(first)
        def _():
            for a in range(2):
                keep(a).start()
                copy(a, 0, (x, y, c), sibling, from_input=True).start()
                for j, chip in enumerate(chips[:2]):
                    copy(a, 1 + j, (x, y, c), (*chip, c), from_input=True).start()
                own_block = pltpu.make_async_copy(own[a], wbuf.at[0, a], load_sems.at[a])
                own_block.start()
                own_block.wait()

        for nxt in range(1, NDEV):
            @pl.when(jnp.logical_and(jp == nxt - 1, i == ni - 1))
            def _(nxt=nxt):
                j, passed = (nxt - 2) // 2, nxt % 2 == 1
                for a in range(2):
                    if nxt == 1:
                        copy(a, 0, sibling, (x, y, c)).wait_recv()
                    elif passed:
                        copy(a, 4 + j, (*chips[j], 1 - c), (x, y, c)).wait_recv()
                    else:
                        copy(a, 1 + j, (*chips[j], c), (x, y, c)).wait_recv()
                        copy(a, 4 + j, (*chips[j], c), sibling).start()
                    load(a, nxt % 2, order_ref[nxt]).start()

        @pl.when(jnp.logical_and(jp > 0, i == 0))
        def _():
            for a in range(2):
                load(a, jp % 2, order_ref[jp]).wait()

        rows = pl.ds(pl.multiple_of(i * tm, tm), tm)

        @pl.when(jp == 0)
        def _():
            xt = x_ref[...]
            r = lax.rsqrt(jnp.mean(xt * xt, axis=-1, keepdims=True) + RMS_EPS)
            h_all[rows, :] = (xt * r * g_ref[...]).astype(BF16)
            for t in range(ni):
                @pl.when(i == t)
                def _(t=t):
                    h_store(t).start()

        slot = jp % 2
        proj_ref[...] = (_dot(h_all[rows, :half], wbuf[slot, 0]) + _dot(h_all[rows, half:], wbuf[slot, 1])).astype(BF16)

        @pl.when(last)
        def _():
            for a in range(2):
                keep(a).wait()
                copy(a, 0, (x, y, c), sibling, from_input=True).wait_send()
                for j, chip in enumerate(chips):
                    copy(a, 1 + j, (x, y, c), (*chip, c), from_input=True).wait_send()
                    copy(a, 4 + j, (*chip, c), sibling).wait_send()
            for t in range(ni):
                h_store(t).wait()
            comm.finish(cins, couts, csems)

    any_spec = pl.BlockSpec(memory_space=pl.ANY)
    blocks = SDS((NDEV, half, SHARD), BF16)
    out = pl.pallas_call(
        body, name="in_proj_first",
        grid_spec=pltpu.PrefetchScalarGridSpec(
            num_scalar_prefetch=1, grid=(NDEV, ni),
            in_specs=[pl.BlockSpec((tm, D), lambda jp, i, order: (jnp.where(jp == 0, i, ni - 1), 0)),
                      pl.BlockSpec((1, D), lambda jp, i, order: (0, 0)),
                      any_spec, any_spec] + [any_spec] * c_in,
            out_specs=[pl.BlockSpec((tm, SHARD), lambda jp, i, order: (i, order[jp])),
                       any_spec, any_spec, any_spec] + [any_spec] * c_out,
            scratch_shapes=[pltpu.VMEM((n, D), BF16), pltpu.VMEM((2, 2, half, SHARD), BF16),
                            pltpu.SemaphoreType.DMA((14,)), pltpu.SemaphoreType.DMA((14,)),
                            pltpu.SemaphoreType.DMA((2,)), pltpu.SemaphoreType.DMA((4,)),
                            pltpu.SemaphoreType.DMA((ni,))] + comm.scratch),
        out_shape=[SDS((n, IN_COLS), BF16), SDS((n, D), BF16), blocks, blocks] + comm.out_shapes,
        compiler_params=_params("arbitrary", "arbitrary"),
    )(order, x2d, g_row, *w_own, *comm.inputs)
    return list(out[:2]), list(out[2:4]), list(out[4:])


def _tril_bf16(w):
    r = lax.broadcasted_iota(jnp.int32, (GW, GW), 0)
    c = lax.broadcasted_iota(jnp.int32, (GW, GW), 1)
    return jnp.where(r >= c, w, 0.0).astype(BF16)


def _mixers_fwd(proj, x2d, wb, wo, pool_w, vecs, conv_w, sgu_w, sgu_bc, sc_w, nb, seq, t, comm=None, head=None):
    n = nb * seq
    nt = seq // t

    def body(p_ref, x_ref, wb_ref, wo_ref, poolw_ref, vec_ref, convw_ref, sguw_ref, sgub_ref, scw_ref, *refs):
        if head is None:
            z_ref, y_ref, q_ref, pooled_ref, m_ref, xo_ref, pext, uext, vext = refs
        else:
            t_ref, fg_ref, z_ref, y_ref, q_ref, pooled_ref, m_ref, dx_ref, loss_ref, dg_ref, pext, uext, vext = refs
        i = pl.program_id(1)

        def gated(k):
            gate = _sigmoid(p_ref[:, N_PIECE_COLS + k * D:N_PIECE_COLS + (k + 1) * D].astype(F32))
            return gate * _dot(z_ref[:, k * BW:(k + 1) * BW], wb_ref[k])

        @pl.when(i == 0)
        def _():
            pext[0:HALO_POOL, :] = jnp.zeros((HALO_POOL, BW), F32)
            uext[0:HALO_CONV, :] = jnp.zeros((HALO_CONV, BW), F32)
            vext[0:HALO_SC, :] = jnp.zeros((HALO_SC, BW), F32)

        @pl.when(i > 0)
        def _():
            pext[0:HALO_POOL, :] = pext[t:t + HALO_POOL, :]
            uext[0:HALO_CONV, :] = uext[t:t + HALO_CONV, :]
            vext[0:HALO_SC, :] = vext[t:t + HALO_SC, :]

        def piece(k):
            return p_ref[:, k * BW:(k + 1) * BW].astype(F32)

        pos = lax.broadcasted_iota(jnp.int32, (t, 1), 0) + i * t

        px = piece(0)
        pext[HALO_POOL:, :] = px
        for j, win in enumerate(POOL_WINDOWS):
            cols = slice(j * GW, (j + 1) * GW)
            s = pext[:, cols]
            step = 1
            while step < win:
                s = s + pltpu.roll(s, step, 0)
                step *= 2
            cnt = jnp.minimum(pos + 1, win).astype(F32)
            pooled = (s[HALO_POOL:, :] / cnt - px[:, cols]).astype(BF16)
            pooled_ref[:, cols] = pooled
            mixed = _dot(pooled, poolw_ref[j].astype(BF16))
            z_ref[:, cols] = (mixed * vec_ref[0:1, cols] * _silu(p_ref[:, BW + j * GW:BW + (j + 1) * GW].astype(F32))).astype(BF16)
        merged = gated(0)

        uext[HALO_CONV:, :] = piece(2) * _sigmoid(piece(3))
        ue = uext[...]
        acc = jnp.zeros((t, BW), F32)
        for b in range(8):
            rb = ue if b == 0 else pltpu.roll(ue, b, 0)
            for a in range((CONV_K - b + 7) // 8):
                j = 8 * a + b
                acc = acc + convw_ref[CONV_K - 1 - j:CONV_K - j, :] * rb[HALO_CONV - 8 * a:HALO_CONV - 8 * a + t, :]
        y = acc + vec_ref[1:2, :]
        y_ref[...] = y
        yhat, _ = _layer_norm_stats(y)
        act = _silu(yhat * vec_ref[2:3, :] + vec_ref[3:4, :])
        z_ref[:, BW:2 * BW] = (act * _silu(piece(4))).astype(BF16)
        merged = merged + gated(1)

        vhat, _ = _layer_norm_stats(piece(6))
        vn = (vhat * vec_ref[4:5, :] + vec_ref[5:6, :]).astype(BF16)
        for g in range(NG):
            ws = _tril_bf16(sguw_ref[g])
            cols = slice(g * GW, (g + 1) * GW)
            for c in range(t // GW):
                rows = slice(c * GW, (c + 1) * GW)
                sp = _dot(ws, vn[rows, cols]) + sgub_ref[g]
                gate = _silu(p_ref[rows, 7 * BW + g * GW:7 * BW + (g + 1) * GW].astype(F32))
                z_ref[rows, 2 * BW + g * GW:2 * BW + (g + 1) * GW] = (
                    p_ref[rows, 5 * BW + g * GW:5 * BW + (g + 1) * GW].astype(F32) * sp * gate).astype(BF16)
        merged = merged + gated(2)

        vext[HALO_SC:, :] = piece(9) * piece(10)
        ve = vext[...]
        q = jnp.zeros((t, BW), F32)
        for j in range(SHORT_K):
            sh = ve if j == 0 else pltpu.roll(ve, j, 0)
            q = q + scw_ref[SHORT_K - 1 - j:SHORT_K - j, :] * sh[HALO_SC:, :]
        q_ref[...] = q
        z_ref[:, 3 * BW:] = (piece(8) * q * _silu(piece(11))).astype(BF16)
        mb = (merged + gated(3)).astype(BF16)
        m_ref[...] = mb
        xo = x_ref[...] + _dot(mb, wo_ref[...])
        if head is None:
            xo_ref[...] = xo
        else:
            @pl.when(jnp.logical_and(pl.program_id(0) == 0, i == 0))
            def _():
                loss_ref[...] = jnp.zeros_like(loss_ref)
                dg_ref[...] = jnp.zeros_like(dg_ref)

            g = fg_ref[...]
            r = lax.rsqrt(jnp.mean(xo * xo, axis=-1, keepdims=True) + RMS_EPS)
            xn = xo * r
            e = xn * g - t_ref[...]
            loss_ref[...] += 0.5 * jnp.sum(jnp.mean(e * e, axis=-1, keepdims=True))
            dy = e * (1.0 / D)
            dg_ref[...] += _colsum(dy * xn)
            gy = dy * g
            dx_ref[...] = r * gy - xn * (r * jnp.mean(gy * xn, axis=-1, keepdims=True))

    row = lambda b, i: (b * nt + i, 0)
    full2 = lambda b, i: (0, 0)
    full3 = lambda b, i: (0, 0, 0)
    tail_in = () if head is None else tuple(head)
    tail_in_specs = [] if head is None else [pl.BlockSpec((t, D), row), pl.BlockSpec((1, D), full2)]
    tail_out_specs = [] if head is None else [pl.BlockSpec((8, 128), full2), pl.BlockSpec((1, D), full2)]
    tail_out_shape = [] if head is None else [SDS((8, 128), F32), SDS((1, D), F32)]
    return _carried_call(
        body, comm, (proj, x2d, wb, wo, pool_w, vecs, conv_w, sgu_w, sgu_bc, sc_w) + tail_in,
        name="mixers_fwd" if head is None else "mixers_fwd_loss", grid=(nb, nt),
        in_specs=[pl.BlockSpec((t, IN_COLS), row),
                  pl.BlockSpec((t, D), row),
                  pl.BlockSpec((4, BW, D), full3),
                  pl.BlockSpec((D, D), full2),
                  pl.BlockSpec((NG, GW, GW), full3),
                  pl.BlockSpec((8, BW), full2),
                  pl.BlockSpec((32, BW), full2),
                  pl.BlockSpec((NG, GW, GW), full3),
                  pl.BlockSpec((NG, GW, 1), full3),
                  pl.BlockSpec((8, BW), full2)] + tail_in_specs,
        out_specs=[pl.BlockSpec((t, 4 * BW), row), pl.BlockSpec((t, BW), row),
                   pl.BlockSpec((t, BW), row), pl.BlockSpec((t, BW), row),
                   pl.BlockSpec((t, D), row), pl.BlockSpec((t, D), row)] + tail_out_specs,
        out_shape=[SDS((n, 4 * BW), BF16), SDS((n, BW), F32), SDS((n, BW), F32), SDS((n, BW), BF16),
                   SDS((n, D), BF16), SDS((n, D), F32)] + tail_out_shape,
        scratch_shapes=[pltpu.VMEM((t + HALO_POOL, BW), F32), pltpu.VMEM((t + HALO_CONV, BW), F32),
                        pltpu.VMEM((t + HALO_SC, BW), F32)],
        semantics=("arbitrary", "arbitrary"))


def _merge_out_bwd(dxo, z, proj, merged, wb, wo, tm):
    n = dxo.shape[0]
    ni = n // tm
    sub = D // NDEV

    def body(dx_ref, z_ref, mg_ref, m_ref, wb_ref, wo_ref,
             dbo_ref, dp_ref, gwo_ref, gwo_b_ref, gwb_ref, gwb_b_ref,
             dm_ref, acc_wo, acc_wb, half_wo, half_wb, sems):
        i = pl.program_id(0)
        h = pl.program_id(1)

        @pl.when(jnp.logical_and(i == 0, h == 0))
        def _():
            acc_wo[...] = jnp.zeros_like(acc_wo)
            acc_wb[...] = jnp.zeros_like(acc_wb)

        @pl.when(h == 0)
        def _():
            dxb = dx_ref[...].astype(BF16)
            dm_ref[...] = _dot_nt(dxb, wo_ref[...])
            acc_wo[...] += _dot_tn(m_ref[...], dxb)

        dm = dm_ref[...]
        for k in range(2):
            zk = z_ref[:, k * BW:(k + 1) * BW]
            w = wb_ref[2 * h + k]
            bo = _dot(zk, w)
            sig = _sigmoid(mg_ref[:, k * D:(k + 1) * D].astype(F32))
            dbo = (dm * sig).astype(BF16)
            dp_ref[:, k * D:(k + 1) * D] = (dm * bo * sig * (1.0 - sig)).astype(BF16)
            dbo_ref[:, k * D:(k + 1) * D] = dbo
            acc_wb[2 * h + k] += _dot_tn(zk, dbo)

        @pl.when(jnp.logical_and(i == ni - 1, h == 1))
        def _():
            half_wo[...] = acc_wo[...].astype(BF16)
            half_wb[...] = acc_wb[...].astype(BF16)
            copies = [pltpu.make_async_copy(acc_wo, gwo_ref, sems.at[0]),
                      pltpu.make_async_copy(half_wo, gwo_b_ref, sems.at[1])]
            for d in range(NDEV):
                cols = pl.ds(d * sub, sub)
                copies.append(pltpu.make_async_copy(acc_wb.at[:, :, cols], gwb_ref.at[d], sems.at[2 + 2 * d]))
                copies.append(pltpu.make_async_copy(half_wb.at[:, :, cols], gwb_b_ref.at[d], sems.at[3 + 2 * d]))
            for cp in copies:
                cp.start()
            for cp in copies:
                cp.wait()

    any_spec = pl.BlockSpec(memory_space=pl.ANY)
    return pl.pallas_call(
        body, name="merge_out_bwd", grid=(ni, 2),
        in_specs=[pl.BlockSpec((tm, D), lambda i, h: (i, 0)),
                  pl.BlockSpec((tm, 2 * BW), lambda i, h: (i, h)),
                  pl.BlockSpec((tm, 2 * D), lambda i, h: (i, 3 + h)),
                  pl.BlockSpec((tm, D), lambda i, h: (i, 0)),
                  pl.BlockSpec((4, BW, D), lambda i, h: (0, 0, 0)),
                  pl.BlockSpec((D, D), lambda i, h: (0, 0))],
        out_specs=[pl.BlockSpec((tm, 2 * D), lambda i, h: (i, h)),
                   pl.BlockSpec((tm, 2 * D), lambda i, h: (i, 3 + h)),
                   any_spec, any_spec, any_spec, any_spec],
        out_shape=[SDS((n, 4 * D), BF16), SDS((n, IN_COLS), BF16), SDS((D, D), F32), SDS((D, D), BF16),
                   SDS((NDEV, 4, BW, sub), F32), SDS((NDEV, 4, BW, sub), BF16)],
        scratch_shapes=[pltpu.VMEM((tm, D), F32), pltpu.VMEM((D, D), F32), pltpu.VMEM((4, BW, D), F32),
                        pltpu.VMEM((D, D), BF16), pltpu.VMEM((4, BW, D), BF16),
                        pltpu.SemaphoreType.DMA((2 + 2 * NDEV,))],
        compiler_params=_params("arbitrary", "arbitrary"),
    )(dxo, z, proj, merged, wb, wo)


def _grad_w_in(h, d_proj, tk, comm=None):
    nk = h.shape[0] // tk

    def body(h_ref, dp_ref, o_ref, ob_ref):
        k = pl.program_id(1)

        @pl.when(k == 0)
        def _():
            o_ref[...] = jnp.zeros_like(o_ref)

        o_ref[...] += _dot_tn(h_ref[...], dp_ref[...])

        @pl.when(k == nk - 1)
        def _():
            ob_ref[...] = o_ref[...].astype(BF16)

    out_spec = pl.BlockSpec((None, D, SHARD), lambda j, k: (j, 0, 0))
    return _carried_call(
        body, comm, (h, d_proj), name="grad_w_in", grid=(NDEV, nk),
        in_specs=[pl.BlockSpec((tk, D), lambda j, k: (k, 0)), pl.BlockSpec((tk, SHARD), lambda j, k: (k, j))],
        out_specs=[out_spec, out_spec], out_shape=[SDS((NDEV, D, SHARD), F32), SDS((NDEV, D, SHARD), BF16)],
        semantics=("parallel", "arbitrary"))


def _mixers_bwd(proj, d_bo, wb, y_conv, q_sc, pooled, d_proj, pool_w, vecs, conv_w, sgu_w, sgu_bc, sc_w, nb, seq, t, comm=None):
    n = nb * seq
    nt = seq // t

    def body(p_ref, dbo_ref, wb_ref, y_ref, q_ref, pooled_ref, dpin_ref,
             poolw_ref, vec_ref, convw_ref, sguw_ref, sgub_ref, scw_ref,
             dp_ref, dvec_ref, dconvw_ref, dscw_ref, dpoolw_ref, dsguw_ref, dsgub_ref,
             rext, dyext, dqext, bacc, dyrot, dwacc):
        b = pl.program_id(0)
        i = pl.program_id(1)
        first = jnp.logical_and(b == 0, i == 0)
        last = jnp.logical_and(b == nb - 1, i == nt - 1)

        @pl.when(first)
        def _():
            dvec_ref[...] = jnp.zeros_like(dvec_ref)
            dconvw_ref[...] = jnp.zeros_like(dconvw_ref)
            dscw_ref[...] = jnp.zeros_like(dscw_ref)
            dpoolw_ref[...] = jnp.zeros_like(dpoolw_ref)
            dsguw_ref[...] = jnp.zeros_like(dsguw_ref)
            dsgub_ref[...] = jnp.zeros_like(dsgub_ref)
            bacc[...] = jnp.zeros_like(bacc)
            dwacc[...] = jnp.zeros_like(dwacc)

        @pl.when(i == 0)
        def _():
            rext[t:, :] = jnp.zeros((HALO_POOL, BW), F32)
            dyext[t:, :] = jnp.zeros((HALO_CONV, BW), F32)
            dqext[t:, :] = jnp.zeros((HALO_SC, BW), F32)

        @pl.when(i > 0)
        def _():
            rext[t:, :] = rext[0:HALO_POOL, :]
            dyext[t:, :] = dyext[0:HALO_CONV, :]
            dqext[t:, :] = dqext[0:HALO_SC, :]

        def piece(k):
            return p_ref[:, k * BW:(k + 1) * BW].astype(F32)

        def put(k, v):
            dp_ref[:, k * BW:(k + 1) * BW] = v.astype(BF16)

        def dz(k):
            return _dot_nt(dbo_ref[:, k * D:(k + 1) * D], wb_ref[k])

        pos = lax.broadcasted_iota(jnp.int32, (t, 1), 0) + (nt - 1 - i) * t

        dzp = dz(0)
        pg = piece(1)
        d_pm = dzp * _silu(pg)
        for j, win in enumerate(POOL_WINDOWS):
            cols = slice(j * GW, (j + 1) * GW)
            pw = poolw_ref[j].astype(BF16)
            pooled = pooled_ref[:, cols]
            mixed = _dot(pooled, pw)
            scale = vec_ref[0:1, cols]
            dvec_ref[0:1, cols] += _colsum(d_pm[:, cols] * mixed)
            dp_ref[:, BW + j * GW:BW + (j + 1) * GW] = (dzp[:, cols] * (mixed * scale) * _dsilu(pg[:, cols])).astype(BF16)
            d_mixed = (d_pm[:, cols] * scale).astype(BF16)
            dpoolw_ref[j] += _dot_tn(pooled, d_mixed)
            d_pooled = _dot_nt(d_mixed, pw)
            cnt = jnp.minimum(pos + 1, win).astype(F32)
            rext[0:t, cols] = d_pooled / cnt
            s = rext[:, cols]
            step = 1
            while step < win:
                s = s + pltpu.roll(s, t + HALO_POOL - step, 0)
                step *= 2
            dp_ref[:, cols] = (s[0:t, :] - d_pooled).astype(BF16)

        dzc = dz(1)
        cgate = piece(4)
        yhat, rstd = _layer_norm_stats(y_ref[...])
        ln = yhat * vec_ref[2:3, :] + vec_ref[3:4, :]
        put(4, dzc * _silu(ln) * _dsilu(cgate))
        d_ln = dzc * _silu(cgate) * _dsilu(ln)
        dvec_ref[2:3, :] += _colsum(d_ln * yhat)
        dvec_ref[3:4, :] += _colsum(d_ln)
        dy = _layer_norm_bwd(d_ln * vec_ref[2:3, :], yhat, rstd)
        dvec_ref[1:2, :] += _colsum(dy)
        dyext[0:t, :] = dy
        dye = dyext[...]
        for b in range(1, 8):
            dyrot[b - 1] = pltpu.roll(dye, t + HALO_CONV - b, 0)

        def conv_rows(c, carry):
            r0 = pl.multiple_of(c * CONV_ROWS, CONV_ROWS)
            rows = pl.ds(r0, CONV_ROWS)
            sig_cb = _sigmoid(p_ref[rows, 3 * BW:4 * BW].astype(F32))
            u = p_ref[rows, 2 * BW:3 * BW].astype(F32) * sig_cb
            du = jnp.zeros((CONV_ROWS, BW), F32)
            for b in range(8):
                for a in range((CONV_K - b + 7) // 8):
                    j = 8 * a + b
                    at = pl.ds(pl.multiple_of(r0 + 8 * a, 8), CONV_ROWS)
                    sh = dyext[at, :] if b == 0 else dyrot[b - 1, at, :]
                    du = du + convw_ref[CONV_K - 1 - j:CONV_K - j, :] * sh
                    dwacc[CONV_K - 1 - j] += jnp.sum((u * sh).reshape(CONV_ROWS // 8, 8, BW), axis=0)
            dp_ref[rows, 2 * BW:3 * BW] = (du * sig_cb).astype(BF16)
            dp_ref[rows, 3 * BW:4 * BW] = (du * u * (1.0 - sig_cb)).astype(BF16)
            return carry

        lax.fori_loop(0, t // CONV_ROWS, conv_rows, 0)

        dzg = dz(2)
        gu = piece(5)
        ggate = piece(7)
        vhat, vrstd = _layer_norm_stats(piece(6))
        vn = (vhat * vec_ref[4:5, :] + vec_ref[5:6, :]).astype(BF16)
        t1 = dzg * _silu(ggate)
        d_sp_all = t1 * gu
        d_vn_cols = []
        sp_cols = []
        for g in range(NG):
            ws = _tril_bf16(sguw_ref[g])
            cols = slice(g * GW, (g + 1) * GW)
            d_vn_rows = []
            sp_rows = []
            for c in range(t // GW):
                rows = slice(c * GW, (c + 1) * GW)
                sp_rows.append(_dot(ws, vn[rows, cols]) + sgub_ref[g])
                d_sp = d_sp_all[rows, cols]
                bacc[g] += d_sp
                d_spb = d_sp.astype(BF16)
                d_vn_rows.append(_dot_tn(ws, d_spb))
                dsguw_ref[g] += _dot_nt(d_spb, vn[rows, cols])
            d_vn_cols.append(jnp.concatenate(d_vn_rows, axis=0))
            sp_cols.append(jnp.concatenate(sp_rows, axis=0))
        d_vn = jnp.concatenate(d_vn_cols, axis=1)
        sp = jnp.concatenate(sp_cols, axis=1)
        put(5, t1 * sp)
        put(7, dzg * gu * sp * _dsilu(ggate))
        dvec_ref[4:5, :] += _colsum(d_vn * vhat)
        dvec_ref[5:6, :] += _colsum(d_vn)
        put(6, _layer_norm_bwd(d_vn * vec_ref[4:5, :], vhat, vrstd))

        dzs = dz(3)
        sb = piece(8)
        scp = piece(9)
        sx = piece(10)
        sgate = piece(11)
        q = q_ref[...]
        t2 = dzs * _silu(sgate)
        put(8, t2 * q)
        put(11, dzs * sb * q * _dsilu(sgate))
        dqext[0:t, :] = t2 * sb
        dqe = dqext[...]
        vv = scp * sx
        dv = jnp.zeros((t, BW), F32)
        for j in range(SHORT_K):
            sh = (dqe if j == 0 else pltpu.roll(dqe, t + HALO_SC - j, 0))[0:t, :]
            dv = dv + scw_ref[SHORT_K - 1 - j:SHORT_K - j, :] * sh
            dscw_ref[SHORT_K - 1 - j:SHORT_K - j, :] += _colsum(vv * sh)
        put(9, dv * sx)
        put(10, dv * scp)

        @pl.when(last)
        def _():
            r = lax.broadcasted_iota(jnp.int32, (GW, GW), 0)
            c = lax.broadcasted_iota(jnp.int32, (GW, GW), 1)
            for g in range(NG):
                dsguw_ref[g] = jnp.where(r >= c, dsguw_ref[g], 0.0)
                dsgub_ref[g:g + 1, :] = _colsum(bacc[g].T)
            dconvw_ref[...] = jnp.sum(dwacc[...], axis=1)

    row = lambda b, i: (b * nt + (nt - 1 - i), 0)
    full2 = lambda b, i: (0, 0)
    full3 = lambda b, i: (0, 0, 0)
    return _carried_call(
        body, comm, (proj, d_bo, wb, y_conv, q_sc, pooled, d_proj, pool_w, vecs, conv_w, sgu_w, sgu_bc, sc_w),
        name="mixers_bwd", grid=(nb, nt),
        in_specs=[pl.BlockSpec((t, N_PIECE_COLS), row),
                  pl.BlockSpec((t, 4 * D), row),
                  pl.BlockSpec((4, BW, D), full3),
                  pl.BlockSpec((t, BW), row), pl.BlockSpec((t, BW), row), pl.BlockSpec((t, BW), row),
                  pl.BlockSpec(memory_space=pl.ANY),
                  pl.BlockSpec((NG, GW, GW), full3),
                  pl.BlockSpec((8, BW), full2),
                  pl.BlockSpec((32, BW), full2),
                  pl.BlockSpec((NG, GW, GW), full3),
                  pl.BlockSpec((NG, GW, 1), full3),
                  pl.BlockSpec((8, BW), full2)],
        out_specs=[pl.BlockSpec((t, N_PIECE_COLS), row),
                   pl.BlockSpec((8, BW), full2), pl.BlockSpec((32, BW), full2), pl.BlockSpec((8, BW), full2),
                   pl.BlockSpec((NG, GW, GW), full3), pl.BlockSpec((NG, GW, GW), full3),
                   pl.BlockSpec((8, GW), full2)],
        out_shape=[SDS((n, IN_COLS), BF16), SDS((8, BW), F32), SDS((32, BW), F32), SDS((8, BW), F32),
                   SDS((NG, GW, GW), F32), SDS((NG, GW, GW), F32), SDS((8, GW), F32)],
        scratch_shapes=[pltpu.VMEM((t + HALO_POOL, BW), F32), pltpu.VMEM((t + HALO_CONV, BW), F32),
                        pltpu.VMEM((t + HALO_SC, BW), F32), pltpu.VMEM((NG, GW, GW), F32),
                        pltpu.VMEM((7, t + HALO_CONV, BW), F32), pltpu.VMEM((32, 8, BW), F32)],
        input_output_aliases={6: 0}, semantics=("arbitrary", "arbitrary"))


def _in_proj_bwd(d_proj, w_g, x2d, g_row, dxo, tm, comm=None):
    n = x2d.shape[0]
    half = D // 2

    def body(dp_ref, wu_ref, wl_ref, x_hbm, g_ref, dxo_hbm, dx_ref, dg_ref, acc, x_buf, dxo_buf, sems):
        i = pl.program_id(0)
        k = pl.program_id(1)
        rows = pl.ds(pl.multiple_of(i * tm, tm), tm)
        late = (pltpu.make_async_copy(x_hbm.at[rows], x_buf, sems.at[0]),
                pltpu.make_async_copy(dxo_hbm.at[rows], dxo_buf, sems.at[1]))

        @pl.when(jnp.logical_and(i == 0, k == 0))
        def _():
            dg_ref[...] = jnp.zeros_like(dg_ref)

        @pl.when(k == 0)
        def _():
            for cp in late:
                cp.start()
            acc[...] = jnp.zeros_like(acc)

        dp0 = dp_ref[:, :SHARD]
        dp1 = dp_ref[:, SHARD:]
        acc[:, :half] += _dot_nt(dp0, wu_ref[0]) + _dot_nt(dp1, wu_ref[1])
        acc[:, half:] += _dot_nt(dp0, wl_ref[0]) + _dot_nt(dp1, wl_ref[1])

        @pl.when(k == NDEV // 2 - 1)
        def _():
            for cp in late:
                cp.wait()
            dh = acc[...]
            x = x_buf[...]
            r = lax.rsqrt(jnp.mean(x * x, axis=-1, keepdims=True) + RMS_EPS)
            xn = x * r
            dg_ref[...] += _colsum(dh * xn)
            gy = dh * g_ref[...]
            dx_ref[...] = dxo_buf[...] + r * gy - xn * (r * jnp.mean(gy * xn, axis=-1, keepdims=True))

    any_spec = pl.BlockSpec(memory_space=pl.ANY)
    return _carried_call(
        body, comm, (d_proj, *w_g, x2d, g_row, dxo), name="in_proj_bwd", grid=(n // tm, NDEV // 2),
        in_specs=[pl.BlockSpec((tm, 2 * SHARD), lambda i, k: (i, k)),
                  pl.BlockSpec((2, half, SHARD), lambda i, k: (k, 0, 0)),
                  pl.BlockSpec((2, half, SHARD), lambda i, k: (k, 0, 0)),
                  any_spec,
                  pl.BlockSpec((1, D), lambda i, k: (0, 0)),
                  any_spec],
        out_specs=[pl.BlockSpec((tm, D), lambda i, k: (i, 0)), pl.BlockSpec((1, D), lambda i, k: (0, 0))],
        out_shape=[SDS((n, D), F32), SDS((1, D), F32)],
        scratch_shapes=[pltpu.VMEM((tm, D), F32), pltpu.VMEM((tm, D), F32), pltpu.VMEM((tm, D), F32),
                        pltpu.SemaphoreType.DMA((2,))],
        semantics=("arbitrary", "arbitrary"))


def _all_gather(arrs, name):
    na = len(arrs)

    def body(*refs):
        ins, outs = refs[:na], refs[na:2 * na]
        send_sems, recv_sems, local_sems = refs[2 * na:]
        x, y, c = _position()
        me, sibling = (x, y, c), (x, y, 1 - c)
        chips = [(1 - x, y), (x, 1 - y), (1 - x, 1 - y)]

        def copy(a, k, block, to, from_input=False):
            px, py, pc = block
            dst = outs[a].at[4 * px + 2 * py + pc]
            return pltpu.make_async_remote_copy(
                src_ref=ins[a] if from_input else dst, dst_ref=dst,
                send_sem=send_sems.at[7 * a + k], recv_sem=recv_sems.at[7 * a + k],
                device_id=to, device_id_type=MESH)

        mine = [pltpu.make_async_copy(ins[a], outs[a].at[4 * x + 2 * y + c], local_sems.at[a]) for a in range(na)]
        for cp in mine:
            cp.start()
        first = []
        for a in range(na):
            first.append(copy(a, 0, me, sibling, from_input=True))
            first += [copy(a, 1 + j, me, (*chip, c), from_input=True) for j, chip in enumerate(chips)]
        for cp in first:
            cp.start()
        passed = []
        for j, chip in enumerate(chips):
            for a in range(na):
                copy(a, 1 + j, (*chip, c), me).wait_recv()
                fwd = copy(a, 4 + j, (*chip, c), sibling)
                fwd.start()
                passed.append(fwd)
        for a in range(na):
            copy(a, 0, sibling, me).wait_recv()
            for j, chip in enumerate(chips):
                copy(a, 4 + j, (*chip, 1 - c), me).wait_recv()
        for cp in first + passed:
            cp.wait_send()
        for cp in mine:
            cp.wait()

    any_spec = pl.BlockSpec(memory_space=pl.ANY)
    return pl.pallas_call(
        body, name=name,
        in_specs=[any_spec] * na, out_specs=[any_spec] * na,
        out_shape=[SDS((NDEV,) + a.shape, a.dtype) for a in arrs],
        scratch_shapes=[pltpu.SemaphoreType.DMA((7 * na,)), pltpu.SemaphoreType.DMA((7 * na,)),
                        pltpu.SemaphoreType.DMA((na,))],
    )(*arrs)


def _adamw_math(g, w, m, v):
    m = ADAM_B1 * m + (1.0 - ADAM_B1) * g
    v = ADAM_B2 * v + (1.0 - ADAM_B2) * (g * g)
    m_hat = m / (1.0 - ADAM_B1 ** ADAM_STEP)
    v_hat = v / (1.0 - ADAM_B2 ** ADAM_STEP)
    delta = -ADAM_LR * (m_hat / (jnp.sqrt(v_hat) + ADAM_EPS) + ADAM_WD * w)
    return delta, m, v


def _adamw_layer(own, landed, w, m, v, layer, earlier, tr, name):
    _, rows, cols = w.shape

    def body(o_ref, l_ref, w_ref, m_ref, v_ref, *rest):
        g_out, d_out, m_out, v_out = rest[-4:]
        g = o_ref[...]
        for r in range(NDEV - 1):
            g = g + l_ref[r].astype(F32)
        g_out[...] = g
        d_out[...], m_out[...], v_out[...] = _adamw_math(g, w_ref[...], m_ref[...], v_ref[...])

    lay = pl.BlockSpec((None, tr, cols), lambda i: (layer, i, 0))
    any_spec = pl.BlockSpec(memory_space=pl.ANY)
    prior = list(earlier) if earlier is not None else []
    return pl.pallas_call(
        body, name=name, grid=(rows // tr,),
        in_specs=[pl.BlockSpec((tr, cols), lambda i: (i, 0)),
                  pl.BlockSpec((NDEV - 1, tr, cols), lambda i: (0, i, 0)), lay, lay, lay] + [any_spec] * len(prior),
        out_specs=[lay] * 4, out_shape=[SDS(w.shape, F32)] * 4,
        input_output_aliases={5 + k: k for k in range(len(prior))},
        compiler_params=_params("parallel"),
    )(own, landed, w, m, v, *prior)


def _adamw_gathered(parts, w, m, v, tr, name):
    rows, cols = w.shape

    def body(p_ref, w_ref, m_ref, v_ref, g_out, d_out, m_out, v_out):
        g = p_ref[0]
        for k in range(1, NDEV):
            g = g + p_ref[k]
        g_out[...] = g
        d_out[...], m_out[...], v_out[...] = _adamw_math(g, w_ref[...], m_ref[...], v_ref[...])

    blk = pl.BlockSpec((tr, cols), lambda i: (i, 0))
    return pl.pallas_call(
        body, name=name, grid=(rows // tr,),
        in_specs=[pl.BlockSpec((NDEV, tr, cols), lambda i: (0, i, 0)), blk, blk, blk],
        out_specs=[blk] * 4, out_shape=[SDS((rows, cols), F32)] * 4,
        compiler_params=_params("parallel"),
    )(parts, w, m, v)


def _adamw_plain(g, w, m, v):
    def body(g_ref, w_ref, m_ref, v_ref, d_out, m_out, v_out):
        d_out[...], m_out[...], v_out[...] = _adamw_math(g_ref[...], w_ref[...], m_ref[...], v_ref[...])

    return pl.pallas_call(body, name="adamw_conv_shards", out_shape=[SDS(w.shape, F32)] * 3)(g, w, m, v)


SMALL = ("norm_g", "pool_w", "pool_scale", "conv_w", "conv_b", "conv_ln_g", "conv_ln_b",
         "sgu_ln_g", "sgu_ln_b", "sgu_w", "sgu_b", "sc_w")
SMALL_SHAPES = {"norm_g": (D,), "pool_w": (NG, GW, GW), "pool_scale": (BW,), "conv_w": (CONV_K, BW), "conv_b": (BW,),
                "conv_ln_g": (BW,), "conv_ln_b": (BW,), "sgu_ln_g": (BW,), "sgu_ln_b": (BW,),
                "sgu_w": (NG, GW, GW), "sgu_b": (NG, GW), "sc_w": (SHORT_K, BW)}
REPLICATED = tuple(k for k in SMALL if k not in ("conv_w", "sc_w"))
PACK_UNIT = 8 * 128


def _size(shape):
    out = 1
    for s in shape:
        out *= s
    return out


def _padded(a):
    flat = a.reshape(-1)
    pad = -flat.shape[0] % PACK_UNIT
    return jnp.pad(flat, (0, pad)) if pad else flat


def _pack(arrays):
    return jnp.concatenate([_padded(a) for a in arrays]).reshape(-1, 128)


def _unpack(pack, shapes):
    flat = pack.reshape(-1)
    out, off = [], 0
    for shape in shapes:
        size = _size(shape)
        out.append(flat[off:off + size].reshape(shape))
        off += size + (-size % PACK_UNIT)
    return out


def _gathered_weights(wb, wo, conv):
    wb = wb.reshape(NDEV, 4, BW, D // NDEV).transpose(1, 2, 0, 3).reshape(4, BW, D)
    wo = wo.reshape(D, D)
    cw = CONV_K * (BW // NDEV)
    sw = SHORT_K * (BW // NDEV)
    sc_at = cw + (-cw % PACK_UNIT)
    flat = conv.reshape(NDEV, -1)
    conv_w = flat[:, :cw].reshape(NDEV, CONV_K, BW // NDEV).transpose(1, 0, 2).reshape(CONV_K, BW)
    sc_w = flat[:, sc_at:sc_at + sw].reshape(NDEV, SHORT_K, BW // NDEV).transpose(1, 0, 2).reshape(SHORT_K, BW)
    return wb, wo, conv_w, sc_w


def _layer_small_inputs(l, rep, conv_w, sc_w):
    vecs = jnp.stack([rep["pool_scale"][l], rep["conv_b"][l], rep["conv_ln_g"][l], rep["conv_ln_b"][l],
                      rep["sgu_ln_g"][l], rep["sgu_ln_b"][l], jnp.zeros((BW,), F32), jnp.zeros((BW,), F32)])
    conv_w = jnp.pad(conv_w, ((0, 32 - CONV_K), (0, 0)))
    sc_w = jnp.pad(sc_w, ((0, 8 - SHORT_K), (0, 0)))
    return (rep["pool_w"][l], vecs, conv_w, rep["sgu_w"][l], rep["sgu_b"][l].reshape(NG, GW, 1), sc_w)


def kernel(x, norm_g, w_in, pool_w, pool_scale, conv_w, conv_b, conv_ln_g, conv_ln_b, sgu_ln_g, sgu_ln_b, sgu_w, sgu_b, sc_w, w_branch, w_o, final_g, loss_target, m_norm_g, m_w_in, m_pool_w, m_pool_scale, m_conv_w, m_conv_b, m_conv_ln_g, m_conv_ln_b, m_sgu_ln_g, m_sgu_ln_b, m_sgu_w, m_sgu_b, m_sc_w, m_w_branch, m_w_o, m_final_g, v_norm_g, v_w_in, v_pool_w, v_pool_scale, v_conv_w, v_conv_b, v_conv_ln_g, v_conv_ln_b, v_sgu_ln_g, v_sgu_ln_b, v_sgu_w, v_sgu_b, v_sc_w, v_w_branch, v_w_o, v_final_g):
    args = dict(locals())
    w = {k: args[k] for k in SMALL + ("w_in", "w_branch", "w_o", "final_g")}
    mom = {k: args["m_" + k] for k in w}
    var = {k: args["v_" + k] for k in w}
    rep = {k: w[k] for k in REPLICATED}
    xi, yi, ci = _position()
    me = 4 * xi + 2 * yi + ci

    nb, seq, _ = x.shape
    n = nb * seq
    tm = min(512, n)
    tm_in = min(1024, n)
    tm_fwd = min(2048, n)
    tk = min(2048, n)
    t_fwd = min(256, seq)
    t_bwd = min(256, seq)

    w_in_b = [(w_in[l, :D // 2].astype(BF16), w_in[l, D // 2:].astype(BF16)) for l in range(DEPTH)]
    wb_b = [w_branch[l].astype(BF16).reshape(4 * BW, D // NDEV) for l in range(DEPTH)]
    wo_b = [w_o[l].astype(BF16) for l in range(DEPTH)]
    conv_b_ = [_pack([conv_w[l], sc_w[l]]) for l in range(DEPTH)]

    x2d = x.reshape(n, D)
    saved = []
    ride = _Exchange(gathers=[wb_b[0], wo_b[0], conv_b_[0]])
    order = jnp.bitwise_xor(me, jnp.array([0, 1, 4, 5, 2, 3, 6, 7], jnp.int32)).astype(jnp.int32)
    (proj, h), w_in_g0, (wb_g, wo_g, conv_g) = _in_proj_first(
        x2d, norm_g[0:1], w_in_b[0], order, tm_in, ride)
    w_in_g = [w_in_g0, None]
    for l in range(DEPTH):
        if l == 1:
            ride = _Exchange(gathers=[wb_b[1], wo_b[1], conv_b_[1]])
            (proj, h), (wb_g, wo_g, conv_g) = _in_proj(x2d, norm_g[l:l + 1], w_in_g[l], tm_fwd, ride)
        wb_l, wo_l, conv_l, sc_l = _gathered_weights(wb_g, wo_g, conv_g)
        small = _layer_small_inputs(l, rep, conv_l, sc_l)
        ride = _Exchange(chip_gathers=list(w_in_b[1])) if l == 0 else None
        head = (loss_target.reshape(n, D), final_g.reshape(1, D)) if l == DEPTH - 1 else None
        (z, y_conv, q_sc, pooled, merged, x_new, *head_out), gathered = _mixers_fwd(
            proj, x2d, wb_l, wo_l, *small, nb, seq, t_fwd, ride, head)
        if l == 0:
            w_in_g[1] = gathered
        saved.append((x2d, proj, h, z, y_conv, q_sc, pooled, merged, small, wb_l, wo_l))
        x2d = x_new

    dx = x2d
    loss_acc, d_final_g = head_out

    grads = {k: [None] * DEPTH for k in SMALL}
    big = {"w_in": None, "w_branch": None, "w_o": None}
    big_w = {"w_in": (w_in, m_w_in, v_w_in, (DEPTH, D, SHARD), 256),
             "w_branch": (w_branch, m_w_branch, v_w_branch, (DEPTH, 4 * BW, D // NDEV), 512),
             "w_o": (w_o, m_w_o, v_w_o, (DEPTH, D // NDEV, D), 64)}

    def adamw(k, own, landed, l):
        wk, mk, vk, shape, tr = big_w[k]
        big[k] = _adamw_layer(own, landed, wk.reshape(shape), mk.reshape(shape), vk.reshape(shape), l, big[k], tr,
                              "adamw_" + k)

    for l in reversed(range(DEPTH)):
        x_in, proj, h, z, y_conv, q_sc, pooled, merged, small, wb_l, wo_l = saved[l]
        d_bo, d_proj, d_wo, d_wo_b, d_wb, d_wb_b = _merge_out_bwd(dx, z, proj, merged, wb_l, wo_l, tm)
        ride = _Exchange(scatters=[(d_wo_b.reshape(NDEV, D // NDEV, D), d_wo.reshape(NDEV, D // NDEV, D)),
                                   (d_wb_b.reshape(NDEV, 4 * BW, D // NDEV), d_wb.reshape(NDEV, 4 * BW, D // NDEV))])
        (d_proj, dvec, dconvw, dscw, dpoolw, dsguw, dsgub), (wo_landed, wo_own, wb_landed, wb_own) = _mixers_bwd(
            proj, d_bo, wb_l, y_conv, q_sc, pooled, d_proj, *small, nb, seq, t_bwd, ride)
        grads["pool_w"][l] = dpoolw
        grads["pool_scale"][l] = dvec[0]
        grads["conv_w"][l] = dconvw[:CONV_K]
        grads["conv_b"][l] = dvec[1]
        grads["conv_ln_g"][l] = dvec[2]
        grads["conv_ln_b"][l] = dvec[3]
        grads["sgu_ln_g"][l] = dvec[4]
        grads["sgu_ln_b"][l] = dvec[5]
        grads["sgu_w"][l] = dsguw
        grads["sgu_b"][l] = dsgub[:NG]
        grads["sc_w"][l] = dscw[:SHORT_K]
        if l == 0:
            grads["norm_g"][0] = jnp.zeros((D,), F32)
            part_pack = _pack([jnp.stack(grads[k]) for k in SMALL] + [d_final_g[0], loss_acc[0, 0:1]])
            ride = _Exchange(gathers=[part_pack])
        else:
            ride = None
        (d_w_in, d_w_in_b), gathered = _grad_w_in(h, d_proj, tk, ride)
        if l == 0:
            parts = gathered[0]
        ride = _Exchange(scatters=[(d_w_in_b, d_w_in)])
        (dx, d_norm_g), (in_landed, in_own) = _in_proj_bwd(d_proj, w_in_g[l], x_in, norm_g[l:l + 1], dx, tm_in, ride)
        adamw("w_o", wo_own, wo_landed, l)
        adamw("w_branch", wb_own, wb_landed, l)
        adamw("w_in", in_own, in_landed, l)
        if l == 1:
            grads["norm_g"][1] = d_norm_g[0]
    grad_x = dx.reshape(nb, seq, D)

    names = list(SMALL) + ["final_g", "loss"]
    shapes = [(DEPTH,) + SMALL_SHAPES[k] for k in SMALL] + [(D,), (1,)]
    whole = REPLICATED + ("final_g",)
    w_pack = _pack([w[k] if k in whole else jnp.zeros(s, F32) for k, s in zip(names, shapes)])
    m_pack = _pack([mom[k] if k in whole else jnp.zeros(s, F32) for k, s in zip(names, shapes)])
    v_pack = _pack([var[k] if k in whole else jnp.ones(s, F32) for k, s in zip(names, shapes)])
    rows = parts.shape[1]
    tr = max(r for r in range(8, 641, 8) if rows % r == 0)
    packs = _adamw_gathered(parts, w_pack, m_pack, v_pack, tr, "adamw_small")
    out_g, out_d, out_m, out_v = [dict(zip(names, _unpack(p, shapes))) for p in packs]
    loss = out_g["loss"][0]

    (tail,) = _all_gather([d_norm_g.reshape(8, 128)], "norm_g_all_gather")
    first = [a[0].reshape(8, 128) for a in (w["norm_g"], mom["norm_g"], var["norm_g"])]
    tails = _adamw_gathered(tail, *first, 8, "adamw_norm_g")
    for dst, t0 in zip((out_g, out_d, out_m, out_v), tails):
        dst["norm_g"] = jnp.concatenate([t0.reshape(1, D), dst["norm_g"][1:]])

    col = me * (BW // NDEV)
    for k in ("conv_w", "sc_w"):
        out_g[k] = lax.dynamic_slice_in_dim(out_g[k], col, BW // NDEV, axis=2)
    sharded_shapes = [w["conv_w"].shape, w["sc_w"].shape]
    d2, m2, v2 = _adamw_plain(_pack([out_g["conv_w"], out_g["sc_w"]]), _pack([w["conv_w"], w["sc_w"]]),
                              _pack([mom["conv_w"], mom["sc_w"]]), _pack([var["conv_w"], var["sc_w"]]))
    for dst, pk in ((out_d, d2), (out_m, m2), (out_v, v2)):
        dst["conv_w"], dst["sc_w"] = _unpack(pk, sharded_shapes)

    for k in big:
        out_g[k], out_d[k], out_m[k], out_v[k] = [o.reshape(w[k].shape) for o in big[k]]

    order = ("norm_g", "w_in", "pool_w", "pool_scale", "conv_w", "conv_b", "conv_ln_g", "conv_ln_b", "sgu_ln_g",
             "sgu_ln_b", "sgu_w", "sgu_b", "sc_w", "w_branch", "w_o", "final_g")
    return (loss, grad_x, *[out_g[k] for k in order], *[out_d[k] for k in order],
            *[out_m[k] for k in order], *[out_v[k] for k in order])
```

```python
import functools

import jax
import jax.numpy as jnp
from jax import lax
from jax.experimental import pallas as pl
from jax.experimental.pallas import tpu as pltpu

F32 = jnp.float32
BF16 = jnp.bfloat16
SDS = jax.ShapeDtypeStruct
MESH = pl.DeviceIdType.MESH

D = 1024
BW = 512
NG = 4
GW = 128
CONV_K = 31
SHORT_K = 3
POOL_WINDOWS = (2, 4, 8, 16)
N_PIECE_COLS = 12 * BW
IN_COLS = N_PIECE_COLS + 4 * D
NDEV = 8
SHARD = IN_COLS // NDEV
DEPTH = 2
RMS_EPS = 1e-6
LN_EPS = 1e-5
HALO_POOL, HALO_CONV, HALO_SC = 16, 32, 8
CONV_ROWS = 32

ADAM_LR = 0.001
ADAM_B1 = 0.9
ADAM_B2 = 0.999
ADAM_EPS = 1e-08
ADAM_WD = 0.01
ADAM_STEP = 10

VMEM_LIMIT = 56 * 1024 * 1024


def _params(*sem):
    return pltpu.CompilerParams(dimension_semantics=sem, vmem_limit_bytes=VMEM_LIMIT)


def _sigmoid(x):
    return jax.nn.sigmoid(x)


def _silu(x):
    return x * _sigmoid(x)


def _dsilu(x):
    s = _sigmoid(x)
    return s * (1.0 + x * (1.0 - s))


def _colsum(x):
    return jnp.sum(x, axis=0, keepdims=True)


def _dot(a, b):
    return jnp.dot(a, b, preferred_element_type=F32)


def _dot_nt(a, b):
    return lax.dot_general(a, b, (((1,), (1,)), ((), ())), preferred_element_type=F32)


def _dot_tn(a, b):
    return lax.dot_general(a, b, (((0,), (0,)), ((), ())), preferred_element_type=F32)


def _layer_norm_stats(y):
    mu = jnp.mean(y, axis=-1, keepdims=True)
    yc = y - mu
    var = jnp.mean(yc * yc, axis=-1, keepdims=True)
    rstd = lax.rsqrt(var + LN_EPS)
    return yc * rstd, rstd


def _layer_norm_bwd(d_hat, hat, rstd):
    return rstd * (d_hat - jnp.mean(d_hat, axis=-1, keepdims=True)
                   - hat * jnp.mean(d_hat * hat, axis=-1, keepdims=True))


def _position():
    return lax.axis_index("x"), lax.axis_index("y"), lax.axis_index("c")


def _flip(v, bit):
    return 1 - v if bit else v


class _Exchange:
    def __init__(self, gathers=(), scatters=(), chip_gathers=()):
        self.gathers = list(gathers)
        self.scatters = list(scatters)
        self.chip_gathers = list(chip_gathers)
        self.inputs = self.gathers + [a for pair in self.scatters for a in pair] + self.chip_gathers
        self.out_shapes = [SDS((NDEV,) + a.shape, a.dtype) for a in self.gathers]
        for send, keep in self.scatters:
            self.out_shapes += [SDS((NDEV - 1,) + send.shape[1:], send.dtype), SDS(keep.shape[1:], keep.dtype)]
        self.out_shapes += [SDS((NDEV,) + a.shape, a.dtype) for a in self.chip_gathers]
        n = len(self.gathers) + len(self.scatters) + len(self.chip_gathers)
        self.scratch = [pltpu.SemaphoreType.DMA((7 * n,)), pltpu.SemaphoreType.DMA((7 * n,)),
                        pltpu.SemaphoreType.DMA((n,))]

    def _copies(self, ins, outs, sems):
        send_sems, recv_sems, local_sems = sems
        x, y, c = _position()
        me = 4 * x + 2 * y + c
        ng = len(self.gathers)
        local, remote = [], []
        for a in range(ng + len(self.scatters)):
            if a < ng:
                local.append(pltpu.make_async_copy(ins[a], outs[a].at[me], local_sems.at[a]))
            else:
                s = a - ng
                send, keep = ins[ng + 2 * s], ins[ng + 2 * s + 1]
                landed, own = outs[ng + 2 * s], outs[ng + 2 * s + 1]
                local.append(pltpu.make_async_copy(keep.at[me], own, local_sems.at[a]))
            for r in range(1, NDEV):
                peer = (_flip(x, r & 4), _flip(y, r & 2), _flip(c, r & 1))
                if a < ng:
                    src, dst = ins[a], outs[a].at[me]
                else:
                    src, dst = send.at[jnp.bitwise_xor(me, r)], landed.at[r - 1]
                remote.append(pltpu.make_async_remote_copy(
                    src_ref=src, dst_ref=dst, send_sem=send_sems.at[7 * a + r - 1],
                    recv_sem=recv_sems.at[7 * a + r - 1], device_id=peer, device_id_type=MESH))
        return local, remote

    def _chip_copy(self, ins, outs, sems, t):
        send_sems, recv_sems, local_sems = sems
        x, y, c = _position()
        at = len(self.gathers) + 2 * len(self.scatters) + t
        a = len(self.gathers) + len(self.scatters) + t

        def copy(k, block, to, from_input=False):
            px, py, pc = block
            dst = outs[at].at[4 * px + 2 * py + pc]
            return pltpu.make_async_remote_copy(
                src_ref=ins[at] if from_input else dst, dst_ref=dst, send_sem=send_sems.at[7 * a + k],
                recv_sem=recv_sems.at[7 * a + k], device_id=to, device_id_type=MESH)

        return copy, pltpu.make_async_copy(ins[at], outs[at].at[4 * x + 2 * y + c], local_sems.at[a])

    def start(self, ins, outs, sems):
        local, remote = self._copies(ins, outs, sems)
        for cp in remote + local:
            cp.start()
        x, y, c = _position()
        for t in range(len(self.chip_gathers)):
            copy, mine = self._chip_copy(ins, outs, sems, t)
            mine.start()
            copy(0, (x, y, c), (x, y, 1 - c), from_input=True).start()
            for j, chip in enumerate([(1 - x, y), (x, 1 - y), (1 - x, 1 - y)]):
                copy(1 + j, (x, y, c), (*chip, c), from_input=True).start()

    def pass_on(self, ins, outs, sems):
        x, y, c = _position()
        for t in range(len(self.chip_gathers)):
            copy, _ = self._chip_copy(ins, outs, sems, t)
            for j, chip in enumerate([(1 - x, y), (x, 1 - y), (1 - x, 1 - y)]):
                copy(1 + j, (*chip, c), (x, y, c)).wait_recv()
                copy(4 + j, (*chip, c), (x, y, 1 - c)).start()

    def finish(self, ins, outs, sems):
        x, y, c = _position()
        me, sibling = (x, y, c), (x, y, 1 - c)
        chips = [(1 - x, y), (x, 1 - y), (1 - x, 1 - y)]
        local, remote = self._copies(ins, outs, sems)
        for cp in remote + local:
            cp.wait()
        for t in range(len(self.chip_gathers)):
            copy, mine = self._chip_copy(ins, outs, sems, t)
            copy(0, sibling, me).wait_recv()
            copy(0, me, sibling, from_input=True).wait_send()
            for j, chip in enumerate(chips):
                copy(4 + j, (*chip, 1 - c), me).wait_recv()
                copy(1 + j, me, (*chip, c), from_input=True).wait_send()
                copy(4 + j, (*chip, c), sibling).wait_send()
            mine.wait()


def _carried_call(body, comm, args, *, name, grid, in_specs, out_specs, out_shape, scratch_shapes=(),
                  semantics, input_output_aliases=None):
    aliases = input_output_aliases or {}
    if comm is None:
        out = pl.pallas_call(body, name=name, grid=grid, in_specs=in_specs, out_specs=out_specs, out_shape=out_shape,
                             scratch_shapes=list(scratch_shapes), input_output_aliases=aliases,
                             compiler_params=_params(*semantics))(*args)
        return list(out), []
    n_in, n_out, n_scr = len(in_specs), len(out_specs), len(scratch_shapes)
    c_in, c_out = len(comm.inputs), len(comm.out_shapes)

    def full_body(*refs):
        refs = list(refs)
        ins, refs = refs[:n_in], refs[n_in:]
        cins, refs = refs[:c_in], refs[c_in:]
        outs, refs = refs[:n_out], refs[n_out:]
        couts, refs = refs[:c_out], refs[c_out:]
        scr, sems = refs[:n_scr], refs[n_scr:]
        ids = [pl.program_id(d) for d in range(len(grid))]
        first = functools.reduce(jnp.logical_and, [i == 0 for i in ids])
        last = functools.reduce(jnp.logical_and, [i == g - 1 for i, g in zip(ids, grid)])

        @pl.when(first)
        def _():
            comm.start(cins, couts, sems)

        body(*ins, *outs, *scr)

        late = functools.reduce(jnp.logical_and, [ids[0] == grid[0] - 1] + [i == 0 for i in ids[1:]])

        @pl.when(late if len(grid) > 1 else last)
        def _():
            comm.pass_on(cins, couts, sems)

        @pl.when(last)
        def _():
            comm.finish(cins, couts, sems)

    any_spec = pl.BlockSpec(memory_space=pl.ANY)
    out = pl.pallas_call(
        full_body, name=name, grid=grid,
        in_specs=list(in_specs) + [any_spec] * c_in, out_specs=list(out_specs) + [any_spec] * c_out,
        out_shape=list(out_shape) + comm.out_shapes, scratch_shapes=list(scratch_shapes) + comm.scratch,
        input_output_aliases=aliases, compiler_params=_params(*["arbitrary"] * len(grid)),
    )(*args, *comm.inputs)
    return list(out[:n_out]), list(out[n_out:])


def _in_proj(x2d, g_row, w_g, tm, comm=None):
    n = x2d.shape[0]
    half = D // 2

    def body(x_ref, g_ref, wu_ref, wl_ref, proj_ref, h_ref):
        @pl.when(pl.program_id(1) == 0)
        def _():
            x = x_ref[...]
            r = lax.rsqrt(jnp.mean(x * x, axis=-1, keepdims=True) + RMS_EPS)
            h_ref[...] = (x * r * g_ref[...]).astype(BF16)

        proj_ref[...] = (_dot(h_ref[:, :half], wu_ref[...]) + _dot(h_ref[:, half:], wl_ref[...])).astype(BF16)

    return _carried_call(
        body, comm, (x2d, g_row, *w_g), name="in_proj", grid=(n // tm, NDEV),
        in_specs=[pl.BlockSpec((tm, D), lambda i, j: (i, 0)),
                  pl.BlockSpec((1, D), lambda i, j: (0, 0)),
                  pl.BlockSpec((None, half, SHARD), lambda i, j: (j, 0, 0)),
                  pl.BlockSpec((None, half, SHARD), lambda i, j: (j, 0, 0))],
        out_specs=[pl.BlockSpec((tm, SHARD), lambda i, j: (i, j)),
                   pl.BlockSpec((tm, D), lambda i, j: (i, 0))],
        out_shape=[SDS((n, IN_COLS), BF16), SDS((n, D), BF16)],
        semantics=("parallel", "arbitrary"))


def _in_proj_first(x2d, g_row, w_own, order, tm, comm):
    n = x2d.shape[0]
    ni = n // tm
    half = D // 2
    c_in, c_out = len(comm.inputs), len(comm.out_shapes)

    def body(order_ref, x_ref, g_ref, wu_hbm, wl_hbm, *refs):
        refs = list(refs)
        cins, refs = refs[:c_in], refs[c_in:]
        proj_ref, h_hbm, gu_hbm, gl_hbm = refs[:4]
        couts, refs = refs[4:4 + c_out], refs[4 + c_out:]
        h_all, wbuf, send_sems, recv_sems, own_sems, load_sems, h_sems = refs[:7]
        csems = refs[7:]
        jp = pl.program_id(0)
        i = pl.program_id(1)
        x, y, c = _position()
        me = 4 * x + 2 * y + c
        sibling = (x, y, 1 - c)
        chips = [(1 - x, y), (x, 1 - y), (1 - x, 1 - y)]
        own = (wu_hbm, wl_hbm)
        gathered = (gu_hbm, gl_hbm)

        def copy(a, k, block, to, from_input=False):
            px, py, pc = block
            dst = gathered[a].at[4 * px + 2 * py + pc]
            return pltpu.make_async_remote_copy(
                src_ref=own[a] if from_input else dst, dst_ref=dst,
                send_sem=send_sems.at[7 * a + k], recv_sem=recv_sems.at[7 * a + k],
                device_id=to, device_id_type=MESH)

        def load(a, slot, block_id):
            return pltpu.make_async_copy(gathered[a].at[block_id], wbuf.at[slot, a], load_sems.at[2 * slot + a])

        def keep(a):
            return pltpu.make_async_copy(own[a], gathered[a].at[me], own_sems.at[a])

        def h_store(t):
            rows = pl.ds(t * tm, tm)
            return pltpu.make_async_copy(h_all.at[rows], h_hbm.at[rows], h_sems.at[t])

        first = jnp.logical_and(jp == 0, i == 0)
        last = jnp.logical_and(jp == NDEV - 1, i == ni - 1)

        @pl.when(jnp.logical_and(jp == NDEV - 3, i == 0))
        def _():
            comm.start(cins, couts, csems)

        @pl.when(jnp.logical_and(jp == 2, i == 0))
        def _():
            for a in range(2):
                copy(a, 3, (x, y, c), (*chips[2], c), from_input=True).start()

        @pl.when(first)
        def _():
            for a in range(2):
                keep(a).start()
                copy(a, 0, (x, y, c), sibling, from_input=True).start()
                for j, chip in enumerate(chips[:2]):
                    copy(a, 1 + j, (x, y, c), (*chip, c), from_input=True).start()
                own_block = pltpu.make_async_copy(own[a], wbuf.at[0, a], load_sems.at[a])
                own_block.start()
                own_block.wait()

        for nxt in range(1, NDEV):
            @pl.when(jnp.logical_and(jp == nxt - 1, i == ni - 1))
            def _(nxt=nxt):
                j, passed = (nxt - 2) // 2, nxt % 2 == 1
                for a in range(2):
                    if nxt == 1:
                        copy(a, 0, sibling, (x, y, c)).wait_recv()
                    elif passed:
                        copy(a, 4 + j, (*chips[j], 1 - c), (x, y, c)).wait_recv()
                    else:
                        copy(a, 1 + j, (*chips[j], c), (x, y, c)).wait_recv()
                        copy(a, 4 + j, (*chips[j], c), sibling).start()
                    load(a, nxt % 2, order_ref[nxt]).start()

        @pl.when(jnp.logical_and(jp > 0, i == 0))
        def _():
            for a in range(2):
                load(a, jp % 2, order_ref[jp]).wait()

        rows = pl.ds(pl.multiple_of(i * tm, tm), tm)

        @pl.when(jp == 0)
        def _():
            xt = x_ref[...]
            r = lax.rsqrt(jnp.mean(xt * xt, axis=-1, keepdims=True) + RMS_EPS)
            h_all[rows, :] = (xt * r * g_ref[...]).astype(BF16)
            for t in range(ni):
                @pl.when(i == t)
                def _(t=t):
                    h_store(t).start()

        slot = jp % 2
        proj_ref[...] = (_dot(h_all[rows, :half], wbuf[slot, 0]) + _dot(h_all[rows, half:], wbuf[slot, 1])).astype(BF16)

        @pl.when(last)
        def _():
            for a in range(2):
                keep(a).wait()
                copy(a, 0, (x, y, c), sibling, from_input=True).wait_send()
                for j, chip in enumerate(chips):
                    copy(a, 1 + j, (x, y, c), (*chip, c), from_input=True).wait_send()
                    copy(a, 4 + j, (*chip, c), sibling).wait_send()
            for t in range(ni):
                h_store(t).wait()
            comm.finish(cins, couts, csems)

    any_spec = pl.BlockSpec(memory_space=pl.ANY)
    blocks = SDS((NDEV, half, SHARD), BF16)
    out = pl.pallas_call(
        body, name="in_proj_first",
        grid_spec=pltpu.PrefetchScalarGridSpec(
            num_scalar_prefetch=1, grid=(NDEV, ni),
            in_specs=[pl.BlockSpec((tm, D), lambda jp, i, order: (jnp.where(jp == 0, i, ni - 1), 0)),
                      pl.BlockSpec((1, D), lambda jp, i, order: (0, 0)),
                      any_spec, any_spec] + [any_spec] * c_in,
            out_specs=[pl.BlockSpec((tm, SHARD), lambda jp, i, order: (i, order[jp])),
                       any_spec, any_spec, any_spec] + [any_spec] * c_out,
            scratch_shapes=[pltpu.VMEM((n, D), BF16), pltpu.VMEM((2, 2, half, SHARD), BF16),
                            pltpu.SemaphoreType.DMA((14,)), pltpu.SemaphoreType.DMA((14,)),
                            pltpu.SemaphoreType.DMA((2,)), pltpu.SemaphoreType.DMA((4,)),
                            pltpu.SemaphoreType.DMA((ni,))] + comm.scratch),
        out_shape=[SDS((n, IN_COLS), BF16), SDS((n, D), BF16), blocks, blocks] + comm.out_shapes,
        compiler_params=_params("arbitrary", "arbitrary"),
    )(order, x2d, g_row, *w_own, *comm.inputs)
    return list(out[:2]), list(out[2:4]), list(out[4:])


def _tril_bf16(w):
    r = lax.broadcasted_iota(jnp.int32, (GW, GW), 0)
    c = lax.broadcasted_iota(jnp.int32, (GW, GW), 1)
    return jnp.where(r >= c, w, 0.0).astype(BF16)


def _mixers_fwd(proj, x2d, wb, wo, pool_w, vecs, conv_w, sgu_w, sgu_bc, sc_w, nb, seq, t, comm=None, head=None):
    n = nb * seq
    nt = seq // t

    def body(p_ref, x_ref, wb_ref, wo_ref, poolw_ref, vec_ref, convw_ref, sguw_ref, sgub_ref, scw_ref, *refs):
        if head is None:
            z_ref, y_ref, q_ref, pooled_ref, m_ref, xo_ref, pext, uext, vext = refs
        else:
            t_ref, fg_ref, z_ref, y_ref, q_ref, pooled_ref, m_ref, dx_ref, loss_ref, dg_ref, pext, uext, vext = refs
        i = pl.program_id(1)

        def gated(k):
            gate = _sigmoid(p_ref[:, N_PIECE_COLS + k * D:N_PIECE_COLS + (k + 1) * D].astype(F32))
            return gate * _dot(z_ref[:, k * BW:(k + 1) * BW], wb_ref[k])

        @pl.when(i == 0)
        def _():
            pext[0:HALO_POOL, :] = jnp.zeros((HALO_POOL, BW), F32)
            uext[0:HALO_CONV, :] = jnp.zeros((HALO_CONV, BW), F32)
            vext[0:HALO_SC, :] = jnp.zeros((HALO_SC, BW), F32)

        @pl.when(i > 0)
        def _():
            pext[0:HALO_POOL, :] = pext[t:t + HALO_POOL, :]
            uext[0:HALO_CONV, :] = uext[t:t + HALO_CONV, :]
            vext[0:HALO_SC, :] = vext[t:t + HALO_SC, :]

        def piece(k):
            return p_ref[:, k * BW:(k + 1) * BW].astype(F32)

        pos = lax.broadcasted_iota(jnp.int32, (t, 1), 0) + i * t

        px = piece(0)
        pext[HALO_POOL:, :] = px
        for j, win in enumerate(POOL_WINDOWS):
            cols = slice(j * GW, (j + 1) * GW)
            s = pext[:, cols]
            step = 1
            while step < win:
                s = s + pltpu.roll(s, step, 0)
                step *= 2
            cnt = jnp.minimum(pos + 1, win).astype(F32)
            pooled = (s[HALO_POOL:, :] / cnt - px[:, cols]).astype(BF16)
            pooled_ref[:, cols] = pooled
            mixed = _dot(pooled, poolw_ref[j].astype(BF16))
            z_ref[:, cols] = (mixed * vec_ref[0:1, cols] * _silu(p_ref[:, BW + j * GW:BW + (j + 1) * GW].astype(F32))).astype(BF16)
        merged = gated(0)

        uext[HALO_CONV:, :] = piece(2) * _sigmoid(piece(3))
        ue = uext[...]
        acc = jnp.zeros((t, BW), F32)
        for b in range(8):
            rb = ue if b == 0 else pltpu.roll(ue, b, 0)
            for a in range((CONV_K - b + 7) // 8):
                j = 8 * a + b
                acc = acc + convw_ref[CONV_K - 1 - j:CONV_K - j, :] * rb[HALO_CONV - 8 * a:HALO_CONV - 8 * a + t, :]
        y = acc + vec_ref[1:2, :]
        y_ref[...] = y
        yhat, _ = _layer_norm_stats(y)
        act = _silu(yhat * vec_ref[2:3, :] + vec_ref[3:4, :])
        z_ref[:, BW:2 * BW] = (act * _silu(piece(4))).astype(BF16)
        merged = merged + gated(1)

        vhat, _ = _layer_norm_stats(piece(6))
        vn = (vhat * vec_ref[4:5, :] + vec_ref[5:6, :]).astype(BF16)
        for g in range(NG):
            ws = _tril_bf16(sguw_ref[g])
            cols = slice(g * GW, (g + 1) * GW)
            for c in range(t // GW):
                rows = slice(c * GW, (c + 1) * GW)
                sp = _dot(ws, vn[rows, cols]) + sgub_ref[g]
                gate = _silu(p_ref[rows, 7 * BW + g * GW:7 * BW + (g + 1) * GW].astype(F32))
                z_ref[rows, 2 * BW + g * GW:2 * BW + (g + 1) * GW] = (
                    p_ref[rows, 5 * BW + g * GW:5 * BW + (g + 1) * GW].astype(F32) * sp * gate).astype(BF16)
        merged = merged + gated(2)

        vext[HALO_SC:, :] = piece(9) * piece(10)
        ve = vext[...]
        q = jnp.zeros((t, BW), F32)
        for j in range(SHORT_K):
            sh = ve if j == 0 else pltpu.roll(ve, j, 0)
            q = q + scw_ref[SHORT_K - 1 - j:SHORT_K - j, :] * sh[HALO_SC:, :]
        q_ref[...] = q
        z_ref[:, 3 * BW:] = (piece(8) * q * _silu(piece(11))).astype(BF16)
        mb = (merged + gated(3)).astype(BF16)
        m_ref[...] = mb
        xo = x_ref[...] + _dot(mb, wo_ref[...])
        if head is None:
            xo_ref[...] = xo
        else:
            @pl.when(jnp.logical_and(pl.program_id(0) == 0, i == 0))
            def _():
                loss_ref[...] = jnp.zeros_like(loss_ref)
                dg_ref[...] = jnp.zeros_like(dg_ref)

            g = fg_ref[...]
            r = lax.rsqrt(jnp.mean(xo * xo, axis=-1, keepdims=True) + RMS_EPS)
            xn = xo * r
            e = xn * g - t_ref[...]
            loss_ref[...] += 0.5 * jnp.sum(jnp.mean(e * e, axis=-1, keepdims=True))
            dy = e * (1.0 / D)
            dg_ref[...] += _colsum(dy * xn)
            gy = dy * g
            dx_ref[...] = r * gy - xn * (r * jnp.mean(gy * xn, axis=-1, keepdims=True))

    row = lambda b, i: (b * nt + i, 0)
    full2 = lambda b, i: (0, 0)
    full3 = lambda b, i: (0, 0, 0)
    tail_in = () if head is None else tuple(head)
    tail_in_specs = [] if head is None else [pl.BlockSpec((t, D), row), pl.BlockSpec((1, D), full2)]
    tail_out_specs = [] if head is None else [pl.BlockSpec((8, 128), full2), pl.BlockSpec((1, D), full2)]
    tail_out_shape = [] if head is None else [SDS((8, 128), F32), SDS((1, D), F32)]
    return _carried_call(
        body, comm, (proj, x2d, wb, wo, pool_w, vecs, conv_w, sgu_w, sgu_bc, sc_w) + tail_in,
        name="mixers_fwd" if head is None else "mixers_fwd_loss", grid=(nb, nt),
        in_specs=[pl.BlockSpec((t, IN_COLS), row),
                  pl.BlockSpec((t, D), row),
                  pl.BlockSpec((4, BW, D), full3),
                  pl.BlockSpec((D, D), full2),
                  pl.BlockSpec((NG, GW, GW), full3),
                  pl.BlockSpec((8, BW), full2),
                  pl.BlockSpec((32, BW), full2),
                  pl.BlockSpec((NG, GW, GW), full3),
                  pl.BlockSpec((NG, GW, 1), full3),
                  pl.BlockSpec((8, BW), full2)] + tail_in_specs,
        out_specs=[pl.BlockSpec((t, 4 * BW), row), pl.BlockSpec((t, BW), row),
                   pl.BlockSpec((t, BW), row), pl.BlockSpec((t, BW), row),
                   pl.BlockSpec((t, D), row), pl.BlockSpec((t, D), row)] + tail_out_specs,
        out_shape=[SDS((n, 4 * BW), BF16), SDS((n, BW), F32), SDS((n, BW), F32), SDS((n, BW), BF16),
                   SDS((n, D), BF16), SDS((n, D), F32)] + tail_out_shape,
        scratch_shapes=[pltpu.VMEM((t + HALO_POOL, BW), F32), pltpu.VMEM((t + HALO_CONV, BW), F32),
                        pltpu.VMEM((t + HALO_SC, BW), F32)],
        semantics=("arbitrary", "arbitrary"))


def _merge_out_bwd(dxo, z, proj, merged, wb, wo, tm):
    n = dxo.shape[0]
    ni = n // tm
    sub = D // NDEV

    def body(dx_ref, z_ref, mg_ref, m_ref, wb_ref, wo_ref,
             dbo_ref, dp_ref, gwo_ref, gwo_b_ref, gwb_ref, gwb_b_ref,
             dm_ref, acc_wo, acc_wb, half_wo, half_wb, sems):
        i = pl.program_id(0)
        h = pl.program_id(1)

        @pl.when(jnp.logical_and(i == 0, h == 0))
        def _():
            acc_wo[...] = jnp.zeros_like(acc_wo)
            acc_wb[...] = jnp.zeros_like(acc_wb)

        @pl.when(h == 0)
        def _():
            dxb = dx_ref[...].astype(BF16)
            dm_ref[...] = _dot_nt(dxb, wo_ref[...])
            acc_wo[...] += _dot_tn(m_ref[...], dxb)

        dm = dm_ref[...]
        for k in range(2):
            zk = z_ref[:, k * BW:(k + 1) * BW]
            w = wb_ref[2 * h + k]
            bo = _dot(zk, w)
            sig = _sigmoid(mg_ref[:, k * D:(k + 1) * D].astype(F32))
            dbo = (dm * sig).astype(BF16)
            dp_ref[:, k * D:(k + 1) * D] = (dm * bo * sig * (1.0 - sig)).astype(BF16)
            dbo_ref[:, k * D:(k + 1) * D] = dbo
            acc_wb[2 * h + k] += _dot_tn(zk, dbo)

        @pl.when(jnp.logical_and(i == ni - 1, h == 1))
        def _():
            half_wo[...] = acc_wo[...].astype(BF16)
            half_wb[...] = acc_wb[...].astype(BF16)
            copies = [pltpu.make_async_copy(acc_wo, gwo_ref, sems.at[0]),
                      pltpu.make_async_copy(half_wo, gwo_b_ref, sems.at[1])]
            for d in range(NDEV):
                cols = pl.ds(d * sub, sub)
                copies.append(pltpu.make_async_copy(acc_wb.at[:, :, cols], gwb_ref.at[d], sems.at[2 + 2 * d]))
                copies.append(pltpu.make_async_copy(half_wb.at[:, :, cols], gwb_b_ref.at[d], sems.at[3 + 2 * d]))
            for cp in copies:
                cp.start()
            for cp in copies:
                cp.wait()

    any_spec = pl.BlockSpec(memory_space=pl.ANY)
    return pl.pallas_call(
        body, name="merge_out_bwd", grid=(ni, 2),
        in_specs=[pl.BlockSpec((tm, D), lambda i, h: (i, 0)),
                  pl.BlockSpec((tm, 2 * BW), lambda i, h: (i, h)),
                  pl.BlockSpec((tm, 2 * D), lambda i, h: (i, 3 + h)),
                  pl.BlockSpec((tm, D), lambda i, h: (i, 0)),
                  pl.BlockSpec((4, BW, D), lambda i, h: (0, 0, 0)),
                  pl.BlockSpec((D, D), lambda i, h: (0, 0))],
        out_specs=[pl.BlockSpec((tm, 2 * D), lambda i, h: (i, h)),
                   pl.BlockSpec((tm, 2 * D), lambda i, h: (i, 3 + h)),
                   any_spec, any_spec, any_spec, any_spec],
        out_shape=[SDS((n, 4 * D), BF16), SDS((n, IN_COLS), BF16), SDS((D, D), F32), SDS((D, D), BF16),
                   SDS((NDEV, 4, BW, sub), F32), SDS((NDEV, 4, BW, sub), BF16)],
        scratch_shapes=[pltpu.VMEM((tm, D), F32), pltpu.VMEM((D, D), F32), pltpu.VMEM((4, BW, D), F32),
                        pltpu.VMEM((D, D), BF16), pltpu.VMEM((4, BW, D), BF16),
                        pltpu.SemaphoreType.DMA((2 + 2 * NDEV,))],
        compiler_params=_params("arbitrary", "arbitrary"),
    )(dxo, z, proj, merged, wb, wo)


def _grad_w_in(h, d_proj, tk, comm=None):
    nk = h.shape[0] // tk

    def body(h_ref, dp_ref, o_ref, ob_ref):
        k = pl.program_id(1)

        @pl.when(k == 0)
        def _():
            o_ref[...] = jnp.zeros_like(o_ref)

        o_ref[...] += _dot_tn(h_ref[...], dp_ref[...])

        @pl.when(k == nk - 1)
        def _():
            ob_ref[...] = o_ref[...].astype(BF16)

    out_spec = pl.BlockSpec((None, D, SHARD), lambda j, k: (j, 0, 0))
    return _carried_call(
        body, comm, (h, d_proj), name="grad_w_in", grid=(NDEV, nk),
        in_specs=[pl.BlockSpec((tk, D), lambda j, k: (k, 0)), pl.BlockSpec((tk, SHARD), lambda j, k: (k, j))],
        out_specs=[out_spec, out_spec], out_shape=[SDS((NDEV, D, SHARD), F32), SDS((NDEV, D, SHARD), BF16)],
        semantics=("parallel", "arbitrary"))


def _mixers_bwd(proj, d_bo, wb, y_conv, q_sc, pooled, d_proj, pool_w, vecs, conv_w, sgu_w, sgu_bc, sc_w, nb, seq, t, comm=None):
    n = nb * seq
    nt = seq // t

    def body(p_ref, dbo_ref, wb_ref, y_ref, q_ref, pooled_ref, dpin_ref,
             poolw_ref, vec_ref, convw_ref, sguw_ref, sgub_ref, scw_ref,
             dp_ref, dvec_ref, dconvw_ref, dscw_ref, dpoolw_ref, dsguw_ref, dsgub_ref,
             rext, dyext, dqext, bacc, dyrot, dwacc):
        b = pl.program_id(0)
        i = pl.program_id(1)
        first = jnp.logical_and(b == 0, i == 0)
        last = jnp.logical_and(b == nb - 1, i == nt - 1)

        @pl.when(first)
        def _():
            dvec_ref[...] = jnp.zeros_like(dvec_ref)
            dconvw_ref[...] = jnp.zeros_like(dconvw_ref)
            dscw_ref[...] = jnp.zeros_like(dscw_ref)
            dpoolw_ref[...] = jnp.zeros_like(dpoolw_ref)
            dsguw_ref[...] = jnp.zeros_like(dsguw_ref)
            dsgub_ref[...] = jnp.zeros_like(dsgub_ref)
            bacc[...] = jnp.zeros_like(bacc)
            dwacc[...] = jnp.zeros_like(dwacc)

        @pl.when(i == 0)
        def _():
            rext[t:, :] = jnp.zeros((HALO_POOL, BW), F32)
            dyext[t:, :] = jnp.zeros((HALO_CONV, BW), F32)
            dqext[t:, :] = jnp.zeros((HALO_SC, BW), F32)

        @pl.when(i > 0)
        def _():
            rext[t:, :] = rext[0:HALO_POOL, :]
            dyext[t:, :] = dyext[0:HALO_CONV, :]
            dqext[t:, :] = dqext[0:HALO_SC, :]

        def piece(k):
            return p_ref[:, k * BW:(k + 1) * BW].astype(F32)

        def put(k, v):
            dp_ref[:, k * BW:(k + 1) * BW] = v.astype(BF16)

        def dz(k):
            return _dot_nt(dbo_ref[:, k * D:(k + 1) * D], wb_ref[k])

        pos = lax.broadcasted_iota(jnp.int32, (t, 1), 0) + (nt - 1 - i) * t

        dzp = dz(0)
        pg = piece(1)
        d_pm = dzp * _silu(pg)
        for j, win in enumerate(POOL_WINDOWS):
            cols = slice(j * GW, (j + 1) * GW)
            pw = poolw_ref[j].astype(BF16)
            pooled = pooled_ref[:, cols]
            mixed = _dot(pooled, pw)
            scale = vec_ref[0:1, cols]
            dvec_ref[0:1, cols] += _colsum(d_pm[:, cols] * mixed)
            dp_ref[:, BW + j * GW:BW + (j + 1) * GW] = (dzp[:, cols] * (mixed * scale) * _dsilu(pg[:, cols])).astype(BF16)
            d_mixed = (d_pm[:, cols] * scale).astype(BF16)
            dpoolw_ref[j] += _dot_tn(pooled, d_mixed)
            d_pooled = _dot_nt(d_mixed, pw)
            cnt = jnp.minimum(pos + 1, win).astype(F32)
            rext[0:t, cols] = d_pooled / cnt
            s = rext[:, cols]
            step = 1
            while step < win:
                s = s + pltpu.roll(s, t + HALO_POOL - step, 0)
                step *= 2
            dp_ref[:, cols] = (s[0:t, :] - d_pooled).astype(BF16)

        dzc = dz(1)
        cgate = piece(4)
        yhat, rstd = _layer_norm_stats(y_ref[...])
        ln = yhat * vec_ref[2:3, :] + vec_ref[3:4, :]
        put(4, dzc * _silu(ln) * _dsilu(cgate))
        d_ln = dzc * _silu(cgate) * _dsilu(ln)
        dvec_ref[2:3, :] += _colsum(d_ln * yhat)
        dvec_ref[3:4, :] += _colsum(d_ln)
        dy = _layer_norm_bwd(d_ln * vec_ref[2:3, :], yhat, rstd)
        dvec_ref[1:2, :] += _colsum(dy)
        dyext[0:t, :] = dy
        dye = dyext[...]
        for b in range(1, 8):
            dyrot[b - 1] = pltpu.roll(dye, t + HALO_CONV - b, 0)

        def conv_rows(c, carry):
            r0 = pl.multiple_of(c * CONV_ROWS, CONV_ROWS)
            rows = pl.ds(r0, CONV_ROWS)
            sig_cb = _sigmoid(p_ref[rows, 3 * BW:4 * BW].astype(F32))
            u = p_ref[rows, 2 * BW:3 * BW].astype(F32) * sig_cb
            du = jnp.zeros((CONV_ROWS, BW), F32)
            for b in range(8):
                for a in range((CONV_K - b + 7) // 8):
                    j = 8 * a + b
                    at = pl.ds(pl.multiple_of(r0 + 8 * a, 8), CONV_ROWS)
                    sh = dyext[at, :] if b == 0 else dyrot[b - 1, at, :]
                    du = du + convw_ref[CONV_K - 1 - j:CONV_K - j, :] * sh
                    dwacc[CONV_K - 1 - j] += jnp.sum((u * sh).reshape(CONV_ROWS // 8, 8, BW), axis=0)
            dp_ref[rows, 2 * BW:3 * BW] = (du * sig_cb).astype(BF16)
            dp_ref[rows, 3 * BW:4 * BW] = (du * u * (1.0 - sig_cb)).astype(BF16)
            return carry

        lax.fori_loop(0, t // CONV_ROWS, conv_rows, 0)

        dzg = dz(2)
        gu = piece(5)
        ggate = piece(7)
        vhat, vrstd = _layer_norm_stats(piece(6))
        vn = (vhat * vec_ref[4:5, :] + vec_ref[5:6, :]).astype(BF16)
        t1 = dzg * _silu(ggate)
        d_sp_all = t1 * gu
        d_vn_cols = []
        sp_cols = []
        for g in range(NG):
            ws = _tril_bf16(sguw_ref[g])
            cols = slice(g * GW, (g + 1) * GW)
            d_vn_rows = []
            sp_rows = []
            for c in range(t // GW):
                rows = slice(c * GW, (c + 1) * GW)
                sp_rows.append(_dot(ws, vn[rows, cols]) + sgub_ref[g])
                d_sp = d_sp_all[rows, cols]
                bacc[g] += d_sp
                d_spb = d_sp.astype(BF16)
                d_vn_rows.append(_dot_tn(ws, d_spb))
                dsguw_ref[g] += _dot_nt(d_spb, vn[rows, cols])
            d_vn_cols.append(jnp.concatenate(d_vn_rows, axis=0))
            sp_cols.append(jnp.concatenate(sp_rows, axis=0))
        d_vn = jnp.concatenate(d_vn_cols, axis=1)
        sp = jnp.concatenate(sp_cols, axis=1)
        put(5, t1 * sp)
        put(7, dzg * gu * sp * _dsilu(ggate))
        dvec_ref[4:5, :] += _colsum(d_vn * vhat)
        dvec_ref[5:6, :] += _colsum(d_vn)
        put(6, _layer_norm_bwd(d_vn * vec_ref[4:5, :], vhat, vrstd))

        dzs = dz(3)
        sb = piece(8)
        scp = piece(9)
        sx = piece(10)
        sgate = piece(11)
        q = q_ref[...]
        t2 = dzs * _silu(sgate)
        put(8, t2 * q)
        put(11, dzs * sb * q * _dsilu(sgate))
        dqext[0:t, :] = t2 * sb
        dqe = dqext[...]
        vv = scp * sx
        dv = jnp.zeros((t, BW), F32)
        for j in range(SHORT_K):
            sh = (dqe if j == 0 else pltpu.roll(dqe, t + HALO_SC - j, 0))[0:t, :]
            dv = dv + scw_ref[SHORT_K - 1 - j:SHORT_K - j, :] * sh
            dscw_ref[SHORT_K - 1 - j:SHORT_K - j, :] += _colsum(vv * sh)
        put(9, dv * sx)
        put(10, dv * scp)

        @pl.when(last)
        def _():
            r = lax.broadcasted_iota(jnp.int32, (GW, GW), 0)
            c = lax.broadcasted_iota(jnp.int32, (GW, GW), 1)
            for g in range(NG):
                dsguw_ref[g] = jnp.where(r >= c, dsguw_ref[g], 0.0)
                dsgub_ref[g:g + 1, :] = _colsum(bacc[g].T)
            dconvw_ref[...] = jnp.sum(dwacc[...], axis=1)

    row = lambda b, i: (b * nt + (nt - 1 - i), 0)
    full2 = lambda b, i: (0, 0)
    full3 = lambda b, i: (0, 0, 0)
    return _carried_call(
        body, comm, (proj, d_bo, wb, y_conv, q_sc, pooled, d_proj, pool_w, vecs, conv_w, sgu_w, sgu_bc, sc_w),
        name="mixers_bwd", grid=(nb, nt),
        in_specs=[pl.BlockSpec((t, N_PIECE_COLS), row),
                  pl.BlockSpec((t, 4 * D), row),
                  pl.BlockSpec((4, BW, D), full3),
                  pl.BlockSpec((t, BW), row), pl.BlockSpec((t, BW), row), pl.BlockSpec((t, BW), row),
                  pl.BlockSpec(memory_space=pl.ANY),
                  pl.BlockSpec((NG, GW, GW), full3),
                  pl.BlockSpec((8, BW), full2),
                  pl.BlockSpec((32, BW), full2),
                  pl.BlockSpec((NG, GW, GW), full3),
                  pl.BlockSpec((NG, GW, 1), full3),
                  pl.BlockSpec((8, BW), full2)],
        out_specs=[pl.BlockSpec((t, N_PIECE_COLS), row),
                   pl.BlockSpec((8, BW), full2), pl.BlockSpec((32, BW), full2), pl.BlockSpec((8, BW), full2),
                   pl.BlockSpec((NG, GW, GW), full3), pl.BlockSpec((NG, GW, GW), full3),
                   pl.BlockSpec((8, GW), full2)],
        out_shape=[SDS((n, IN_COLS), BF16), SDS((8, BW), F32), SDS((32, BW), F32), SDS((8, BW), F32),
                   SDS((NG, GW, GW), F32), SDS((NG, GW, GW), F32), SDS((8, GW), F32)],
        scratch_shapes=[pltpu.VMEM((t + HALO_POOL, BW), F32), pltpu.VMEM((t + HALO_CONV, BW), F32),
                        pltpu.VMEM((t + HALO_SC, BW), F32), pltpu.VMEM((NG, GW, GW), F32),
                        pltpu.VMEM((7, t + HALO_CONV, BW), F32), pltpu.VMEM((32, 8, BW), F32)],
        input_output_aliases={6: 0}, semantics=("arbitrary", "arbitrary"))


def _in_proj_bwd(d_proj, w_g, x2d, g_row, dxo, tm, comm=None):
    n = x2d.shape[0]
    half = D // 2

    def body(dp_ref, wu_ref, wl_ref, x_hbm, g_ref, dxo_hbm, dx_ref, dg_ref, acc, x_buf, dxo_buf, sems):
        i = pl.program_id(0)
        k = pl.program_id(1)
        rows = pl.ds(pl.multiple_of(i * tm, tm), tm)
        late = (pltpu.make_async_copy(x_hbm.at[rows], x_buf, sems.at[0]),
                pltpu.make_async_copy(dxo_hbm.at[rows], dxo_buf, sems.at[1]))

        @pl.when(jnp.logical_and(i == 0, k == 0))
        def _():
            dg_ref[...] = jnp.zeros_like(dg_ref)

        @pl.when(k == 0)
        def _():
            for cp in late:
                cp.start()
            acc[...] = jnp.zeros_like(acc)

        dp0 = dp_ref[:, :SHARD]
        dp1 = dp_ref[:, SHARD:]
        acc[:, :half] += _dot_nt(dp0, wu_ref[0]) + _dot_nt(dp1, wu_ref[1])
        acc[:, half:] += _dot_nt(dp0, wl_ref[0]) + _dot_nt(dp1, wl_ref[1])

        @pl.when(k == NDEV // 2 - 1)
        def _():
            for cp in late:
                cp.wait()
            dh = acc[...]
            x = x_buf[...]
            r = lax.rsqrt(jnp.mean(x * x, axis=-1, keepdims=True) + RMS_EPS)
            xn = x * r
            dg_ref[...] += _colsum(dh * xn)
            gy = dh * g_ref[...]
            dx_ref[...] = dxo_buf[...] + r * gy - xn * (r * jnp.mean(gy * xn, axis=-1, keepdims=True))

    any_spec = pl.BlockSpec(memory_space=pl.ANY)
    return _carried_call(
        body, comm, (d_proj, *w_g, x2d, g_row, dxo), name="in_proj_bwd", grid=(n // tm, NDEV // 2),
        in_specs=[pl.BlockSpec((tm, 2 * SHARD), lambda i, k: (i, k)),
                  pl.BlockSpec((2, half, SHARD), lambda i, k: (k, 0, 0)),
                  pl.BlockSpec((2, half, SHARD), lambda i, k: (k, 0, 0)),
                  any_spec,
                  pl.BlockSpec((1, D), lambda i, k: (0, 0)),
                  any_spec],
        out_specs=[pl.BlockSpec((tm, D), lambda i, k: (i, 0)), pl.BlockSpec((1, D), lambda i, k: (0, 0))],
        out_shape=[SDS((n, D), F32), SDS((1, D), F32)],
        scratch_shapes=[pltpu.VMEM((tm, D), F32), pltpu.VMEM((tm, D), F32), pltpu.VMEM((tm, D), F32),
                        pltpu.SemaphoreType.DMA((2,))],
        semantics=("arbitrary", "arbitrary"))


def _all_gather(arrs, name):
    na = len(arrs)

    def body(*refs):
        ins, outs = refs[:na], refs[na:2 * na]
        send_sems, recv_sems, local_sems = refs[2 * na:]
        x, y, c = _position()
        me, sibling = (x, y, c), (x, y, 1 - c)
        chips = [(1 - x, y), (x, 1 - y), (1 - x, 1 - y)]

        def copy(a, k, block, to, from_input=False):
            px, py, pc = block
            dst = outs[a].at[4 * px + 2 * py + pc]
            return pltpu.make_async_remote_copy(
                src_ref=ins[a] if from_input else dst, dst_ref=dst,
                send_sem=send_sems.at[7 * a + k], recv_sem=recv_sems.at[7 * a + k],
                device_id=to, device_id_type=MESH)

        mine = [pltpu.make_async_copy(ins[a], outs[a].at[4 * x + 2 * y + c], local_sems.at[a]) for a in range(na)]
        for cp in mine:
            cp.start()
        first = []
        for a in range(na):
            first.append(copy(a, 0, me, sibling, from_input=True))
            first += [copy(a, 1 + j, me, (*chip, c), from_input=True) for j, chip in enumerate(chips)]
        for cp in first:
            cp.start()
        passed = []
        for j, chip in enumerate(chips):
            for a in range(na):
                copy(a, 1 + j, (*chip, c), me).wait_recv()
                fwd = copy(a, 4 + j, (*chip, c), sibling)
                fwd.start()
                passed.append(fwd)
        for a in range(na):
            copy(a, 0, sibling, me).wait_recv()
            for j, chip in enumerate(chips):
                copy(a, 4 + j, (*chip, 1 - c), me).wait_recv()
        for cp in first + passed:
            cp.wait_send()
        for cp in mine:
            cp.wait()

    any_spec = pl.BlockSpec(memory_space=pl.ANY)
    return pl.pallas_call(
        body, name=name,
        in_specs=[any_spec] * na, out_specs=[any_spec] * na,
        out_shape=[SDS((NDEV,) + a.shape, a.dtype) for a in arrs],
        scratch_shapes=[pltpu.SemaphoreType.DMA((7 * na,)), pltpu.SemaphoreType.DMA((7 * na,)),
                        pltpu.SemaphoreType.DMA((na,))],
    )(*arrs)


def _adamw_math(g, w, m, v):
    m = ADAM_B1 * m + (1.0 - ADAM_B1) * g
    v = ADAM_B2 * v + (1.0 - ADAM_B2) * (g * g)
    m_hat = m / (1.0 - ADAM_B1 ** ADAM_STEP)
    v_hat = v / (1.0 - ADAM_B2 ** ADAM_STEP)
    delta = -ADAM_LR * (m_hat / (jnp.sqrt(v_hat) + ADAM_EPS) + ADAM_WD * w)
    return delta, m, v


def _adamw_layer(own, landed, w, m, v, layer, earlier, tr, name):
    _, rows, cols = w.shape

    def body(o_ref, l_ref, w_ref, m_ref, v_ref, *rest):
        g_out, d_out, m_out, v_out = rest[-4:]
        g = o_ref[...]
        for r in range(NDEV - 1):
            g = g + l_ref[r].astype(F32)
        g_out[...] = g
        d_out[...], m_out[...], v_out[...] = _adamw_math(g, w_ref[...], m_ref[...], v_ref[...])

    lay = pl.BlockSpec((None, tr, cols), lambda i: (layer, i, 0))
    any_spec = pl.BlockSpec(memory_space=pl.ANY)
    prior = list(earlier) if earlier is not None else []
    return pl.pallas_call(
        body, name=name, grid=(rows // tr,),
        in_specs=[pl.BlockSpec((tr, cols), lambda i: (i, 0)),
                  pl.BlockSpec((NDEV - 1, tr, cols), lambda i: (0, i, 0)), lay, lay, lay] + [any_spec] * len(prior),
        out_specs=[lay] * 4, out_shape=[SDS(w.shape, F32)] * 4,
        input_output_aliases={5 + k: k for k in range(len(prior))},
        compiler_params=_params("parallel"),
    )(own, landed, w, m, v, *prior)


def _adamw_gathered(parts, w, m, v, tr, name):
    rows, cols = w.shape

    def body(p_ref, w_ref, m_ref, v_ref, g_out, d_out, m_out, v_out):
        g = p_ref[0]
        for k in range(1, NDEV):
            g = g + p_ref[k]
        g_out[...] = g
        d_out[...], m_out[...], v_out[...] = _adamw_math(g, w_ref[...], m_ref[...], v_ref[...])

    blk = pl.BlockSpec((tr, cols), lambda i: (i, 0))
    return pl.pallas_call(
        body, name=name, grid=(rows // tr,),
        in_specs=[pl.BlockSpec((NDEV, tr, cols), lambda i: (0, i, 0)), blk, blk, blk],
        out_specs=[blk] * 4, out_shape=[SDS((rows, cols), F32)] * 4,
        compiler_params=_params("parallel"),
    )(parts, w, m, v)


def _adamw_plain(g, w, m, v):
    def body(g_ref, w_ref, m_ref, v_ref, d_out, m_out, v_out):
        d_out[...], m_out[...], v_out[...] = _adamw_math(g_ref[...], w_ref[...], m_ref[...], v_ref[...])

    return pl.pallas_call(body, name="adamw_conv_shards", out_shape=[SDS(w.shape, F32)] * 3)(g, w, m, v)


SMALL = ("norm_g", "pool_w", "pool_scale", "conv_w", "conv_b", "conv_ln_g", "conv_ln_b",
         "sgu_ln_g", "sgu_ln_b", "sgu_w", "sgu_b", "sc_w")
SMALL_SHAPES = {"norm_g": (D,), "pool_w": (NG, GW, GW), "pool_scale": (BW,), "conv_w": (CONV_K, BW), "conv_b": (BW,),
                "conv_ln_g": (BW,), "conv_ln_b": (BW,), "sgu_ln_g": (BW,), "sgu_ln_b": (BW,),
                "sgu_w": (NG, GW, GW), "sgu_b": (NG, GW), "sc_w": (SHORT_K, BW)}
REPLICATED = tuple(k for k in SMALL if k not in ("conv_w", "sc_w"))
PACK_UNIT = 8 * 128


def _size(shape):
    out = 1
    for s in shape:
        out *= s
    return out


def _padded(a):
    flat = a.reshape(-1)
    pad = -flat.shape[0] % PACK_UNIT
    return jnp.pad(flat, (0, pad)) if pad else flat


def _pack(arrays):
    return jnp.concatenate([_padded(a) for a in arrays]).reshape(-1, 128)


def _unpack(pack, shapes):
    flat = pack.reshape(-1)
    out, off = [], 0
    for shape in shapes:
        size = _size(shape)
        out.append(flat[off:off + size].reshape(shape))
        off += size + (-size % PACK_UNIT)
    return out


def _gathered_weights(wb, wo, conv):
    wb = wb.reshape(NDEV, 4, BW, D // NDEV).transpose(1, 2, 0, 3).reshape(4, BW, D)
    wo = wo.reshape(D, D)
    cw = CONV_K * (BW // NDEV)
    sw = SHORT_K * (BW // NDEV)
    sc_at = cw + (-cw % PACK_UNIT)
    flat = conv.reshape(NDEV, -1)
    conv_w = flat[:, :cw].reshape(NDEV, CONV_K, BW // NDEV).transpose(1, 0, 2).reshape(CONV_K, BW)
    sc_w = flat[:, sc_at:sc_at + sw].reshape(NDEV, SHORT_K, BW // NDEV).transpose(1, 0, 2).reshape(SHORT_K, BW)
    return wb, wo, conv_w, sc_w


def _layer_small_inputs(l, rep, conv_w, sc_w):
    vecs = jnp.stack([rep["pool_scale"][l], rep["conv_b"][l], rep["conv_ln_g"][l], rep["conv_ln_b"][l],
                      rep["sgu_ln_g"][l], rep["sgu_ln_b"][l], jnp.zeros((BW,), F32), jnp.zeros((BW,), F32)])
    conv_w = jnp.pad(conv_w, ((0, 32 - CONV_K), (0, 0)))
    sc_w = jnp.pad(sc_w, ((0, 8 - SHORT_K), (0, 0)))
    return (rep["pool_w"][l], vecs, conv_w, rep["sgu_w"][l], rep["sgu_b"][l].reshape(NG, GW, 1), sc_w)


def kernel(x, norm_g, w_in, pool_w, pool_scale, conv_w, conv_b, conv_ln_g, conv_ln_b, sgu_ln_g, sgu_ln_b, sgu_w, sgu_b, sc_w, w_branch, w_o, final_g, loss_target, m_norm_g, m_w_in, m_pool_w, m_pool_scale, m_conv_w, m_conv_b, m_conv_ln_g, m_conv_ln_b, m_sgu_ln_g, m_sgu_ln_b, m_sgu_w, m_sgu_b, m_sc_w, m_w_branch, m_w_o, m_final_g, v_norm_g, v_w_in, v_pool_w, v_pool_scale, v_conv_w, v_conv_b, v_conv_ln_g, v_conv_ln_b, v_sgu_ln_g, v_sgu_ln_b, v_sgu_w, v_sgu_b, v_sc_w, v_w_branch, v_w_o, v_final_g):
    args = dict(locals())
    w = {k: args[k] for k in SMALL + ("w_in", "w_branch", "w_o", "final_g")}
    mom = {k: args["m_" + k] for k in w}
    var = {k: args["v_" + k] for k in w}
    rep = {k: w[k] for k in REPLICATED}
    xi, yi, ci = _position()
    me = 4 * xi + 2 * yi + ci

    nb, seq, _ = x.shape
    n = nb * seq
    tm = min(512, n)
    tm_in = min(1024, n)
    tm_fwd = min(2048, n)
    tk = min(2048, n)
    t_fwd = min(256, seq)
    t_bwd = min(256, seq)

    w_in_b = [(w_in[l, :D // 2].astype(BF16), w_in[l, D // 2:].astype(BF16)) for l in range(DEPTH)]
    wb_b = [w_branch[l].astype(BF16).reshape(4 * BW, D // NDEV) for l in range(DEPTH)]
    wo_b = [w_o[l].astype(BF16) for l in range(DEPTH)]
    conv_b_ = [_pack([conv_w[l], sc_w[l]]) for l in range(DEPTH)]

    x2d = x.reshape(n, D)
    saved = []
    ride = _Exchange(gathers=[wb_b[0], wo_b[0], conv_b_[0]])
    order = jnp.bitwise_xor(me, jnp.array([0, 1, 4, 5, 2, 3, 6, 7], jnp.int32)).astype(jnp.int32)
    (proj, h), w_in_g0, (wb_g, wo_g, conv_g) = _in_proj_first(
        x2d, norm_g[0:1], w_in_b[0], order, tm_in, ride)
    w_in_g = [w_in_g0, None]
    for l in range(DEPTH):
        if l == 1:
            ride = _Exchange(gathers=[wb_b[1], wo_b[1], conv_b_[1]])
            (proj, h), (wb_g, wo_g, conv_g) = _in_proj(x2d, norm_g[l:l + 1], w_in_g[l], tm_fwd, ride)
        wb_l, wo_l, conv_l, sc_l = _gathered_weights(wb_g, wo_g, conv_g)
        small = _layer_small_inputs(l, rep, conv_l, sc_l)
        ride = _Exchange(chip_gathers=list(w_in_b[1])) if l == 0 else None
        head = (loss_target.reshape(n, D), final_g.reshape(1, D)) if l == DEPTH - 1 else None
        (z, y_conv, q_sc, pooled, merged, x_new, *head_out), gathered = _mixers_fwd(
            proj, x2d, wb_l, wo_l, *small, nb, seq, t_fwd if head is not None else min(2 * t_fwd, seq), ride, head)
        if l == 0:
            w_in_g[1] = gathered
        saved.append((x2d, proj, h, z, y_conv, q_sc, pooled, merged, small, wb_l, wo_l))
        x2d = x_new

    dx = x2d
    loss_acc, d_final_g = head_out

    grads = {k: [None] * DEPTH for k in SMALL}
    big = {"w_in": None, "w_branch": None, "w_o": None}
    big_w = {"w_in": (w_in, m_w_in, v_w_in, (DEPTH, D, SHARD), 256),
             "w_branch": (w_branch, m_w_branch, v_w_branch, (DEPTH, 4 * BW, D // NDEV), 512),
             "w_o": (w_o, m_w_o, v_w_o, (DEPTH, D // NDEV, D), 64)}

    def adamw(k, own, landed, l):
        wk, mk, vk, shape, tr = big_w[k]
        big[k] = _adamw_layer(own, landed, wk.reshape(shape), mk.reshape(shape), vk.reshape(shape), l, big[k], tr,
                              "adamw_" + k)

    for l in reversed(range(DEPTH)):
        x_in, proj, h, z, y_conv, q_sc, pooled, merged, small, wb_l, wo_l = saved[l]
        d_bo, d_proj, d_wo, d_wo_b, d_wb, d_wb_b = _merge_out_bwd(dx, z, proj, merged, wb_l, wo_l, tm)
        ride = _Exchange(scatters=[(d_wo_b.reshape(NDEV, D // NDEV, D), d_wo.reshape(NDEV, D // NDEV, D)),
                                   (d_wb_b.reshape(NDEV, 4 * BW, D // NDEV), d_wb.reshape(NDEV, 4 * BW, D // NDEV))])
        (d_proj, dvec, dconvw, dscw, dpoolw, dsguw, dsgub), (wo_landed, wo_own, wb_landed, wb_own) = _mixers_bwd(
            proj, d_bo, wb_l, y_conv, q_sc, pooled, d_proj, *small, nb, seq, t_bwd, ride)
        grads["pool_w"][l] = dpoolw
        grads["pool_scale"][l] = dvec[0]
        grads["conv_w"][l] = dconvw[:CONV_K]
        grads["conv_b"][l] = dvec[1]
        grads["conv_ln_g"][l] = dvec[2]
        grads["conv_ln_b"][l] = dvec[3]
        grads["sgu_ln_g"][l] = dvec[4]
        grads["sgu_ln_b"][l] = dvec[5]
        grads["sgu_w"][l] = dsguw
        grads["sgu_b"][l] = dsgub[:NG]
        grads["sc_w"][l] = dscw[:SHORT_K]
        if l == 0:
            grads["norm_g"][0] = jnp.zeros((D,), F32)
            part_pack = _pack([jnp.stack(grads[k]) for k in SMALL] + [d_final_g[0], loss_acc[0, 0:1]])
            ride = _Exchange(gathers=[part_pack])
        else:
            ride = None
        (d_w_in, d_w_in_b), gathered = _grad_w_in(h, d_proj, tk, ride)
        if l == 0:
            parts = gathered[0]
        ride = _Exchange(scatters=[(d_w_in_b, d_w_in)])
        (dx, d_norm_g), (in_landed, in_own) = _in_proj_bwd(d_proj, w_in_g[l], x_in, norm_g[l:l + 1], dx, tm_in, ride)
        adamw("w_o", wo_own, wo_landed, l)
        adamw("w_branch", wb_own, wb_landed, l)
        adamw("w_in", in_own, in_landed, l)
        if l == 1:
            grads["norm_g"][1] = d_norm_g[0]
    grad_x = dx.reshape(nb, seq, D)

    names = list(SMALL) + ["final_g", "loss"]
    shapes = [(DEPTH,) + SMALL_SHAPES[k] for k in SMALL] + [(D,), (1,)]
    whole = REPLICATED + ("final_g",)
    w_pack = _pack([w[k] if k in whole else jnp.zeros(s, F32) for k, s in zip(names, shapes)])
    m_pack = _pack([mom[k] if k in whole else jnp.zeros(s, F32) for k, s in zip(names, shapes)])
    v_pack = _pack([var[k] if k in whole else jnp.ones(s, F32) for k, s in zip(names, shapes)])
    rows = parts.shape[1]
    tr = max(r for r in range(8, 641, 8) if rows % r == 0)
    packs = _adamw_gathered(parts, w_pack, m_pack, v_pack, tr, "adamw_small")
    out_g, out_d, out_m, out_v = [dict(zip(names, _unpack(p, shapes))) for p in packs]
    loss = out_g["loss"][0]

    (tail,) = _all_gather([d_norm_g.reshape(8, 128)], "norm_g_all_gather")
    first = [a[0].reshape(8, 128) for a in (w["norm_g"], mom["norm_g"], var["norm_g"])]
    tails = _adamw_gathered(tail, *first, 8, "adamw_norm_g")
    for dst, t0 in zip((out_g, out_d, out_m, out_v), tails):
        dst["norm_g"] = jnp.concatenate([t0.reshape(1, D), dst["norm_g"][1:]])

    col = me * (BW // NDEV)
    for k in ("conv_w", "sc_w"):
        out_g[k] = lax.dynamic_slice_in_dim(out_g[k], col, BW // NDEV, axis=2)
    sharded_shapes = [w["conv_w"].shape, w["sc_w"].shape]
    d2, m2, v2 = _adamw_plain(_pack([out_g["conv_w"], out_g["sc_w"]]), _pack([w["conv_w"], w["sc_w"]]),
                              _pack([mom["conv_w"], mom["sc_w"]]), _pack([var["conv_w"], var["sc_w"]]))
    for dst, pk in ((out_d, d2), (out_m, m2), (out_v, v2)):
        dst["conv_w"], dst["sc_w"] = _unpack(pk, sharded_shapes)

    for k in big:
        out_g[k], out_d[k], out_m[k], out_v[k] = [o.reshape(w[k].shape) for o in big[k]]

    order = ("norm_g", "w_in", "pool_w", "pool_scale", "conv_w", "conv_b", "conv_ln_g", "conv_ln_b", "sgu_ln_g",
             "sgu_ln_b", "sgu_w", "sgu_b", "sc_w", "w_branch", "w_o", "final_g")
    return (loss, grad_x, *[out_g[k] for k in order], *[out_d[k] for k in order],
            *[out_m[k] for k in order], *[out_v[k] for k in order])
```

```python
import functools

import jax
import jax.numpy as jnp
from jax import lax
from jax.experimental import pallas as pl
from jax.experimental.pallas import tpu as pltpu

F32 = jnp.float32
BF16 = jnp.bfloat16
SDS = jax.ShapeDtypeStruct
MESH = pl.DeviceIdType.MESH

D = 1024
BW = 512
NG = 4
GW = 128
CONV_K = 31
SHORT_K = 3
POOL_WINDOWS = (2, 4, 8, 16)
N_PIECE_COLS = 12 * BW
IN_COLS = N_PIECE_COLS + 4 * D
NDEV = 8
SHARD = IN_COLS // NDEV
DEPTH = 2
RMS_EPS = 1e-6
LN_EPS = 1e-5
HALO_POOL, HALO_CONV, HALO_SC = 16, 32, 8
CONV_ROWS = 32

ADAM_LR = 0.001
ADAM_B1 = 0.9
ADAM_B2 = 0.999
ADAM_EPS = 1e-08
ADAM_WD = 0.01
ADAM_STEP = 10

VMEM_LIMIT = 62 * 1024 * 1024


def _params(*sem):
    return pltpu.CompilerParams(dimension_semantics=sem, vmem_limit_bytes=VMEM_LIMIT)


def _sigmoid(x):
    return jax.nn.sigmoid(x)


def _silu(x):
    return x * _sigmoid(x)


def _dsilu(x):
    s = _sigmoid(x)
    return s * (1.0 + x * (1.0 - s))


def _colsum(x):
    return jnp.sum(x, axis=0, keepdims=True)


def _dot(a, b):
    return jnp.dot(a, b, preferred_element_type=F32)


def _dot_nt(a, b):
    return lax.dot_general(a, b, (((1,), (1,)), ((), ())), preferred_element_type=F32)


def _dot_tn(a, b):
    return lax.dot_general(a, b, (((0,), (0,)), ((), ())), preferred_element_type=F32)


def _layer_norm_stats(y):
    mu = jnp.mean(y, axis=-1, keepdims=True)
    yc = y - mu
    var = jnp.mean(yc * yc, axis=-1, keepdims=True)
    rstd = lax.rsqrt(var + LN_EPS)
    return yc * rstd, rstd


def _layer_norm_bwd(d_hat, hat, rstd):
    return rstd * (d_hat - jnp.mean(d_hat, axis=-1, keepdims=True)
                   - hat * jnp.mean(d_hat * hat, axis=-1, keepdims=True))


def _position():
    return lax.axis_index("x"), lax.axis_index("y"), lax.axis_index("c")


def _flip(v, bit):
    return 1 - v if bit else v


class _Exchange:
    def __init__(self, gathers=(), scatters=(), chip_gathers=()):
        self.gathers = list(gathers)
        self.scatters = list(scatters)
        self.chip_gathers = list(chip_gathers)
        self.inputs = self.gathers + [a for pair in self.scatters for a in pair] + self.chip_gathers
        self.out_shapes = [SDS((NDEV,) + a.shape, a.dtype) for a in self.gathers]
        for send, keep in self.scatters:
            self.out_shapes += [SDS((NDEV - 1,) + send.shape[1:], send.dtype), SDS(keep.shape[1:], keep.dtype)]
        self.out_shapes += [SDS((NDEV,) + a.shape, a.dtype) for a in self.chip_gathers]
        n = len(self.gathers) + len(self.scatters) + len(self.chip_gathers)
        self.scratch = [pltpu.SemaphoreType.DMA((7 * n,)), pltpu.SemaphoreType.DMA((7 * n,)),
                        pltpu.SemaphoreType.DMA((n,))]

    def _copies(self, ins, outs, sems):
        send_sems, recv_sems, local_sems = sems
        x, y, c = _position()
        me = 4 * x + 2 * y + c
        ng = len(self.gathers)
        local, remote = [], []
        for a in range(ng + len(self.scatters)):
            if a < ng:
                local.append(pltpu.make_async_copy(ins[a], outs[a].at[me], local_sems.at[a]))
            else:
                s = a - ng
                send, keep = ins[ng + 2 * s], ins[ng + 2 * s + 1]
                landed, own = outs[ng + 2 * s], outs[ng + 2 * s + 1]
                local.append(pltpu.make_async_copy(keep.at[me], own, local_sems.at[a]))
            for r in range(1, NDEV):
                peer = (_flip(x, r & 4), _flip(y, r & 2), _flip(c, r & 1))
                if a < ng:
                    src, dst = ins[a], outs[a].at[me]
                else:
                    src, dst = send.at[jnp.bitwise_xor(me, r)], landed.at[r - 1]
                remote.append(pltpu.make_async_remote_copy(
                    src_ref=src, dst_ref=dst, send_sem=send_sems.at[7 * a + r - 1],
                    recv_sem=recv_sems.at[7 * a + r - 1], device_id=peer, device_id_type=MESH))
        return local, remote

    def _chip_copy(self, ins, outs, sems, t):
        send_sems, recv_sems, local_sems = sems
        x, y, c = _position()
        at = len(self.gathers) + 2 * len(self.scatters) + t
        a = len(self.gathers) + len(self.scatters) + t

        def copy(k, block, to, from_input=False):
            px, py, pc = block
            dst = outs[at].at[4 * px + 2 * py + pc]
            return pltpu.make_async_remote_copy(
                src_ref=ins[at] if from_input else dst, dst_ref=dst, send_sem=send_sems.at[7 * a + k],
                recv_sem=recv_sems.at[7 * a + k], device_id=to, device_id_type=MESH)

        return copy, pltpu.make_async_copy(ins[at], outs[at].at[4 * x + 2 * y + c], local_sems.at[a])

    def start(self, ins, outs, sems):
        local, remote = self._copies(ins, outs, sems)
        for cp in remote + local:
            cp.start()
        x, y, c = _position()
        for t in range(len(self.chip_gathers)):
            copy, mine = self._chip_copy(ins, outs, sems, t)
            mine.start()
            copy(0, (x, y, c), (x, y, 1 - c), from_input=True).start()
            for j, chip in enumerate([(1 - x, y), (x, 1 - y), (1 - x, 1 - y)]):
                copy(1 + j, (x, y, c), (*chip, c), from_input=True).start()

    def pass_on(self, ins, outs, sems):
        x, y, c = _position()
        for t in range(len(self.chip_gathers)):
            copy, _ = self._chip_copy(ins, outs, sems, t)
            for j, chip in enumerate([(1 - x, y), (x, 1 - y), (1 - x, 1 - y)]):
                copy(1 + j, (*chip, c), (x, y, c)).wait_recv()
                copy(4 + j, (*chip, c), (x, y, 1 - c)).start()

    def finish(self, ins, outs, sems):
        x, y, c = _position()
        me, sibling = (x, y, c), (x, y, 1 - c)
        chips = [(1 - x, y), (x, 1 - y), (1 - x, 1 - y)]
        local, remote = self._copies(ins, outs, sems)
        for cp in remote + local:
            cp.wait()
        for t in range(len(self.chip_gathers)):
            copy, mine = self._chip_copy(ins, outs, sems, t)
            copy(0, sibling, me).wait_recv()
            copy(0, me, sibling, from_input=True).wait_send()
            for j, chip in enumerate(chips):
                copy(4 + j, (*chip, 1 - c), me).wait_recv()
                copy(1 + j, me, (*chip, c), from_input=True).wait_send()
                copy(4 + j, (*chip, c), sibling).wait_send()
            mine.wait()


def _carried_call(body, comm, args, *, name, grid, in_specs, out_specs, out_shape, scratch_shapes=(),
                  semantics, input_output_aliases=None):
    aliases = input_output_aliases or {}
    if comm is None:
        out = pl.pallas_call(body, name=name, grid=grid, in_specs=in_specs, out_specs=out_specs, out_shape=out_shape,
                             scratch_shapes=list(scratch_shapes), input_output_aliases=aliases,
                             compiler_params=_params(*semantics))(*args)
        return list(out), []
    n_in, n_out, n_scr = len(in_specs), len(out_specs), len(scratch_shapes)
    c_in, c_out = len(comm.inputs), len(comm.out_shapes)

    def full_body(*refs):
        refs = list(refs)
        ins, refs = refs[:n_in], refs[n_in:]
        cins, refs = refs[:c_in], refs[c_in:]
        outs, refs = refs[:n_out], refs[n_out:]
        couts, refs = refs[:c_out], refs[c_out:]
        scr, sems = refs[:n_scr], refs[n_scr:]
        ids = [pl.program_id(d) for d in range(len(grid))]
        first = functools.reduce(jnp.logical_and, [i == 0 for i in ids])
        last = functools.reduce(jnp.logical_and, [i == g - 1 for i, g in zip(ids, grid)])

        @pl.when(first)
        def _():
            comm.start(cins, couts, sems)

        body(*ins, *outs, *scr)

        late = functools.reduce(jnp.logical_and, [ids[0] == grid[0] - 1] + [i == 0 for i in ids[1:]])

        @pl.when(late if len(grid) > 1 else last)
        def _():
            comm.pass_on(cins, couts, sems)

        @pl.when(last)
        def _():
            comm.finish(cins, couts, sems)

    any_spec = pl.BlockSpec(memory_space=pl.ANY)
    out = pl.pallas_call(
        full_body, name=name, grid=grid,
        in_specs=list(in_specs) + [any_spec] * c_in, out_specs=list(out_specs) + [any_spec] * c_out,
        out_shape=list(out_shape) + comm.out_shapes, scratch_shapes=list(scratch_shapes) + comm.scratch,
        input_output_aliases=aliases, compiler_params=_params(*["arbitrary"] * len(grid)),
    )(*args, *comm.inputs)
    return list(out[:n_out]), list(out[n_out:])


def _in_proj(x2d, g_row, w_g, tm, comm=None):
    n = x2d.shape[0]
    half = D // 2

    def body(x_ref, g_ref, wu_ref, wl_ref, proj_ref, h_ref):
        @pl.when(pl.program_id(1) == 0)
        def _():
            x = x_ref[...]
            r = lax.rsqrt(jnp.mean(x * x, axis=-1, keepdims=True) + RMS_EPS)
            h_ref[...] = (x * r * g_ref[...]).astype(BF16)

        proj_ref[...] = (_dot(h_ref[:, :half], wu_ref[...]) + _dot(h_ref[:, half:], wl_ref[...])).astype(BF16)

    return _carried_call(
        body, comm, (x2d, g_row, *w_g), name="in_proj", grid=(n // tm, NDEV),
        in_specs=[pl.BlockSpec((tm, D), lambda i, j: (i, 0)),
                  pl.BlockSpec((1, D), lambda i, j: (0, 0)),
                  pl.BlockSpec((None, half, SHARD), lambda i, j: (j, 0, 0)),
                  pl.BlockSpec((None, half, SHARD), lambda i, j: (j, 0, 0))],
        out_specs=[pl.BlockSpec((tm, SHARD), lambda i, j: (i, j)),
                   pl.BlockSpec((tm, D), lambda i, j: (i, 0))],
        out_shape=[SDS((n, IN_COLS), BF16), SDS((n, D), BF16)],
        semantics=("parallel", "arbitrary"))


def _in_proj_first(x2d, g_row, w_own, order, tm, comm):
    n = x2d.shape[0]
    ni = n // tm
    half = D // 2
    c_in, c_out = len(comm.inputs), len(comm.out_shapes)

    def body(order_ref, x_ref, g_ref, wu_hbm, wl_hbm, *refs):
        refs = list(refs)
        cins, refs = refs[:c_in], refs[c_in:]
        proj_ref, h_hbm, gu_hbm, gl_hbm = refs[:4]
        couts, refs = refs[4:4 + c_out], refs[4 + c_out:]
        h_all, wbuf, send_sems, recv_sems, own_sems, load_sems, h_sems = refs[:7]
        csems = refs[7:]
        jp = pl.program_id(0)
        i = pl.program_id(1)
        x, y, c = _position()
        me = 4 * x + 2 * y + c
        sibling = (x, y, 1 - c)
        chips = [(1 - x, y), (x, 1 - y), (1 - x, 1 - y)]
        own = (wu_hbm, wl_hbm)
        gathered = (gu_hbm, gl_hbm)

        def copy(a, k, block, to, from_input=False):
            px, py, pc = block
            dst = gathered[a].at[4 * px + 2 * py + pc]
            return pltpu.make_async_remote_copy(
                src_ref=own[a] if from_input else dst, dst_ref=dst,
                send_sem=send_sems.at[7 * a + k], recv_sem=recv_sems.at[7 * a + k],
                device_id=to, device_id_type=MESH)

        def load(a, slot, block_id):
            return pltpu.make_async_copy(gathered[a].at[block_id], wbuf.at[slot, a], load_sems.at[2 * slot + a])

        def keep(a):
            return pltpu.make_async_copy(own[a], gathered[a].at[me], own_sems.at[a])

        def h_store(t):
            rows = pl.ds(t * tm, tm)
            return pltpu.make_async_copy(h_all.at[rows], h_hbm.at[rows], h_sems.at[t])

        first = jnp.logical_and(jp == 0, i == 0)
        last = jnp.logical_and(jp == NDEV - 1, i == ni - 1)

        @pl.when(jnp.logical_and(jp == NDEV - 3, i == 0))
        def _():
            comm.start(cins, couts, csems)

        @pl.when(jnp.logical_and(jp == 2, i == 0))
        def _():
            for a in range(2):
                copy(a, 3, (x, y, c), (*chips[2], c), from_input=True).start()

        @pl.when(first)
        def _():
            for a in range(2):
                keep(a).start()
                copy(a, 0, (x, y, c), sibling, from_input=True).start()
                for j, chip in enumerate(chips[:2]):
                    copy(a, 1 + j, (x, y, c), (*chip, c), from_input=True).start()
                own_block = pltpu.make_async_copy(own[a], wbuf.at[0, a], load_sems.at[a])
                own_block.start()
                own_block.wait()

        for nxt in range(1, NDEV):
            @pl.when(jnp.logical_and(jp == nxt - 1, i == ni - 1))
            def _(nxt=nxt):
                j, passed = (nxt - 2) // 2, nxt % 2 == 1
                for a in range(2):
                    if nxt == 1:
                        copy(a, 0, sibling, (x, y, c)).wait_recv()
                    elif passed:
                        copy(a, 4 + j, (*chips[j], 1 - c), (x, y, c)).wait_recv()
                    else:
                        copy(a, 1 + j, (*chips[j], c), (x, y, c)).wait_recv()
                        copy(a, 4 + j, (*chips[j], c), sibling).start()
                    load(a, nxt % 2, order_ref[nxt]).start()

        @pl.when(jnp.logical_and(jp > 0, i == 0))
        def _():
            for a in range(2):
                load(a, jp % 2, order_ref[jp]).wait()

        rows = pl.ds(pl.multiple_of(i * tm, tm), tm)

        @pl.when(jp == 0)
        def _():
            xt = x_ref[...]
            r = lax.rsqrt(jnp.mean(xt * xt, axis=-1, keepdims=True) + RMS_EPS)
            h_all[rows, :] = (xt * r * g_ref[...]).astype(BF16)
            for t in range(ni):
                @pl.when(i == t)
                def _(t=t):
                    h_store(t).start()

        slot = jp % 2
        proj_ref[...] = (_dot(h_all[rows, :half], wbuf[slot, 0]) + _dot(h_all[rows, half:], wbuf[slot, 1])).astype(BF16)

        @pl.when(last)
        def _():
            for a in range(2):
                keep(a).wait()
                copy(a, 0, (x, y, c), sibling, from_input=True).wait_send()
                for j, chip in enumerate(chips):
                    copy(a, 1 + j, (x, y, c), (*chip, c), from_input=True).wait_send()
                    copy(a, 4 + j, (*chip, c), sibling).wait_send()
            for t in range(ni):
                h_store(t).wait()
            comm.finish(cins, couts, csems)

    any_spec = pl.BlockSpec(memory_space=pl.ANY)
    blocks = SDS((NDEV, half, SHARD), BF16)
    out = pl.pallas_call(
        body, name="in_proj_first",
        grid_spec=pltpu.PrefetchScalarGridSpec(
            num_scalar_prefetch=1, grid=(NDEV, ni),
            in_specs=[pl.BlockSpec((tm, D), lambda jp, i, order: (jnp.where(jp == 0, i, ni - 1), 0)),
                      pl.BlockSpec((1, D), lambda jp, i, order: (0, 0)),
                      any_spec, any_spec] + [any_spec] * c_in,
            out_specs=[pl.BlockSpec((tm, SHARD), lambda jp, i, order: (i, order[jp])),
                       any_spec, any_spec, any_spec] + [any_spec] * c_out,
            scratch_shapes=[pltpu.VMEM((n, D), BF16), pltpu.VMEM((2, 2, half, SHARD), BF16),
                            pltpu.SemaphoreType.DMA((14,)), pltpu.SemaphoreType.DMA((14,)),
                            pltpu.SemaphoreType.DMA((2,)), pltpu.SemaphoreType.DMA((4,)),
                            pltpu.SemaphoreType.DMA((ni,))] + comm.scratch),
        out_shape=[SDS((n, IN_COLS), BF16), SDS((n, D), BF16), blocks, blocks] + comm.out_shapes,
        compiler_params=_params("arbitrary", "arbitrary"),
    )(order, x2d, g_row, *w_own, *comm.inputs)
    return list(out[:2]), list(out[2:4]), list(out[4:])


def _tril_bf16(w):
    r = lax.broadcasted_iota(jnp.int32, (GW, GW), 0)
    c = lax.broadcasted_iota(jnp.int32, (GW, GW), 1)
    return jnp.where(r >= c, w, 0.0).astype(BF16)


def _mixers_fwd(proj, x2d, wb, wo, pool_w, vecs, conv_w, sgu_w, sgu_bc, sc_w, nb, seq, t, comm=None, head=None):
    n = nb * seq
    nt = seq // t

    def body(p_ref, x_ref, wb_ref, wo_ref, poolw_ref, vec_ref, convw_ref, sguw_ref, sgub_ref, scw_ref, *refs):
        if head is None:
            z_ref, y_ref, q_ref, pooled_ref, m_ref, xo_ref, pext, uext, vext = refs
        else:
            t_ref, fg_ref, z_ref, y_ref, q_ref, pooled_ref, m_ref, dx_ref, loss_ref, dg_ref, pext, uext, vext = refs
        i = pl.program_id(1)

        def gated(k):
            gate = _sigmoid(p_ref[:, N_PIECE_COLS + k * D:N_PIECE_COLS + (k + 1) * D].astype(F32))
            return gate * _dot(z_ref[:, k * BW:(k + 1) * BW], wb_ref[k])

        @pl.when(i == 0)
        def _():
            pext[0:HALO_POOL, :] = jnp.zeros((HALO_POOL, BW), F32)
            uext[0:HALO_CONV, :] = jnp.zeros((HALO_CONV, BW), F32)
            vext[0:HALO_SC, :] = jnp.zeros((HALO_SC, BW), F32)

        @pl.when(i > 0)
        def _():
            pext[0:HALO_POOL, :] = pext[t:t + HALO_POOL, :]
            uext[0:HALO_CONV, :] = uext[t:t + HALO_CONV, :]
            vext[0:HALO_SC, :] = vext[t:t + HALO_SC, :]

        def piece(k):
            return p_ref[:, k * BW:(k + 1) * BW].astype(F32)

        pos = lax.broadcasted_iota(jnp.int32, (t, 1), 0) + i * t

        px = piece(0)
        pext[HALO_POOL:, :] = px
        for j, win in enumerate(POOL_WINDOWS):
            cols = slice(j * GW, (j + 1) * GW)
            s = pext[:, cols]
            step = 1
            while step < win:
                s = s + pltpu.roll(s, step, 0)
                step *= 2
            cnt = jnp.minimum(pos + 1, win).astype(F32)
            pooled = (s[HALO_POOL:, :] / cnt - px[:, cols]).astype(BF16)
            pooled_ref[:, cols] = pooled
            mixed = _dot(pooled, poolw_ref[j].astype(BF16))
            z_ref[:, cols] = (mixed * vec_ref[0:1, cols] * _silu(p_ref[:, BW + j * GW:BW + (j + 1) * GW].astype(F32))).astype(BF16)
        merged = gated(0)

        uext[HALO_CONV:, :] = piece(2) * _sigmoid(piece(3))
        ue = uext[...]
        acc = jnp.zeros((t, BW), F32)
        for b in range(8):
            rb = ue if b == 0 else pltpu.roll(ue, b, 0)
            for a in range((CONV_K - b + 7) // 8):
                j = 8 * a + b
                acc = acc + convw_ref[CONV_K - 1 - j:CONV_K - j, :] * rb[HALO_CONV - 8 * a:HALO_CONV - 8 * a + t, :]
        y = acc + vec_ref[1:2, :]
        y_ref[...] = y
        yhat, _ = _layer_norm_stats(y)
        act = _silu(yhat * vec_ref[2:3, :] + vec_ref[3:4, :])
        z_ref[:, BW:2 * BW] = (act * _silu(piece(4))).astype(BF16)
        merged = merged + gated(1)

        vhat, _ = _layer_norm_stats(piece(6))
        vn = (vhat * vec_ref[4:5, :] + vec_ref[5:6, :]).astype(BF16)
        for g in range(NG):
            ws = _tril_bf16(sguw_ref[g])
            cols = slice(g * GW, (g + 1) * GW)
            for c in range(t // GW):
                rows = slice(c * GW, (c + 1) * GW)
                sp = _dot(ws, vn[rows, cols]) + sgub_ref[g]
                gate = _silu(p_ref[rows, 7 * BW + g * GW:7 * BW + (g + 1) * GW].astype(F32))
                z_ref[rows, 2 * BW + g * GW:2 * BW + (g + 1) * GW] = (
                    p_ref[rows, 5 * BW + g * GW:5 * BW + (g + 1) * GW].astype(F32) * sp * gate).astype(BF16)
        merged = merged + gated(2)

        vext[HALO_SC:, :] = piece(9) * piece(10)
        ve = vext[...]
        q = jnp.zeros((t, BW), F32)
        for j in range(SHORT_K):
            sh = ve if j == 0 else pltpu.roll(ve, j, 0)
            q = q + scw_ref[SHORT_K - 1 - j:SHORT_K - j, :] * sh[HALO_SC:, :]
        q_ref[...] = q
        z_ref[:, 3 * BW:] = (piece(8) * q * _silu(piece(11))).astype(BF16)
        mb = (merged + gated(3)).astype(BF16)
        m_ref[...] = mb
        xo = x_ref[...] + _dot(mb, wo_ref[...])
        if head is None:
            xo_ref[...] = xo
        else:
            @pl.when(jnp.logical_and(pl.program_id(0) == 0, i == 0))
            def _():
                loss_ref[...] = jnp.zeros_like(loss_ref)
                dg_ref[...] = jnp.zeros_like(dg_ref)

            g = fg_ref[...]
            r = lax.rsqrt(jnp.mean(xo * xo, axis=-1, keepdims=True) + RMS_EPS)
            xn = xo * r
            e = xn * g - t_ref[...]
            loss_ref[...] += 0.5 * jnp.sum(jnp.mean(e * e, axis=-1, keepdims=True))
            dy = e * (1.0 / D)
            dg_ref[...] += _colsum(dy * xn)
            gy = dy * g
            dx_ref[...] = r * gy - xn * (r * jnp.mean(gy * xn, axis=-1, keepdims=True))

    row = lambda b, i: (b * nt + i, 0)
    full2 = lambda b, i: (0, 0)
    full3 = lambda b, i: (0, 0, 0)
    tail_in = () if head is None else tuple(head)
    tail_in_specs = [] if head is None else [pl.BlockSpec((t, D), row), pl.BlockSpec((1, D), full2)]
    tail_out_specs = [] if head is None else [pl.BlockSpec((8, 128), full2), pl.BlockSpec((1, D), full2)]
    tail_out_shape = [] if head is None else [SDS((8, 128), F32), SDS((1, D), F32)]
    return _carried_call(
        body, comm, (proj, x2d, wb, wo, pool_w, vecs, conv_w, sgu_w, sgu_bc, sc_w) + tail_in,
        name="mixers_fwd" if head is None else "mixers_fwd_loss", grid=(nb, nt),
        in_specs=[pl.BlockSpec((t, IN_COLS), row),
                  pl.BlockSpec((t, D), row),
                  pl.BlockSpec((4, BW, D), full3),
                  pl.BlockSpec((D, D), full2),
                  pl.BlockSpec((NG, GW, GW), full3),
                  pl.BlockSpec((8, BW), full2),
                  pl.BlockSpec((32, BW), full2),
                  pl.BlockSpec((NG, GW, GW), full3),
                  pl.BlockSpec((NG, GW, 1), full3),
                  pl.BlockSpec((8, BW), full2)] + tail_in_specs,
        out_specs=[pl.BlockSpec((t, 4 * BW), row), pl.BlockSpec((t, BW), row),
                   pl.BlockSpec((t, BW), row), pl.BlockSpec((t, BW), row),
                   pl.BlockSpec((t, D), row), pl.BlockSpec((t, D), row)] + tail_out_specs,
        out_shape=[SDS((n, 4 * BW), BF16), SDS((n, BW), F32), SDS((n, BW), F32), SDS((n, BW), BF16),
                   SDS((n, D), BF16), SDS((n, D), F32)] + tail_out_shape,
        scratch_shapes=[pltpu.VMEM((t + HALO_POOL, BW), F32), pltpu.VMEM((t + HALO_CONV, BW), F32),
                        pltpu.VMEM((t + HALO_SC, BW), F32)],
        semantics=("arbitrary", "arbitrary"))


def _merge_out_bwd(dxo, z, proj, merged, wb, wo, tm):
    n = dxo.shape[0]
    ni = n // tm
    sub = D // NDEV

    def body(dx_ref, z_ref, mg_ref, m_ref, wb_ref, wo_ref,
             dbo_ref, dp_ref, gwo_ref, gwo_b_ref, gwb_ref, gwb_b_ref,
             dm_ref, acc_wo, acc_wb, half_wo, half_wb, sems):
        i = pl.program_id(0)
        h = pl.program_id(1)

        @pl.when(jnp.logical_and(i == 0, h == 0))
        def _():
            acc_wo[...] = jnp.zeros_like(acc_wo)
            acc_wb[...] = jnp.zeros_like(acc_wb)

        @pl.when(h == 0)
        def _():
            dxb = dx_ref[...].astype(BF16)
            dm_ref[...] = _dot_nt(dxb, wo_ref[...])
            acc_wo[...] += _dot_tn(m_ref[...], dxb)

        dm = dm_ref[...]
        for k in range(2):
            zk = z_ref[:, k * BW:(k + 1) * BW]
            w = wb_ref[2 * h + k]
            bo = _dot(zk, w)
            sig = _sigmoid(mg_ref[:, k * D:(k + 1) * D].astype(F32))
            dbo = (dm * sig).astype(BF16)
            dp_ref[:, k * D:(k + 1) * D] = (dm * bo * sig * (1.0 - sig)).astype(BF16)
            dbo_ref[:, k * D:(k + 1) * D] = dbo
            acc_wb[2 * h + k] += _dot_tn(zk, dbo)

        @pl.when(jnp.logical_and(i == ni - 1, h == 1))
        def _():
            half_wo[...] = acc_wo[...].astype(BF16)
            half_wb[...] = acc_wb[...].astype(BF16)
            copies = [pltpu.make_async_copy(acc_wo, gwo_ref, sems.at[0]),
                      pltpu.make_async_copy(half_wo, gwo_b_ref, sems.at[1])]
            for d in range(NDEV):
                cols = pl.ds(d * sub, sub)
                copies.append(pltpu.make_async_copy(acc_wb.at[:, :, cols], gwb_ref.at[d], sems.at[2 + 2 * d]))
                copies.append(pltpu.make_async_copy(half_wb.at[:, :, cols], gwb_b_ref.at[d], sems.at[3 + 2 * d]))
            for cp in copies:
                cp.start()
            for cp in copies:
                cp.wait()

    any_spec = pl.BlockSpec(memory_space=pl.ANY)
    return pl.pallas_call(
        body, name="merge_out_bwd", grid=(ni, 2),
        in_specs=[pl.BlockSpec((tm, D), lambda i, h: (i, 0)),
                  pl.BlockSpec((tm, 2 * BW), lambda i, h: (i, h)),
                  pl.BlockSpec((tm, 2 * D), lambda i, h: (i, 3 + h)),
                  pl.BlockSpec((tm, D), lambda i, h: (i, 0)),
                  pl.BlockSpec((4, BW, D), lambda i, h: (0, 0, 0)),
                  pl.BlockSpec((D, D), lambda i, h: (0, 0))],
        out_specs=[pl.BlockSpec((tm, 2 * D), lambda i, h: (i, h)),
                   pl.BlockSpec((tm, 2 * D), lambda i, h: (i, 3 + h)),
                   any_spec, any_spec, any_spec, any_spec],
        out_shape=[SDS((n, 4 * D), BF16), SDS((n, IN_COLS), BF16), SDS((D, D), F32), SDS((D, D), BF16),
                   SDS((NDEV, 4, BW, sub), F32), SDS((NDEV, 4, BW, sub), BF16)],
        scratch_shapes=[pltpu.VMEM((tm, D), F32), pltpu.VMEM((D, D), F32), pltpu.VMEM((4, BW, D), F32),
                        pltpu.VMEM((D, D), BF16), pltpu.VMEM((4, BW, D), BF16),
                        pltpu.SemaphoreType.DMA((2 + 2 * NDEV,))],
        compiler_params=_params("arbitrary", "arbitrary"),
    )(dxo, z, proj, merged, wb, wo)


def _grad_w_in(h, d_proj, tk, comm=None):
    nk = h.shape[0] // tk

    def body(h_ref, dp_ref, o_ref, ob_ref):
        k = pl.program_id(1)

        @pl.when(k == 0)
        def _():
            o_ref[...] = jnp.zeros_like(o_ref)

        o_ref[...] += _dot_tn(h_ref[...], dp_ref[...])

        @pl.when(k == nk - 1)
        def _():
            ob_ref[...] = o_ref[...].astype(BF16)

    out_spec = pl.BlockSpec((None, D, SHARD), lambda j, k: (j, 0, 0))
    return _carried_call(
        body, comm, (h, d_proj), name="grad_w_in", grid=(NDEV, nk),
        in_specs=[pl.BlockSpec((tk, D), lambda j, k: (k, 0)), pl.BlockSpec((tk, SHARD), lambda j, k: (k, j))],
        out_specs=[out_spec, out_spec], out_shape=[SDS((NDEV, D, SHARD), F32), SDS((NDEV, D, SHARD), BF16)],
        semantics=("parallel", "arbitrary"))


def _mixers_bwd(proj, d_bo, wb, y_conv, q_sc, pooled, d_proj, pool_w, vecs, conv_w, sgu_w, sgu_bc, sc_w, nb, seq, t, comm=None):
    n = nb * seq
    nt = seq // t

    def body(p_ref, dbo_ref, wb_ref, y_ref, q_ref, pooled_ref, dpin_ref,
             poolw_ref, vec_ref, convw_ref, sguw_ref, sgub_ref, scw_ref,
             dp_ref, dvec_ref, dconvw_ref, dscw_ref, dpoolw_ref, dsguw_ref, dsgub_ref,
             rext, dyext, dqext, bacc, dyrot, dwacc):
        b = pl.program_id(0)
        i = pl.program_id(1)
        first = jnp.logical_and(b == 0, i == 0)
        last = jnp.logical_and(b == nb - 1, i == nt - 1)

        @pl.when(first)
        def _():
            dvec_ref[...] = jnp.zeros_like(dvec_ref)
            dconvw_ref[...] = jnp.zeros_like(dconvw_ref)
            dscw_ref[...] = jnp.zeros_like(dscw_ref)
            dpoolw_ref[...] = jnp.zeros_like(dpoolw_ref)
            dsguw_ref[...] = jnp.zeros_like(dsguw_ref)
            dsgub_ref[...] = jnp.zeros_like(dsgub_ref)
            bacc[...] = jnp.zeros_like(bacc)
            dwacc[...] = jnp.zeros_like(dwacc)

        @pl.when(i == 0)
        def _():
            rext[t:, :] = jnp.zeros((HALO_POOL, BW), F32)
            dyext[t:, :] = jnp.zeros((HALO_CONV, BW), F32)
            dqext[t:, :] = jnp.zeros((HALO_SC, BW), F32)

        @pl.when(i > 0)
        def _():
            rext[t:, :] = rext[0:HALO_POOL, :]
            dyext[t:, :] = dyext[0:HALO_CONV, :]
            dqext[t:, :] = dqext[0:HALO_SC, :]

        def piece(k):
            return p_ref[:, k * BW:(k + 1) * BW].astype(F32)

        def put(k, v):
            dp_ref[:, k * BW:(k + 1) * BW] = v.astype(BF16)

        def dz(k):
            return _dot_nt(dbo_ref[:, k * D:(k + 1) * D], wb_ref[k])

        pos = lax.broadcasted_iota(jnp.int32, (t, 1), 0) + (nt - 1 - i) * t

        dzp = dz(0)
        pg = piece(1)
        d_pm = dzp * _silu(pg)
        for j, win in enumerate(POOL_WINDOWS):
            cols = slice(j * GW, (j + 1) * GW)
            pw = poolw_ref[j].astype(BF16)
            pooled = pooled_ref[:, cols]
            mixed = _dot(pooled, pw)
            scale = vec_ref[0:1, cols]
            dvec_ref[0:1, cols] += _colsum(d_pm[:, cols] * mixed)
            dp_ref[:, BW + j * GW:BW + (j + 1) * GW] = (dzp[:, cols] * (mixed * scale) * _dsilu(pg[:, cols])).astype(BF16)
            d_mixed = (d_pm[:, cols] * scale).astype(BF16)
            dpoolw_ref[j] += _dot_tn(pooled, d_mixed)
            d_pooled = _dot_nt(d_mixed, pw)
            cnt = jnp.minimum(pos + 1, win).astype(F32)
            rext[0:t, cols] = d_pooled / cnt
            s = rext[:, cols]
            step = 1
            while step < win:
                s = s + pltpu.roll(s, t + HALO_POOL - step, 0)
                step *= 2
            dp_ref[:, cols] = (s[0:t, :] - d_pooled).astype(BF16)

        dzc = dz(1)
        cgate = piece(4)
        yhat, rstd = _layer_norm_stats(y_ref[...])
        ln = yhat * vec_ref[2:3, :] + vec_ref[3:4, :]
        put(4, dzc * _silu(ln) * _dsilu(cgate))
        d_ln = dzc * _silu(cgate) * _dsilu(ln)
        dvec_ref[2:3, :] += _colsum(d_ln * yhat)
        dvec_ref[3:4, :] += _colsum(d_ln)
        dy = _layer_norm_bwd(d_ln * vec_ref[2:3, :], yhat, rstd)
        dvec_ref[1:2, :] += _colsum(dy)
        dyext[0:t, :] = dy
        dye = dyext[...]
        for b in range(1, 8):
            dyrot[b - 1] = pltpu.roll(dye, t + HALO_CONV - b, 0)

        def conv_rows(c, carry):
            r0 = pl.multiple_of(c * CONV_ROWS, CONV_ROWS)
            rows = pl.ds(r0, CONV_ROWS)
            sig_cb = _sigmoid(p_ref[rows, 3 * BW:4 * BW].astype(F32))
            u = p_ref[rows, 2 * BW:3 * BW].astype(F32) * sig_cb
            du = jnp.zeros((CONV_ROWS, BW), F32)
            for b in range(8):
                for a in range((CONV_K - b + 7) // 8):
                    j = 8 * a + b
                    at = pl.ds(pl.multiple_of(r0 + 8 * a, 8), CONV_ROWS)
                    sh = dyext[at, :] if b == 0 else dyrot[b - 1, at, :]
                    du = du + convw_ref[CONV_K - 1 - j:CONV_K - j, :] * sh
                    dwacc[CONV_K - 1 - j] += jnp.sum((u * sh).reshape(CONV_ROWS // 8, 8, BW), axis=0)
            dp_ref[rows, 2 * BW:3 * BW] = (du * sig_cb).astype(BF16)
            dp_ref[rows, 3 * BW:4 * BW] = (du * u * (1.0 - sig_cb)).astype(BF16)
            return carry

        lax.fori_loop(0, t // CONV_ROWS, conv_rows, 0)

        dzg = dz(2)
        gu = piece(5)
        ggate = piece(7)
        vhat, vrstd = _layer_norm_stats(piece(6))
        vn = (vhat * vec_ref[4:5, :] + vec_ref[5:6, :]).astype(BF16)
        t1 = dzg * _silu(ggate)
        d_sp_all = t1 * gu
        d_vn_cols = []
        sp_cols = []
        for g in range(NG):
            ws = _tril_bf16(sguw_ref[g])
            cols = slice(g * GW, (g + 1) * GW)
            d_vn_rows = []
            sp_rows = []
            for c in range(t // GW):
                rows = slice(c * GW, (c + 1) * GW)
                sp_rows.append(_dot(ws, vn[rows, cols]) + sgub_ref[g])
                d_sp = d_sp_all[rows, cols]
                bacc[g] += d_sp
                d_spb = d_sp.astype(BF16)
                d_vn_rows.append(_dot_tn(ws, d_spb))
                dsguw_ref[g] += _dot_nt(d_spb, vn[rows, cols])
            d_vn_cols.append(jnp.concatenate(d_vn_rows, axis=0))
            sp_cols.append(jnp.concatenate(sp_rows, axis=0))
        d_vn = jnp.concatenate(d_vn_cols, axis=1)
        sp = jnp.concatenate(sp_cols, axis=1)
        put(5, t1 * sp)
        put(7, dzg * gu * sp * _dsilu(ggate))
        dvec_ref[4:5, :] += _colsum(d_vn * vhat)
        dvec_ref[5:6, :] += _colsum(d_vn)
        put(6, _layer_norm_bwd(d_vn * vec_ref[4:5, :], vhat, vrstd))

        dzs = dz(3)
        sb = piece(8)
        scp = piece(9)
        sx = piece(10)
        sgate = piece(11)
        q = q_ref[...]
        t2 = dzs * _silu(sgate)
        put(8, t2 * q)
        put(11, dzs * sb * q * _dsilu(sgate))
        dqext[0:t, :] = t2 * sb
        dqe = dqext[...]
        vv = scp * sx
        dv = jnp.zeros((t, BW), F32)
        for j in range(SHORT_K):
            sh = (dqe if j == 0 else pltpu.roll(dqe, t + HALO_SC - j, 0))[0:t, :]
            dv = dv + scw_ref[SHORT_K - 1 - j:SHORT_K - j, :] * sh
            dscw_ref[SHORT_K - 1 - j:SHORT_K - j, :] += _colsum(vv * sh)
        put(9, dv * sx)
        put(10, dv * scp)

        @pl.when(last)
        def _():
            r = lax.broadcasted_iota(jnp.int32, (GW, GW), 0)
            c = lax.broadcasted_iota(jnp.int32, (GW, GW), 1)
            for g in range(NG):
                dsguw_ref[g] = jnp.where(r >= c, dsguw_ref[g], 0.0)
                dsgub_ref[g:g + 1, :] = _colsum(bacc[g].T)
            dconvw_ref[...] = jnp.sum(dwacc[...], axis=1)

    row = lambda b, i: (b * nt + (nt - 1 - i), 0)
    full2 = lambda b, i: (0, 0)
    full3 = lambda b, i: (0, 0, 0)
    return _carried_call(
        body, comm, (proj, d_bo, wb, y_conv, q_sc, pooled, d_proj, pool_w, vecs, conv_w, sgu_w, sgu_bc, sc_w),
        name="mixers_bwd", grid=(nb, nt),
        in_specs=[pl.BlockSpec((t, N_PIECE_COLS), row),
                  pl.BlockSpec((t, 4 * D), row),
                  pl.BlockSpec((4, BW, D), full3),
                  pl.BlockSpec((t, BW), row), pl.BlockSpec((t, BW), row), pl.BlockSpec((t, BW), row),
                  pl.BlockSpec(memory_space=pl.ANY),
                  pl.BlockSpec((NG, GW, GW), full3),
                  pl.BlockSpec((8, BW), full2),
                  pl.BlockSpec((32, BW), full2),
                  pl.BlockSpec((NG, GW, GW), full3),
                  pl.BlockSpec((NG, GW, 1), full3),
                  pl.BlockSpec((8, BW), full2)],
        out_specs=[pl.BlockSpec((t, N_PIECE_COLS), row),
                   pl.BlockSpec((8, BW), full2), pl.BlockSpec((32, BW), full2), pl.BlockSpec((8, BW), full2),
                   pl.BlockSpec((NG, GW, GW), full3), pl.BlockSpec((NG, GW, GW), full3),
                   pl.BlockSpec((8, GW), full2)],
        out_shape=[SDS((n, IN_COLS), BF16), SDS((8, BW), F32), SDS((32, BW), F32), SDS((8, BW), F32),
                   SDS((NG, GW, GW), F32), SDS((NG, GW, GW), F32), SDS((8, GW), F32)],
        scratch_shapes=[pltpu.VMEM((t + HALO_POOL, BW), F32), pltpu.VMEM((t + HALO_CONV, BW), F32),
                        pltpu.VMEM((t + HALO_SC, BW), F32), pltpu.VMEM((NG, GW, GW), F32),
                        pltpu.VMEM((7, t + HALO_CONV, BW), F32), pltpu.VMEM((32, 8, BW), F32)],
        input_output_aliases={6: 0}, semantics=("arbitrary", "arbitrary"))


def _in_proj_bwd(d_proj, w_g, x2d, g_row, dxo, tm, comm=None):
    n = x2d.shape[0]
    half = D // 2

    def body(dp_ref, wu_ref, wl_ref, x_hbm, g_ref, dxo_hbm, dx_ref, dg_ref, acc, x_buf, dxo_buf, sems):
        i = pl.program_id(0)
        k = pl.program_id(1)
        rows = pl.ds(pl.multiple_of(i * tm, tm), tm)
        late = (pltpu.make_async_copy(x_hbm.at[rows], x_buf, sems.at[0]),
                pltpu.make_async_copy(dxo_hbm.at[rows], dxo_buf, sems.at[1]))

        @pl.when(jnp.logical_and(i == 0, k == 0))
        def _():
            dg_ref[...] = jnp.zeros_like(dg_ref)

        @pl.when(k == 0)
        def _():
            for cp in late:
                cp.start()
            acc[...] = jnp.zeros_like(acc)

        dp0 = dp_ref[:, :SHARD]
        dp1 = dp_ref[:, SHARD:]
        acc[:, :half] += _dot_nt(dp0, wu_ref[0]) + _dot_nt(dp1, wu_ref[1])
        acc[:, half:] += _dot_nt(dp0, wl_ref[0]) + _dot_nt(dp1, wl_ref[1])

        @pl.when(k == NDEV // 2 - 1)
        def _():
            for cp in late:
                cp.wait()
            dh = acc[...]
            x = x_buf[...]
            r = lax.rsqrt(jnp.mean(x * x, axis=-1, keepdims=True) + RMS_EPS)
            xn = x * r
            dg_ref[...] += _colsum(dh * xn)
            gy = dh * g_ref[...]
            dx_ref[...] = dxo_buf[...] + r * gy - xn * (r * jnp.mean(gy * xn, axis=-1, keepdims=True))

    any_spec = pl.BlockSpec(memory_space=pl.ANY)
    return _carried_call(
        body, comm, (d_proj, *w_g, x2d, g_row, dxo), name="in_proj_bwd", grid=(n // tm, NDEV // 2),
        in_specs=[pl.BlockSpec((tm, 2 * SHARD), lambda i, k: (i, k)),
                  pl.BlockSpec((2, half, SHARD), lambda i, k: (k, 0, 0)),
                  pl.BlockSpec((2, half, SHARD), lambda i, k: (k, 0, 0)),
                  any_spec,
                  pl.BlockSpec((1, D), lambda i, k: (0, 0)),
                  any_spec],
        out_specs=[pl.BlockSpec((tm, D), lambda i, k: (i, 0)), pl.BlockSpec((1, D), lambda i, k: (0, 0))],
        out_shape=[SDS((n, D), F32), SDS((1, D), F32)],
        scratch_shapes=[pltpu.VMEM((tm, D), F32), pltpu.VMEM((tm, D), F32), pltpu.VMEM((tm, D), F32),
                        pltpu.SemaphoreType.DMA((2,))],
        semantics=("arbitrary", "arbitrary"))


def _all_gather(arrs, name):
    na = len(arrs)

    def body(*refs):
        ins, outs = refs[:na], refs[na:2 * na]
        send_sems, recv_sems, local_sems = refs[2 * na:]
        x, y, c = _position()
        me, sibling = (x, y, c), (x, y, 1 - c)
        chips = [(1 - x, y), (x, 1 - y), (1 - x, 1 - y)]

        def copy(a, k, block, to, from_input=False):
            px, py, pc = block
            dst = outs[a].at[4 * px + 2 * py + pc]
            return pltpu.make_async_remote_copy(
                src_ref=ins[a] if from_input else dst, dst_ref=dst,
                send_sem=send_sems.at[7 * a + k], recv_sem=recv_sems.at[7 * a + k],
                device_id=to, device_id_type=MESH)

        mine = [pltpu.make_async_copy(ins[a], outs[a].at[4 * x + 2 * y + c], local_sems.at[a]) for a in range(na)]
        for cp in mine:
            cp.start()
        first = []
        for a in range(na):
            first.append(copy(a, 0, me, sibling, from_input=True))
            first += [copy(a, 1 + j, me, (*chip, c), from_input=True) for j, chip in enumerate(chips)]
        for cp in first:
            cp.start()
        passed = []
        for j, chip in enumerate(chips):
            for a in range(na):
                copy(a, 1 + j, (*chip, c), me).wait_recv()
                fwd = copy(a, 4 + j, (*chip, c), sibling)
                fwd.start()
                passed.append(fwd)
        for a in range(na):
            copy(a, 0, sibling, me).wait_recv()
            for j, chip in enumerate(chips):
                copy(a, 4 + j, (*chip, 1 - c), me).wait_recv()
        for cp in first + passed:
            cp.wait_send()
        for cp in mine:
            cp.wait()

    any_spec = pl.BlockSpec(memory_space=pl.ANY)
    return pl.pallas_call(
        body, name=name,
        in_specs=[any_spec] * na, out_specs=[any_spec] * na,
        out_shape=[SDS((NDEV,) + a.shape, a.dtype) for a in arrs],
        scratch_shapes=[pltpu.SemaphoreType.DMA((7 * na,)), pltpu.SemaphoreType.DMA((7 * na,)),
                        pltpu.SemaphoreType.DMA((na,))],
    )(*arrs)


def _adamw_math(g, w, m, v):
    m = ADAM_B1 * m + (1.0 - ADAM_B1) * g
    v = ADAM_B2 * v + (1.0 - ADAM_B2) * (g * g)
    m_hat = m / (1.0 - ADAM_B1 ** ADAM_STEP)
    v_hat = v / (1.0 - ADAM_B2 ** ADAM_STEP)
    delta = -ADAM_LR * (m_hat / (jnp.sqrt(v_hat) + ADAM_EPS) + ADAM_WD * w)
    return delta, m, v


def _adamw_layer(own, landed, w, m, v, layer, earlier, tr, name):
    _, rows, cols = w.shape

    def body(o_ref, l_ref, w_ref, m_ref, v_ref, *rest):
        g_out, d_out, m_out, v_out = rest[-4:]
        g = o_ref[...]
        for r in range(NDEV - 1):
            g = g + l_ref[r].astype(F32)
        g_out[...] = g
        d_out[...], m_out[...], v_out[...] = _adamw_math(g, w_ref[...], m_ref[...], v_ref[...])

    lay = pl.BlockSpec((None, tr, cols), lambda i: (layer, i, 0))
    any_spec = pl.BlockSpec(memory_space=pl.ANY)
    prior = list(earlier) if earlier is not None else []
    return pl.pallas_call(
        body, name=name, grid=(rows // tr,),
        in_specs=[pl.BlockSpec((tr, cols), lambda i: (i, 0)),
                  pl.BlockSpec((NDEV - 1, tr, cols), lambda i: (0, i, 0)), lay, lay, lay] + [any_spec] * len(prior),
        out_specs=[lay] * 4, out_shape=[SDS(w.shape, F32)] * 4,
        input_output_aliases={5 + k: k for k in range(len(prior))},
        compiler_params=_params("parallel"),
    )(own, landed, w, m, v, *prior)


def _adamw_gathered(parts, w, m, v, tr, name):
    rows, cols = w.shape

    def body(p_ref, w_ref, m_ref, v_ref, g_out, d_out, m_out, v_out):
        g = p_ref[0]
        for k in range(1, NDEV):
            g = g + p_ref[k]
        g_out[...] = g
        d_out[...], m_out[...], v_out[...] = _adamw_math(g, w_ref[...], m_ref[...], v_ref[...])

    blk = pl.BlockSpec((tr, cols), lambda i: (i, 0))
    return pl.pallas_call(
        body, name=name, grid=(rows // tr,),
        in_specs=[pl.BlockSpec((NDEV, tr, cols), lambda i: (0, i, 0)), blk, blk, blk],
        out_specs=[blk] * 4, out_shape=[SDS((rows, cols), F32)] * 4,
        compiler_params=_params("parallel"),
    )(parts, w, m, v)


def _adamw_plain(g, w, m, v):
    def body(g_ref, w_ref, m_ref, v_ref, d_out, m_out, v_out):
        d_out[...], m_out[...], v_out[...] = _adamw_math(g_ref[...], w_ref[...], m_ref[...], v_ref[...])

    return pl.pallas_call(body, name="adamw_conv_shards", out_shape=[SDS(w.shape, F32)] * 3)(g, w, m, v)


SMALL = ("norm_g", "pool_w", "pool_scale", "conv_w", "conv_b", "conv_ln_g", "conv_ln_b",
         "sgu_ln_g", "sgu_ln_b", "sgu_w", "sgu_b", "sc_w")
SMALL_SHAPES = {"norm_g": (D,), "pool_w": (NG, GW, GW), "pool_scale": (BW,), "conv_w": (CONV_K, BW), "conv_b": (BW,),
                "conv_ln_g": (BW,), "conv_ln_b": (BW,), "sgu_ln_g": (BW,), "sgu_ln_b": (BW,),
                "sgu_w": (NG, GW, GW), "sgu_b": (NG, GW), "sc_w": (SHORT_K, BW)}
REPLICATED = tuple(k for k in SMALL if k not in ("conv_w", "sc_w"))
PACK_UNIT = 8 * 128


def _size(shape):
    out = 1
    for s in shape:
        out *= s
    return out


def _padded(a):
    flat = a.reshape(-1)
    pad = -flat.shape[0] % PACK_UNIT
    return jnp.pad(flat, (0, pad)) if pad else flat


def _pack(arrays):
    return jnp.concatenate([_padded(a) for a in arrays]).reshape(-1, 128)


def _unpack(pack, shapes):
    flat = pack.reshape(-1)
    out, off = [], 0
    for shape in shapes:
        size = _size(shape)
        out.append(flat[off:off + size].reshape(shape))
        off += size + (-size % PACK_UNIT)
    return out


def _gathered_weights(wb, wo, conv):
    wb = wb.reshape(NDEV, 4, BW, D // NDEV).transpose(1, 2, 0, 3).reshape(4, BW, D)
    wo = wo.reshape(D, D)
    cw = CONV_K * (BW // NDEV)
    sw = SHORT_K * (BW // NDEV)
    sc_at = cw + (-cw % PACK_UNIT)
    flat = conv.reshape(NDEV, -1)
    conv_w = flat[:, :cw].reshape(NDEV, CONV_K, BW // NDEV).transpose(1, 0, 2).reshape(CONV_K, BW)
    sc_w = flat[:, sc_at:sc_at + sw].reshape(NDEV, SHORT_K, BW // NDEV).transpose(1, 0, 2).reshape(SHORT_K, BW)
    return wb, wo, conv_w, sc_w


def _layer_small_inputs(l, rep, conv_w, sc_w):
    vecs = jnp.stack([rep["pool_scale"][l], rep["conv_b"][l], rep["conv_ln_g"][l], rep["conv_ln_b"][l],
                      rep["sgu_ln_g"][l], rep["sgu_ln_b"][l], jnp.zeros((BW,), F32), jnp.zeros((BW,), F32)])
    conv_w = jnp.pad(conv_w, ((0, 32 - CONV_K), (0, 0)))
    sc_w = jnp.pad(sc_w, ((0, 8 - SHORT_K), (0, 0)))
    return (rep["pool_w"][l], vecs, conv_w, rep["sgu_w"][l], rep["sgu_b"][l].reshape(NG, GW, 1), sc_w)


def kernel(x, norm_g, w_in, pool_w, pool_scale, conv_w, conv_b, conv_ln_g, conv_ln_b, sgu_ln_g, sgu_ln_b, sgu_w, sgu_b, sc_w, w_branch, w_o, final_g, loss_target, m_norm_g, m_w_in, m_pool_w, m_pool_scale, m_conv_w, m_conv_b, m_conv_ln_g, m_conv_ln_b, m_sgu_ln_g, m_sgu_ln_b, m_sgu_w, m_sgu_b, m_sc_w, m_w_branch, m_w_o, m_final_g, v_norm_g, v_w_in, v_pool_w, v_pool_scale, v_conv_w, v_conv_b, v_conv_ln_g, v_conv_ln_b, v_sgu_ln_g, v_sgu_ln_b, v_sgu_w, v_sgu_b, v_sc_w, v_w_branch, v_w_o, v_final_g):
    args = dict(locals())
    w = {k: args[k] for k in SMALL + ("w_in", "w_branch", "w_o", "final_g")}
    mom = {k: args["m_" + k] for k in w}
    var = {k: args["v_" + k] for k in w}
    rep = {k: w[k] for k in REPLICATED}
    xi, yi, ci = _position()
    me = 4 * xi + 2 * yi + ci

    nb, seq, _ = x.shape
    n = nb * seq
    tm = min(512, n)
    tm_in = min(1024, n)
    tm_fwd = min(2048, n)
    tk = min(2048, n)
    t_fwd = min(256, seq)
    t_bwd = min(512, seq)

    w_in_b = [(w_in[l, :D // 2].astype(BF16), w_in[l, D // 2:].astype(BF16)) for l in range(DEPTH)]
    wb_b = [w_branch[l].astype(BF16).reshape(4 * BW, D // NDEV) for l in range(DEPTH)]
    wo_b = [w_o[l].astype(BF16) for l in range(DEPTH)]
    conv_b_ = [_pack([conv_w[l], sc_w[l]]) for l in range(DEPTH)]

    x2d = x.reshape(n, D)
    saved = []
    ride = _Exchange(gathers=[wb_b[0], wo_b[0], conv_b_[0]])
    order = jnp.bitwise_xor(me, jnp.array([0, 1, 4, 5, 2, 3, 6, 7], jnp.int32)).astype(jnp.int32)
    (proj, h), w_in_g0, (wb_g, wo_g, conv_g) = _in_proj_first(
        x2d, norm_g[0:1], w_in_b[0], order, tm_in, ride)
    w_in_g = [w_in_g0, None]
    for l in range(DEPTH):
        if l == 1:
            ride = _Exchange(gathers=[wb_b[1], wo_b[1], conv_b_[1]])
            (proj, h), (wb_g, wo_g, conv_g) = _in_proj(x2d, norm_g[l:l + 1], w_in_g[l], tm_fwd, ride)
        wb_l, wo_l, conv_l, sc_l = _gathered_weights(wb_g, wo_g, conv_g)
        small = _layer_small_inputs(l, rep, conv_l, sc_l)
        ride = _Exchange(chip_gathers=list(w_in_b[1])) if l == 0 else None
        head = (loss_target.reshape(n, D), final_g.reshape(1, D)) if l == DEPTH - 1 else None
        (z, y_conv, q_sc, pooled, merged, x_new, *head_out), gathered = _mixers_fwd(
            proj, x2d, wb_l, wo_l, *small, nb, seq, t_fwd if head is not None else min(2 * t_fwd, seq), ride, head)
        if l == 0:
            w_in_g[1] = gathered
        saved.append((x2d, proj, h, z, y_conv, q_sc, pooled, merged, small, wb_l, wo_l))
        x2d = x_new

    dx = x2d
    loss_acc, d_final_g = head_out

    grads = {k: [None] * DEPTH for k in SMALL}
    big = {"w_in": None, "w_branch": None, "w_o": None}
    big_w = {"w_in": (w_in, m_w_in, v_w_in, (DEPTH, D, SHARD), 256),
             "w_branch": (w_branch, m_w_branch, v_w_branch, (DEPTH, 4 * BW, D // NDEV), 512),
             "w_o": (w_o, m_w_o, v_w_o, (DEPTH, D // NDEV, D), 64)}

    def adamw(k, own, landed, l):
        wk, mk, vk, shape, tr = big_w[k]
        big[k] = _adamw_layer(own, landed, wk.reshape(shape), mk.reshape(shape), vk.reshape(shape), l, big[k], tr,
                              "adamw_" + k)

    for l in reversed(range(DEPTH)):
        x_in, proj, h, z, y_conv, q_sc, pooled, merged, small, wb_l, wo_l = saved[l]
        d_bo, d_proj, d_wo, d_wo_b, d_wb, d_wb_b = _merge_out_bwd(dx, z, proj, merged, wb_l, wo_l, tm)
        ride = _Exchange(scatters=[(d_wo_b.reshape(NDEV, D // NDEV, D), d_wo.reshape(NDEV, D // NDEV, D)),
                                   (d_wb_b.reshape(NDEV, 4 * BW, D // NDEV), d_wb.reshape(NDEV, 4 * BW, D // NDEV))])
        (d_proj, dvec, dconvw, dscw, dpoolw, dsguw, dsgub), (wo_landed, wo_own, wb_landed, wb_own) = _mixers_bwd(
            proj, d_bo, wb_l, y_conv, q_sc, pooled, d_proj, *small, nb, seq, t_bwd, ride)
        grads["pool_w"][l] = dpoolw
        grads["pool_scale"][l] = dvec[0]
        grads["conv_w"][l] = dconvw[:CONV_K]
        grads["conv_b"][l] = dvec[1]
        grads["conv_ln_g"][l] = dvec[2]
        grads["conv_ln_b"][l] = dvec[3]
        grads["sgu_ln_g"][l] = dvec[4]
        grads["sgu_ln_b"][l] = dvec[5]
        grads["sgu_w"][l] = dsguw
        grads["sgu_b"][l] = dsgub[:NG]
        grads["sc_w"][l] = dscw[:SHORT_K]
        if l == 0:
            grads["norm_g"][0] = jnp.zeros((D,), F32)
            part_pack = _pack([jnp.stack(grads[k]) for k in SMALL] + [d_final_g[0], loss_acc[0, 0:1]])
            ride = _Exchange(gathers=[part_pack])
        else:
            ride = None
        (d_w_in, d_w_in_b), gathered = _grad_w_in(h, d_proj, tk, ride)
        if l == 0:
            parts = gathered[0]
        ride = _Exchange(scatters=[(d_w_in_b, d_w_in)])
        (dx, d_norm_g), (in_landed, in_own) = _in_proj_bwd(d_proj, w_in_g[l], x_in, norm_g[l:l + 1], dx, tm_in, ride)
        adamw("w_o", wo_own, wo_landed, l)
        adamw("w_branch", wb_own, wb_landed, l)
        adamw("w_in", in_own, in_landed, l)
        if l == 1:
            grads["norm_g"][1] = d_norm_g[0]
    grad_x = dx.reshape(nb, seq, D)

    names = list(SMALL) + ["final_g", "loss"]
    shapes = [(DEPTH,) + SMALL_SHAPES[k] for k in SMALL] + [(D,), (1,)]
    whole = REPLICATED + ("final_g",)
    w_pack = _pack([w[k] if k in whole else jnp.zeros(s, F32) for k, s in zip(names, shapes)])
    m_pack = _pack([mom[k] if k in whole else jnp.zeros(s, F32) for k, s in zip(names, shapes)])
    v_pack = _pack([var[k] if k in whole else jnp.ones(s, F32) for k, s in zip(names, shapes)])
    rows = parts.shape[1]
    tr = max(r for r in range(8, 641, 8) if rows % r == 0)
    packs = _adamw_gathered(parts, w_pack, m_pack, v_pack, tr, "adamw_small")
    out_g, out_d, out_m, out_v = [dict(zip(names, _unpack(p, shapes))) for p in packs]
    loss = out_g["loss"][0]

    (tail,) = _all_gather([d_norm_g.reshape(8, 128)], "norm_g_all_gather")
    first = [a[0].reshape(8, 128) for a in (w["norm_g"], mom["norm_g"], var["norm_g"])]
    tails = _adamw_gathered(tail, *first, 8, "adamw_norm_g")
    for dst, t0 in zip((out_g, out_d, out_m, out_v), tails):
        dst["norm_g"] = jnp.concatenate([t0.reshape(1, D), dst["norm_g"][1:]])

    col = me * (BW // NDEV)
    for k in ("conv_w", "sc_w"):
        out_g[k] = lax.dynamic_slice_in_dim(out_g[k], col, BW // NDEV, axis=2)
    sharded_shapes = [w["conv_w"].shape, w["sc_w"].shape]
    d2, m2, v2 = _adamw_plain(_pack([out_g["conv_w"], out_g["sc_w"]]), _pack([w["conv_w"], w["sc_w"]]),
                              _pack([mom["conv_w"], mom["sc_w"]]), _pack([var["conv_w"], var["sc_w"]]))
    for dst, pk in ((out_d, d2), (out_m, m2), (out_v, v2)):
        dst["conv_w"], dst["sc_w"] = _unpack(pk, sharded_shapes)

    for k in big:
        out_g[k], out_d[k], out_m[k], out_v[k] = [o.reshape(w[k].shape) for o in big[k]]

    order = ("norm_g", "w_in", "pool_w", "pool_scale", "conv_w", "conv_b", "conv_ln_g", "conv_ln_b", "sgu_ln_g",
             "sgu_ln_b", "sgu_w", "sgu_b", "sc_w", "w_branch", "w_o", "final_g")
    return (loss, grad_x, *[out_g[k] for k in order], *[out_d[k] for k in order],
            *[out_m[k] for k in order], *[out_v[k] for k in order])
```
